```python
import math
import jax, jax.numpy as jnp
from jax import lax
import numpy as np

D_MODEL = 1024
BATCH = 8
SEQ = 8192
DEPTH = 2
DEC_BATCH = 8
DEC_SEQ = 2048
PAST_LEN = 128

RET_HEADS = 6
RET_HEAD_DIM = D_MODEL // 16
RET_WIDTH = RET_HEADS * RET_HEAD_DIM
RET_CHUNK = 128
DIFF_HEADS = 6
DIFF_QK_DIM = D_MODEL // 32
DIFF_V_DIM = 2 * DIFF_QK_DIM
DIFF_WIDTH = DIFF_HEADS * DIFF_V_DIM
ATTN_Q_BLOCK = 128
CONV_CHANNELS = D_MODEL - RET_WIDTH - DIFF_WIDTH
CONV_WIDTH = 3
MIX_WIDTH = RET_WIDTH + DIFF_WIDTH + CONV_CHANNELS
IN_WIDTH = 4 * RET_WIDTH + 3 * DIFF_WIDTH + 3 * CONV_CHANNELS
N_GROUPS = 4
EXPERTS_PER_GROUP = 4
N_EXPERTS = N_GROUPS * EXPERTS_PER_GROUP
TOP_K_IN_GROUP = 2
EXPERT_FF = D_MODEL // 2
EPS = 1e-6

kernel_name = "hymba_style_ret_diffattn_conv_hmoe_encoder"


def rms_norm(x, w):
    xf = x.astype(jnp.float32)
    y = xf * lax.rsqrt(jnp.mean(xf * xf, axis=-1, keepdims=True) + EPS)
    return (y * w.astype(jnp.float32)).astype(x.dtype)


def retention_scan(q, k, v, log_gamma, include_diag):
    B, S, H, Dh = q.shape
    C = RET_CHUNK
    n = S // C
    to_chunks = lambda a: a.reshape(B, n, C, H, Dh).transpose(1, 0, 3, 2, 4)
    qc, kc, vc = to_chunks(q), to_chunks(k), to_chunks(v)
    idx = jnp.arange(C, dtype=jnp.float32)
    diff = idx[:, None] - idx[None, :]
    mask = (diff >= 0) if include_diag else (diff > 0)
    lg = log_gamma[:, None, None]
    intra_decay = jnp.where(mask, jnp.exp(jnp.where(mask, diff, 0.0)[None] * lg), 0.0)
    q_decay = jnp.exp((idx + 1.0)[None, :] * log_gamma[:, None])[None, :, :, None]
    k_decay = jnp.exp((C - 1.0 - idx)[None, :] * log_gamma[:, None])[None, :, :, None]
    chunk_decay = jnp.exp(C * log_gamma)[None, :, None, None]

    def step(state, inp):
        qi, ki, vi = inp
        scores = jnp.einsum('bhid,bhjd->bhij', qi, ki) * intra_decay
        inner = jnp.einsum('bhij,bhjd->bhid', scores, vi)
        cross = jnp.einsum('bhid,bhde->bhie', qi, state) * q_decay
        new_state = state * chunk_decay + jnp.einsum('bhjd,bhje->bhde', ki * k_decay, vi)
        return new_state, inner + cross

    state0 = jnp.zeros((B, H, Dh, Dh), jnp.float32)
    _, out = lax.scan(step, state0, (qc, kc, vc))
    return out.transpose(1, 0, 3, 2, 4).reshape(B, S, H, Dh)


def retention_mixer(q, k, v, g, decay_logit, gn_w):
    B, S, _ = q.shape
    shp = (B, S, RET_HEADS, RET_HEAD_DIM)
    qf = q.reshape(shp).astype(jnp.float32)
    kf = k.reshape(shp).astype(jnp.float32) * (RET_HEAD_DIM ** -0.5)
    vf = v.reshape(shp).astype(jnp.float32)
    log_gamma = jax.nn.log_sigmoid(decay_logit.astype(jnp.float32))
    fwd = retention_scan(qf, kf, vf, log_gamma[0], True)
    bwd = jnp.flip(retention_scan(jnp.flip(qf, 1), jnp.flip(kf, 1), jnp.flip(vf, 1), log_gamma[1], False), 1)
    y = fwd + bwd
    mu = jnp.mean(y, axis=-1, keepdims=True)
    var = jnp.mean(jnp.square(y - mu), axis=-1, keepdims=True)
    y = ((y - mu) * lax.rsqrt(var + EPS)).reshape(B, S, RET_WIDTH) * gn_w.astype(jnp.float32)
    return (jax.nn.silu(g.astype(jnp.float32)) * y).astype(q.dtype)


def alibi_slopes(n_heads):
    return 2.0 ** (-8.0 * jnp.arange(1, n_heads + 1, dtype=jnp.float32) / n_heads)


def diff_attention_mixer(q, k, v, lam, subln_w, layer_idx):
    B, S, _ = q.shape
    qh = q.reshape(B, S, DIFF_HEADS, 2, DIFF_QK_DIM).astype(jnp.float32)
    kh = k.reshape(B, S, DIFF_HEADS, 2, DIFF_QK_DIM).astype(jnp.float32)
    vh = v.reshape(B, S, DIFF_HEADS, DIFF_V_DIM).astype(jnp.float32)
    lam_init = 0.8 - 0.6 * math.exp(-0.3 * layer_idx)
    lamf = lam.astype(jnp.float32)
    lambda_full = jnp.exp(jnp.sum(lamf[0] * lamf[1])) - jnp.exp(jnp.sum(lamf[2] * lamf[3])) + lam_init
    slopes = alibi_slopes(DIFF_HEADS)[None, :, None, None, None]
    nb = S // ATTN_Q_BLOCK
    qb = qh.reshape(B, nb, ATTN_Q_BLOCK, DIFF_HEADS, 2, DIFF_QK_DIM).transpose(1, 0, 2, 3, 4, 5)
    kpos = jnp.arange(S, dtype=jnp.float32)
    scale = DIFF_QK_DIM ** -0.5

    def block(args):
        qi, bidx = args
        scores = jnp.einsum('bqhcd,bkhcd->bhcqk', qi, kh) * scale
        qpos = (bidx * ATTN_Q_BLOCK + jnp.arange(ATTN_Q_BLOCK)).astype(jnp.float32)
        dist = jnp.abs(qpos[:, None] - kpos[None, :])
        probs = jax.nn.softmax(scores - slopes * dist[None, None, None], axis=-1)
        attn = probs[:, :, 0] - lambda_full * probs[:, :, 1]
        return jnp.einsum('bhqk,bkhd->bqhd', attn, vh)

    out = lax.map(block, (qb, jnp.arange(nb)))
    out = out.transpose(1, 0, 2, 3, 4).reshape(B, S, DIFF_HEADS, DIFF_V_DIM)
    out = out * lax.rsqrt(jnp.mean(out * out, axis=-1, keepdims=True) + EPS)
    out = out * subln_w.astype(jnp.float32) * (1.0 - lam_init)
    return out.reshape(B, S, DIFF_WIDTH).astype(q.dtype)


def short_conv_mixer(h, b_gate, c_gate, conv_w):
    u = c_gate * h
    pad = CONV_WIDTH // 2
    y = lax.conv_general_dilated(u, conv_w[:, None, :].astype(u.dtype), window_strides=(1,),
                                 padding=[(pad, pad)], dimension_numbers=('NWC', 'WIO', 'NWC'),
                                 feature_group_count=CONV_CHANNELS)
    return b_gate * y


def hier_moe(x, wg, bg, we, be, w_gate, w_up, w_down):
    B, S, D = x.shape
    xf = x.reshape(-1, D)
    T = xf.shape[0]
    g_probs = jax.nn.softmax((xf @ wg).astype(jnp.float32) + bg.astype(jnp.float32), axis=-1)
    g_w, g_idx = lax.top_k(g_probs, 1)
    e_logits = ((xf @ we).astype(jnp.float32) + be.astype(jnp.float32)).reshape(T, N_GROUPS, EXPERTS_PER_GROUP)
    g_onehot = jax.nn.one_hot(g_idx[:, 0], N_GROUPS, dtype=jnp.float32)
    e_sel = jnp.sum(e_logits * g_onehot[:, :, None], axis=1)
    e_probs = jax.nn.softmax(e_sel, axis=-1)
    top_p, top_i = lax.top_k(e_probs, TOP_K_IN_GROUP)
    top_p = top_p / jnp.sum(top_p, axis=-1, keepdims=True) * g_w
    expert_id = g_idx * EXPERTS_PER_GROUP + top_i
    gates = jnp.sum(jax.nn.one_hot(expert_id, N_EXPERTS, dtype=jnp.float32) * top_p[..., None], axis=1)
    out = jnp.zeros((T, D), jnp.float32)
    for e in range(N_EXPERTS):
        h = jax.nn.silu(xf @ w_gate[e]) * (xf @ w_up[e])
        out = out + gates[:, e:e + 1] * (h @ w_down[e]).astype(jnp.float32)
    return out.astype(x.dtype).reshape(B, S, D)


def encoder_layer(x, l, norm1_w, w_in, ret_decay_logit, ret_gn_w, diff_lambda, diff_subln_w, conv_w,
                  w_out, norm2_w, router_group_w, router_group_b, router_expert_w, router_expert_b,
                  expert_w_gate, expert_w_up, expert_w_down):
    h = rms_norm(x, norm1_w[l])
    proj = h @ w_in[l]
    R, A, C = RET_WIDTH, DIFF_WIDTH, CONV_CHANNELS
    offs = np.cumsum([R, R, R, R, A, A, A, C, C]).tolist()
    rq, rk, rv, rg, dq, dk, dv, ch, cb, cc = jnp.split(proj, offs, axis=-1)
    ret_out = retention_mixer(rq, rk, rv, rg, ret_decay_logit[l], ret_gn_w[l])
    diff_out = diff_attention_mixer(dq, dk, dv, diff_lambda[l], diff_subln_w[l], l)
    conv_out = short_conv_mixer(ch, cb, cc, conv_w[l])
    mixed = jnp.concatenate([ret_out, diff_out, conv_out], axis=-1)
    x = x + mixed @ w_out[l]
    x = x + hier_moe(rms_norm(x, norm2_w[l]), router_group_w[l], router_group_b[l], router_expert_w[l],
                     router_expert_b[l], expert_w_gate[l], expert_w_up[l], expert_w_down[l])
    return x


def trunk(x, norm1_w, w_in, ret_decay_logit, ret_gn_w, diff_lambda, diff_subln_w, conv_w, w_out, norm2_w,
          router_group_w, router_group_b, router_expert_w, router_expert_b, expert_w_gate, expert_w_up,
          expert_w_down, final_norm_w):
    for l in range(DEPTH):
        x = encoder_layer(x, l, norm1_w, w_in, ret_decay_logit, ret_gn_w, diff_lambda, diff_subln_w, conv_w,
                          w_out, norm2_w, router_group_w, router_group_b, router_expert_w, router_expert_b,
                          expert_w_gate, expert_w_up, expert_w_down)
    return rms_norm(x, final_norm_w)


def setup_inputs(seed: int = 0) -> dict:
    key = jax.random.key(seed)
    ks = jax.random.split(key, 20)
    f32 = jnp.float32
    nrm = lambda k, shape, s: jax.random.normal(k, shape, f32) * s
    base = np.log(2.0 ** (5.0 + np.arange(RET_HEADS)) - 1.0).astype(np.float32)
    ret_decay_logit = jnp.asarray(base)[None, None, :] + nrm(ks[2], (DEPTH, 2, RET_HEADS), 0.1)
    return {
        "x_prompt": jax.random.normal(ks[0], (BATCH, SEQ, D_MODEL), f32),
        "x_sample": jax.random.normal(ks[1], (DEC_BATCH, DEC_SEQ, D_MODEL), f32),
        "norm1_w": 1.0 + nrm(ks[3], (DEPTH, D_MODEL), 0.01),
        "w_in": nrm(ks[4], (DEPTH, D_MODEL, IN_WIDTH), D_MODEL ** -0.5),
        "ret_decay_logit": ret_decay_logit,
        "ret_gn_w": 1.0 + nrm(ks[5], (DEPTH, RET_WIDTH), 0.01),
        "diff_lambda": nrm(ks[6], (DEPTH, 4, DIFF_QK_DIM), 0.1),
        "diff_subln_w": 1.0 + nrm(ks[7], (DEPTH, DIFF_V_DIM), 0.01),
        "conv_w": nrm(ks[8], (DEPTH, CONV_WIDTH, CONV_CHANNELS), CONV_WIDTH ** -0.5),
        "w_out": nrm(ks[9], (DEPTH, MIX_WIDTH, D_MODEL), MIX_WIDTH ** -0.5),
        "norm2_w": 1.0 + nrm(ks[10], (DEPTH, D_MODEL), 0.01),
        "router_group_w": nrm(ks[11], (DEPTH, D_MODEL, N_GROUPS), D_MODEL ** -0.5),
        "router_group_b": nrm(ks[12], (DEPTH, N_GROUPS), 0.01),
        "router_expert_w": nrm(ks[13], (DEPTH, D_MODEL, N_EXPERTS), D_MODEL ** -0.5),
        "router_expert_b": nrm(ks[14], (DEPTH, N_EXPERTS), 0.01),
        "expert_w_gate": nrm(ks[15], (DEPTH, N_EXPERTS, D_MODEL, EXPERT_FF), D_MODEL ** -0.5),
        "expert_w_up": nrm(ks[16], (DEPTH, N_EXPERTS, D_MODEL, EXPERT_FF), D_MODEL ** -0.5),
        "expert_w_down": nrm(ks[17], (DEPTH, N_EXPERTS, EXPERT_FF, D_MODEL), EXPERT_FF ** -0.5),
        "final_norm_w": 1.0 + nrm(ks[18], (D_MODEL,), 0.01),
    }


def reference(x_prompt, x_sample, norm1_w, w_in, ret_decay_logit, ret_gn_w, diff_lambda, diff_subln_w, conv_w,
              w_out, norm2_w, router_group_w, router_group_b, router_expert_w, router_expert_b,
              expert_w_gate, expert_w_up, expert_w_down, final_norm_w):
    y_prompt = trunk(x_prompt, norm1_w, w_in, ret_decay_logit, ret_gn_w, diff_lambda, diff_subln_w, conv_w,
                     w_out, norm2_w, router_group_w, router_group_b, router_expert_w, router_expert_b,
                     expert_w_gate, expert_w_up, expert_w_down, final_norm_w)
    y_sample = trunk(x_sample, norm1_w, w_in, ret_decay_logit, ret_gn_w, diff_lambda, diff_subln_w, conv_w,
                     w_out, norm2_w, router_group_w, router_group_b, router_expert_w, router_expert_b,
                     expert_w_gate, expert_w_up, expert_w_down, final_norm_w)
    return (y_prompt, y_sample)
```

```python
import functools
import math

import jax
import jax.numpy as jnp
from jax import lax
from jax.experimental import pallas as pl
from jax.experimental.pallas import tpu as pltpu

F32 = jnp.float32
BF16 = jnp.bfloat16

D_MODEL = 1024
RET_HEADS = 6
RET_HEAD_DIM = 64
RET_WIDTH = RET_HEADS * RET_HEAD_DIM
DIFF_HEADS = 6
DIFF_QK_DIM = 32
DIFF_V_DIM = 64
DIFF_WIDTH = DIFF_HEADS * DIFF_V_DIM
CONV_CHANNELS = 256
CONV_WIDTH = 3
N_GROUPS = 4
EXPERTS_PER_GROUP = 4
N_EXPERTS = 16
EXPERT_FF = 512
EPS = 1e-6

LANES = 128
HEAD_PAIRS = RET_HEADS // 2
RET_CHUNK = 128
SUBLANES = 8
VMEM_LIMIT = 56 * 1024 * 1024
LOG2E = math.log2(math.e)
NEG_BIG = -1e30

_OFF_RG = 3 * RET_WIDTH
_OFF_DQ = 4 * RET_WIDTH
_OFF_DK = _OFF_DQ + DIFF_WIDTH
_OFF_DV = _OFF_DK + DIFF_WIDTH
_OFF_CONV = _OFF_DV + DIFF_WIDTH


def _params(*sem):
    return pltpu.CompilerParams(dimension_semantics=sem, vmem_limit_bytes=VMEM_LIMIT)


def _split_hi_lo(x):
    hi = x.astype(BF16)
    lo = (x - hi.astype(F32)).astype(BF16)
    return hi, lo


def _in_proj_kernel(x_ref, nw_ref, w_ref, wt_ref, rqkv_ref, rg_ref, dk_ref, dqvt_ref, cv_ref):
    x = x_ref[0]
    ms = jnp.mean(x * x, axis=-1, keepdims=True)
    h = (x * lax.rsqrt(ms + EPS) * nw_ref[...]).astype(BF16)
    a = jnp.dot(h, w_ref[:, 0:_OFF_DQ], preferred_element_type=F32)
    rqkv_ref[0, :, 0:RET_WIDTH] = a[:, 0:RET_WIDTH].astype(BF16)
    rqkv_ref[0, :, RET_WIDTH:2 * RET_WIDTH] = (a[:, RET_WIDTH:2 * RET_WIDTH] * RET_HEAD_DIM ** -0.5).astype(BF16)
    rqkv_ref[0, :, 2 * RET_WIDTH:3 * RET_WIDTH] = a[:, 2 * RET_WIDTH:3 * RET_WIDTH].astype(BF16)
    rg_ref[0] = a[:, _OFF_RG:_OFF_DQ]
    dk_ref[0] = jnp.dot(h, w_ref[:, _OFF_DQ:_OFF_DQ + DIFF_WIDTH], preferred_element_type=F32).astype(BF16)
    cv_ref[0] = jnp.dot(h, w_ref[:, _OFF_DQ + DIFF_WIDTH:], preferred_element_type=F32)
    t = lax.dot_general(wt_ref[...], h, (((1,), (1,)), ((), ())), preferred_element_type=F32)
    dqvt_ref[0, 0:DIFF_WIDTH, :] = (t[0:DIFF_WIDTH] * (DIFF_QK_DIM ** -0.5 * LOG2E)).astype(BF16)
    dqvt_ref[0, DIFF_WIDTH:, :] = t[DIFF_WIDTH:].astype(BF16)


def _in_proj(x, nw, w_nat, w_t, tm):
    B, S, D = x.shape
    n_nat = w_nat.shape[1]
    return pl.pallas_call(
        _in_proj_kernel,
        grid=(B, S // tm),
        in_specs=[
            pl.BlockSpec((1, tm, D), lambda b, i: (b, i, 0)),
            pl.BlockSpec((1, D), lambda b, i: (0, 0)),
            pl.BlockSpec((D, n_nat), lambda b, i: (0, 0)),
            pl.BlockSpec((2 * DIFF_WIDTH, D), lambda b, i: (0, 0)),
        ],
        out_specs=[
            pl.BlockSpec((1, tm, 3 * RET_WIDTH), lambda b, i: (b, i, 0)),
            pl.BlockSpec((1, tm, RET_WIDTH), lambda b, i: (b, i, 0)),
            pl.BlockSpec((1, tm, DIFF_WIDTH), lambda b, i: (b, i, 0)),
            pl.BlockSpec((1, 2 * DIFF_WIDTH, tm), lambda b, i: (b, 0, i)),
            pl.BlockSpec((1, tm, 3 * CONV_CHANNELS), lambda b, i: (b, i, 0)),
        ],
        out_shape=[
            jax.ShapeDtypeStruct((B, S, 3 * RET_WIDTH), BF16),
            jax.ShapeDtypeStruct((B, S, RET_WIDTH), F32),
            jax.ShapeDtypeStruct((B, S, DIFF_WIDTH), BF16),
            jax.ShapeDtypeStruct((B, 2 * DIFF_WIDTH, S), BF16),
            jax.ShapeDtypeStruct((B, S, 3 * CONV_CHANNELS), F32),
        ],
        compiler_params=_params("parallel", "parallel"),
        name="in_proj",
    )(x, nw, w_nat, w_t)


def _pair_lane_value(lg_ref, direction, pair, shape, axis):
    idx = lax.broadcasted_iota(jnp.int32, shape, axis)
    return jnp.where(idx < RET_HEAD_DIM, lg_ref[direction, 2 * pair], lg_ref[direction, 2 * pair + 1])


def _same_head_mask():
    r = lax.broadcasted_iota(jnp.int32, (LANES, LANES), 0)
    c = lax.broadcasted_iota(jnp.int32, (LANES, LANES), 1)
    return (r < RET_HEAD_DIM) == (c < RET_HEAD_DIM)


def _ret_fwd_kernel(lg_ref, qkv_ref, y_ref, state_ref, dmat_ref, qdec_ref, kdec_ref, cdec_ref):
    C = RET_CHUNK

    @pl.when((pl.program_id(0) == 0) & (pl.program_id(1) == 0))
    def _build_tables():
        diff = (lax.broadcasted_iota(jnp.int32, (C, C), 0) - lax.broadcasted_iota(jnp.int32, (C, C), 1)).astype(F32)
        for h in range(RET_HEADS):
            lower = jnp.exp(jnp.maximum(diff, 0.0) * lg_ref[0, h])
            upper = jnp.exp(jnp.maximum(-diff, 0.0) * lg_ref[1, h])
            dmat_ref[h] = jnp.where(diff >= 0, lower, upper)
        pos = lax.broadcasted_iota(jnp.int32, (C, LANES), 0).astype(F32)
        for p in range(HEAD_PAIRS):
            lg_lane = _pair_lane_value(lg_ref, 0, p, (C, LANES), 1)
            qdec_ref[p] = jnp.exp((pos + 1.0) * lg_lane)
            kdec_ref[p] = jnp.exp((C - 1.0 - pos) * lg_lane)
            cdec_ref[p] = jnp.exp(float(C) * _pair_lane_value(lg_ref, 0, p, (LANES, LANES), 0))

    @pl.when(pl.program_id(1) == 0)
    def _reset():
        state_ref[...] = jnp.zeros(state_ref.shape, F32)

    lane = lax.broadcasted_iota(jnp.int32, (C, LANES), 1)
    same_head = _same_head_mask()
    for p in range(HEAD_PAIRS):
        q = qkv_ref[0, :, p * LANES:(p + 1) * LANES]
        k = qkv_ref[0, :, RET_WIDTH + p * LANES:RET_WIDTH + (p + 1) * LANES]
        v = qkv_ref[0, :, 2 * RET_WIDTH + p * LANES:2 * RET_WIDTH + (p + 1) * LANES]
        inner = []
        for h in range(2):
            head_lanes = (lane >= h * RET_HEAD_DIM) & (lane < (h + 1) * RET_HEAD_DIM)
            qm = jnp.where(head_lanes, q, jnp.zeros_like(q))
            sc = lax.dot_general(qm, k, (((1,), (1,)), ((), ())), preferred_element_type=F32)
            a = (sc * dmat_ref[2 * p + h]).astype(BF16)
            inner.append(jnp.dot(a, v, preferred_element_type=F32))
        state = state_ref[p]
        cross = jnp.dot(q, state.astype(BF16), preferred_element_type=F32) * qdec_ref[p]
        y_ref[0, :, p * LANES:(p + 1) * LANES] = jnp.where(lane < RET_HEAD_DIM, inner[0], inner[1]) + cross
        kd = (k.astype(F32) * kdec_ref[p]).astype(BF16)
        kv = lax.dot_general(kd, v, (((0,), (0,)), ((), ())), preferred_element_type=F32)
        state_ref[p] = cdec_ref[p] * state + jnp.where(same_head, kv, 0.0)


def _ret_bwd_kernel(lg_ref, qkv_ref, y1_ref, g_ref, gnw_ref, o_ref, state_ref, qdec_ref, kdec_ref, cdec_ref):
    C = RET_CHUNK

    @pl.when((pl.program_id(0) == 0) & (pl.program_id(1) == 0))
    def _build_tables():
        pos = lax.broadcasted_iota(jnp.int32, (C, LANES), 0).astype(F32)
        for p in range(HEAD_PAIRS):
            lg_lane = _pair_lane_value(lg_ref, 1, p, (C, LANES), 1)
            qdec_ref[p] = jnp.exp((float(C) - pos) * lg_lane)
            kdec_ref[p] = jnp.exp(pos * lg_lane)
            cdec_ref[p] = jnp.exp(float(C) * _pair_lane_value(lg_ref, 1, p, (LANES, LANES), 0))

    @pl.when(pl.program_id(1) == 0)
    def _reset():
        state_ref[...] = jnp.zeros(state_ref.shape, F32)

    same_head = _same_head_mask()
    head_avg = jnp.where(same_head, 1.0 / RET_HEAD_DIM, 0.0).astype(BF16)

    def head_mean(t):
        hi, lo = _split_hi_lo(t)
        return (jnp.dot(hi, head_avg, preferred_element_type=F32)
                + jnp.dot(lo, head_avg, preferred_element_type=F32))

    for p in range(HEAD_PAIRS):
        sl = slice(p * LANES, (p + 1) * LANES)
        q = qkv_ref[0, :, sl]
        k = qkv_ref[0, :, RET_WIDTH + p * LANES:RET_WIDTH + (p + 1) * LANES]
        v = qkv_ref[0, :, 2 * RET_WIDTH + p * LANES:2 * RET_WIDTH + (p + 1) * LANES]
        state = state_ref[p]
        y = y1_ref[0, :, sl] + jnp.dot(q, state.astype(BF16), preferred_element_type=F32) * qdec_ref[p]
        kd = (k.astype(F32) * kdec_ref[p]).astype(BF16)
        kv = lax.dot_general(kd, v, (((0,), (0,)), ((), ())), preferred_element_type=F32)
        state_ref[p] = cdec_ref[p] * state + jnp.where(same_head, kv, 0.0)
        d = y - head_mean(y)
        var = head_mean(d * d)
        g = g_ref[0, :, sl]
        o = d * lax.rsqrt(var + EPS) * gnw_ref[:, sl] * (g * jax.nn.sigmoid(g))
        o_ref[0, :, sl] = o.astype(o_ref.dtype)


def _retention(lg, rqkv, rg, gn_w):
    B, S, _ = rqkv.shape
    C = RET_CHUNK
    n = S // C
    smem = pl.BlockSpec(memory_space=pltpu.SMEM)
    table = pltpu.VMEM((HEAD_PAIRS, C, LANES), F32)
    state = pltpu.VMEM((HEAD_PAIRS, LANES, LANES), F32)
    y1 = pl.pallas_call(
        _ret_fwd_kernel,
        grid=(B, n),
        in_specs=[smem, pl.BlockSpec((1, C, 3 * RET_WIDTH), lambda b, i: (b, i, 0))],
        out_specs=pl.BlockSpec((1, C, RET_WIDTH), lambda b, i: (b, i, 0)),
        out_shape=jax.ShapeDtypeStruct((B, S, RET_WIDTH), F32),
        scratch_shapes=[state, pltpu.VMEM((RET_HEADS, C, C), F32), table, table, state],
        compiler_params=_params("arbitrary", "arbitrary"),
        name="ret_fwd",
    )(lg, rqkv)
    rev = lambda b, i: (b, n - 1 - i, 0)
    return pl.pallas_call(
        _ret_bwd_kernel,
        grid=(B, n),
        in_specs=[
            smem,
            pl.BlockSpec((1, C, 3 * RET_WIDTH), rev),
            pl.BlockSpec((1, C, RET_WIDTH), rev),
            pl.BlockSpec((1, C, RET_WIDTH), rev),
            pl.BlockSpec((1, RET_WIDTH), lambda b, i: (0, 0)),
        ],
        out_specs=pl.BlockSpec((1, C, RET_WIDTH), rev),
        out_shape=jax.ShapeDtypeStruct((B, S, RET_WIDTH), BF16),
        scratch_shapes=[state, table, table, state],
        compiler_params=_params("arbitrary", "arbitrary"),
        name="ret_bwd",
    )(lg, rqkv, y1, rg, gn_w)


def _attn_kernel(slopes_ref, lam_ref, sw_ref, qt_ref, k_ref, vt_ref, o_ref, m_ref, acc_ref, *, seq, tq, tk,
                 lam_init):
    pair = pl.program_id(1)
    q0 = pl.program_id(2) * tq
    qt = qt_ref[0]
    row = lax.broadcasted_iota(jnp.int32, (LANES, tq), 0)
    q_masked = []
    for hc in range(4):
        lo = hc * DIFF_QK_DIM
        q_masked.append(jnp.where((row >= lo) & (row < lo + DIFF_QK_DIM), qt, jnp.zeros_like(qt)))
    m_ref[...] = jnp.full(m_ref.shape, NEG_BIG, F32)
    acc_ref[...] = jnp.zeros(acc_ref.shape, F32)
    rel = (lax.broadcasted_iota(jnp.int32, (tk, tq), 1) - lax.broadcasted_iota(jnp.int32, (tk, tq), 0)).astype(F32)
    vrow = lax.broadcasted_iota(jnp.int32, (LANES, tk), 0)
    slope = [slopes_ref[2 * pair + h] for h in range(2)]

    def body(kb, carry):
        k0 = pl.multiple_of(kb * tk, tk)
        kblk = k_ref[0, pl.ds(k0, tk), :]
        vt = vt_ref[0, :, pl.ds(k0, tk)]
        dist = jnp.abs(rel + (q0 - k0).astype(F32))
        for h in range(2):
            bias = dist * slope[h]
            own = (vrow >= h * DIFF_V_DIM) & (vrow < (h + 1) * DIFF_V_DIM)
            v_ones = jnp.where(own, vt, jnp.ones_like(vt))
            for c in range(2):
                hc = 2 * h + c
                s = jnp.dot(kblk, q_masked[hc], preferred_element_type=F32) - bias
                m_old = m_ref[hc]
                m_new = jnp.maximum(m_old, jnp.max(s, axis=0, keepdims=True))
                p = jnp.exp2(s - m_new).astype(BF16)
                acc_ref[hc] = jnp.exp2(m_old - m_new) * acc_ref[hc] + jnp.dot(v_ones, p, preferred_element_type=F32)
                m_ref[hc] = m_new
        return carry

    lax.fori_loop(0, seq // tk, body, 0)

    lam = lam_ref[...]
    lam_full = (jnp.exp(jnp.sum(lam[0:1] * lam[1:2], axis=-1, keepdims=True))
                - jnp.exp(jnp.sum(lam[2:3] * lam[3:4], axis=-1, keepdims=True)) + lam_init)
    outs = []
    for h in range(2):
        a1 = acc_ref[2 * h]
        a2 = acc_ref[2 * h + 1]
        ones_row = DIFF_V_DIM if h == 0 else 0
        rows = slice(h * DIFF_V_DIM, (h + 1) * DIFF_V_DIM)
        o = a1[rows] / a1[ones_row:ones_row + 1] - lam_full * (a2[rows] / a2[ones_row:ones_row + 1])
        o = o * lax.rsqrt(jnp.mean(o * o, axis=0, keepdims=True) + EPS)
        outs.append(o * sw_ref[...] * (1.0 - lam_init))
    o_ref[0] = jnp.concatenate(outs, axis=0).T.astype(o_ref.dtype)


def _diff_attention(slopes, lam, subln_w, dqvt, dk, layer_idx, tq, tk):
    B, S, _ = dk.shape
    lam_init = 0.8 - 0.6 * math.exp(-0.3 * layer_idx)
    kern = functools.partial(_attn_kernel, seq=S, tq=tq, tk=tk, lam_init=lam_init)
    v_block0 = DIFF_WIDTH // LANES
    return pl.pallas_call(
        kern,
        grid=(B, HEAD_PAIRS, S // tq),
        in_specs=[
            pl.BlockSpec(memory_space=pltpu.SMEM),
            pl.BlockSpec((4, DIFF_QK_DIM), lambda b, j, i: (0, 0)),
            pl.BlockSpec((DIFF_V_DIM, 1), lambda b, j, i: (0, 0)),
            pl.BlockSpec((1, LANES, tq), lambda b, j, i: (b, j, i)),
            pl.BlockSpec((1, S, LANES), lambda b, j, i: (b, 0, j)),
            pl.BlockSpec((1, LANES, S), lambda b, j, i: (b, v_block0 + j, 0)),
        ],
        out_specs=pl.BlockSpec((1, tq, LANES), lambda b, j, i: (b, i, j)),
        out_shape=jax.ShapeDtypeStruct((B, S, DIFF_WIDTH), BF16),
        scratch_shapes=[pltpu.VMEM((4, 1, tq), F32), pltpu.VMEM((4, LANES, tq), F32)],
        compiler_params=_params("parallel", "parallel", "arbitrary"),
        name="diff_attn",
    )(slopes, lam, subln_w, dqvt, dk, dqvt)


def _out_proj_kernel(x_ref, ret_ref, diff_ref, cv_ref, prev_ref, next_ref, cw_ref, w_ref, o_ref, *, tm):
    i = pl.program_id(1)
    last = pl.num_programs(1) - 1
    cc = CONV_CHANNELS
    cv = cv_ref[0]
    u = cv[:, 2 * cc:3 * cc] * cv[:, 0:cc]
    prev = prev_ref[0]
    nxt = next_ref[0]
    u_prev = prev[SUBLANES - 1:SUBLANES, 2 * cc:3 * cc] * prev[SUBLANES - 1:SUBLANES, 0:cc]
    u_next = nxt[0:1, 2 * cc:3 * cc] * nxt[0:1, 0:cc]
    u_prev = jnp.where(i == 0, 0.0, u_prev)
    u_next = jnp.where(i == last, 0.0, u_next)
    rows = lax.broadcasted_iota(jnp.int32, (tm, cc), 0)
    u_m1 = jnp.where(rows == 0, u_prev, pltpu.roll(u, 1, 0))
    u_p1 = jnp.where(rows == tm - 1, u_next, pltpu.roll(u, tm - 1, 0))
    y = cw_ref[0:1] * u_m1 + cw_ref[1:2] * u + cw_ref[2:3] * u_p1
    conv = (cv[:, cc:2 * cc] * y).astype(BF16)
    acc = jnp.dot(ret_ref[0], w_ref[0:RET_WIDTH], preferred_element_type=F32)
    acc += jnp.dot(diff_ref[0], w_ref[RET_WIDTH:RET_WIDTH + DIFF_WIDTH], preferred_element_type=F32)
    acc += jnp.dot(conv, w_ref[RET_WIDTH + DIFF_WIDTH:], preferred_element_type=F32)
    o_ref[0] = x_ref[0] + acc


def _out_proj(x, ret, diff, cv, conv_w, w_out, tm):
    B, S, D = x.shape
    per = tm // SUBLANES
    nblk8 = S // SUBLANES
    tile = lambda width: pl.BlockSpec((1, tm, width), lambda b, i: (b, i, 0))
    return pl.pallas_call(
        functools.partial(_out_proj_kernel, tm=tm),
        grid=(B, S // tm),
        in_specs=[
            tile(D), tile(RET_WIDTH), tile(DIFF_WIDTH), tile(3 * CONV_CHANNELS),
            pl.BlockSpec((1, SUBLANES, 3 * CONV_CHANNELS), lambda b, i: (b, jnp.maximum(i * per - 1, 0), 0)),
            pl.BlockSpec((1, SUBLANES, 3 * CONV_CHANNELS),
                         lambda b, i: (b, jnp.minimum((i + 1) * per, nblk8 - 1), 0)),
            pl.BlockSpec((CONV_WIDTH, CONV_CHANNELS), lambda b, i: (0, 0)),
            pl.BlockSpec((D, D), lambda b, i: (0, 0)),
        ],
        out_specs=tile(D),
        out_shape=jax.ShapeDtypeStruct((B, S, D), F32),
        compiler_params=_params("parallel", "parallel"),
        name="out_proj",
    )(x, ret, diff, cv, cv, cv, conv_w, w_out)


_GROUP_LANE0 = N_EXPERTS


def _router_gates(logits):
    lane = lax.broadcasted_iota(jnp.int32, logits.shape, 1)
    big = jnp.int32(LANES)
    is_group = (lane >= _GROUP_LANE0) & (lane < _GROUP_LANE0 + N_GROUPS)
    gl = jnp.where(is_group, logits, -jnp.inf)
    g_max = jnp.max(gl, axis=-1, keepdims=True)
    g_idx = jnp.min(jnp.where(gl == g_max, lane - _GROUP_LANE0, big), axis=-1, keepdims=True)
    g_w = 1.0 / jnp.sum(jnp.where(is_group, jnp.exp(gl - g_max), 0.0), axis=-1, keepdims=True)
    in_group = (lane >= g_idx * EXPERTS_PER_GROUP) & (lane < (g_idx + 1) * EXPERTS_PER_GROUP)
    el = jnp.where(in_group, logits, -jnp.inf)
    e1 = jnp.max(el, axis=-1, keepdims=True)
    i1 = jnp.min(jnp.where(el == e1, lane, big), axis=-1, keepdims=True)
    el2 = jnp.where(lane == i1, -jnp.inf, el)
    e2 = jnp.max(el2, axis=-1, keepdims=True)
    i2 = jnp.min(jnp.where(el2 == e2, lane, big), axis=-1, keepdims=True)
    r = jnp.exp(e2 - e1)
    p1 = g_w / (1.0 + r)
    p2 = g_w * r / (1.0 + r)
    return jnp.where(lane == i1, p1, jnp.where(lane == i2, p2, 0.0))


def _moe_kernel(x_ref, nw_ref, rhi_ref, rlo_ref, rb_ref, wg_ref, wu_ref, wd_ref, fw_ref, o_ref,
                xn_ref, gates_ref, acc_ref, *, final_norm):
    e = pl.program_id(1)

    @pl.when(e == 0)
    def _route():
        x = x_ref[...]
        xn = x * lax.rsqrt(jnp.mean(x * x, axis=-1, keepdims=True) + EPS) * nw_ref[...]
        hi, lo = _split_hi_lo(xn)
        logits = (jnp.dot(hi, rhi_ref[...], preferred_element_type=F32)
                  + jnp.dot(lo, rhi_ref[...], preferred_element_type=F32)
                  + jnp.dot(hi, rlo_ref[...], preferred_element_type=F32)) + rb_ref[...]
        gates_ref[...] = _router_gates(logits)
        xn_ref[...] = hi
        acc_ref[...] = jnp.zeros(acc_ref.shape, F32)

    xn = xn_ref[...]
    gate = jnp.dot(xn, wg_ref[0], preferred_element_type=F32)
    up = jnp.dot(xn, wu_ref[0], preferred_element_type=F32)
    hmid = (gate * jax.nn.sigmoid(gate) * up).astype(BF16)
    gates = gates_ref[...]
    lane = lax.broadcasted_iota(jnp.int32, gates.shape, 1)
    g_e = jnp.sum(jnp.where(lane == e, gates, 0.0), axis=-1, keepdims=True)
    acc_ref[...] += g_e * jnp.dot(hmid, wd_ref[0], preferred_element_type=F32)

    @pl.when(e == pl.num_programs(1) - 1)
    def _finish():
        y = x_ref[...] + acc_ref[...]
        if final_norm:
            y = y * lax.rsqrt(jnp.mean(y * y, axis=-1, keepdims=True) + EPS) * fw_ref[...]
        o_ref[...] = y


def _moe(x, nw, r_hi, r_lo, r_b, wg, wu, wd, final_w, final_norm, tm):
    T, D = x.shape
    const = lambda shape: pl.BlockSpec(shape, lambda i, e: (0,) * len(shape))
    return pl.pallas_call(
        functools.partial(_moe_kernel, final_norm=final_norm),
        grid=(T // tm, N_EXPERTS),
        in_specs=[
            pl.BlockSpec((tm, D), lambda i, e: (i, 0)),
            const((1, D)), const((D, LANES)), const((D, LANES)), const((1, LANES)),
            pl.BlockSpec((1, D, EXPERT_FF), lambda i, e: (e, 0, 0)),
            pl.BlockSpec((1, D, EXPERT_FF), lambda i, e: (e, 0, 0)),
            pl.BlockSpec((1, EXPERT_FF, D), lambda i, e: (e, 0, 0)),
            const((1, D)),
        ],
        out_specs=pl.BlockSpec((tm, D), lambda i, e: (i, 0)),
        out_shape=jax.ShapeDtypeStruct((T, D), F32),
        scratch_shapes=[pltpu.VMEM((tm, D), BF16), pltpu.VMEM((tm, LANES), F32), pltpu.VMEM((tm, D), F32)],
        compiler_params=_params("parallel", "arbitrary"),
        name="moe",
    )(x, nw, r_hi, r_lo, r_b, wg, wu, wd, final_w)


def _tile(n, pref):
    t = min(n, pref)
    assert n % t == 0, (n, t)
    return t


def _prep_layer(l, norm1_w, w_in, ret_decay_logit, ret_gn_w, diff_lambda, diff_subln_w, conv_w, w_out, norm2_w,
                router_group_w, router_group_b, router_expert_w, router_expert_b, expert_w_gate, expert_w_up,
                expert_w_down):
    w = w_in[l]
    w_nat = jnp.concatenate([w[:, :_OFF_DQ], w[:, _OFF_DK:_OFF_DV], w[:, _OFF_CONV:]], axis=1).astype(BF16)
    w_t = jnp.concatenate([w[:, _OFF_DQ:_OFF_DK], w[:, _OFF_DV:_OFF_CONV]], axis=1).T.astype(BF16)
    router = jnp.zeros((D_MODEL, LANES), F32)
    router = router.at[:, :N_EXPERTS].set(router_expert_w[l]).at[:, N_EXPERTS:N_EXPERTS + N_GROUPS].set(
        router_group_w[l])
    r_hi, r_lo = _split_hi_lo(router)
    r_b = jnp.zeros((1, LANES), F32).at[0, :N_EXPERTS].set(router_expert_b[l]).at[
        0, N_EXPERTS:N_EXPERTS + N_GROUPS].set(router_group_b[l])
    return dict(
        norm1=norm1_w[l][None, :], w_nat=w_nat, w_t=w_t,
        lg=jax.nn.log_sigmoid(ret_decay_logit[l].astype(F32)),
        gn_w=ret_gn_w[l][None, :].astype(F32),
        lam=diff_lambda[l].astype(F32), subln=diff_subln_w[l][:, None].astype(F32),
        conv_w=conv_w[l].astype(F32), w_out=w_out[l].astype(BF16),
        norm2=norm2_w[l][None, :], r_hi=r_hi, r_lo=r_lo, r_b=r_b,
        wg=expert_w_gate[l].astype(BF16), wu=expert_w_up[l].astype(BF16), wd=expert_w_down[l].astype(BF16),
    )


def _trunk(x, layers, final_w, slopes):
    B, S, D = x.shape
    tm = _tile(S, 512)
    tq = _tile(S, 512)
    tk = _tile(S, 256)
    t_moe = _tile(B * S, 1024)
    for l, lw in enumerate(layers):
        rqkv, rg, dk, dqvt, cv = _in_proj(x, lw["norm1"], lw["w_nat"], lw["w_t"], tm)
        ret = _retention(lw["lg"], rqkv, rg, lw["gn_w"])
        diff = _diff_attention(slopes, lw["lam"], lw["subln"], dqvt, dk, l, tq, tk)
        x = _out_proj(x, ret, diff, cv, lw["conv_w"], lw["w_out"], tm)
        last = l == len(layers) - 1
        x = _moe(x.reshape(B * S, D), lw["norm2"], lw["r_hi"], lw["r_lo"], lw["r_b"], lw["wg"], lw["wu"],
                 lw["wd"], final_w, last, t_moe).reshape(B, S, D)
    return x


def kernel(x_prompt, x_sample, norm1_w, w_in, ret_decay_logit, ret_gn_w, diff_lambda, diff_subln_w, conv_w, w_out,
           norm2_w, router_group_w, router_group_b, router_expert_w, router_expert_b, expert_w_gate, expert_w_up,
           expert_w_down, final_norm_w):
    depth = w_in.shape[0]
    layers = [
        _prep_layer(l, norm1_w, w_in, ret_decay_logit, ret_gn_w, diff_lambda, diff_subln_w, conv_w, w_out, norm2_w,
                    router_group_w, router_group_b, router_expert_w, router_expert_b, expert_w_gate, expert_w_up,
                    expert_w_down)
        for l in range(depth)
    ]
    final_w = final_norm_w[None, :].astype(F32)
    slopes = (2.0 ** (-8.0 * jnp.arange(1, DIFF_HEADS + 1, dtype=F32) / DIFF_HEADS)) * LOG2E
    return (_trunk(x_prompt, layers, final_w, slopes), _trunk(x_sample, layers, final_w, slopes))
```

```python
import functools
import math

import jax
import jax.numpy as jnp
import numpy as np
from jax import lax
from jax.experimental import pallas as pl
from jax.experimental.pallas import tpu as pltpu

F32 = jnp.float32
BF16 = jnp.bfloat16

D_MODEL = 1024
RET_HEADS = 6
RET_HEAD_DIM = 64
RET_WIDTH = RET_HEADS * RET_HEAD_DIM
DIFF_HEADS = 6
DIFF_QK_DIM = 32
DIFF_V_DIM = 64
DIFF_WIDTH = DIFF_HEADS * DIFF_V_DIM
CONV_CHANNELS = 256
CONV_WIDTH = 3
N_GROUPS = 4
EXPERTS_PER_GROUP = 4
N_EXPERTS = 16
EXPERT_FF = 512
EPS = 1e-6

LANES = 128
HEAD_PAIRS = RET_HEADS // 2
RET_CHUNK = 128
SUBLANES = 8
VMEM_LIMIT = 56 * 1024 * 1024
LOG2E = math.log2(math.e)
NEG_BIG = -1e30

_OFF_RG = 3 * RET_WIDTH
_OFF_DQ = 4 * RET_WIDTH
_OFF_DK = _OFF_DQ + DIFF_WIDTH
_OFF_DV = _OFF_DK + DIFF_WIDTH
_OFF_CONV = _OFF_DV + DIFF_WIDTH


def _params(*sem):
    return pltpu.CompilerParams(dimension_semantics=sem, vmem_limit_bytes=VMEM_LIMIT)


def _split_hi_lo(x):
    hi = x.astype(BF16)
    lo = (x - hi.astype(F32)).astype(BF16)
    return hi, lo


def _in_proj_kernel(x_ref, nw_ref, w_ref, wt_ref, rqkv_ref, rg_ref, dk_ref, dqvt_ref, cv_ref, kn_ref):
    x = x_ref[0]
    ms = jnp.mean(x * x, axis=-1, keepdims=True)
    h = (x * lax.rsqrt(ms + EPS) * nw_ref[...]).astype(BF16)
    a = jnp.dot(h, w_ref[:, 0:_OFF_DQ], preferred_element_type=F32)
    rqkv_ref[0, :, 0:RET_WIDTH] = a[:, 0:RET_WIDTH].astype(BF16)
    rqkv_ref[0, :, RET_WIDTH:2 * RET_WIDTH] = (a[:, RET_WIDTH:2 * RET_WIDTH] * RET_HEAD_DIM ** -0.5).astype(BF16)
    rqkv_ref[0, :, 2 * RET_WIDTH:3 * RET_WIDTH] = a[:, 2 * RET_WIDTH:3 * RET_WIDTH].astype(BF16)
    rg_ref[0] = a[:, _OFF_RG:_OFF_DQ]
    dk = jnp.dot(h, w_ref[:, _OFF_DQ:_OFF_DQ + DIFF_WIDTH], preferred_element_type=F32).astype(BF16)
    dk_ref[0] = dk
    dkf = dk.astype(F32)
    sq_hi, sq_lo = _split_hi_lo(dkf * dkf)
    grp = (lax.shift_right_logical(lax.broadcasted_iota(jnp.int32, (DIFF_WIDTH, LANES), 0), 5)
           == lax.broadcasted_iota(jnp.int32, (DIFF_WIDTH, LANES), 1))
    grp = jnp.where(grp, 1.0, 0.0).astype(BF16)
    kn2 = jnp.dot(sq_hi, grp, preferred_element_type=F32) + jnp.dot(sq_lo, grp, preferred_element_type=F32)
    kn_ref[0, 0] = jnp.broadcast_to(jnp.max(kn2, axis=0, keepdims=True), (SUBLANES, LANES))
    cv_ref[0] =jnp.dot(h, w_ref[:, _OFF_DQ + DIFF_WIDTH:], preferred_element_type=F32)
    t = lax.dot_general(wt_ref[...], h, (((1,), (1,)), ((), ())), preferred_element_type=F32)
    dqvt_ref[0, 0:DIFF_WIDTH, :] = (t[0:DIFF_WIDTH] * (DIFF_QK_DIM ** -0.5 * LOG2E)).astype(BF16)
    dqvt_ref[0, DIFF_WIDTH:, :] = t[DIFF_WIDTH:].astype(BF16)


def _in_proj(x, nw, w_nat, w_t, tm):
    B, S, D = x.shape
    n_nat = w_nat.shape[1]
    return pl.pallas_call(
        _in_proj_kernel,
        grid=(B, S // tm),
        in_specs=[
            pl.BlockSpec((1, tm, D), lambda b, i: (b, i, 0)),
            pl.BlockSpec((1, D), lambda b, i: (0, 0)),
            pl.BlockSpec((D, n_nat), lambda b, i: (0, 0)),
            pl.BlockSpec((2 * DIFF_WIDTH, D), lambda b, i: (0, 0)),
        ],
        out_specs=[
            pl.BlockSpec((1, tm, 3 * RET_WIDTH), lambda b, i: (b, i, 0)),
            pl.BlockSpec((1, tm, RET_WIDTH), lambda b, i: (b, i, 0)),
            pl.BlockSpec((1, tm, DIFF_WIDTH), lambda b, i: (b, i, 0)),
            pl.BlockSpec((1, 2 * DIFF_WIDTH, tm), lambda b, i: (b, 0, i)),
            pl.BlockSpec((1, tm, 3 * CONV_CHANNELS), lambda b, i: (b, i, 0)),
            pl.BlockSpec((1, 1, SUBLANES, LANES), lambda b, i: (b, i, 0, 0)),
        ],
        out_shape=[
            jax.ShapeDtypeStruct((B, S, 3 * RET_WIDTH), BF16),
            jax.ShapeDtypeStruct((B, S, RET_WIDTH), F32),
            jax.ShapeDtypeStruct((B, S, DIFF_WIDTH), BF16),
            jax.ShapeDtypeStruct((B, 2 * DIFF_WIDTH, S), BF16),
            jax.ShapeDtypeStruct((B, S, 3 * CONV_CHANNELS), F32),
            jax.ShapeDtypeStruct((B, S // tm, SUBLANES, LANES), F32),
        ],
        compiler_params=_params("parallel", "parallel"),
        name="in_proj",
    )(x, nw, w_nat, w_t)


def _pair_lane_value(lg_ref, direction, pair, shape, axis):
    idx = lax.broadcasted_iota(jnp.int32, shape, axis)
    return jnp.where(idx < RET_HEAD_DIM, lg_ref[direction, 2 * pair], lg_ref[direction, 2 * pair + 1])


def _same_head_mask():
    r = lax.broadcasted_iota(jnp.int32, (LANES, LANES), 0)
    c = lax.broadcasted_iota(jnp.int32, (LANES, LANES), 1)
    return (r < RET_HEAD_DIM) == (c < RET_HEAD_DIM)


def _ret_fwd_kernel(lg_ref, qkv_ref, y_ref, state_ref, dmat_ref, qdec_ref, kdec_ref, cdec_ref):
    C = RET_CHUNK

    @pl.when((pl.program_id(0) == 0) & (pl.program_id(1) == 0))
    def _build_tables():
        diff = (lax.broadcasted_iota(jnp.int32, (C, C), 0) - lax.broadcasted_iota(jnp.int32, (C, C), 1)).astype(F32)
        for h in range(RET_HEADS):
            lower = jnp.exp(jnp.maximum(diff, 0.0) * lg_ref[0, h])
            upper = jnp.exp(jnp.maximum(-diff, 0.0) * lg_ref[1, h])
            dmat_ref[h] = jnp.where(diff >= 0, lower, upper)
        pos = lax.broadcasted_iota(jnp.int32, (C, LANES), 0).astype(F32)
        for p in range(HEAD_PAIRS):
            lg_lane = _pair_lane_value(lg_ref, 0, p, (C, LANES), 1)
            qdec_ref[p] = jnp.exp((pos + 1.0) * lg_lane)
            kdec_ref[p] = jnp.exp((C - 1.0 - pos) * lg_lane)
            cdec_ref[p] = jnp.exp(float(C) * _pair_lane_value(lg_ref, 0, p, (LANES, LANES), 0))

    @pl.when(pl.program_id(1) == 0)
    def _reset():
        state_ref[...] = jnp.zeros(state_ref.shape, F32)

    lane = lax.broadcasted_iota(jnp.int32, (C, LANES), 1)
    same_head = _same_head_mask()
    for p in range(HEAD_PAIRS):
        q = qkv_ref[0, :, p * LANES:(p + 1) * LANES]
        k = qkv_ref[0, :, RET_WIDTH + p * LANES:RET_WIDTH + (p + 1) * LANES]
        v = qkv_ref[0, :, 2 * RET_WIDTH + p * LANES:2 * RET_WIDTH + (p + 1) * LANES]
        inner = []
        for h in range(2):
            head_lanes = (lane >= h * RET_HEAD_DIM) & (lane < (h + 1) * RET_HEAD_DIM)
            qm = jnp.where(head_lanes, q, jnp.zeros_like(q))
            sc = lax.dot_general(qm, k, (((1,), (1,)), ((), ())), preferred_element_type=F32)
            a = (sc * dmat_ref[2 * p + h]).astype(BF16)
            inner.append(jnp.dot(a, v, preferred_element_type=F32))
        state = state_ref[p]
        cross = jnp.dot(q, state.astype(BF16), preferred_element_type=F32) * qdec_ref[p]
        y_ref[0, :, p * LANES:(p + 1) * LANES] = jnp.where(lane < RET_HEAD_DIM, inner[0], inner[1]) + cross
        kd = (k.astype(F32) * kdec_ref[p]).astype(BF16)
        kv = lax.dot_general(kd, v, (((0,), (0,)), ((), ())), preferred_element_type=F32)
        state_ref[p] = cdec_ref[p] * state + jnp.where(same_head, kv, 0.0)


def _ret_bwd_kernel(lg_ref, qkv_ref, y1_ref, g_ref, gnw_ref, o_ref, state_ref, qdec_ref, kdec_ref, cdec_ref):
    C = RET_CHUNK

    @pl.when((pl.program_id(0) == 0) & (pl.program_id(1) == 0))
    def _build_tables():
        pos = lax.broadcasted_iota(jnp.int32, (C, LANES), 0).astype(F32)
        for p in range(HEAD_PAIRS):
            lg_lane = _pair_lane_value(lg_ref, 1, p, (C, LANES), 1)
            qdec_ref[p] = jnp.exp((float(C) - pos) * lg_lane)
            kdec_ref[p] = jnp.exp(pos * lg_lane)
            cdec_ref[p] = jnp.exp(float(C) * _pair_lane_value(lg_ref, 1, p, (LANES, LANES), 0))

    @pl.when(pl.program_id(1) == 0)
    def _reset():
        state_ref[...] = jnp.zeros(state_ref.shape, F32)

    same_head = _same_head_mask()
    head_avg = jnp.where(same_head, 1.0 / RET_HEAD_DIM, 0.0).astype(BF16)

    def head_mean(t):
        hi, lo = _split_hi_lo(t)
        return (jnp.dot(hi, head_avg, preferred_element_type=F32)
                + jnp.dot(lo, head_avg, preferred_element_type=F32))

    for p in range(HEAD_PAIRS):
        sl = slice(p * LANES, (p + 1) * LANES)
        q = qkv_ref[0, :, sl]
        k = qkv_ref[0, :, RET_WIDTH + p * LANES:RET_WIDTH + (p + 1) * LANES]
        v = qkv_ref[0, :, 2 * RET_WIDTH + p * LANES:2 * RET_WIDTH + (p + 1) * LANES]
        state = state_ref[p]
        y = y1_ref[0, :, sl] + jnp.dot(q, state.astype(BF16), preferred_element_type=F32) * qdec_ref[p]
        kd = (k.astype(F32) * kdec_ref[p]).astype(BF16)
        kv = lax.dot_general(kd, v, (((0,), (0,)), ((), ())), preferred_element_type=F32)
        state_ref[p] = cdec_ref[p] * state + jnp.where(same_head, kv, 0.0)
        d = y - head_mean(y)
        var = head_mean(d * d)
        g = g_ref[0, :, sl]
        o = d * lax.rsqrt(var + EPS) * gnw_ref[:, sl] * (g * jax.nn.sigmoid(g))
        o_ref[0, :, sl] = o.astype(o_ref.dtype)


def _retention(lg, rqkv, rg, gn_w):
    B, S, _ = rqkv.shape
    C = RET_CHUNK
    n = S // C
    smem = pl.BlockSpec(memory_space=pltpu.SMEM)
    table = pltpu.VMEM((HEAD_PAIRS, C, LANES), F32)
    state = pltpu.VMEM((HEAD_PAIRS, LANES, LANES), F32)
    y1 = pl.pallas_call(
        _ret_fwd_kernel,
        grid=(B, n),
        in_specs=[smem, pl.BlockSpec((1, C, 3 * RET_WIDTH), lambda b, i: (b, i, 0))],
        out_specs=pl.BlockSpec((1, C, RET_WIDTH), lambda b, i: (b, i, 0)),
        out_shape=jax.ShapeDtypeStruct((B, S, RET_WIDTH), F32),
        scratch_shapes=[state, pltpu.VMEM((RET_HEADS, C, C), F32), table, table, state],
        compiler_params=_params("arbitrary", "arbitrary"),
        name="ret_fwd",
    )(lg, rqkv)
    rev = lambda b, i: (b, n - 1 - i, 0)
    return pl.pallas_call(
        _ret_bwd_kernel,
        grid=(B, n),
        in_specs=[
            smem,
            pl.BlockSpec((1, C, 3 * RET_WIDTH), rev),
            pl.BlockSpec((1, C, RET_WIDTH), rev),
            pl.BlockSpec((1, C, RET_WIDTH), rev),
            pl.BlockSpec((1, RET_WIDTH), lambda b, i: (0, 0)),
        ],
        out_specs=pl.BlockSpec((1, C, RET_WIDTH), rev),
        out_shape=jax.ShapeDtypeStruct((B, S, RET_WIDTH), BF16),
        scratch_shapes=[state, table, table, state],
        compiler_params=_params("arbitrary", "arbitrary"),
        name="ret_bwd",
    )(lg, rqkv, y1, rg, gn_w)


AUX_BLOCK = 256
AUX_ROWS = 16
GUARD_LOG2 = 60.0
UNDERFLOW_LOG2 = 150.0


def _position_features(seq):
    j = jnp.arange(seq, dtype=jnp.int32)
    ones = jnp.ones((seq,), F32)
    n = (j // AUX_BLOCK).astype(F32)
    jc = (j % AUX_BLOCK - AUX_BLOCK // 2).astype(F32)
    feat = jnp.stack([ones] * 3 + [n] * 3 + [jc] * 3, axis=1)
    return jnp.pad(feat, ((0, 0), (0, LANES - feat.shape[1]))).astype(BF16)


def _split3(x):
    hi = x.astype(BF16).astype(F32)
    r = x - hi
    mid = r.astype(BF16).astype(F32)
    return hi, mid, (r - mid).astype(BF16).astype(F32)


def _attn_kernel(slopes_ref, pieces_ref, lam_ref, sw_ref, kn_ref, qt_ref, k_ref, aux_ref, vt_ref, o_ref,
                 m_ref, acc_ref, rhs_ref, ta_ref, tb_ref, *, seq, tq, tk, lam_init):
    pair = pl.program_id(1)
    qi = pl.program_id(2)
    q0 = qi * tq
    nk = seq // tk
    per = tq // tk
    kd0 = qi * per
    qt = qt_ref[0]
    row = lax.broadcasted_iota(jnp.int32, (LANES, tq), 0)
    q_masked = []
    for hc in range(4):
        lo = hc * DIFF_QK_DIM
        q_masked.append(jnp.where((row >= lo) & (row < lo + DIFF_QK_DIM), qt, jnp.zeros_like(qt)))
    m_ref[...] = jnp.full(m_ref.shape, NEG_BIG, F32)
    acc_ref[...] = jnp.zeros(acc_ref.shape, F32)
    rel = (lax.broadcasted_iota(jnp.int32, (tk, tq), 1) - lax.broadcasted_iota(jnp.int32, (tk, tq), 0)).astype(F32)
    vrow = lax.broadcasted_iota(jnp.int32, (LANES, tk), 0)
    kn2 = jnp.max(jnp.max(kn_ref[0], axis=0), axis=0, keepdims=True)
    kn_lane = lax.broadcasted_iota(jnp.int32, (1, LANES), 1)
    ipos = (q0 + lax.broadcasted_iota(jnp.int32, (1, tq), 1)).astype(F32)
    aug_row = lax.broadcasted_iota(jnp.int32, (AUX_ROWS, tq), 0)

    for h in range(2):
        slope = slopes_ref[2 * pair + h]
        own = (vrow >= h * DIFF_V_DIM) & (vrow < (h + 1) * DIFF_V_DIM)

        def v_with_ones(k0, own=own):
            vt = vt_ref[0, :, pl.ds(k0, tk)]
            return jnp.where(own, vt, jnp.ones_like(vt))

        def online_step(kb, carry, h=h, slope=slope, v_with_ones=v_with_ones):
            k0 = pl.multiple_of(kb * tk, tk)
            kblk = k_ref[0, pl.ds(k0, tk), :]
            v_ones = v_with_ones(k0)
            bias = jnp.abs(rel + (q0 - k0).astype(F32)) * slope
            for c in range(2):
                hc = 2 * h + c
                s = jnp.dot(kblk, q_masked[hc], preferred_element_type=F32) - bias
                m_old = m_ref[hc]
                m_new = jnp.maximum(m_old, jnp.max(s, axis=0, keepdims=True))
                p = jnp.exp2(s - m_new).astype(BF16)
                acc_ref[hc] = jnp.exp2(m_old - m_new) * acc_ref[hc] + jnp.dot(v_ones, p, preferred_element_type=F32)
                m_ref[hc] = m_new
            return carry

        for d in range(per):
            online_step(kd0 + d, 0)

        excess = None
        for c in range(2):
            hc = 2 * h + c
            qf = q_masked[hc].astype(F32)
            qn2 = jnp.sum(qf * qf, axis=0, keepdims=True)
            kn2_hc = jnp.max(jnp.where(kn_lane == 4 * pair + hc, kn2, 0.0), axis=-1, keepdims=True)
            bound = jnp.sqrt(qn2 * kn2_hc) * 1.01 + 1e-3
            e = jnp.max(bound - m_ref[hc], axis=-1, keepdims=True)
            excess = e if excess is None else jnp.maximum(excess, e)
        reach = (excess + UNDERFLOW_LOG2) / slope
        q0f = q0.astype(F32)
        lo_blk = jnp.clip(jnp.floor((q0f - (tk - 1.0) - reach) / tk), 0.0, float(nk))
        hi_blk = jnp.clip(jnp.floor((reach + q0f + (tq - 1.0)) / tk) + 1.0, 0.0, float(nk))
        lo_blk = jnp.minimum(lo_blk.astype(jnp.int32)[0, 0], kd0)
        hi_blk = jnp.maximum(hi_blk.astype(jnp.int32)[0, 0], kd0 + per)
        fast = jnp.where(excess <= GUARD_LOG2, 1, 0).astype(jnp.int32)[0, 0] == 1

        @pl.when(fast)
        def _single_pass(h=h, slope=slope, v_with_ones=v_with_ones, lo_blk=lo_blk, hi_blk=hi_blk):
            pieces = [pieces_ref[(2 * pair + h) * 3 + i] for i in range(3)]
            for c in range(2):
                hc = 2 * h + c
                for side, sign in ((0, 1.0), (1, -1.0)):
                    const = (-sign) * (slope * ipos) - m_ref[hc] + sign * (0.5 * AUX_BLOCK) * slope
                    c_hi, c_mid, c_lo = _split3(const)
                    aug = jnp.where(aug_row == 0, c_hi,
                                    jnp.where(aug_row == 1, c_mid, jnp.where(aug_row == 2, c_lo, 0.0)))
                    for i in range(3):
                        aug = jnp.where(aug_row == 3 + i, sign * AUX_BLOCK * pieces[i], aug)
                        aug = jnp.where(aug_row == 6 + i, sign * pieces[i], aug)
                    rhs_ref[side * 4 + hc, 0:LANES, :] = q_masked[hc]
                    rhs_ref[side * 4 + hc, LANES:LANES + AUX_ROWS, :] = aug.astype(BF16)
                    rhs_ref[side * 4 + hc, LANES + AUX_ROWS:, :] = jnp.zeros((LANES - AUX_ROWS, tq), BF16)

            odd = (kd0 - lo_blk + hi_blk - kd0 - per) % 2
            lo_even = jnp.where((odd == 1) & (lo_blk > 0), lo_blk - 1, lo_blk)
            hi_even = jnp.where((odd == 1) & (lo_blk == 0), hi_blk + 1, hi_blk)
            n_lo = kd0 - lo_even
            total = n_lo + hi_even - kd0 - per

            def block_of(i):
                return jnp.where(i < n_lo, lo_even + i, kd0 + per + i - n_lo)

            def exponents(i, t_ref):
                k0 = pl.multiple_of(block_of(i) * tk, tk)
                side = jnp.where(i < n_lo, 0, 1)
                lhs = jnp.concatenate([k_ref[0, pl.ds(k0, tk), :], aux_ref[pl.ds(k0, tk), :]], axis=1)
                for c in range(2):
                    t_ref[c] = jnp.dot(lhs, rhs_ref[side * 4 + 2 * h + c], preferred_element_type=F32)

            def accumulate(i, t_ref):
                v_ones = v_with_ones(pl.multiple_of(block_of(i) * tk, tk))
                for c in range(2):
                    p = jnp.exp2(t_ref[c]).astype(BF16)
                    acc_ref[2 * h + c] += jnp.dot(v_ones, p, preferred_element_type=F32)

            @pl.when(total > 0)
            def _blocks():
                exponents(0, ta_ref)

                def two_blocks(j, carry):
                    i = 2 * j
                    exponents(i + 1, tb_ref)
                    accumulate(i, ta_ref)
                    exponents(jnp.minimum(i + 2, total - 1), ta_ref)
                    accumulate(i + 1, tb_ref)
                    return carry

                lax.fori_loop(0, total // 2, two_blocks, 0)

        @pl.when(jnp.logical_not(fast))
        def _online(online_step=online_step):
            lax.fori_loop(0, kd0, online_step, 0)
            lax.fori_loop(kd0 + per, nk, online_step, 0)

    lam = lam_ref[...]
    lam_full = (jnp.exp(jnp.sum(lam[0:1] * lam[1:2], axis=-1, keepdims=True))
                - jnp.exp(jnp.sum(lam[2:3] * lam[3:4], axis=-1, keepdims=True)) + lam_init)
    outs = []
    for h in range(2):
        a1 = acc_ref[2 * h]
        a2 = acc_ref[2 * h + 1]
        ones_row = DIFF_V_DIM if h == 0 else 0
        rows = slice(h * DIFF_V_DIM, (h + 1) * DIFF_V_DIM)
        o = a1[rows] / a1[ones_row:ones_row + 1] - lam_full * (a2[rows] / a2[ones_row:ones_row + 1])
        o = o * lax.rsqrt(jnp.mean(o * o, axis=0, keepdims=True) + EPS)
        outs.append(o * sw_ref[...] * (1.0 - lam_init))
    o_ref[0] = jnp.concatenate(outs, axis=0).T.astype(o_ref.dtype)


def _diff_attention(slopes, pieces, lam, subln_w, kn, dqvt, dk, aux, layer_idx, tq, tk):
    B, S, _ = dk.shape
    assert tq % tk == 0 and tk <= AUX_BLOCK and S // AUX_BLOCK <= AUX_BLOCK
    lam_init = 0.8 - 0.6 * math.exp(-0.3 * layer_idx)
    kern = functools.partial(_attn_kernel, seq=S, tq=tq, tk=tk, lam_init=lam_init)
    v_block0 = DIFF_WIDTH // LANES
    smem = pl.BlockSpec(memory_space=pltpu.SMEM)
    return pl.pallas_call(
        kern,
        grid=(B, HEAD_PAIRS, S // tq),
        in_specs=[
            smem, smem,
            pl.BlockSpec((4, DIFF_QK_DIM), lambda b, j, i: (0, 0)),
            pl.BlockSpec((DIFF_V_DIM, 1), lambda b, j, i: (0, 0)),
            pl.BlockSpec((1,) + kn.shape[1:], lambda b, j, i: (b, 0, 0, 0)),
            pl.BlockSpec((1, LANES, tq), lambda b, j, i: (b, j, i)),
            pl.BlockSpec((1, S, LANES), lambda b, j, i: (b, 0, j)),
            pl.BlockSpec((S, LANES), lambda b, j, i: (0, 0)),
            pl.BlockSpec((1, LANES, S), lambda b, j, i: (b, v_block0 + j, 0)),
        ],
        out_specs=pl.BlockSpec((1, tq, LANES), lambda b, j, i: (b, i, j)),
        out_shape=jax.ShapeDtypeStruct((B, S, DIFF_WIDTH), BF16),
        scratch_shapes=[pltpu.VMEM((4, 1, tq), F32), pltpu.VMEM((4, LANES, tq), F32),
                        pltpu.VMEM((8, 2 * LANES, tq), BF16),
                        pltpu.VMEM((2, tk, tq), F32), pltpu.VMEM((2, tk, tq), F32)],
        compiler_params=_params("parallel", "parallel", "arbitrary"),
        name="diff_attn",
    )(slopes, pieces, lam, subln_w, kn, dqvt, dk, aux, dqvt)


def _out_proj_kernel(x_ref, ret_ref, diff_ref, cv_ref, prev_ref, next_ref, cw_ref, w_ref, o_ref, *, tm):
    i = pl.program_id(1)
    last = pl.num_programs(1) - 1
    cc = CONV_CHANNELS
    cv = cv_ref[0]
    u = cv[:, 2 * cc:3 * cc] * cv[:, 0:cc]
    prev = prev_ref[0]
    nxt = next_ref[0]
    u_prev = prev[SUBLANES - 1:SUBLANES, 2 * cc:3 * cc] * prev[SUBLANES - 1:SUBLANES, 0:cc]
    u_next = nxt[0:1, 2 * cc:3 * cc] * nxt[0:1, 0:cc]
    u_prev = jnp.where(i == 0, 0.0, u_prev)
    u_next = jnp.where(i == last, 0.0, u_next)
    rows = lax.broadcasted_iota(jnp.int32, (tm, cc), 0)
    u_m1 = jnp.where(rows == 0, u_prev, pltpu.roll(u, 1, 0))
    u_p1 = jnp.where(rows == tm - 1, u_next, pltpu.roll(u, tm - 1, 0))
    y = cw_ref[0:1] * u_m1 + cw_ref[1:2] * u + cw_ref[2:3] * u_p1
    conv = (cv[:, cc:2 * cc] * y).astype(BF16)
    acc = jnp.dot(ret_ref[0], w_ref[0:RET_WIDTH], preferred_element_type=F32)
    acc += jnp.dot(diff_ref[0], w_ref[RET_WIDTH:RET_WIDTH + DIFF_WIDTH], preferred_element_type=F32)
    acc += jnp.dot(conv, w_ref[RET_WIDTH + DIFF_WIDTH:], preferred_element_type=F32)
    o_ref[0] = x_ref[0] + acc


def _out_proj(x, ret, diff, cv, conv_w, w_out, tm):
    B, S, D = x.shape
    per = tm // SUBLANES
    nblk8 = S // SUBLANES
    tile = lambda width: pl.BlockSpec((1, tm, width), lambda b, i: (b, i, 0))
    return pl.pallas_call(
        functools.partial(_out_proj_kernel, tm=tm),
        grid=(B, S // tm),
        in_specs=[
            tile(D), tile(RET_WIDTH), tile(DIFF_WIDTH), tile(3 * CONV_CHANNELS),
            pl.BlockSpec((1, SUBLANES, 3 * CONV_CHANNELS), lambda b, i: (b, jnp.maximum(i * per - 1, 0), 0)),
            pl.BlockSpec((1, SUBLANES, 3 * CONV_CHANNELS),
                         lambda b, i: (b, jnp.minimum((i + 1) * per, nblk8 - 1), 0)),
            pl.BlockSpec((CONV_WIDTH, CONV_CHANNELS), lambda b, i: (0, 0)),
            pl.BlockSpec((D, D), lambda b, i: (0, 0)),
        ],
        out_specs=tile(D),
        out_shape=jax.ShapeDtypeStruct((B, S, D), F32),
        compiler_params=_params("parallel", "parallel"),
        name="out_proj",
    )(x, ret, diff, cv, cv, cv, conv_w, w_out)


_GROUP_LANE0 = N_EXPERTS


def _router_gates(logits):
    lane = lax.broadcasted_iota(jnp.int32, logits.shape, 1)
    big = jnp.int32(LANES)
    is_group = (lane >= _GROUP_LANE0) & (lane < _GROUP_LANE0 + N_GROUPS)
    gl = jnp.where(is_group, logits, -jnp.inf)
    g_max = jnp.max(gl, axis=-1, keepdims=True)
    g_idx = jnp.min(jnp.where(gl == g_max, lane - _GROUP_LANE0, big), axis=-1, keepdims=True)
    g_w = 1.0 / jnp.sum(jnp.where(is_group, jnp.exp(gl - g_max), 0.0), axis=-1, keepdims=True)
    in_group = (lane >= g_idx * EXPERTS_PER_GROUP) & (lane < (g_idx + 1) * EXPERTS_PER_GROUP)
    el = jnp.where(in_group, logits, -jnp.inf)
    e1 = jnp.max(el, axis=-1, keepdims=True)
    i1 = jnp.min(jnp.where(el == e1, lane, big), axis=-1, keepdims=True)
    el2 = jnp.where(lane == i1, -jnp.inf, el)
    e2 = jnp.max(el2, axis=-1, keepdims=True)
    i2 = jnp.min(jnp.where(el2 == e2, lane, big), axis=-1, keepdims=True)
    r = jnp.exp(e2 - e1)
    p1 = g_w / (1.0 + r)
    p2 = g_w * r / (1.0 + r)
    return jnp.where(lane == i1, p1, jnp.where(lane == i2, p2, 0.0))


def _moe_kernel(x_ref, nw_ref, rw_ref, rb_ref, wg_ref, wu_ref, wd_ref, fw_ref, o_ref,
                xn_ref, gates_ref, acc_ref, *, final_norm):
    e = pl.program_id(1)

    @pl.when(e == 0)
    def _route():
        x = x_ref[...]
        xn = x * lax.rsqrt(jnp.mean(x * x, axis=-1, keepdims=True) + EPS) * nw_ref[...]
        hi, lo = _split_hi_lo(xn)
        r_hi, r_lo = _split_hi_lo(rw_ref[...])
        logits = (jnp.dot(hi, r_hi, preferred_element_type=F32)
                  + jnp.dot(lo, r_hi, preferred_element_type=F32)
                  + jnp.dot(hi, r_lo, preferred_element_type=F32)) + rb_ref[...]
        gates_ref[...] = _router_gates(logits)
        xn_ref[...] = hi
        acc_ref[...] = jnp.zeros(acc_ref.shape, F32)

    xn = xn_ref[...]
    gate = jnp.dot(xn, wg_ref[0], preferred_element_type=F32)
    up = jnp.dot(xn, wu_ref[0], preferred_element_type=F32)
    hmid = (gate * jax.nn.sigmoid(gate) * up).astype(BF16)
    gates = gates_ref[...]
    lane = lax.broadcasted_iota(jnp.int32, gates.shape, 1)
    g_e = jnp.sum(jnp.where(lane == e, gates, 0.0), axis=-1, keepdims=True)
    acc_ref[...] += g_e * jnp.dot(hmid, wd_ref[0], preferred_element_type=F32)

    @pl.when(e == pl.num_programs(1) - 1)
    def _finish():
        y = x_ref[...] + acc_ref[...]
        if final_norm:
            y = y * lax.rsqrt(jnp.mean(y * y, axis=-1, keepdims=True) + EPS) * fw_ref[...]
        o_ref[...] = y


def _moe(x, nw, r_w, r_b, wg, wu, wd, final_w, final_norm, tm):
    T, D = x.shape
    const = lambda shape: pl.BlockSpec(shape, lambda i, e: (0,) * len(shape))
    return pl.pallas_call(
        functools.partial(_moe_kernel, final_norm=final_norm),
        grid=(T // tm, N_EXPERTS),
        in_specs=[
            pl.BlockSpec((tm, D), lambda i, e: (i, 0)),
            const((1, D)), const((D, LANES)), const((1, LANES)),
            pl.BlockSpec((1, D, EXPERT_FF), lambda i, e: (e, 0, 0)),
            pl.BlockSpec((1, D, EXPERT_FF), lambda i, e: (e, 0, 0)),
            pl.BlockSpec((1, EXPERT_FF, D), lambda i, e: (e, 0, 0)),
            const((1, D)),
        ],
        out_specs=pl.BlockSpec((tm, D), lambda i, e: (i, 0)),
        out_shape=jax.ShapeDtypeStruct((T, D), F32),
        scratch_shapes=[pltpu.VMEM((tm, D), BF16), pltpu.VMEM((tm, LANES), F32), pltpu.VMEM((tm, D), F32)],
        compiler_params=_params("parallel", "arbitrary"),
        name="moe",
    )(x, nw, r_w, r_b, wg, wu, wd, final_w)


def _tile(n, pref):
    t = min(n, pref)
    assert n % t == 0, (n, t)
    return t


def _prep_layer(l, norm1_w, w_in, ret_decay_logit, ret_gn_w, diff_lambda, diff_subln_w, conv_w, w_out, norm2_w,
                router_group_w, router_group_b, router_expert_w, router_expert_b, expert_w_gate, expert_w_up,
                expert_w_down):
    w = w_in[l]
    w_nat = jnp.concatenate([w[:, :_OFF_DQ], w[:, _OFF_DK:_OFF_DV], w[:, _OFF_CONV:]], axis=1).astype(BF16)
    w_t = jnp.concatenate([w[:, _OFF_DQ:_OFF_DK], w[:, _OFF_DV:_OFF_CONV]], axis=1).T.astype(BF16)
    router = jnp.zeros((D_MODEL, LANES), F32)
    router = router.at[:, :N_EXPERTS].set(router_expert_w[l]).at[:, N_EXPERTS:N_EXPERTS + N_GROUPS].set(
        router_group_w[l])
    r_b = jnp.zeros((1, LANES), F32).at[0, :N_EXPERTS].set(router_expert_b[l]).at[
        0, N_EXPERTS:N_EXPERTS + N_GROUPS].set(router_group_b[l])
    return dict(
        norm1=norm1_w[l][None, :], w_nat=w_nat, w_t=w_t,
        lg=jax.nn.log_sigmoid(ret_decay_logit[l].astype(F32)),
        gn_w=ret_gn_w[l][None, :].astype(F32),
        lam=diff_lambda[l].astype(F32), subln=diff_subln_w[l][:, None].astype(F32),
        conv_w=conv_w[l].astype(F32), w_out=w_out[l].astype(BF16),
        norm2=norm2_w[l][None, :], r_w=router, r_b=r_b,
        wg=expert_w_gate[l].astype(BF16), wu=expert_w_up[l].astype(BF16), wd=expert_w_down[l].astype(BF16),
    )


def _trunk(x, layers, final_w, slopes, pieces):
    B, S, D = x.shape
    tm = _tile(S, 512)
    tq = _tile(S, 512)
    tk = _tile(S, 256)
    t_moe = _tile(B * S, 1024)
    aux = _position_features(S)
    for l, lw in enumerate(layers):
        rqkv, rg, dk, dqvt, cv, kn = _in_proj(x, lw["norm1"], lw["w_nat"], lw["w_t"], tm)
        ret = _retention(lw["lg"], rqkv, rg, lw["gn_w"])
        diff = _diff_attention(slopes, pieces, lw["lam"], lw["subln"], kn, dqvt, dk, aux, l, tq, tk)
        x = _out_proj(x, ret, diff, cv, lw["conv_w"], lw["w_out"], tm)
        last = l == len(layers) - 1
        x = _moe(x.reshape(B * S, D), lw["norm2"], lw["r_w"], lw["r_b"], lw["wg"], lw["wu"],
                 lw["wd"], final_w, last, t_moe).reshape(B, S, D)
    return x


def kernel(x_prompt, x_sample, norm1_w, w_in, ret_decay_logit, ret_gn_w, diff_lambda, diff_subln_w, conv_w, w_out,
           norm2_w, router_group_w, router_group_b, router_expert_w, router_expert_b, expert_w_gate, expert_w_up,
           expert_w_down, final_norm_w):
    depth = w_in.shape[0]
    layers = [
        _prep_layer(l, norm1_w, w_in, ret_decay_logit, ret_gn_w, diff_lambda, diff_subln_w, conv_w, w_out, norm2_w,
                    router_group_w, router_group_b, router_expert_w, router_expert_b, expert_w_gate, expert_w_up,
                    expert_w_down)
        for l in range(depth)
    ]
    final_w = final_norm_w[None, :].astype(F32)
    slopes = (np.float32(2.0) ** (np.float32(-8.0) * np.arange(1, DIFF_HEADS + 1, dtype=np.float32)
                                  / np.float32(DIFF_HEADS))) * np.float32(LOG2E)
    s_hi = slopes.astype(BF16).astype(np.float32)
    s_mid = (slopes - s_hi).astype(BF16).astype(np.float32)
    s_lo = (slopes - s_hi - s_mid).astype(BF16).astype(np.float32)
    pieces = jnp.asarray(np.stack([s_hi, s_mid, s_lo], axis=1).reshape(-1))
    slopes = jnp.asarray(slopes)
    return (_trunk(x_prompt, layers, final_w, slopes, pieces), _trunk(x_sample, layers, final_w, slopes, pieces))
```

```python
import functools
import math

import jax
import jax.numpy as jnp
import numpy as np
from jax import lax
from jax.experimental import pallas as pl
from jax.experimental.pallas import tpu as pltpu

F32 = jnp.float32
BF16 = jnp.bfloat16

D_MODEL = 1024
RET_HEADS = 6
RET_HEAD_DIM = 64
RET_WIDTH = RET_HEADS * RET_HEAD_DIM
DIFF_HEADS = 6
DIFF_QK_DIM = 32
DIFF_V_DIM = 64
DIFF_WIDTH = DIFF_HEADS * DIFF_V_DIM
CONV_CHANNELS = 256
CONV_WIDTH = 3
N_GROUPS = 4
EXPERTS_PER_GROUP = 4
N_EXPERTS = 16
EXPERT_FF = 512
EPS = 1e-6

LANES = 128
HEAD_PAIRS = RET_HEADS // 2
RET_CHUNK = 128
SUBLANES = 8
VMEM_LIMIT = 56 * 1024 * 1024
LOG2E = math.log2(math.e)
NEG_BIG = -1e30

_OFF_RG = 3 * RET_WIDTH
_OFF_DQ = 4 * RET_WIDTH
_OFF_DK = _OFF_DQ + DIFF_WIDTH
_OFF_DV = _OFF_DK + DIFF_WIDTH
_OFF_CONV = _OFF_DV + DIFF_WIDTH


def _params(*sem):
    return pltpu.CompilerParams(dimension_semantics=sem, vmem_limit_bytes=VMEM_LIMIT)


def _split_hi_lo(x):
    hi = x.astype(BF16)
    lo = (x - hi.astype(F32)).astype(BF16)
    return hi, lo


def _in_proj_kernel(x_ref, nw_ref, w_ref, wt_ref, rqkv_ref, rg_ref, dk_ref, dqvt_ref, cv_ref, kn_ref):
    x = x_ref[0]
    ms = jnp.mean(x * x, axis=-1, keepdims=True)
    h = (x * lax.rsqrt(ms + EPS) * nw_ref[...]).astype(BF16)
    a = jnp.dot(h, w_ref[:, 0:_OFF_DQ], preferred_element_type=F32)
    rqkv_ref[0, :, 0:RET_WIDTH] = a[:, 0:RET_WIDTH].astype(BF16)
    rqkv_ref[0, :, RET_WIDTH:2 * RET_WIDTH] = (a[:, RET_WIDTH:2 * RET_WIDTH] * RET_HEAD_DIM ** -0.5).astype(BF16)
    rqkv_ref[0, :, 2 * RET_WIDTH:3 * RET_WIDTH] = a[:, 2 * RET_WIDTH:3 * RET_WIDTH].astype(BF16)
    rg_ref[0] = a[:, _OFF_RG:_OFF_DQ]
    dk = jnp.dot(h, w_ref[:, _OFF_DQ:_OFF_DQ + DIFF_WIDTH], preferred_element_type=F32).astype(BF16)
    dk_ref[0] = dk
    dkf = dk.astype(F32)
    sq_hi, sq_lo = _split_hi_lo(dkf * dkf)
    grp = (lax.shift_right_logical(lax.broadcasted_iota(jnp.int32, (DIFF_WIDTH, LANES), 0), 5)
           == lax.broadcasted_iota(jnp.int32, (DIFF_WIDTH, LANES), 1))
    grp = jnp.where(grp, 1.0, 0.0).astype(BF16)
    kn2 = jnp.dot(sq_hi, grp, preferred_element_type=F32) + jnp.dot(sq_lo, grp, preferred_element_type=F32)
    kn_ref[0, 0] = jnp.broadcast_to(jnp.max(kn2, axis=0, keepdims=True), (SUBLANES, LANES))
    cv_ref[0] =jnp.dot(h, w_ref[:, _OFF_DQ + DIFF_WIDTH:], preferred_element_type=F32)
    t = lax.dot_general(wt_ref[...], h, (((1,), (1,)), ((), ())), preferred_element_type=F32)
    dqvt_ref[0, 0:DIFF_WIDTH, :] = (t[0:DIFF_WIDTH] * (DIFF_QK_DIM ** -0.5 * LOG2E)).astype(BF16)
    dqvt_ref[0, DIFF_WIDTH:, :] = t[DIFF_WIDTH:].astype(BF16)


def _in_proj(x, nw, w_nat, w_t, tm):
    B, S, D = x.shape
    n_nat = w_nat.shape[1]
    return pl.pallas_call(
        _in_proj_kernel,
        grid=(B, S // tm),
        in_specs=[
            pl.BlockSpec((1, tm, D), lambda b, i: (b, i, 0)),
            pl.BlockSpec((1, D), lambda b, i: (0, 0)),
            pl.BlockSpec((D, n_nat), lambda b, i: (0, 0)),
            pl.BlockSpec((2 * DIFF_WIDTH, D), lambda b, i: (0, 0)),
        ],
        out_specs=[
            pl.BlockSpec((1, tm, 3 * RET_WIDTH), lambda b, i: (b, i, 0)),
            pl.BlockSpec((1, tm, RET_WIDTH), lambda b, i: (b, i, 0)),
            pl.BlockSpec((1, tm, DIFF_WIDTH), lambda b, i: (b, i, 0)),
            pl.BlockSpec((1, 2 * DIFF_WIDTH, tm), lambda b, i: (b, 0, i)),
            pl.BlockSpec((1, tm, 3 * CONV_CHANNELS), lambda b, i: (b, i, 0)),
            pl.BlockSpec((1, 1, SUBLANES, LANES), lambda b, i: (b, i, 0, 0)),
        ],
        out_shape=[
            jax.ShapeDtypeStruct((B, S, 3 * RET_WIDTH), BF16),
            jax.ShapeDtypeStruct((B, S, RET_WIDTH), F32),
            jax.ShapeDtypeStruct((B, S, DIFF_WIDTH), BF16),
            jax.ShapeDtypeStruct((B, 2 * DIFF_WIDTH, S), BF16),
            jax.ShapeDtypeStruct((B, S, 3 * CONV_CHANNELS), F32),
            jax.ShapeDtypeStruct((B, S // tm, SUBLANES, LANES), F32),
        ],
        compiler_params=_params("parallel", "parallel"),
        name="in_proj",
    )(x, nw, w_nat, w_t)


def _pair_lane_value(lg_ref, direction, pair, shape, axis):
    idx = lax.broadcasted_iota(jnp.int32, shape, axis)
    return jnp.where(idx < RET_HEAD_DIM, lg_ref[direction, 2 * pair], lg_ref[direction, 2 * pair + 1])


def _same_head_mask():
    r = lax.broadcasted_iota(jnp.int32, (LANES, LANES), 0)
    c = lax.broadcasted_iota(jnp.int32, (LANES, LANES), 1)
    return (r < RET_HEAD_DIM) == (c < RET_HEAD_DIM)


def _ret_fwd_kernel(lg_ref, qkv_ref, y_ref, state_ref, dmat_ref, qdec_ref, kdec_ref, cdec_ref):
    C = RET_CHUNK

    @pl.when((pl.program_id(0) == 0) & (pl.program_id(1) == 0))
    def _build_tables():
        diff = (lax.broadcasted_iota(jnp.int32, (C, C), 0) - lax.broadcasted_iota(jnp.int32, (C, C), 1)).astype(F32)
        for h in range(RET_HEADS):
            lower = jnp.exp(jnp.maximum(diff, 0.0) * lg_ref[0, h])
            upper = jnp.exp(jnp.maximum(-diff, 0.0) * lg_ref[1, h])
            dmat_ref[h] = jnp.where(diff >= 0, lower, upper)
        pos = lax.broadcasted_iota(jnp.int32, (C, LANES), 0).astype(F32)
        for p in range(HEAD_PAIRS):
            lg_lane = _pair_lane_value(lg_ref, 0, p, (C, LANES), 1)
            qdec_ref[p] = jnp.exp((pos + 1.0) * lg_lane)
            kdec_ref[p] = jnp.exp((C - 1.0 - pos) * lg_lane)
            cdec_ref[p] = jnp.exp(float(C) * _pair_lane_value(lg_ref, 0, p, (LANES, LANES), 0))

    @pl.when(pl.program_id(1) == 0)
    def _reset():
        state_ref[...] = jnp.zeros(state_ref.shape, F32)

    lane = lax.broadcasted_iota(jnp.int32, (C, LANES), 1)
    same_head = _same_head_mask()
    for p in range(HEAD_PAIRS):
        q = qkv_ref[0, :, p * LANES:(p + 1) * LANES]
        k = qkv_ref[0, :, RET_WIDTH + p * LANES:RET_WIDTH + (p + 1) * LANES]
        v = qkv_ref[0, :, 2 * RET_WIDTH + p * LANES:2 * RET_WIDTH + (p + 1) * LANES]
        inner = []
        for h in range(2):
            head_lanes = (lane >= h * RET_HEAD_DIM) & (lane < (h + 1) * RET_HEAD_DIM)
            qm = jnp.where(head_lanes, q, jnp.zeros_like(q))
            sc = lax.dot_general(qm, k, (((1,), (1,)), ((), ())), preferred_element_type=F32)
            a = (sc * dmat_ref[2 * p + h]).astype(BF16)
            inner.append(jnp.dot(a, v, preferred_element_type=F32))
        state = state_ref[p]
        cross = jnp.dot(q, state.astype(BF16), preferred_element_type=F32) * qdec_ref[p]
        y_ref[0, :, p * LANES:(p + 1) * LANES] = jnp.where(lane < RET_HEAD_DIM, inner[0], inner[1]) + cross
        kd = (k.astype(F32) * kdec_ref[p]).astype(BF16)
        kv = lax.dot_general(kd, v, (((0,), (0,)), ((), ())), preferred_element_type=F32)
        state_ref[p] = cdec_ref[p] * state + jnp.where(same_head, kv, 0.0)


def _ret_bwd_kernel(lg_ref, qkv_ref, y1_ref, g_ref, gnw_ref, o_ref, state_ref, qdec_ref, kdec_ref, cdec_ref):
    C = RET_CHUNK

    @pl.when((pl.program_id(0) == 0) & (pl.program_id(1) == 0))
    def _build_tables():
        pos = lax.broadcasted_iota(jnp.int32, (C, LANES), 0).astype(F32)
        for p in range(HEAD_PAIRS):
            lg_lane = _pair_lane_value(lg_ref, 1, p, (C, LANES), 1)
            qdec_ref[p] = jnp.exp((float(C) - pos) * lg_lane)
            kdec_ref[p] = jnp.exp(pos * lg_lane)
            cdec_ref[p] = jnp.exp(float(C) * _pair_lane_value(lg_ref, 1, p, (LANES, LANES), 0))

    @pl.when(pl.program_id(1) == 0)
    def _reset():
        state_ref[...] = jnp.zeros(state_ref.shape, F32)

    same_head = _same_head_mask()
    head_avg = jnp.where(same_head, 1.0 / RET_HEAD_DIM, 0.0).astype(BF16)

    def head_mean(t):
        hi, lo = _split_hi_lo(t)
        return (jnp.dot(hi, head_avg, preferred_element_type=F32)
                + jnp.dot(lo, head_avg, preferred_element_type=F32))

    for p in range(HEAD_PAIRS):
        sl = slice(p * LANES, (p + 1) * LANES)
        q = qkv_ref[0, :, sl]
        k = qkv_ref[0, :, RET_WIDTH + p * LANES:RET_WIDTH + (p + 1) * LANES]
        v = qkv_ref[0, :, 2 * RET_WIDTH + p * LANES:2 * RET_WIDTH + (p + 1) * LANES]
        state = state_ref[p]
        y = y1_ref[0, :, sl] + jnp.dot(q, state.astype(BF16), preferred_element_type=F32) * qdec_ref[p]
        kd = (k.astype(F32) * kdec_ref[p]).astype(BF16)
        kv = lax.dot_general(kd, v, (((0,), (0,)), ((), ())), preferred_element_type=F32)
        state_ref[p] = cdec_ref[p] * state + jnp.where(same_head, kv, 0.0)
        d = y - head_mean(y)
        var = head_mean(d * d)
        g = g_ref[0, :, sl]
        o = d * lax.rsqrt(var + EPS) * gnw_ref[:, sl] * (g * jax.nn.sigmoid(g))
        o_ref[0, :, sl] = o.astype(o_ref.dtype)


def _retention(lg, rqkv, rg, gn_w):
    B, S, _ = rqkv.shape
    C = RET_CHUNK
    n = S // C
    smem = pl.BlockSpec(memory_space=pltpu.SMEM)
    table = pltpu.VMEM((HEAD_PAIRS, C, LANES), F32)
    state = pltpu.VMEM((HEAD_PAIRS, LANES, LANES), F32)
    y1 = pl.pallas_call(
        _ret_fwd_kernel,
        grid=(B, n),
        in_specs=[smem, pl.BlockSpec((1, C, 3 * RET_WIDTH), lambda b, i: (b, i, 0))],
        out_specs=pl.BlockSpec((1, C, RET_WIDTH), lambda b, i: (b, i, 0)),
        out_shape=jax.ShapeDtypeStruct((B, S, RET_WIDTH), F32),
        scratch_shapes=[state, pltpu.VMEM((RET_HEADS, C, C), F32), table, table, state],
        compiler_params=_params("arbitrary", "arbitrary"),
        name="ret_fwd",
    )(lg, rqkv)
    rev = lambda b, i: (b, n - 1 - i, 0)
    return pl.pallas_call(
        _ret_bwd_kernel,
        grid=(B, n),
        in_specs=[
            smem,
            pl.BlockSpec((1, C, 3 * RET_WIDTH), rev),
            pl.BlockSpec((1, C, RET_WIDTH), rev),
            pl.BlockSpec((1, C, RET_WIDTH), rev),
            pl.BlockSpec((1, RET_WIDTH), lambda b, i: (0, 0)),
        ],
        out_specs=pl.BlockSpec((1, C, RET_WIDTH), rev),
        out_shape=jax.ShapeDtypeStruct((B, S, RET_WIDTH), BF16),
        scratch_shapes=[state, table, table, state],
        compiler_params=_params("arbitrary", "arbitrary"),
        name="ret_bwd",
    )(lg, rqkv, y1, rg, gn_w)


AUX_BLOCK = 256
AUX_ROWS = 16
V_ROWS = DIFF_V_DIM + SUBLANES
GUARD_LOG2 = 60.0
UNDERFLOW_LOG2 = 150.0


def _position_features(seq):
    j = jnp.arange(seq, dtype=jnp.int32)
    ones = jnp.ones((seq,), F32)
    n = (j // AUX_BLOCK).astype(F32)
    jc = (j % AUX_BLOCK - AUX_BLOCK // 2).astype(F32)
    feat = jnp.stack([ones] * 3 + [n] * 3 + [jc] * 3, axis=1)
    return jnp.pad(feat, ((0, 0), (0, LANES - feat.shape[1]))).astype(BF16)


def _split3(x):
    hi = x.astype(BF16).astype(F32)
    r = x - hi
    mid = r.astype(BF16).astype(F32)
    return hi, mid, (r - mid).astype(BF16).astype(F32)


def _attn_kernel(slopes_ref, pieces_ref, lam_ref, sw_ref, kn_ref, qt_ref, k_ref, aux_ref, vt_ref, o_ref,
                 m_ref, acc_ref, rhs_ref, pa_ref, pb_ref, dist_ref, sd_ref, *, seq, tq, tk, lam_init):
    pair = pl.program_id(1)
    qi = pl.program_id(2)
    q0 = qi * tq
    nk = seq // tk
    per = tq // tk
    kd0 = qi * per
    qt = qt_ref[0]
    row = lax.broadcasted_iota(jnp.int32, (LANES, tq), 0)
    q_masked = []
    for hc in range(4):
        lo = hc * DIFF_QK_DIM
        q_masked.append(jnp.where((row >= lo) & (row < lo + DIFF_QK_DIM), qt, jnp.zeros_like(qt)))
    rel = (lax.broadcasted_iota(jnp.int32, (tk, tq), 1) - lax.broadcasted_iota(jnp.int32, (tk, tq), 0)).astype(F32)

    @pl.when(qi == 0)
    def _distance_table():
        d = lax.broadcasted_iota(jnp.int32, (tq, tq), 1) - lax.broadcasted_iota(jnp.int32, (tq, tq), 0)
        dist_ref[...] = jnp.abs(d).astype(F32)

    kn2 = jnp.max(jnp.max(kn_ref[0], axis=0), axis=0, keepdims=True)
    kn_lane = lax.broadcasted_iota(jnp.int32, (1, LANES), 1)
    ipos = (q0 + lax.broadcasted_iota(jnp.int32, (1, tq), 1)).astype(F32)
    aug_row = lax.broadcasted_iota(jnp.int32, (AUX_ROWS, tq), 0)

    def v_rows_with_ones(h, k0, n):
        vt = vt_ref[0, h * DIFF_V_DIM:(h + 1) * DIFF_V_DIM, pl.ds(k0, n)]
        return jnp.concatenate([vt, jnp.ones((SUBLANES, n), BF16)], axis=0)

    kd = pl.multiple_of(q0, tq)
    kdiag = k_ref[0, pl.ds(kd, tq), :]
    for hc in range(4):
        s = jnp.dot(kdiag, q_masked[hc], preferred_element_type=F32)
        sd_ref[hc] = s - dist_ref[...] * slopes_ref[2 * pair + hc // 2]
    for hc in range(4):
        s = sd_ref[hc]
        m_diag = jnp.max(s, axis=0, keepdims=True)
        p = jnp.exp2(s - m_diag).astype(BF16)
        acc_ref[hc] = jnp.dot(v_rows_with_ones(hc // 2, kd, tq), p, preferred_element_type=F32)
        m_ref[hc] = m_diag

    for h in range(2):
        slope = slopes_ref[2 * pair + h]

        def v_with_ones(k0, h=h):
            return v_rows_with_ones(h, k0, tk)

        def online_step(kb, carry, h=h, slope=slope, v_with_ones=v_with_ones):
            k0 = pl.multiple_of(kb * tk, tk)
            kblk = k_ref[0, pl.ds(k0, tk), :]
            v_ones = v_with_ones(k0)
            bias = jnp.abs(rel + (q0 - k0).astype(F32)) * slope
            for c in range(2):
                hc = 2 * h + c
                s = jnp.dot(kblk, q_masked[hc], preferred_element_type=F32) - bias
                m_old = m_ref[hc]
                m_new = jnp.maximum(m_old, jnp.max(s, axis=0, keepdims=True))
                p = jnp.exp2(s - m_new).astype(BF16)
                acc_ref[hc] = jnp.exp2(m_old - m_new) * acc_ref[hc] + jnp.dot(v_ones, p, preferred_element_type=F32)
                m_ref[hc] = m_new
            return carry

        excess = None
        for c in range(2):
            hc = 2 * h + c
            qf = q_masked[hc].astype(F32)
            qn2 = jnp.sum(qf * qf, axis=0, keepdims=True)
            kn2_hc = jnp.max(jnp.where(kn_lane == 4 * pair + hc, kn2, 0.0), axis=-1, keepdims=True)
            bound = jnp.sqrt(qn2 * kn2_hc) * 1.01 + 1e-3
            e = jnp.max(bound - m_ref[hc], axis=-1, keepdims=True)
            excess = e if excess is None else jnp.maximum(excess, e)
        reach = (excess + UNDERFLOW_LOG2) / slope
        q0f = q0.astype(F32)
        lo_blk = jnp.clip(jnp.floor((q0f - (tk - 1.0) - reach) / tk), 0.0, float(nk))
        hi_blk = jnp.clip(jnp.floor((reach + q0f + (tq - 1.0)) / tk) + 1.0, 0.0, float(nk))
        lo_blk = jnp.minimum(lo_blk.astype(jnp.int32)[0, 0], kd0)
        hi_blk = jnp.maximum(hi_blk.astype(jnp.int32)[0, 0], kd0 + per)
        fast = jnp.where(excess <= GUARD_LOG2, 1, 0).astype(jnp.int32)[0, 0] == 1

        @pl.when(fast)
        def _single_pass(h=h, slope=slope, v_with_ones=v_with_ones, lo_blk=lo_blk, hi_blk=hi_blk):
            pieces = [pieces_ref[(2 * pair + h) * 3 + i] for i in range(3)]
            for c in range(2):
                hc = 2 * h + c
                for side, sign in ((0, 1.0), (1, -1.0)):
                    const = (-sign) * (slope * ipos) - m_ref[hc] + sign * (0.5 * AUX_BLOCK) * slope
                    c_hi, c_mid, c_lo = _split3(const)
                    aug = jnp.where(aug_row == 0, c_hi,
                                    jnp.where(aug_row == 1, c_mid, jnp.where(aug_row == 2, c_lo, 0.0)))
                    for i in range(3):
                        aug = jnp.where(aug_row == 3 + i, sign * AUX_BLOCK * pieces[i], aug)
                        aug = jnp.where(aug_row == 6 + i, sign * pieces[i], aug)
                    rhs_ref[side * 4 + hc, 0:LANES, :] = q_masked[hc]
                    rhs_ref[side * 4 + hc, LANES:LANES + AUX_ROWS, :] = aug.astype(BF16)
                    rhs_ref[side * 4 + hc, LANES + AUX_ROWS:, :] = jnp.zeros((LANES - AUX_ROWS, tq), BF16)

            odd = (kd0 - lo_blk + hi_blk - kd0 - per) % 2
            lo_even = jnp.where((odd == 1) & (lo_blk > 0), lo_blk - 1, lo_blk)
            hi_even = jnp.where((odd == 1) & (lo_blk == 0), hi_blk + 1, hi_blk)
            n_lo = kd0 - lo_even
            total = n_lo + hi_even - kd0 - per

            def block_of(i):
                return jnp.where(i < n_lo, lo_even + i, kd0 + per + i - n_lo)

            def stage_a(pair_idx, p_ref):
                for half in range(2):
                    i = 2 * pair_idx + half
                    k0 = pl.multiple_of(block_of(i) * tk, tk)
                    side = jnp.where(i < n_lo, 0, 1)
                    lhs = jnp.concatenate([k_ref[0, pl.ds(k0, tk), :], aux_ref[pl.ds(k0, tk), :]], axis=1)
                    for c in range(2):
                        t = jnp.dot(lhs, rhs_ref[side * 4 + 2 * h + c], preferred_element_type=F32)
                        p_ref[2 * half + c] = jnp.exp2(t).astype(BF16)

            def stage_b(pair_idx, p_ref):
                va = v_with_ones(pl.multiple_of(block_of(2 * pair_idx) * tk, tk))
                vb = v_with_ones(pl.multiple_of(block_of(2 * pair_idx + 1) * tk, tk))
                for c in range(2):
                    acc_ref[2 * h + c] += (jnp.dot(va, p_ref[c], preferred_element_type=F32)
                                           + jnp.dot(vb, p_ref[2 + c], preferred_element_type=F32))

            n_pairs = total // 2

            @pl.when(n_pairs > 0)
            def _blocks():
                stage_a(0, pa_ref)

                def two_pairs(j, carry):
                    stage_a(2 * j + 1, pb_ref)
                    stage_b(2 * j, pa_ref)
                    stage_a(2 * j + 2, pa_ref)
                    stage_b(2 * j + 1, pb_ref)
                    return carry

                lax.fori_loop(0, (n_pairs - 1) // 2, two_pairs, 0)

                @pl.when(n_pairs % 2 == 1)
                def _last_one():
                    stage_b(n_pairs - 1, pa_ref)

                @pl.when(n_pairs % 2 == 0)
                def _last_two():
                    stage_a(n_pairs - 1, pb_ref)
                    stage_b(n_pairs - 2, pa_ref)
                    stage_b(n_pairs - 1, pb_ref)

        @pl.when(jnp.logical_not(fast))
        def _online(online_step=online_step):
            lax.fori_loop(0, kd0, online_step, 0)
            lax.fori_loop(kd0 + per, nk, online_step, 0)

    lam = lam_ref[...]
    lam_full = (jnp.exp(jnp.sum(lam[0:1] * lam[1:2], axis=-1, keepdims=True))
                - jnp.exp(jnp.sum(lam[2:3] * lam[3:4], axis=-1, keepdims=True)) + lam_init)
    outs = []
    for h in range(2):
        a1 = acc_ref[2 * h]
        a2 = acc_ref[2 * h + 1]
        rows = slice(0, DIFF_V_DIM)
        ones_row = slice(DIFF_V_DIM, DIFF_V_DIM + 1)
        o = a1[rows] / a1[ones_row] - lam_full * (a2[rows] / a2[ones_row])
        o = o * lax.rsqrt(jnp.mean(o * o, axis=0, keepdims=True) + EPS)
        outs.append(o * sw_ref[...] * (1.0 - lam_init))
    o_ref[0] = jnp.concatenate(outs, axis=0).T.astype(o_ref.dtype)


def _diff_attention(slopes, pieces, lam, subln_w, kn, dqvt, dk, aux, layer_idx, tq, tk):
    B, S, _ = dk.shape
    assert tq % tk == 0 and tk <= AUX_BLOCK and S // AUX_BLOCK <= AUX_BLOCK
    lam_init = 0.8 - 0.6 * math.exp(-0.3 * layer_idx)
    kern = functools.partial(_attn_kernel, seq=S, tq=tq, tk=tk, lam_init=lam_init)
    v_block0 = DIFF_WIDTH // LANES
    smem = pl.BlockSpec(memory_space=pltpu.SMEM)
    return pl.pallas_call(
        kern,
        grid=(B, HEAD_PAIRS, S // tq),
        in_specs=[
            smem, smem,
            pl.BlockSpec((4, DIFF_QK_DIM), lambda b, j, i: (0, 0)),
            pl.BlockSpec((DIFF_V_DIM, 1), lambda b, j, i: (0, 0)),
            pl.BlockSpec((1,) + kn.shape[1:], lambda b, j, i: (b, 0, 0, 0)),
            pl.BlockSpec((1, LANES, tq), lambda b, j, i: (b, j, i)),
            pl.BlockSpec((1, S, LANES), lambda b, j, i: (b, 0, j)),
            pl.BlockSpec((S, LANES), lambda b, j, i: (0, 0)),
            pl.BlockSpec((1, LANES, S), lambda b, j, i: (b, v_block0 + j, 0)),
        ],
        out_specs=pl.BlockSpec((1, tq, LANES), lambda b, j, i: (b, i, j)),
        out_shape=jax.ShapeDtypeStruct((B, S, DIFF_WIDTH), BF16),
        scratch_shapes=[pltpu.VMEM((4, 1, tq), F32), pltpu.VMEM((4, V_ROWS, tq), F32),
                        pltpu.VMEM((8, 2 * LANES, tq), BF16),
                        pltpu.VMEM((4, tk, tq), BF16), pltpu.VMEM((4, tk, tq), BF16),
                        pltpu.VMEM((tq, tq), F32), pltpu.VMEM((4, tq, tq), F32)],
        compiler_params=_params("parallel", "parallel", "arbitrary"),
        name="diff_attn",
    )(slopes, pieces, lam, subln_w, kn, dqvt, dk, aux, dqvt)


def _out_proj_kernel(x_ref, ret_ref, diff_ref, cv_ref, prev_ref, next_ref, cw_ref, w_ref, o_ref, *, tm):
    i = pl.program_id(1)
    last = pl.num_programs(1) - 1
    cc = CONV_CHANNELS
    cv = cv_ref[0]
    u = cv[:, 2 * cc:3 * cc] * cv[:, 0:cc]
    prev = prev_ref[0]
    nxt = next_ref[0]
    u_prev = prev[SUBLANES - 1:SUBLANES, 2 * cc:3 * cc] * prev[SUBLANES - 1:SUBLANES, 0:cc]
    u_next = nxt[0:1, 2 * cc:3 * cc] * nxt[0:1, 0:cc]
    u_prev = jnp.where(i == 0, 0.0, u_prev)
    u_next = jnp.where(i == last, 0.0, u_next)
    rows = lax.broadcasted_iota(jnp.int32, (tm, cc), 0)
    u_m1 = jnp.where(rows == 0, u_prev, pltpu.roll(u, 1, 0))
    u_p1 = jnp.where(rows == tm - 1, u_next, pltpu.roll(u, tm - 1, 0))
    y = cw_ref[0:1] * u_m1 + cw_ref[1:2] * u + cw_ref[2:3] * u_p1
    conv = (cv[:, cc:2 * cc] * y).astype(BF16)
    acc = jnp.dot(ret_ref[0], w_ref[0:RET_WIDTH], preferred_element_type=F32)
    acc += jnp.dot(diff_ref[0], w_ref[RET_WIDTH:RET_WIDTH + DIFF_WIDTH], preferred_element_type=F32)
    acc += jnp.dot(conv, w_ref[RET_WIDTH + DIFF_WIDTH:], preferred_element_type=F32)
    o_ref[0] = x_ref[0] + acc


def _out_proj(x, ret, diff, cv, conv_w, w_out, tm):
    B, S, D = x.shape
    per = tm // SUBLANES
    nblk8 = S // SUBLANES
    tile = lambda width: pl.BlockSpec((1, tm, width), lambda b, i: (b, i, 0))
    return pl.pallas_call(
        functools.partial(_out_proj_kernel, tm=tm),
        grid=(B, S // tm),
        in_specs=[
            tile(D), tile(RET_WIDTH), tile(DIFF_WIDTH), tile(3 * CONV_CHANNELS),
            pl.BlockSpec((1, SUBLANES, 3 * CONV_CHANNELS), lambda b, i: (b, jnp.maximum(i * per - 1, 0), 0)),
            pl.BlockSpec((1, SUBLANES, 3 * CONV_CHANNELS),
                         lambda b, i: (b, jnp.minimum((i + 1) * per, nblk8 - 1), 0)),
            pl.BlockSpec((CONV_WIDTH, CONV_CHANNELS), lambda b, i: (0, 0)),
            pl.BlockSpec((D, D), lambda b, i: (0, 0)),
        ],
        out_specs=tile(D),
        out_shape=jax.ShapeDtypeStruct((B, S, D), F32),
        compiler_params=_params("parallel", "parallel"),
        name="out_proj",
    )(x, ret, diff, cv, cv, cv, conv_w, w_out)


_GROUP_LANE0 = N_EXPERTS


def _router_gates(logits):
    lane = lax.broadcasted_iota(jnp.int32, logits.shape, 1)
    big = jnp.int32(LANES)
    is_group = (lane >= _GROUP_LANE0) & (lane < _GROUP_LANE0 + N_GROUPS)
    gl = jnp.where(is_group, logits, -jnp.inf)
    g_max = jnp.max(gl, axis=-1, keepdims=True)
    g_idx = jnp.min(jnp.where(gl == g_max, lane - _GROUP_LANE0, big), axis=-1, keepdims=True)
    g_w = 1.0 / jnp.sum(jnp.where(is_group, jnp.exp(gl - g_max), 0.0), axis=-1, keepdims=True)
    in_group = (lane >= g_idx * EXPERTS_PER_GROUP) & (lane < (g_idx + 1) * EXPERTS_PER_GROUP)
    el = jnp.where(in_group, logits, -jnp.inf)
    e1 = jnp.max(el, axis=-1, keepdims=True)
    i1 = jnp.min(jnp.where(el == e1, lane, big), axis=-1, keepdims=True)
    el2 = jnp.where(lane == i1, -jnp.inf, el)
    e2 = jnp.max(el2, axis=-1, keepdims=True)
    i2 = jnp.min(jnp.where(el2 == e2, lane, big), axis=-1, keepdims=True)
    r = jnp.exp(e2 - e1)
    p1 = g_w / (1.0 + r)
    p2 = g_w * r / (1.0 + r)
    return jnp.where(lane == i1, p1, jnp.where(lane == i2, p2, 0.0))


def _moe_kernel(x_ref, nw_ref, rw_ref, rb_ref, wg_ref, wu_ref, wd_ref, fw_ref, o_ref,
                xn_ref, gates_ref, acc_ref, *, final_norm):
    e = pl.program_id(1)

    @pl.when(e == 0)
    def _route():
        x = x_ref[...]
        xn = x * lax.rsqrt(jnp.mean(x * x, axis=-1, keepdims=True) + EPS) * nw_ref[...]
        hi, lo = _split_hi_lo(xn)
        r_hi, r_lo = _split_hi_lo(rw_ref[...])
        logits = (jnp.dot(hi, r_hi, preferred_element_type=F32)
                  + jnp.dot(lo, r_hi, preferred_element_type=F32)
                  + jnp.dot(hi, r_lo, preferred_element_type=F32)) + rb_ref[...]
        gates_ref[...] = _router_gates(logits)
        xn_ref[...] = hi
        acc_ref[...] = jnp.zeros(acc_ref.shape, F32)

    xn = xn_ref[...]
    gate = jnp.dot(xn, wg_ref[0], preferred_element_type=F32)
    up = jnp.dot(xn, wu_ref[0], preferred_element_type=F32)
    hmid = (gate * jax.nn.sigmoid(gate) * up).astype(BF16)
    gates = gates_ref[...]
    lane = lax.broadcasted_iota(jnp.int32, gates.shape, 1)
    g_e = jnp.sum(jnp.where(lane == e, gates, 0.0), axis=-1, keepdims=True)
    acc_ref[...] += g_e * jnp.dot(hmid, wd_ref[0], preferred_element_type=F32)

    @pl.when(e == pl.num_programs(1) - 1)
    def _finish():
        y = x_ref[...] + acc_ref[...]
        if final_norm:
            y = y * lax.rsqrt(jnp.mean(y * y, axis=-1, keepdims=True) + EPS) * fw_ref[...]
        o_ref[...] = y


def _moe(x, nw, r_w, r_b, wg, wu, wd, final_w, final_norm, tm):
    T, D = x.shape
    const = lambda shape: pl.BlockSpec(shape, lambda i, e: (0,) * len(shape))
    return pl.pallas_call(
        functools.partial(_moe_kernel, final_norm=final_norm),
        grid=(T // tm, N_EXPERTS),
        in_specs=[
            pl.BlockSpec((tm, D), lambda i, e: (i, 0)),
            const((1, D)), const((D, LANES)), const((1, LANES)),
            pl.BlockSpec((1, D, EXPERT_FF), lambda i, e: (e, 0, 0)),
            pl.BlockSpec((1, D, EXPERT_FF), lambda i, e: (e, 0, 0)),
            pl.BlockSpec((1, EXPERT_FF, D), lambda i, e: (e, 0, 0)),
            const((1, D)),
        ],
        out_specs=pl.BlockSpec((tm, D), lambda i, e: (i, 0)),
        out_shape=jax.ShapeDtypeStruct((T, D), F32),
        scratch_shapes=[pltpu.VMEM((tm, D), BF16), pltpu.VMEM((tm, LANES), F32), pltpu.VMEM((tm, D), F32)],
        compiler_params=_params("parallel", "arbitrary"),
        name="moe",
    )(x, nw, r_w, r_b, wg, wu, wd, final_w)


def _tile(n, pref):
    t = min(n, pref)
    assert n % t == 0, (n, t)
    return t


def _prep_layer(l, norm1_w, w_in, ret_decay_logit, ret_gn_w, diff_lambda, diff_subln_w, conv_w, w_out, norm2_w,
                router_group_w, router_group_b, router_expert_w, router_expert_b, expert_w_gate, expert_w_up,
                expert_w_down):
    w = w_in[l]
    w_nat = jnp.concatenate([w[:, :_OFF_DQ], w[:, _OFF_DK:_OFF_DV], w[:, _OFF_CONV:]], axis=1).astype(BF16)
    w_t = jnp.concatenate([w[:, _OFF_DQ:_OFF_DK], w[:, _OFF_DV:_OFF_CONV]], axis=1).T.astype(BF16)
    router = jnp.zeros((D_MODEL, LANES), F32)
    router = router.at[:, :N_EXPERTS].set(router_expert_w[l]).at[:, N_EXPERTS:N_EXPERTS + N_GROUPS].set(
        router_group_w[l])
    r_b = jnp.zeros((1, LANES), F32).at[0, :N_EXPERTS].set(router_expert_b[l]).at[
        0, N_EXPERTS:N_EXPERTS + N_GROUPS].set(router_group_b[l])
    return dict(
        norm1=norm1_w[l][None, :], w_nat=w_nat, w_t=w_t,
        lg=jax.nn.log_sigmoid(ret_decay_logit[l].astype(F32)),
        gn_w=ret_gn_w[l][None, :].astype(F32),
        lam=diff_lambda[l].astype(F32), subln=diff_subln_w[l][:, None].astype(F32),
        conv_w=conv_w[l].astype(F32), w_out=w_out[l].astype(BF16),
        norm2=norm2_w[l][None, :], r_w=router, r_b=r_b,
        wg=expert_w_gate[l].astype(BF16), wu=expert_w_up[l].astype(BF16), wd=expert_w_down[l].astype(BF16),
    )


def _trunk(x, layers, final_w, slopes, pieces):
    B, S, D = x.shape
    tm = _tile(S, 512)
    tq = _tile(S, 512)
    tk = _tile(S, 256)
    t_moe = _tile(B * S, 1024)
    aux = _position_features(S)
    for l, lw in enumerate(layers):
        rqkv, rg, dk, dqvt, cv, kn = _in_proj(x, lw["norm1"], lw["w_nat"], lw["w_t"], tm)
        ret = _retention(lw["lg"], rqkv, rg, lw["gn_w"])
        diff = _diff_attention(slopes, pieces, lw["lam"], lw["subln"], kn, dqvt, dk, aux, l, tq, tk)
        x = _out_proj(x, ret, diff, cv, lw["conv_w"], lw["w_out"], tm)
        last = l == len(layers) - 1
        x = _moe(x.reshape(B * S, D), lw["norm2"], lw["r_w"], lw["r_b"], lw["wg"], lw["wu"],
                 lw["wd"], final_w, last, t_moe).reshape(B, S, D)
    return x


def kernel(x_prompt, x_sample, norm1_w, w_in, ret_decay_logit, ret_gn_w, diff_lambda, diff_subln_w, conv_w, w_out,
           norm2_w, router_group_w, router_group_b, router_expert_w, router_expert_b, expert_w_gate, expert_w_up,
           expert_w_down, final_norm_w):
    depth = w_in.shape[0]
    layers = [
        _prep_layer(l, norm1_w, w_in, ret_decay_logit, ret_gn_w, diff_lambda, diff_subln_w, conv_w, w_out, norm2_w,
                    router_group_w, router_group_b, router_expert_w, router_expert_b, expert_w_gate, expert_w_up,
                    expert_w_down)
        for l in range(depth)
    ]
    final_w = final_norm_w[None, :].astype(F32)
    slopes = (np.float32(2.0) ** (np.float32(-8.0) * np.arange(1, DIFF_HEADS + 1, dtype=np.float32)
                                  / np.float32(DIFF_HEADS))) * np.float32(LOG2E)
    s_hi = slopes.astype(BF16).astype(np.float32)
    s_mid = (slopes - s_hi).astype(BF16).astype(np.float32)
    s_lo = (slopes - s_hi - s_mid).astype(BF16).astype(np.float32)
    pieces = jnp.asarray(np.stack([s_hi, s_mid, s_lo], axis=1).reshape(-1))
    slopes = jnp.asarray(slopes)
    return (_trunk(x_prompt, layers, final_w, slopes, pieces), _trunk(x_sample, layers, final_w, slopes, pieces))
```

```python
import functools
import math

import jax
import jax.numpy as jnp
import numpy as np
from jax import lax
from jax.experimental import pallas as pl
from jax.experimental.pallas import tpu as pltpu

F32 = jnp.float32
BF16 = jnp.bfloat16

D_MODEL = 1024
RET_HEADS = 6
RET_HEAD_DIM = 64
RET_WIDTH = RET_HEADS * RET_HEAD_DIM
DIFF_HEADS = 6
DIFF_QK_DIM = 32
DIFF_V_DIM = 64
DIFF_WIDTH = DIFF_HEADS * DIFF_V_DIM
CONV_CHANNELS = 256
CONV_WIDTH = 3
N_GROUPS = 4
EXPERTS_PER_GROUP = 4
N_EXPERTS = 16
EXPERT_FF = 512
EPS = 1e-6

LANES = 128
HEAD_PAIRS = RET_HEADS // 2
RET_CHUNK = 128
SUBLANES = 8
VMEM_LIMIT = 56 * 1024 * 1024
LOG2E = math.log2(math.e)
NEG_BIG = -1e30

_OFF_RG = 3 * RET_WIDTH
_OFF_DQ = 4 * RET_WIDTH
_OFF_DK = _OFF_DQ + DIFF_WIDTH
_OFF_DV = _OFF_DK + DIFF_WIDTH
_OFF_CONV = _OFF_DV + DIFF_WIDTH


def _params(*sem):
    return pltpu.CompilerParams(dimension_semantics=sem, vmem_limit_bytes=VMEM_LIMIT)


def _split_hi_lo(x):
    hi = x.astype(BF16)
    lo = (x - hi.astype(F32)).astype(BF16)
    return hi, lo


def _in_proj_kernel(x_ref, nw_ref, w_ref, wt_ref, rqkv_ref, rg_ref, dk_ref, dqvt_ref, cv_ref, kn_ref):
    x = x_ref[0]
    ms = jnp.mean(x * x, axis=-1, keepdims=True)
    h = (x * lax.rsqrt(ms + EPS) * nw_ref[...]).astype(BF16)
    a = jnp.dot(h, w_ref[:, 0:_OFF_DQ], preferred_element_type=F32)
    rqkv_ref[0, :, 0:RET_WIDTH] = a[:, 0:RET_WIDTH].astype(BF16)
    rqkv_ref[0, :, RET_WIDTH:2 * RET_WIDTH] = (a[:, RET_WIDTH:2 * RET_WIDTH] * RET_HEAD_DIM ** -0.5).astype(BF16)
    rqkv_ref[0, :, 2 * RET_WIDTH:3 * RET_WIDTH] = a[:, 2 * RET_WIDTH:3 * RET_WIDTH].astype(BF16)
    rg_ref[0] = a[:, _OFF_RG:_OFF_DQ]
    dk = jnp.dot(h, w_ref[:, _OFF_DQ:_OFF_DQ + DIFF_WIDTH], preferred_element_type=F32).astype(BF16)
    dk_ref[0] = dk
    dkf = dk.astype(F32)
    sq_hi, sq_lo = _split_hi_lo(dkf * dkf)
    grp = (lax.shift_right_logical(lax.broadcasted_iota(jnp.int32, (DIFF_WIDTH, LANES), 0), 5)
           == lax.broadcasted_iota(jnp.int32, (DIFF_WIDTH, LANES), 1))
    grp = jnp.where(grp, 1.0, 0.0).astype(BF16)
    kn2 = jnp.dot(sq_hi, grp, preferred_element_type=F32) + jnp.dot(sq_lo, grp, preferred_element_type=F32)
    kn_ref[0, 0] = jnp.broadcast_to(jnp.max(kn2, axis=0, keepdims=True), (SUBLANES, LANES))
    cv_ref[0] =jnp.dot(h, w_ref[:, _OFF_DQ + DIFF_WIDTH:], preferred_element_type=F32)
    t = lax.dot_general(wt_ref[...], h, (((1,), (1,)), ((), ())), preferred_element_type=F32)
    dqvt_ref[0, 0:DIFF_WIDTH, :] = (t[0:DIFF_WIDTH] * (DIFF_QK_DIM ** -0.5 * LOG2E)).astype(BF16)
    dqvt_ref[0, DIFF_WIDTH:, :] = t[DIFF_WIDTH:].astype(BF16)


def _in_proj(x, nw, w_nat, w_t, tm):
    B, S, D = x.shape
    n_nat = w_nat.shape[1]
    return pl.pallas_call(
        _in_proj_kernel,
        grid=(B, S // tm),
        in_specs=[
            pl.BlockSpec((1, tm, D), lambda b, i: (b, i, 0)),
            pl.BlockSpec((1, D), lambda b, i: (0, 0)),
            pl.BlockSpec((D, n_nat), lambda b, i: (0, 0)),
            pl.BlockSpec((2 * DIFF_WIDTH, D), lambda b, i: (0, 0)),
        ],
        out_specs=[
            pl.BlockSpec((1, tm, 3 * RET_WIDTH), lambda b, i: (b, i, 0)),
            pl.BlockSpec((1, tm, RET_WIDTH), lambda b, i: (b, i, 0)),
            pl.BlockSpec((1, tm, DIFF_WIDTH), lambda b, i: (b, i, 0)),
            pl.BlockSpec((1, 2 * DIFF_WIDTH, tm), lambda b, i: (b, 0, i)),
            pl.BlockSpec((1, tm, 3 * CONV_CHANNELS), lambda b, i: (b, i, 0)),
            pl.BlockSpec((1, 1, SUBLANES, LANES), lambda b, i: (b, i, 0, 0)),
        ],
        out_shape=[
            jax.ShapeDtypeStruct((B, S, 3 * RET_WIDTH), BF16),
            jax.ShapeDtypeStruct((B, S, RET_WIDTH), F32),
            jax.ShapeDtypeStruct((B, S, DIFF_WIDTH), BF16),
            jax.ShapeDtypeStruct((B, 2 * DIFF_WIDTH, S), BF16),
            jax.ShapeDtypeStruct((B, S, 3 * CONV_CHANNELS), F32),
            jax.ShapeDtypeStruct((B, S // tm, SUBLANES, LANES), F32),
        ],
        compiler_params=_params("parallel", "parallel"),
        name="in_proj",
    )(x, nw, w_nat, w_t)


def _pair_lane_value(lg_ref, direction, pair, shape, axis):
    idx = lax.broadcasted_iota(jnp.int32, shape, axis)
    return jnp.where(idx < RET_HEAD_DIM, lg_ref[direction, 2 * pair], lg_ref[direction, 2 * pair + 1])


def _same_head_mask():
    r = lax.broadcasted_iota(jnp.int32, (LANES, LANES), 0)
    c = lax.broadcasted_iota(jnp.int32, (LANES, LANES), 1)
    return (r < RET_HEAD_DIM) == (c < RET_HEAD_DIM)


def _ret_fwd_kernel(lg_ref, qkv_ref, y_ref, state_ref, dmat_ref, qdec_ref, kdec_ref, cdec_ref, *, chunks):
    C = RET_CHUNK

    @pl.when((pl.program_id(0) == 0) & (pl.program_id(1) == 0))
    def _build_tables():
        diff = (lax.broadcasted_iota(jnp.int32, (C, C), 0) - lax.broadcasted_iota(jnp.int32, (C, C), 1)).astype(F32)
        for h in range(RET_HEADS):
            lower = jnp.exp(jnp.maximum(diff, 0.0) * lg_ref[0, h])
            upper = jnp.exp(jnp.maximum(-diff, 0.0) * lg_ref[1, h])
            dmat_ref[h] = jnp.where(diff >= 0, lower, upper)
        pos = lax.broadcasted_iota(jnp.int32, (C, LANES), 0).astype(F32)
        for p in range(HEAD_PAIRS):
            lg_lane = _pair_lane_value(lg_ref, 0, p, (C, LANES), 1)
            qdec_ref[p] = jnp.exp((pos + 1.0) * lg_lane)
            kdec_ref[p] = jnp.exp((C - 1.0 - pos) * lg_lane)
            cdec_ref[p] = jnp.exp(float(C) * _pair_lane_value(lg_ref, 0, p, (LANES, LANES), 0))

    @pl.when(pl.program_id(1) == 0)
    def _reset():
        state_ref[...] = jnp.zeros(state_ref.shape, F32)

    lane = lax.broadcasted_iota(jnp.int32, (C, LANES), 1)
    same_head = _same_head_mask()
    def operands(p, ci):
        rows = slice(ci * C, (ci + 1) * C)
        return (qkv_ref[0, rows, p * LANES:(p + 1) * LANES],
                qkv_ref[0, rows, RET_WIDTH + p * LANES:RET_WIDTH + (p + 1) * LANES],
                qkv_ref[0, rows, 2 * RET_WIDTH + p * LANES:2 * RET_WIDTH + (p + 1) * LANES])

    items = [(p, ci) for p in range(HEAD_PAIRS) for ci in range(chunks)]
    scores, kvs = {}, {}
    for p, ci in items:
        q, k, v = operands(p, ci)
        for h in range(2):
            head_lanes = (lane >= h * RET_HEAD_DIM) & (lane < (h + 1) * RET_HEAD_DIM)
            qm = jnp.where(head_lanes, q, jnp.zeros_like(q))
            scores[p, ci, h] = lax.dot_general(qm, k, (((1,), (1,)), ((), ())), preferred_element_type=F32)
        kd = (k.astype(F32) * kdec_ref[p]).astype(BF16)
        kvs[p, ci] = lax.dot_general(kd, v, (((0,), (0,)), ((), ())), preferred_element_type=F32)
    states = {}
    for p in range(HEAD_PAIRS):
        state = state_ref[p]
        for ci in range(chunks):
            states[p, ci] = state.astype(BF16)
            state = cdec_ref[p] * state + jnp.where(same_head, kvs[p, ci], 0.0)
        state_ref[p] = state
    for p, ci in items:
        q, _, v = operands(p, ci)
        inner = [jnp.dot((scores[p, ci, h] * dmat_ref[2 * p + h]).astype(BF16), v, preferred_element_type=F32)
                 for h in range(2)]
        cross = jnp.dot(q, states[p, ci], preferred_element_type=F32) * qdec_ref[p]
        y_ref[0, ci * C:(ci + 1) * C, p * LANES:(p + 1) * LANES] = (
            jnp.where(lane < RET_HEAD_DIM, inner[0], inner[1]) + cross)


def _ret_bwd_kernel(lg_ref, qkv_ref, y1_ref, g_ref, gnw_ref, o_ref, state_ref, qdec_ref, kdec_ref, cdec_ref, *,
                    chunks):
    C = RET_CHUNK

    @pl.when((pl.program_id(0) == 0) & (pl.program_id(1) == 0))
    def _build_tables():
        pos = lax.broadcasted_iota(jnp.int32, (C, LANES), 0).astype(F32)
        for p in range(HEAD_PAIRS):
            lg_lane = _pair_lane_value(lg_ref, 1, p, (C, LANES), 1)
            qdec_ref[p] = jnp.exp((float(C) - pos) * lg_lane)
            kdec_ref[p] = jnp.exp(pos * lg_lane)
            cdec_ref[p] = jnp.exp(float(C) * _pair_lane_value(lg_ref, 1, p, (LANES, LANES), 0))

    @pl.when(pl.program_id(1) == 0)
    def _reset():
        state_ref[...] = jnp.zeros(state_ref.shape, F32)

    same_head = _same_head_mask()
    head_avg = jnp.where(same_head, 1.0 / RET_HEAD_DIM, 0.0).astype(BF16)

    def head_mean(t):
        hi, lo = _split_hi_lo(t)
        return (jnp.dot(hi, head_avg, preferred_element_type=F32)
                + jnp.dot(lo, head_avg, preferred_element_type=F32))

    items = [(p, ci) for p in range(HEAD_PAIRS) for ci in range(chunks)]
    kvs = {}
    for p, ci in items:
        rows = slice(ci * C, (ci + 1) * C)
        k = qkv_ref[0, rows, RET_WIDTH + p * LANES:RET_WIDTH + (p + 1) * LANES]
        v = qkv_ref[0, rows, 2 * RET_WIDTH + p * LANES:2 * RET_WIDTH + (p + 1) * LANES]
        kd = (k.astype(F32) * kdec_ref[p]).astype(BF16)
        kvs[p, ci] = lax.dot_general(kd, v, (((0,), (0,)), ((), ())), preferred_element_type=F32)
    states = {}
    for p in range(HEAD_PAIRS):
        state = state_ref[p]
        for ci in reversed(range(chunks)):
            states[p, ci] = state.astype(BF16)
            state = cdec_ref[p] * state + jnp.where(same_head, kvs[p, ci], 0.0)
        state_ref[p] = state
    ys = {}
    for p, ci in items:
        rows = slice(ci * C, (ci + 1) * C)
        sl = slice(p * LANES, (p + 1) * LANES)
        cross = jnp.dot(qkv_ref[0, rows, sl], states[p, ci], preferred_element_type=F32)
        ys[p, ci] = y1_ref[0, rows, sl] + cross * qdec_ref[p]
    means = {key: head_mean(y) for key, y in ys.items()}
    devs = {key: ys[key] - means[key] for key in ys}
    variances = {key: head_mean(d * d) for key, d in devs.items()}
    for p, ci in items:
        rows = slice(ci * C, (ci + 1) * C)
        sl = slice(p * LANES, (p + 1) * LANES)
        g = g_ref[0, rows, sl]
        o = devs[p, ci] * lax.rsqrt(variances[p, ci] + EPS) * gnw_ref[:, sl] * (g * jax.nn.sigmoid(g))
        o_ref[0, rows, sl] = o.astype(o_ref.dtype)


def _retention(lg, rqkv, rg, gn_w):
    B, S, _ = rqkv.shape
    chunks = math.gcd(S // RET_CHUNK, 4)
    C = chunks * RET_CHUNK
    n = S // C
    smem = pl.BlockSpec(memory_space=pltpu.SMEM)
    table = pltpu.VMEM((HEAD_PAIRS, RET_CHUNK, LANES), F32)
    state = pltpu.VMEM((HEAD_PAIRS, LANES, LANES), F32)
    y1 = pl.pallas_call(
        functools.partial(_ret_fwd_kernel, chunks=chunks),
        grid=(B, n),
        in_specs=[smem, pl.BlockSpec((1, C, 3 * RET_WIDTH), lambda b, i: (b, i, 0))],
        out_specs=pl.BlockSpec((1, C, RET_WIDTH), lambda b, i: (b, i, 0)),
        out_shape=jax.ShapeDtypeStruct((B, S, RET_WIDTH), F32),
        scratch_shapes=[state, pltpu.VMEM((RET_HEADS, RET_CHUNK, RET_CHUNK), F32), table, table, state],
        compiler_params=_params("arbitrary", "arbitrary"),
        name="ret_fwd",
    )(lg, rqkv)
    rev = lambda b, i: (b, n - 1 - i, 0)
    return pl.pallas_call(
        functools.partial(_ret_bwd_kernel, chunks=chunks),
        grid=(B, n),
        in_specs=[
            smem,
            pl.BlockSpec((1, C, 3 * RET_WIDTH), rev),
            pl.BlockSpec((1, C, RET_WIDTH), rev),
            pl.BlockSpec((1, C, RET_WIDTH), rev),
            pl.BlockSpec((1, RET_WIDTH), lambda b, i: (0, 0)),
        ],
        out_specs=pl.BlockSpec((1, C, RET_WIDTH), rev),
        out_shape=jax.ShapeDtypeStruct((B, S, RET_WIDTH), BF16),
        scratch_shapes=[state, table, table, state],
        compiler_params=_params("arbitrary", "arbitrary"),
        name="ret_bwd",
    )(lg, rqkv, y1, rg, gn_w)


AUX_BLOCK = 256
AUX_ROWS = 16
V_ROWS = DIFF_V_DIM + SUBLANES
GUARD_LOG2 = 60.0
UNDERFLOW_LOG2 = 150.0


def _position_features(seq):
    j = jnp.arange(seq, dtype=jnp.int32)
    ones = jnp.ones((seq,), F32)
    n = (j // AUX_BLOCK).astype(F32)
    jc = (j % AUX_BLOCK - AUX_BLOCK // 2).astype(F32)
    feat = jnp.stack([ones] * 3 + [n] * 3 + [jc] * 3, axis=1)
    return jnp.pad(feat, ((0, 0), (0, LANES - feat.shape[1]))).astype(BF16)


def _split3(x):
    hi = x.astype(BF16).astype(F32)
    r = x - hi
    mid = r.astype(BF16).astype(F32)
    return hi, mid, (r - mid).astype(BF16).astype(F32)


def _attn_kernel(slopes_ref, pieces_ref, lam_ref, sw_ref, kn_ref, qt_ref, k_ref, aux_ref, vt_ref, o_ref,
                 m_ref, acc_ref, rhs_ref, pa_ref, pb_ref, dist_ref, sd_ref, *, seq, tq, tk, lam_init):
    pair = pl.program_id(1)
    qi = pl.program_id(2)
    q0 = qi * tq
    nk = seq // tk
    per = tq // tk
    kd0 = qi * per
    qt = qt_ref[0]
    row = lax.broadcasted_iota(jnp.int32, (LANES, tq), 0)
    q_masked = []
    for hc in range(4):
        lo = hc * DIFF_QK_DIM
        q_masked.append(jnp.where((row >= lo) & (row < lo + DIFF_QK_DIM), qt, jnp.zeros_like(qt)))
    rel = (lax.broadcasted_iota(jnp.int32, (tk, tq), 1) - lax.broadcasted_iota(jnp.int32, (tk, tq), 0)).astype(F32)

    @pl.when(qi == 0)
    def _distance_table():
        d = lax.broadcasted_iota(jnp.int32, (tq, tq), 1) - lax.broadcasted_iota(jnp.int32, (tq, tq), 0)
        dist_ref[...] = jnp.abs(d).astype(F32)

    kn2 = jnp.max(jnp.max(kn_ref[0], axis=0), axis=0, keepdims=True)
    kn_lane = lax.broadcasted_iota(jnp.int32, (1, LANES), 1)
    ipos = (q0 + lax.broadcasted_iota(jnp.int32, (1, tq), 1)).astype(F32)
    aug_row = lax.broadcasted_iota(jnp.int32, (AUX_ROWS, tq), 0)

    def v_rows_with_ones(h, k0, n):
        vt = vt_ref[0, h * DIFF_V_DIM:(h + 1) * DIFF_V_DIM, pl.ds(k0, n)]
        return jnp.concatenate([vt, jnp.ones((SUBLANES, n), BF16)], axis=0)

    kd = pl.multiple_of(q0, tq)
    kdiag = k_ref[0, pl.ds(kd, tq), :]
    for hc in range(4):
        s = jnp.dot(kdiag, q_masked[hc], preferred_element_type=F32)
        sd_ref[hc] = s - dist_ref[...] * slopes_ref[2 * pair + hc // 2]
    for hc in range(4):
        m_ref[hc] = jnp.max(sd_ref[hc], axis=0, keepdims=True)
    probs = [jnp.exp2(sd_ref[hc] - m_ref[hc]).astype(BF16) for hc in range(4)]
    for hc in range(4):
        acc_ref[hc] = jnp.dot(v_rows_with_ones(hc // 2, kd, tq), probs[hc], preferred_element_type=F32)

    for h in range(2):
        slope = slopes_ref[2 * pair + h]

        def v_with_ones(k0, h=h):
            return v_rows_with_ones(h, k0, tk)

        def online_step(kb, carry, h=h, slope=slope, v_with_ones=v_with_ones):
            k0 = pl.multiple_of(kb * tk, tk)
            kblk = k_ref[0, pl.ds(k0, tk), :]
            v_ones = v_with_ones(k0)
            bias = jnp.abs(rel + (q0 - k0).astype(F32)) * slope
            for c in range(2):
                hc = 2 * h + c
                s = jnp.dot(kblk, q_masked[hc], preferred_element_type=F32) - bias
                m_old = m_ref[hc]
                m_new = jnp.maximum(m_old, jnp.max(s, axis=0, keepdims=True))
                p = jnp.exp2(s - m_new).astype(BF16)
                acc_ref[hc] = jnp.exp2(m_old - m_new) * acc_ref[hc] + jnp.dot(v_ones, p, preferred_element_type=F32)
                m_ref[hc] = m_new
            return carry

        excess = None
        for c in range(2):
            hc = 2 * h + c
            qf = q_masked[hc].astype(F32)
            qn2 = jnp.sum(qf * qf, axis=0, keepdims=True)
            kn2_hc = jnp.max(jnp.where(kn_lane == 4 * pair + hc, kn2, 0.0), axis=-1, keepdims=True)
            bound = jnp.sqrt(qn2 * kn2_hc) * 1.01 + 1e-3
            e = jnp.max(bound - m_ref[hc], axis=-1, keepdims=True)
            excess = e if excess is None else jnp.maximum(excess, e)
        reach = (excess + UNDERFLOW_LOG2) / slope
        q0f = q0.astype(F32)
        lo_blk = jnp.clip(jnp.floor((q0f - (tk - 1.0) - reach) / tk), 0.0, float(nk))
        hi_blk = jnp.clip(jnp.floor((reach + q0f + (tq - 1.0)) / tk) + 1.0, 0.0, float(nk))
        lo_blk = jnp.minimum(lo_blk.astype(jnp.int32)[0, 0], kd0)
        hi_blk = jnp.maximum(hi_blk.astype(jnp.int32)[0, 0], kd0 + per)
        fast = jnp.where(excess <= GUARD_LOG2, 1, 0).astype(jnp.int32)[0, 0] == 1

        @pl.when(fast)
        def _single_pass(h=h, slope=slope, v_with_ones=v_with_ones, lo_blk=lo_blk, hi_blk=hi_blk):
            pieces = [pieces_ref[(2 * pair + h) * 3 + i] for i in range(3)]
            for c in range(2):
                hc = 2 * h + c
                for side, sign in ((0, 1.0), (1, -1.0)):
                    const = (-sign) * (slope * ipos) - m_ref[hc] + sign * (0.5 * AUX_BLOCK) * slope
                    c_hi, c_mid, c_lo = _split3(const)
                    aug = jnp.where(aug_row == 0, c_hi,
                                    jnp.where(aug_row == 1, c_mid, jnp.where(aug_row == 2, c_lo, 0.0)))
                    for i in range(3):
                        aug = jnp.where(aug_row == 3 + i, sign * AUX_BLOCK * pieces[i], aug)
                        aug = jnp.where(aug_row == 6 + i, sign * pieces[i], aug)
                    rhs_ref[side * 4 + hc, 0:LANES, :] = q_masked[hc]
                    rhs_ref[side * 4 + hc, LANES:LANES + AUX_ROWS, :] = aug.astype(BF16)
                    rhs_ref[side * 4 + hc, LANES + AUX_ROWS:, :] = jnp.zeros((LANES - AUX_ROWS, tq), BF16)

            odd = (kd0 - lo_blk + hi_blk - kd0 - per) % 2
            lo_even = jnp.where((odd == 1) & (lo_blk > 0), lo_blk - 1, lo_blk)
            hi_even = jnp.where((odd == 1) & (lo_blk == 0), hi_blk + 1, hi_blk)
            n_lo = kd0 - lo_even
            total = n_lo + hi_even - kd0 - per

            def block_of(i):
                return jnp.where(i < n_lo, lo_even + i, kd0 + per + i - n_lo)

            def stage_a(pair_idx, p_ref):
                for half in range(2):
                    i = 2 * pair_idx + half
                    k0 = pl.multiple_of(block_of(i) * tk, tk)
                    side = jnp.where(i < n_lo, 0, 1)
                    lhs = jnp.concatenate([k_ref[0, pl.ds(k0, tk), :], aux_ref[pl.ds(k0, tk), :]], axis=1)
                    for c in range(2):
                        t = jnp.dot(lhs, rhs_ref[side * 4 + 2 * h + c], preferred_element_type=F32)
                        p_ref[2 * half + c] = jnp.exp2(t).astype(BF16)

            def stage_b(pair_idx, p_ref):
                va = v_with_ones(pl.multiple_of(block_of(2 * pair_idx) * tk, tk))
                vb = v_with_ones(pl.multiple_of(block_of(2 * pair_idx + 1) * tk, tk))
                for c in range(2):
                    acc_ref[2 * h + c] += (jnp.dot(va, p_ref[c], preferred_element_type=F32)
                                           + jnp.dot(vb, p_ref[2 + c], preferred_element_type=F32))

            n_pairs = total // 2

            @pl.when(n_pairs > 0)
            def _blocks():
                stage_a(0, pa_ref)

                def two_pairs(j, carry):
                    stage_a(2 * j + 1, pb_ref)
                    stage_b(2 * j, pa_ref)
                    stage_a(2 * j + 2, pa_ref)
                    stage_b(2 * j + 1, pb_ref)
                    return carry

                lax.fori_loop(0, (n_pairs - 1) // 2, two_pairs, 0)

                @pl.when(n_pairs % 2 == 1)
                def _last_one():
                    stage_b(n_pairs - 1, pa_ref)

                @pl.when(n_pairs % 2 == 0)
                def _last_two():
                    stage_a(n_pairs - 1, pb_ref)
                    stage_b(n_pairs - 2, pa_ref)
                    stage_b(n_pairs - 1, pb_ref)

        @pl.when(jnp.logical_not(fast))
        def _online(online_step=online_step):
            lax.fori_loop(0, kd0, online_step, 0)
            lax.fori_loop(kd0 + per, nk, online_step, 0)

    lam = lam_ref[...]
    lam_full = (jnp.exp(jnp.sum(lam[0:1] * lam[1:2], axis=-1, keepdims=True))
                - jnp.exp(jnp.sum(lam[2:3] * lam[3:4], axis=-1, keepdims=True)) + lam_init)
    outs = []
    for h in range(2):
        a1 = acc_ref[2 * h]
        a2 = acc_ref[2 * h + 1]
        rows = slice(0, DIFF_V_DIM)
        ones_row = slice(DIFF_V_DIM, DIFF_V_DIM + 1)
        o = a1[rows] / a1[ones_row] - lam_full * (a2[rows] / a2[ones_row])
        o = o * lax.rsqrt(jnp.mean(o * o, axis=0, keepdims=True) + EPS)
        outs.append(o * sw_ref[...] * (1.0 - lam_init))
    o_ref[0] = jnp.concatenate(outs, axis=0).T.astype(o_ref.dtype)


def _diff_attention(slopes, pieces, lam, subln_w, kn, dqvt, dk, aux, layer_idx, tq, tk):
    B, S, _ = dk.shape
    assert tq % tk == 0 and tk <= AUX_BLOCK and S // AUX_BLOCK <= AUX_BLOCK
    lam_init = 0.8 - 0.6 * math.exp(-0.3 * layer_idx)
    kern = functools.partial(_attn_kernel, seq=S, tq=tq, tk=tk, lam_init=lam_init)
    v_block0 = DIFF_WIDTH // LANES
    smem = pl.BlockSpec(memory_space=pltpu.SMEM)
    return pl.pallas_call(
        kern,
        grid=(B, HEAD_PAIRS, S // tq),
        in_specs=[
            smem, smem,
            pl.BlockSpec((4, DIFF_QK_DIM), lambda b, j, i: (0, 0)),
            pl.BlockSpec((DIFF_V_DIM, 1), lambda b, j, i: (0, 0)),
            pl.BlockSpec((1,) + kn.shape[1:], lambda b, j, i: (b, 0, 0, 0)),
            pl.BlockSpec((1, LANES, tq), lambda b, j, i: (b, j, i)),
            pl.BlockSpec((1, S, LANES), lambda b, j, i: (b, 0, j)),
            pl.BlockSpec((S, LANES), lambda b, j, i: (0, 0)),
            pl.BlockSpec((1, LANES, S), lambda b, j, i: (b, v_block0 + j, 0)),
        ],
        out_specs=pl.BlockSpec((1, tq, LANES), lambda b, j, i: (b, i, j)),
        out_shape=jax.ShapeDtypeStruct((B, S, DIFF_WIDTH), BF16),
        scratch_shapes=[pltpu.VMEM((4, 1, tq), F32), pltpu.VMEM((4, V_ROWS, tq), F32),
                        pltpu.VMEM((8, 2 * LANES, tq), BF16),
                        pltpu.VMEM((4, tk, tq), BF16), pltpu.VMEM((4, tk, tq), BF16),
                        pltpu.VMEM((tq, tq), F32), pltpu.VMEM((4, tq, tq), F32)],
        compiler_params=_params("parallel", "parallel", "arbitrary"),
        name="diff_attn",
    )(slopes, pieces, lam, subln_w, kn, dqvt, dk, aux, dqvt)


def _out_proj_kernel(x_ref, ret_ref, diff_ref, cv_ref, prev_ref, next_ref, cw_ref, w_ref, o_ref, *, tm):
    i = pl.program_id(1)
    last = pl.num_programs(1) - 1
    cc = CONV_CHANNELS
    cv = cv_ref[0]
    u = cv[:, 2 * cc:3 * cc] * cv[:, 0:cc]
    prev = prev_ref[0]
    nxt = next_ref[0]
    u_prev = prev[SUBLANES - 1:SUBLANES, 2 * cc:3 * cc] * prev[SUBLANES - 1:SUBLANES, 0:cc]
    u_next = nxt[0:1, 2 * cc:3 * cc] * nxt[0:1, 0:cc]
    u_prev = jnp.where(i == 0, 0.0, u_prev)
    u_next = jnp.where(i == last, 0.0, u_next)
    rows = lax.broadcasted_iota(jnp.int32, (tm, cc), 0)
    u_m1 = jnp.where(rows == 0, u_prev, pltpu.roll(u, 1, 0))
    u_p1 = jnp.where(rows == tm - 1, u_next, pltpu.roll(u, tm - 1, 0))
    y = cw_ref[0:1] * u_m1 + cw_ref[1:2] * u + cw_ref[2:3] * u_p1
    conv = (cv[:, cc:2 * cc] * y).astype(BF16)
    acc = jnp.dot(ret_ref[0], w_ref[0:RET_WIDTH], preferred_element_type=F32)
    acc += jnp.dot(diff_ref[0], w_ref[RET_WIDTH:RET_WIDTH + DIFF_WIDTH], preferred_element_type=F32)
    acc += jnp.dot(conv, w_ref[RET_WIDTH + DIFF_WIDTH:], preferred_element_type=F32)
    o_ref[0] = x_ref[0] + acc


def _out_proj(x, ret, diff, cv, conv_w, w_out, tm):
    B, S, D = x.shape
    per = tm // SUBLANES
    nblk8 = S // SUBLANES
    tile = lambda width: pl.BlockSpec((1, tm, width), lambda b, i: (b, i, 0))
    return pl.pallas_call(
        functools.partial(_out_proj_kernel, tm=tm),
        grid=(B, S // tm),
        in_specs=[
            tile(D), tile(RET_WIDTH), tile(DIFF_WIDTH), tile(3 * CONV_CHANNELS),
            pl.BlockSpec((1, SUBLANES, 3 * CONV_CHANNELS), lambda b, i: (b, jnp.maximum(i * per - 1, 0), 0)),
            pl.BlockSpec((1, SUBLANES, 3 * CONV_CHANNELS),
                         lambda b, i: (b, jnp.minimum((i + 1) * per, nblk8 - 1), 0)),
            pl.BlockSpec((CONV_WIDTH, CONV_CHANNELS), lambda b, i: (0, 0)),
            pl.BlockSpec((D, D), lambda b, i: (0, 0)),
        ],
        out_specs=tile(D),
        out_shape=jax.ShapeDtypeStruct((B, S, D), F32),
        compiler_params=_params("parallel", "parallel"),
        name="out_proj",
    )(x, ret, diff, cv, cv, cv, conv_w, w_out)


_GROUP_LANE0 = N_EXPERTS


def _router_gates(logits):
    lane = lax.broadcasted_iota(jnp.int32, logits.shape, 1)
    big = jnp.int32(LANES)
    is_group = (lane >= _GROUP_LANE0) & (lane < _GROUP_LANE0 + N_GROUPS)
    gl = jnp.where(is_group, logits, -jnp.inf)
    g_max = jnp.max(gl, axis=-1, keepdims=True)
    g_idx = jnp.min(jnp.where(gl == g_max, lane - _GROUP_LANE0, big), axis=-1, keepdims=True)
    g_w = 1.0 / jnp.sum(jnp.where(is_group, jnp.exp(gl - g_max), 0.0), axis=-1, keepdims=True)
    in_group = (lane >= g_idx * EXPERTS_PER_GROUP) & (lane < (g_idx + 1) * EXPERTS_PER_GROUP)
    el = jnp.where(in_group, logits, -jnp.inf)
    e1 = jnp.max(el, axis=-1, keepdims=True)
    i1 = jnp.min(jnp.where(el == e1, lane, big), axis=-1, keepdims=True)
    el2 = jnp.where(lane == i1, -jnp.inf, el)
    e2 = jnp.max(el2, axis=-1, keepdims=True)
    i2 = jnp.min(jnp.where(el2 == e2, lane, big), axis=-1, keepdims=True)
    r = jnp.exp(e2 - e1)
    p1 = g_w / (1.0 + r)
    p2 = g_w * r / (1.0 + r)
    return jnp.where(lane == i1, p1, jnp.where(lane == i2, p2, 0.0))


def _moe_kernel(x_ref, nw_ref, rw_ref, rb_ref, wg_ref, wu_ref, wd_ref, fw_ref, o_ref,
                xn_ref, gates_ref, acc_ref, *, final_norm):
    e = pl.program_id(1)

    @pl.when(e == 0)
    def _route():
        x = x_ref[...]
        xn = x * lax.rsqrt(jnp.mean(x * x, axis=-1, keepdims=True) + EPS) * nw_ref[...]
        hi, lo = _split_hi_lo(xn)
        r_hi, r_lo = _split_hi_lo(rw_ref[...])
        logits = (jnp.dot(hi, r_hi, preferred_element_type=F32)
                  + jnp.dot(lo, r_hi, preferred_element_type=F32)
                  + jnp.dot(hi, r_lo, preferred_element_type=F32)) + rb_ref[...]
        gates_ref[...] = _router_gates(logits)
        xn_ref[...] = hi
        acc_ref[...] = jnp.zeros(acc_ref.shape, F32)

    xn = xn_ref[...]
    gate = jnp.dot(xn, wg_ref[0], preferred_element_type=F32)
    up = jnp.dot(xn, wu_ref[0], preferred_element_type=F32)
    hmid = (gate * jax.nn.sigmoid(gate) * up).astype(BF16)
    gates = gates_ref[...]
    lane = lax.broadcasted_iota(jnp.int32, gates.shape, 1)
    g_e = jnp.sum(jnp.where(lane == e, gates, 0.0), axis=-1, keepdims=True)
    acc_ref[...] += g_e * jnp.dot(hmid, wd_ref[0], preferred_element_type=F32)

    @pl.when(e == pl.num_programs(1) - 1)
    def _finish():
        y = x_ref[...] + acc_ref[...]
        if final_norm:
            y = y * lax.rsqrt(jnp.mean(y * y, axis=-1, keepdims=True) + EPS) * fw_ref[...]
        o_ref[...] = y


def _moe(x, nw, r_w, r_b, wg, wu, wd, final_w, final_norm, tm):
    T, D = x.shape
    const = lambda shape: pl.BlockSpec(shape, lambda i, e: (0,) * len(shape))
    return pl.pallas_call(
        functools.partial(_moe_kernel, final_norm=final_norm),
        grid=(T // tm, N_EXPERTS),
        in_specs=[
            pl.BlockSpec((tm, D), lambda i, e: (i, 0)),
            const((1, D)), const((D, LANES)), const((1, LANES)),
            pl.BlockSpec((1, D, EXPERT_FF), lambda i, e: (e, 0, 0)),
            pl.BlockSpec((1, D, EXPERT_FF), lambda i, e: (e, 0, 0)),
            pl.BlockSpec((1, EXPERT_FF, D), lambda i, e: (e, 0, 0)),
            const((1, D)),
        ],
        out_specs=pl.BlockSpec((tm, D), lambda i, e: (i, 0)),
        out_shape=jax.ShapeDtypeStruct((T, D), F32),
        scratch_shapes=[pltpu.VMEM((tm, D), BF16), pltpu.VMEM((tm, LANES), F32), pltpu.VMEM((tm, D), F32)],
        compiler_params=_params("parallel", "arbitrary"),
        name="moe",
    )(x, nw, r_w, r_b, wg, wu, wd, final_w)


def _tile(n, pref):
    t = min(n, pref)
    assert n % t == 0, (n, t)
    return t


def _prep_layer(l, norm1_w, w_in, ret_decay_logit, ret_gn_w, diff_lambda, diff_subln_w, conv_w, w_out, norm2_w,
                router_group_w, router_group_b, router_expert_w, router_expert_b, expert_w_gate, expert_w_up,
                expert_w_down):
    w = w_in[l]
    w_nat = jnp.concatenate([w[:, :_OFF_DQ], w[:, _OFF_DK:_OFF_DV], w[:, _OFF_CONV:]], axis=1).astype(BF16)
    w_t = jnp.concatenate([w[:, _OFF_DQ:_OFF_DK], w[:, _OFF_DV:_OFF_CONV]], axis=1).T.astype(BF16)
    router = jnp.zeros((D_MODEL, LANES), F32)
    router = router.at[:, :N_EXPERTS].set(router_expert_w[l]).at[:, N_EXPERTS:N_EXPERTS + N_GROUPS].set(
        router_group_w[l])
    r_b = jnp.zeros((1, LANES), F32).at[0, :N_EXPERTS].set(router_expert_b[l]).at[
        0, N_EXPERTS:N_EXPERTS + N_GROUPS].set(router_group_b[l])
    return dict(
        norm1=norm1_w[l][None, :], w_nat=w_nat, w_t=w_t,
        lg=jax.nn.log_sigmoid(ret_decay_logit[l].astype(F32)),
        gn_w=ret_gn_w[l][None, :].astype(F32),
        lam=diff_lambda[l].astype(F32), subln=diff_subln_w[l][:, None].astype(F32),
        conv_w=conv_w[l].astype(F32), w_out=w_out[l].astype(BF16),
        norm2=norm2_w[l][None, :], r_w=router, r_b=r_b,
        wg=expert_w_gate[l].astype(BF16), wu=expert_w_up[l].astype(BF16), wd=expert_w_down[l].astype(BF16),
    )


def _trunk(x, layers, final_w, slopes, pieces):
    B, S, D = x.shape
    tm = _tile(S, 512)
    tq = _tile(S, 512)
    tk = _tile(S, 256)
    t_moe = _tile(B * S, 1024)
    aux = _position_features(S)
    for l, lw in enumerate(layers):
        rqkv, rg, dk, dqvt, cv, kn = _in_proj(x, lw["norm1"], lw["w_nat"], lw["w_t"], tm)
        ret = _retention(lw["lg"], rqkv, rg, lw["gn_w"])
        diff = _diff_attention(slopes, pieces, lw["lam"], lw["subln"], kn, dqvt, dk, aux, l, tq, tk)
        x = _out_proj(x, ret, diff, cv, lw["conv_w"], lw["w_out"], tm)
        last = l == len(layers) - 1
        x = _moe(x.reshape(B * S, D), lw["norm2"], lw["r_w"], lw["r_b"], lw["wg"], lw["wu"],
                 lw["wd"], final_w, last, t_moe).reshape(B, S, D)
    return x


def kernel(x_prompt, x_sample, norm1_w, w_in, ret_decay_logit, ret_gn_w, diff_lambda, diff_subln_w, conv_w, w_out,
           norm2_w, router_group_w, router_group_b, router_expert_w, router_expert_b, expert_w_gate, expert_w_up,
           expert_w_down, final_norm_w):
    depth = w_in.shape[0]
    layers = [
        _prep_layer(l, norm1_w, w_in, ret_decay_logit, ret_gn_w, diff_lambda, diff_subln_w, conv_w, w_out, norm2_w,
                    router_group_w, router_group_b, router_expert_w, router_expert_b, expert_w_gate, expert_w_up,
                    expert_w_down)
        for l in range(depth)
    ]
    final_w = final_norm_w[None, :].astype(F32)
    slopes = (np.float32(2.0) ** (np.float32(-8.0) * np.arange(1, DIFF_HEADS + 1, dtype=np.float32)
                                  / np.float32(DIFF_HEADS))) * np.float32(LOG2E)
    s_hi = slopes.astype(BF16).astype(np.float32)
    s_mid = (slopes - s_hi).astype(BF16).astype(np.float32)
    s_lo = (slopes - s_hi - s_mid).astype(BF16).astype(np.float32)
    pieces = jnp.asarray(np.stack([s_hi, s_mid, s_lo], axis=1).reshape(-1))
    slopes = jnp.asarray(slopes)
    return (_trunk(x_prompt, layers, final_w, slopes, pieces), _trunk(x_sample, layers, final_w, slopes, pieces))
```

```python
import functools
import math

import jax
import jax.numpy as jnp
import numpy as np
from jax import lax
from jax.experimental import pallas as pl
from jax.experimental.pallas import tpu as pltpu

F32 = jnp.float32
BF16 = jnp.bfloat16

D_MODEL = 1024
RET_HEADS = 6
RET_HEAD_DIM = 64
RET_WIDTH = RET_HEADS * RET_HEAD_DIM
DIFF_HEADS = 6
DIFF_QK_DIM = 32
DIFF_V_DIM = 64
DIFF_WIDTH = DIFF_HEADS * DIFF_V_DIM
CONV_CHANNELS = 256
CONV_WIDTH = 3
N_GROUPS = 4
EXPERTS_PER_GROUP = 4
N_EXPERTS = 16
EXPERT_FF = 512
EPS = 1e-6

LANES = 128
HEAD_PAIRS = RET_HEADS // 2
RET_CHUNK = 128
SUBLANES = 8
VMEM_LIMIT = 56 * 1024 * 1024
LOG2E = math.log2(math.e)
NEG_BIG = -1e30

_OFF_RG = 3 * RET_WIDTH
_OFF_DQ = 4 * RET_WIDTH
_OFF_DK = _OFF_DQ + DIFF_WIDTH
_OFF_DV = _OFF_DK + DIFF_WIDTH
_OFF_CONV = _OFF_DV + DIFF_WIDTH


def _params(*sem):
    return pltpu.CompilerParams(dimension_semantics=sem, vmem_limit_bytes=VMEM_LIMIT)


def _split_hi_lo(x):
    hi = x.astype(BF16)
    lo = (x - hi.astype(F32)).astype(BF16)
    return hi, lo


def _in_proj_kernel(x_ref, nw_ref, w_ref, wt_ref, rqkv_ref, rg_ref, dk_ref, dqvt_ref, cv_ref, kn_ref):
    x = x_ref[0]
    ms = jnp.mean(x * x, axis=-1, keepdims=True)
    h = (x * lax.rsqrt(ms + EPS) * nw_ref[...]).astype(BF16)
    a = jnp.dot(h, w_ref[:, 0:_OFF_DQ], preferred_element_type=F32)
    rqkv_ref[0, :, 0:RET_WIDTH] = a[:, 0:RET_WIDTH].astype(BF16)
    rqkv_ref[0, :, RET_WIDTH:2 * RET_WIDTH] = (a[:, RET_WIDTH:2 * RET_WIDTH] * RET_HEAD_DIM ** -0.5).astype(BF16)
    rqkv_ref[0, :, 2 * RET_WIDTH:3 * RET_WIDTH] = a[:, 2 * RET_WIDTH:3 * RET_WIDTH].astype(BF16)
    rg_ref[0] = a[:, _OFF_RG:_OFF_DQ]
    dk = jnp.dot(h, w_ref[:, _OFF_DQ:_OFF_DQ + DIFF_WIDTH], preferred_element_type=F32).astype(BF16)
    dk_ref[0] = dk
    dkf = dk.astype(F32)
    sq_hi, sq_lo = _split_hi_lo(dkf * dkf)
    grp = (lax.shift_right_logical(lax.broadcasted_iota(jnp.int32, (DIFF_WIDTH, LANES), 0), 5)
           == lax.broadcasted_iota(jnp.int32, (DIFF_WIDTH, LANES), 1))
    grp = jnp.where(grp, 1.0, 0.0).astype(BF16)
    kn2 = jnp.dot(sq_hi, grp, preferred_element_type=F32) + jnp.dot(sq_lo, grp, preferred_element_type=F32)
    kn_ref[0, 0] = jnp.broadcast_to(jnp.max(kn2, axis=0, keepdims=True), (SUBLANES, LANES))
    cv_ref[0] =jnp.dot(h, w_ref[:, _OFF_DQ + DIFF_WIDTH:], preferred_element_type=F32)
    t = lax.dot_general(wt_ref[...], h, (((1,), (1,)), ((), ())), preferred_element_type=F32)
    dqvt_ref[0, 0:DIFF_WIDTH, :] = (t[0:DIFF_WIDTH] * (DIFF_QK_DIM ** -0.5 * LOG2E)).astype(BF16)
    dqvt_ref[0, DIFF_WIDTH:, :] = t[DIFF_WIDTH:].astype(BF16)


def _in_proj(x, nw, w_nat, w_t, tm):
    B, S, D = x.shape
    n_nat = w_nat.shape[1]
    return pl.pallas_call(
        _in_proj_kernel,
        grid=(B, S // tm),
        in_specs=[
            pl.BlockSpec((1, tm, D), lambda b, i: (b, i, 0)),
            pl.BlockSpec((1, D), lambda b, i: (0, 0)),
            pl.BlockSpec((D, n_nat), lambda b, i: (0, 0)),
            pl.BlockSpec((2 * DIFF_WIDTH, D), lambda b, i: (0, 0)),
        ],
        out_specs=[
            pl.BlockSpec((1, tm, 3 * RET_WIDTH), lambda b, i: (b, i, 0)),
            pl.BlockSpec((1, tm, RET_WIDTH), lambda b, i: (b, i, 0)),
            pl.BlockSpec((1, tm, DIFF_WIDTH), lambda b, i: (b, i, 0)),
            pl.BlockSpec((1, 2 * DIFF_WIDTH, tm), lambda b, i: (b, 0, i)),
            pl.BlockSpec((1, tm, 3 * CONV_CHANNELS), lambda b, i: (b, i, 0)),
            pl.BlockSpec((1, 1, SUBLANES, LANES), lambda b, i: (b, i, 0, 0)),
        ],
        out_shape=[
            jax.ShapeDtypeStruct((B, S, 3 * RET_WIDTH), BF16),
            jax.ShapeDtypeStruct((B, S, RET_WIDTH), F32),
            jax.ShapeDtypeStruct((B, S, DIFF_WIDTH), BF16),
            jax.ShapeDtypeStruct((B, 2 * DIFF_WIDTH, S), BF16),
            jax.ShapeDtypeStruct((B, S, 3 * CONV_CHANNELS), F32),
            jax.ShapeDtypeStruct((B, S // tm, SUBLANES, LANES), F32),
        ],
        compiler_params=_params("parallel", "parallel"),
        name="in_proj",
    )(x, nw, w_nat, w_t)


def _pair_lane_value(lg_ref, direction, pair, shape, axis):
    idx = lax.broadcasted_iota(jnp.int32, shape, axis)
    return jnp.where(idx < RET_HEAD_DIM, lg_ref[direction, 2 * pair], lg_ref[direction, 2 * pair + 1])


def _same_head_mask():
    r = lax.broadcasted_iota(jnp.int32, (LANES, LANES), 0)
    c = lax.broadcasted_iota(jnp.int32, (LANES, LANES), 1)
    return (r < RET_HEAD_DIM) == (c < RET_HEAD_DIM)


def _ret_fwd_kernel(lg_ref, qkv_ref, y_ref, state_ref, dmat_ref, qdec_ref, kdec_ref, cdec_ref, *, chunks):
    C = RET_CHUNK

    @pl.when((pl.program_id(0) == 0) & (pl.program_id(1) == 0))
    def _build_tables():
        diff = (lax.broadcasted_iota(jnp.int32, (C, C), 0) - lax.broadcasted_iota(jnp.int32, (C, C), 1)).astype(F32)
        for h in range(RET_HEADS):
            lower = jnp.exp(jnp.maximum(diff, 0.0) * lg_ref[0, h])
            upper = jnp.exp(jnp.maximum(-diff, 0.0) * lg_ref[1, h])
            dmat_ref[h] = jnp.where(diff >= 0, lower, upper)
        pos = lax.broadcasted_iota(jnp.int32, (C, LANES), 0).astype(F32)
        for p in range(HEAD_PAIRS):
            lg_lane = _pair_lane_value(lg_ref, 0, p, (C, LANES), 1)
            qdec_ref[p] = jnp.exp((pos + 1.0) * lg_lane)
            kdec_ref[p] = jnp.exp((C - 1.0 - pos) * lg_lane)
            cdec_ref[p] = jnp.exp(float(C) * _pair_lane_value(lg_ref, 0, p, (LANES, LANES), 0))

    @pl.when(pl.program_id(1) == 0)
    def _reset():
        state_ref[...] = jnp.zeros(state_ref.shape, F32)

    lane = lax.broadcasted_iota(jnp.int32, (C, LANES), 1)
    same_head = _same_head_mask()
    def operands(p, ci):
        rows = slice(ci * C, (ci + 1) * C)
        return (qkv_ref[0, rows, p * LANES:(p + 1) * LANES],
                qkv_ref[0, rows, RET_WIDTH + p * LANES:RET_WIDTH + (p + 1) * LANES],
                qkv_ref[0, rows, 2 * RET_WIDTH + p * LANES:2 * RET_WIDTH + (p + 1) * LANES])

    items = [(p, ci) for p in range(HEAD_PAIRS) for ci in range(chunks)]
    scores, kvs = {}, {}
    for p, ci in items:
        q, k, v = operands(p, ci)
        for h in range(2):
            head_lanes = (lane >= h * RET_HEAD_DIM) & (lane < (h + 1) * RET_HEAD_DIM)
            qm = jnp.where(head_lanes, q, jnp.zeros_like(q))
            scores[p, ci, h] = lax.dot_general(qm, k, (((1,), (1,)), ((), ())), preferred_element_type=F32)
        kd = (k.astype(F32) * kdec_ref[p]).astype(BF16)
        kvs[p, ci] = lax.dot_general(kd, v, (((0,), (0,)), ((), ())), preferred_element_type=F32)
    states = {}
    for p in range(HEAD_PAIRS):
        state = state_ref[p]
        for ci in range(chunks):
            states[p, ci] = state.astype(BF16)
            state = cdec_ref[p] * state + jnp.where(same_head, kvs[p, ci], 0.0)
        state_ref[p] = state
    for p, ci in items:
        q, _, v = operands(p, ci)
        inner = [jnp.dot((scores[p, ci, h] * dmat_ref[2 * p + h]).astype(BF16), v, preferred_element_type=F32)
                 for h in range(2)]
        cross = jnp.dot(q, states[p, ci], preferred_element_type=F32) * qdec_ref[p]
        y_ref[0, ci * C:(ci + 1) * C, p * LANES:(p + 1) * LANES] = (
            jnp.where(lane < RET_HEAD_DIM, inner[0], inner[1]) + cross)


def _ret_bwd_kernel(lg_ref, qkv_ref, y1_ref, g_ref, gnw_ref, o_ref, state_ref, qdec_ref, kdec_ref, cdec_ref, *,
                    chunks):
    C = RET_CHUNK

    @pl.when((pl.program_id(0) == 0) & (pl.program_id(1) == 0))
    def _build_tables():
        pos = lax.broadcasted_iota(jnp.int32, (C, LANES), 0).astype(F32)
        for p in range(HEAD_PAIRS):
            lg_lane = _pair_lane_value(lg_ref, 1, p, (C, LANES), 1)
            qdec_ref[p] = jnp.exp((float(C) - pos) * lg_lane)
            kdec_ref[p] = jnp.exp(pos * lg_lane)
            cdec_ref[p] = jnp.exp(float(C) * _pair_lane_value(lg_ref, 1, p, (LANES, LANES), 0))

    @pl.when(pl.program_id(1) == 0)
    def _reset():
        state_ref[...] = jnp.zeros(state_ref.shape, F32)

    same_head = _same_head_mask()
    head_avg = jnp.where(same_head, 1.0 / RET_HEAD_DIM, 0.0).astype(BF16)

    def head_mean(t):
        hi, lo = _split_hi_lo(t)
        return (jnp.dot(hi, head_avg, preferred_element_type=F32)
                + jnp.dot(lo, head_avg, preferred_element_type=F32))

    items = [(p, ci) for p in range(HEAD_PAIRS) for ci in range(chunks)]
    kvs = {}
    for p, ci in items:
        rows = slice(ci * C, (ci + 1) * C)
        k = qkv_ref[0, rows, RET_WIDTH + p * LANES:RET_WIDTH + (p + 1) * LANES]
        v = qkv_ref[0, rows, 2 * RET_WIDTH + p * LANES:2 * RET_WIDTH + (p + 1) * LANES]
        kd = (k.astype(F32) * kdec_ref[p]).astype(BF16)
        kvs[p, ci] = lax.dot_general(kd, v, (((0,), (0,)), ((), ())), preferred_element_type=F32)
    states = {}
    for p in range(HEAD_PAIRS):
        state = state_ref[p]
        for ci in reversed(range(chunks)):
            states[p, ci] = state.astype(BF16)
            state = cdec_ref[p] * state + jnp.where(same_head, kvs[p, ci], 0.0)
        state_ref[p] = state
    ys = {}
    for p, ci in items:
        rows = slice(ci * C, (ci + 1) * C)
        sl = slice(p * LANES, (p + 1) * LANES)
        cross = jnp.dot(qkv_ref[0, rows, sl], states[p, ci], preferred_element_type=F32)
        ys[p, ci] = y1_ref[0, rows, sl] + cross * qdec_ref[p]
    means = {key: head_mean(y) for key, y in ys.items()}
    devs = {key: ys[key] - means[key] for key in ys}
    variances = {key: head_mean(d * d) for key, d in devs.items()}
    for p, ci in items:
        rows = slice(ci * C, (ci + 1) * C)
        sl = slice(p * LANES, (p + 1) * LANES)
        g = g_ref[0, rows, sl]
        o = devs[p, ci] * lax.rsqrt(variances[p, ci] + EPS) * gnw_ref[:, sl] * (g * jax.nn.sigmoid(g))
        o_ref[0, rows, sl] = o.astype(o_ref.dtype)


def _retention(lg, rqkv, rg, gn_w):
    B, S, _ = rqkv.shape
    chunks = math.gcd(S // RET_CHUNK, 4)
    C = chunks * RET_CHUNK
    n = S // C
    smem = pl.BlockSpec(memory_space=pltpu.SMEM)
    table = pltpu.VMEM((HEAD_PAIRS, RET_CHUNK, LANES), F32)
    state = pltpu.VMEM((HEAD_PAIRS, LANES, LANES), F32)
    y1 = pl.pallas_call(
        functools.partial(_ret_fwd_kernel, chunks=chunks),
        grid=(B, n),
        in_specs=[smem, pl.BlockSpec((1, C, 3 * RET_WIDTH), lambda b, i: (b, i, 0))],
        out_specs=pl.BlockSpec((1, C, RET_WIDTH), lambda b, i: (b, i, 0)),
        out_shape=jax.ShapeDtypeStruct((B, S, RET_WIDTH), F32),
        scratch_shapes=[state, pltpu.VMEM((RET_HEADS, RET_CHUNK, RET_CHUNK), F32), table, table, state],
        compiler_params=_params("arbitrary", "arbitrary"),
        name="ret_fwd",
    )(lg, rqkv)
    rev = lambda b, i: (b, n - 1 - i, 0)
    return pl.pallas_call(
        functools.partial(_ret_bwd_kernel, chunks=chunks),
        grid=(B, n),
        in_specs=[
            smem,
            pl.BlockSpec((1, C, 3 * RET_WIDTH), rev),
            pl.BlockSpec((1, C, RET_WIDTH), rev),
            pl.BlockSpec((1, C, RET_WIDTH), rev),
            pl.BlockSpec((1, RET_WIDTH), lambda b, i: (0, 0)),
        ],
        out_specs=pl.BlockSpec((1, C, RET_WIDTH), rev),
        out_shape=jax.ShapeDtypeStruct((B, S, RET_WIDTH), BF16),
        scratch_shapes=[state, table, table, state],
        compiler_params=_params("arbitrary", "arbitrary"),
        name="ret_bwd",
    )(lg, rqkv, y1, rg, gn_w)


AUX_BLOCK = 256
AUX_ROWS = 16
V_ROWS = DIFF_V_DIM + SUBLANES
GUARD_LOG2 = 60.0
UNDERFLOW_LOG2 = 150.0


def _position_features(seq):
    j = jnp.arange(seq, dtype=jnp.int32)
    ones = jnp.ones((seq,), F32)
    n = (j // AUX_BLOCK).astype(F32)
    jc = (j % AUX_BLOCK - AUX_BLOCK // 2).astype(F32)
    feat = jnp.stack([ones] * 3 + [n] * 3 + [jc] * 3, axis=1)
    return jnp.pad(feat, ((0, 0), (0, LANES - feat.shape[1]))).astype(BF16)


def _split3(x):
    hi = x.astype(BF16).astype(F32)
    r = x - hi
    mid = r.astype(BF16).astype(F32)
    return hi, mid, (r - mid).astype(BF16).astype(F32)


def _attn_kernel(slopes_ref, pieces_ref, lam_ref, sw_ref, kn_ref, qt_ref, k_ref, aux_ref, vt_ref, o_ref,
                 m_ref, acc_ref, rhs_ref, pa_ref, pb_ref, dist_ref, sd_ref, *, seq, tq, tk, lam_init):
    pair = pl.program_id(1)
    qi = pl.program_id(2)
    q0 = qi * tq
    nk = seq // tk
    per = tq // tk
    kd0 = qi * per
    qt = qt_ref[0]
    row = lax.broadcasted_iota(jnp.int32, (LANES, tq), 0)
    q_masked = []
    for hc in range(4):
        lo = hc * DIFF_QK_DIM
        q_masked.append(jnp.where((row >= lo) & (row < lo + DIFF_QK_DIM), qt, jnp.zeros_like(qt)))
    rel = (lax.broadcasted_iota(jnp.int32, (tk, tq), 1) - lax.broadcasted_iota(jnp.int32, (tk, tq), 0)).astype(F32)

    @pl.when(qi == 0)
    def _distance_table():
        d = lax.broadcasted_iota(jnp.int32, (tq, tq), 1) - lax.broadcasted_iota(jnp.int32, (tq, tq), 0)
        dist_ref[...] = jnp.abs(d).astype(F32)

    kn2 = jnp.max(jnp.max(kn_ref[0], axis=0), axis=0, keepdims=True)
    kn_lane = lax.broadcasted_iota(jnp.int32, (1, LANES), 1)
    ipos = (q0 + lax.broadcasted_iota(jnp.int32, (1, tq), 1)).astype(F32)
    aug_row = lax.broadcasted_iota(jnp.int32, (AUX_ROWS, tq), 0)

    def v_rows_with_ones(h, k0, n):
        vt = vt_ref[0, h * DIFF_V_DIM:(h + 1) * DIFF_V_DIM, pl.ds(k0, n)]
        return jnp.concatenate([vt, jnp.ones((SUBLANES, n), BF16)], axis=0)

    kd = pl.multiple_of(q0, tq)
    kdiag = k_ref[0, pl.ds(kd, tq), :]
    for hc in range(4):
        s = jnp.dot(kdiag, q_masked[hc], preferred_element_type=F32)
        sd_ref[hc] = s - dist_ref[...] * slopes_ref[2 * pair + hc // 2]
    for hc in range(4):
        m_ref[hc] = jnp.max(sd_ref[hc], axis=0, keepdims=True)
    probs = [jnp.exp2(sd_ref[hc] - m_ref[hc]).astype(BF16) for hc in range(4)]
    for hc in range(4):
        acc_ref[hc] = jnp.dot(v_rows_with_ones(hc // 2, kd, tq), probs[hc], preferred_element_type=F32)

    for h in range(2):
        slope = slopes_ref[2 * pair + h]

        def v_with_ones(k0, h=h):
            return v_rows_with_ones(h, k0, tk)

        def online_step(kb, carry, h=h, slope=slope, v_with_ones=v_with_ones):
            k0 = pl.multiple_of(kb * tk, tk)
            kblk = k_ref[0, pl.ds(k0, tk), :]
            v_ones = v_with_ones(k0)
            bias = jnp.abs(rel + (q0 - k0).astype(F32)) * slope
            for c in range(2):
                hc = 2 * h + c
                s = jnp.dot(kblk, q_masked[hc], preferred_element_type=F32) - bias
                m_old = m_ref[hc]
                m_new = jnp.maximum(m_old, jnp.max(s, axis=0, keepdims=True))
                p = jnp.exp2(s - m_new).astype(BF16)
                acc_ref[hc] = jnp.exp2(m_old - m_new) * acc_ref[hc] + jnp.dot(v_ones, p, preferred_element_type=F32)
                m_ref[hc] = m_new
            return carry

        excess = None
        for c in range(2):
            hc = 2 * h + c
            qf = q_masked[hc].astype(F32)
            qn2 = jnp.sum(qf * qf, axis=0, keepdims=True)
            kn2_hc = jnp.max(jnp.where(kn_lane == 4 * pair + hc, kn2, 0.0), axis=-1, keepdims=True)
            bound = jnp.sqrt(qn2 * kn2_hc) * 1.01 + 1e-3
            e = jnp.max(bound - m_ref[hc], axis=-1, keepdims=True)
            excess = e if excess is None else jnp.maximum(excess, e)
        reach = (excess + UNDERFLOW_LOG2) / slope
        q0f = q0.astype(F32)
        lo_blk = jnp.clip(jnp.floor((q0f - (tk - 1.0) - reach) / tk), 0.0, float(nk))
        hi_blk = jnp.clip(jnp.floor((reach + q0f + (tq - 1.0)) / tk) + 1.0, 0.0, float(nk))
        lo_blk = jnp.minimum(lo_blk.astype(jnp.int32)[0, 0], kd0)
        hi_blk = jnp.maximum(hi_blk.astype(jnp.int32)[0, 0], kd0 + per)
        fast = jnp.where(excess <= GUARD_LOG2, 1, 0).astype(jnp.int32)[0, 0] == 1

        @pl.when(fast)
        def _single_pass(h=h, slope=slope, v_with_ones=v_with_ones, lo_blk=lo_blk, hi_blk=hi_blk):
            pieces = [pieces_ref[(2 * pair + h) * 3 + i] for i in range(3)]
            for c in range(2):
                hc = 2 * h + c
                for side, sign in ((0, 1.0), (1, -1.0)):
                    const = (-sign) * (slope * ipos) - m_ref[hc] + sign * (0.5 * AUX_BLOCK) * slope
                    c_hi, c_mid, c_lo = _split3(const)
                    aug = jnp.where(aug_row == 0, c_hi,
                                    jnp.where(aug_row == 1, c_mid, jnp.where(aug_row == 2, c_lo, 0.0)))
                    for i in range(3):
                        aug = jnp.where(aug_row == 3 + i, sign * AUX_BLOCK * pieces[i], aug)
                        aug = jnp.where(aug_row == 6 + i, sign * pieces[i], aug)
                    rhs_ref[side * 4 + hc, 0:LANES, :] = q_masked[hc]
                    rhs_ref[side * 4 + hc, LANES:LANES + AUX_ROWS, :] = aug.astype(BF16)
                    rhs_ref[side * 4 + hc, LANES + AUX_ROWS:, :] = jnp.zeros((LANES - AUX_ROWS, tq), BF16)

            odd = (kd0 - lo_blk + hi_blk - kd0 - per) % 2
            lo_even = jnp.where((odd == 1) & (lo_blk > 0), lo_blk - 1, lo_blk)
            hi_even = jnp.where((odd == 1) & (lo_blk == 0), hi_blk + 1, hi_blk)
            n_lo = kd0 - lo_even
            total = n_lo + hi_even - kd0 - per

            def block_of(i):
                return jnp.where(i < n_lo, lo_even + i, kd0 + per + i - n_lo)

            def stage_a(pair_idx, p_ref):
                for half in range(2):
                    i = 2 * pair_idx + half
                    k0 = pl.multiple_of(block_of(i) * tk, tk)
                    side = jnp.where(i < n_lo, 0, 1)
                    lhs = jnp.concatenate([k_ref[0, pl.ds(k0, tk), :], aux_ref[pl.ds(k0, tk), :]], axis=1)
                    for c in range(2):
                        t = jnp.dot(lhs, rhs_ref[side * 4 + 2 * h + c], preferred_element_type=F32)
                        p_ref[2 * half + c] = jnp.exp2(t).astype(BF16)

            def stage_b(pair_idx, p_ref):
                va = v_with_ones(pl.multiple_of(block_of(2 * pair_idx) * tk, tk))
                vb = v_with_ones(pl.multiple_of(block_of(2 * pair_idx + 1) * tk, tk))
                for c in range(2):
                    acc_ref[2 * h + c] += (jnp.dot(va, p_ref[c], preferred_element_type=F32)
                                           + jnp.dot(vb, p_ref[2 + c], preferred_element_type=F32))

            n_pairs = total // 2

            @pl.when(n_pairs > 0)
            def _blocks():
                stage_a(0, pa_ref)

                def two_pairs(j, carry):
                    stage_a(2 * j + 1, pb_ref)
                    stage_b(2 * j, pa_ref)
                    stage_a(2 * j + 2, pa_ref)
                    stage_b(2 * j + 1, pb_ref)
                    return carry

                lax.fori_loop(0, (n_pairs - 1) // 2, two_pairs, 0)

                @pl.when(n_pairs % 2 == 1)
                def _last_one():
                    stage_b(n_pairs - 1, pa_ref)

                @pl.when(n_pairs % 2 == 0)
                def _last_two():
                    stage_a(n_pairs - 1, pb_ref)
                    stage_b(n_pairs - 2, pa_ref)
                    stage_b(n_pairs - 1, pb_ref)

        @pl.when(jnp.logical_not(fast))
        def _online(online_step=online_step):
            lax.fori_loop(0, kd0, online_step, 0)
            lax.fori_loop(kd0 + per, nk, online_step, 0)

    lam = lam_ref[...]
    lam_full = (jnp.exp(jnp.sum(lam[0:1] * lam[1:2], axis=-1, keepdims=True))
                - jnp.exp(jnp.sum(lam[2:3] * lam[3:4], axis=-1, keepdims=True)) + lam_init)
    outs = []
    for h in range(2):
        a1 = acc_ref[2 * h]
        a2 = acc_ref[2 * h + 1]
        rows = slice(0, DIFF_V_DIM)
        ones_row = slice(DIFF_V_DIM, DIFF_V_DIM + 1)
        o = a1[rows] / a1[ones_row] - lam_full * (a2[rows] / a2[ones_row])
        o = o * lax.rsqrt(jnp.mean(o * o, axis=0, keepdims=True) + EPS)
        outs.append(o * sw_ref[...] * (1.0 - lam_init))
    o_ref[0] = jnp.concatenate(outs, axis=0).T.astype(o_ref.dtype)


def _diff_attention(slopes, pieces, lam, subln_w, kn, dqvt, dk, aux, layer_idx, tq, tk):
    B, S, _ = dk.shape
    assert tq % tk == 0 and tk <= AUX_BLOCK and S // AUX_BLOCK <= AUX_BLOCK
    lam_init = 0.8 - 0.6 * math.exp(-0.3 * layer_idx)
    kern = functools.partial(_attn_kernel, seq=S, tq=tq, tk=tk, lam_init=lam_init)
    v_block0 = DIFF_WIDTH // LANES
    smem = pl.BlockSpec(memory_space=pltpu.SMEM)
    return pl.pallas_call(
        kern,
        grid=(B, HEAD_PAIRS, S // tq),
        in_specs=[
            smem, smem,
            pl.BlockSpec((4, DIFF_QK_DIM), lambda b, j, i: (0, 0)),
            pl.BlockSpec((DIFF_V_DIM, 1), lambda b, j, i: (0, 0)),
            pl.BlockSpec((1,) + kn.shape[1:], lambda b, j, i: (b, 0, 0, 0)),
            pl.BlockSpec((1, LANES, tq), lambda b, j, i: (b, j, i)),
            pl.BlockSpec((1, S, LANES), lambda b, j, i: (b, 0, j)),
            pl.BlockSpec((S, LANES), lambda b, j, i: (0, 0)),
            pl.BlockSpec((1, LANES, S), lambda b, j, i: (b, v_block0 + j, 0)),
        ],
        out_specs=pl.BlockSpec((1, tq, LANES), lambda b, j, i: (b, i, j)),
        out_shape=jax.ShapeDtypeStruct((B, S, DIFF_WIDTH), BF16),
        scratch_shapes=[pltpu.VMEM((4, 1, tq), F32), pltpu.VMEM((4, V_ROWS, tq), F32),
                        pltpu.VMEM((8, 2 * LANES, tq), BF16),
                        pltpu.VMEM((4, tk, tq), BF16), pltpu.VMEM((4, tk, tq), BF16),
                        pltpu.VMEM((tq, tq), F32), pltpu.VMEM((4, tq, tq), F32)],
        compiler_params=_params("parallel", "parallel", "arbitrary"),
        name="diff_attn",
    )(slopes, pieces, lam, subln_w, kn, dqvt, dk, aux, dqvt)


def _out_proj_kernel(x_ref, ret_ref, diff_ref, cv_ref, prev_ref, next_ref, cw_ref, w_ref, o_ref, *, tm):
    i = pl.program_id(1)
    last = pl.num_programs(1) - 1
    cc = CONV_CHANNELS
    cv = cv_ref[0]
    u = cv[:, 2 * cc:3 * cc] * cv[:, 0:cc]
    prev = prev_ref[0]
    nxt = next_ref[0]
    u_prev = prev[SUBLANES - 1:SUBLANES, 2 * cc:3 * cc] * prev[SUBLANES - 1:SUBLANES, 0:cc]
    u_next = nxt[0:1, 2 * cc:3 * cc] * nxt[0:1, 0:cc]
    u_prev = jnp.where(i == 0, 0.0, u_prev)
    u_next = jnp.where(i == last, 0.0, u_next)
    rows = lax.broadcasted_iota(jnp.int32, (tm, cc), 0)
    u_m1 = jnp.where(rows == 0, u_prev, pltpu.roll(u, 1, 0))
    u_p1 = jnp.where(rows == tm - 1, u_next, pltpu.roll(u, tm - 1, 0))
    y = cw_ref[0:1] * u_m1 + cw_ref[1:2] * u + cw_ref[2:3] * u_p1
    conv = (cv[:, cc:2 * cc] * y).astype(BF16)
    acc = jnp.dot(ret_ref[0], w_ref[0:RET_WIDTH], preferred_element_type=F32)
    acc += jnp.dot(diff_ref[0], w_ref[RET_WIDTH:RET_WIDTH + DIFF_WIDTH], preferred_element_type=F32)
    acc += jnp.dot(conv, w_ref[RET_WIDTH + DIFF_WIDTH:], preferred_element_type=F32)
    o_ref[0] = x_ref[0] + acc


def _out_proj(x, ret, diff, cv, conv_w, w_out, tm):
    B, S, D = x.shape
    per = tm // SUBLANES
    nblk8 = S // SUBLANES
    tile = lambda width: pl.BlockSpec((1, tm, width), lambda b, i: (b, i, 0))
    return pl.pallas_call(
        functools.partial(_out_proj_kernel, tm=tm),
        grid=(B, S // tm),
        in_specs=[
            tile(D), tile(RET_WIDTH), tile(DIFF_WIDTH), tile(3 * CONV_CHANNELS),
            pl.BlockSpec((1, SUBLANES, 3 * CONV_CHANNELS), lambda b, i: (b, jnp.maximum(i * per - 1, 0), 0)),
            pl.BlockSpec((1, SUBLANES, 3 * CONV_CHANNELS),
                         lambda b, i: (b, jnp.minimum((i + 1) * per, nblk8 - 1), 0)),
            pl.BlockSpec((CONV_WIDTH, CONV_CHANNELS), lambda b, i: (0, 0)),
            pl.BlockSpec((D, D), lambda b, i: (0, 0)),
        ],
        out_specs=tile(D),
        out_shape=jax.ShapeDtypeStruct((B, S, D), F32),
        compiler_params=_params("parallel", "parallel"),
        name="out_proj",
    )(x, ret, diff, cv, cv, cv, conv_w, w_out)


_GROUP_LANE0 = N_EXPERTS
MOE_ROW_BLOCK = 256


def _router_gates(logits):
    lane = lax.broadcasted_iota(jnp.int32, logits.shape, 1)
    big = jnp.int32(LANES)
    is_group = (lane >= _GROUP_LANE0) & (lane < _GROUP_LANE0 + N_GROUPS)
    gl = jnp.where(is_group, logits, -jnp.inf)
    g_max = jnp.max(gl, axis=-1, keepdims=True)
    g_idx = jnp.min(jnp.where(gl == g_max, lane - _GROUP_LANE0, big), axis=-1, keepdims=True)
    g_w = 1.0 / jnp.sum(jnp.where(is_group, jnp.exp(gl - g_max), 0.0), axis=-1, keepdims=True)
    in_group = (lane >= g_idx * EXPERTS_PER_GROUP) & (lane < (g_idx + 1) * EXPERTS_PER_GROUP)
    el = jnp.where(in_group, logits, -jnp.inf)
    e1 = jnp.max(el, axis=-1, keepdims=True)
    i1 = jnp.min(jnp.where(el == e1, lane, big), axis=-1, keepdims=True)
    el2 = jnp.where(lane == i1, -jnp.inf, el)
    e2 = jnp.max(el2, axis=-1, keepdims=True)
    i2 = jnp.min(jnp.where(el2 == e2, lane, big), axis=-1, keepdims=True)
    r = jnp.exp(e2 - e1)
    p1 = g_w / (1.0 + r)
    p2 = g_w * r / (1.0 + r)
    return jnp.where(lane == i1, p1, jnp.where(lane == i2, p2, 0.0)), g_idx


def _moe_kernel(x_ref, nw_ref, rw_ref, rb_ref, wg_ref, wu_ref, wd_ref, fw_ref, o_ref,
                xs_ref, gs_ref, ys_ref, dest_ref, ltri_ref, seg_ref, *, final_norm, tm):
    tile = pl.program_id(0)
    e = pl.program_id(1)
    sorted_rows = tm + N_GROUPS * MOE_ROW_BLOCK

    @pl.when((tile == 0) & (e == 0))
    def _strictly_lower():
        r = lax.broadcasted_iota(jnp.int32, (tm, tm), 0)
        c = lax.broadcasted_iota(jnp.int32, (tm, tm), 1)
        ltri_ref[...] = jnp.where(c < r, 1.0, 0.0).astype(BF16)

    @pl.when(e == 0)
    def _route_and_sort():
        x = x_ref[...]
        xn = x * lax.rsqrt(jnp.mean(x * x, axis=-1, keepdims=True) + EPS) * nw_ref[...]
        hi, lo = _split_hi_lo(xn)
        r_hi, r_lo = _split_hi_lo(rw_ref[...])
        logits = (jnp.dot(hi, r_hi, preferred_element_type=F32)
                  + jnp.dot(lo, r_hi, preferred_element_type=F32)
                  + jnp.dot(hi, r_lo, preferred_element_type=F32)) + rb_ref[...]
        gates, g_idx = _router_gates(logits)
        lane = lax.broadcasted_iota(jnp.int32, (tm, LANES), 1)
        member = jnp.where(lane == g_idx, 1.0, 0.0)
        earlier = jnp.dot(ltri_ref[...], member.astype(BF16), preferred_element_type=F32)
        rank = jnp.sum(member * earlier, axis=-1, keepdims=True)
        count = jnp.sum(member, axis=0, keepdims=True)
        seg = jnp.floor((count + (MOE_ROW_BLOCK - 1.0)) * (1.0 / MOE_ROW_BLOCK)) * MOE_ROW_BLOCK
        lane1 = lax.broadcasted_iota(jnp.int32, (1, LANES), 1)
        s0, s1, s2 = seg[:, 0:1], seg[:, 1:2], seg[:, 2:3]
        start = jnp.where(lane1 == 1, s0, jnp.where(lane1 == 2, s0 + s1, jnp.where(lane1 == 3, s0 + s1 + s2, 0.0)))
        dest = rank + jnp.sum(member * start, axis=-1, keepdims=True)
        dest_ref[...] = jnp.broadcast_to(dest, (tm, LANES))
        dest_row = dest_ref[...].T[0:1, :]
        row_id = lax.broadcasted_iota(jnp.int32, (sorted_rows, tm), 0).astype(F32)
        perm = jnp.where(row_id == dest_row, 1.0, 0.0).astype(BF16)
        g_hi, g_mid, g_lo = _split3(gates)
        g_pieces = g_hi + pltpu.roll(g_mid, N_EXPERTS, 1) + pltpu.roll(g_lo, 2 * N_EXPERTS, 1)
        moved = jnp.dot(perm, jnp.concatenate([hi, g_pieces.astype(BF16)], axis=1), preferred_element_type=F32)
        xs_ref[...] = moved[:, :D_MODEL].astype(BF16)
        g_moved = moved[:, D_MODEL:]
        gs_ref[...] = (g_moved + pltpu.roll(g_moved, LANES - N_EXPERTS, 1)
                       + pltpu.roll(g_moved, LANES - 2 * N_EXPERTS, 1))
        ys_ref[...] = jnp.zeros(ys_ref.shape, F32)
        start_i = start.astype(jnp.int32)
        blocks_i = (seg * (1.0 / MOE_ROW_BLOCK)).astype(jnp.int32)
        for g in range(N_GROUPS):
            seg_ref[g] = start_i[0, g]
            seg_ref[N_GROUPS + g] = blocks_i[0, g]

    group = e // EXPERTS_PER_GROUP
    first_row = seg_ref[group]

    def row_block(b, carry):
        r0 = pl.multiple_of(first_row + b * MOE_ROW_BLOCK, MOE_ROW_BLOCK)
        xb = xs_ref[pl.ds(r0, MOE_ROW_BLOCK), :]
        gate = jnp.dot(xb, wg_ref[0], preferred_element_type=F32)
        up = jnp.dot(xb, wu_ref[0], preferred_element_type=F32)
        hmid = (gate * jax.nn.sigmoid(gate) * up).astype(BF16)
        gsb = gs_ref[pl.ds(r0, MOE_ROW_BLOCK), :]
        lane = lax.broadcasted_iota(jnp.int32, gsb.shape, 1)
        g_e = jnp.sum(jnp.where(lane == e, gsb, 0.0), axis=-1, keepdims=True)
        ys_ref[pl.ds(r0, MOE_ROW_BLOCK), :] += g_e * jnp.dot(hmid, wd_ref[0], preferred_element_type=F32)
        return carry

    lax.fori_loop(0, seg_ref[N_GROUPS + group], row_block, 0)

    @pl.when(e == pl.num_programs(1) - 1)
    def _finish():
        col_id = lax.broadcasted_iota(jnp.int32, (tm, sorted_rows), 1).astype(F32)
        unperm = jnp.where(col_id == dest_ref[:, 0:1], 1.0, 0.0).astype(BF16)
        y = x_ref[...] + jnp.dot(unperm, ys_ref[...].astype(BF16), preferred_element_type=F32)
        if final_norm:
            y = y * lax.rsqrt(jnp.mean(y * y, axis=-1, keepdims=True) + EPS) * fw_ref[...]
        o_ref[...] = y


def _moe(x, nw, r_w, r_b, wg, wu, wd, final_w, final_norm, tm):
    T, D = x.shape
    const = lambda shape: pl.BlockSpec(shape, lambda i, e: (0,) * len(shape))
    sorted_rows = tm + N_GROUPS * MOE_ROW_BLOCK
    return pl.pallas_call(
        functools.partial(_moe_kernel, final_norm=final_norm, tm=tm),
        grid=(T // tm, N_EXPERTS),
        in_specs=[
            pl.BlockSpec((tm, D), lambda i, e: (i, 0)),
            const((1, D)), const((D, LANES)), const((1, LANES)),
            pl.BlockSpec((1, D, EXPERT_FF), lambda i, e: (e, 0, 0)),
            pl.BlockSpec((1, D, EXPERT_FF), lambda i, e: (e, 0, 0)),
            pl.BlockSpec((1, EXPERT_FF, D), lambda i, e: (e, 0, 0)),
            const((1, D)),
        ],
        out_specs=pl.BlockSpec((tm, D), lambda i, e: (i, 0)),
        out_shape=jax.ShapeDtypeStruct((T, D), F32),
        scratch_shapes=[pltpu.VMEM((sorted_rows, D), BF16), pltpu.VMEM((sorted_rows, LANES), F32),
                        pltpu.VMEM((sorted_rows, D), F32), pltpu.VMEM((tm, LANES), F32),
                        pltpu.VMEM((tm, tm), BF16), pltpu.SMEM((2 * N_GROUPS,), jnp.int32)],
        compiler_params=_params("arbitrary", "arbitrary"),
        name="moe",
    )(x, nw, r_w, r_b, wg, wu, wd, final_w)


def _tile(n, pref):
    t = min(n, pref)
    assert n % t == 0, (n, t)
    return t


def _prep_layer(l, norm1_w, w_in, ret_decay_logit, ret_gn_w, diff_lambda, diff_subln_w, conv_w, w_out, norm2_w,
                router_group_w, router_group_b, router_expert_w, router_expert_b, expert_w_gate, expert_w_up,
                expert_w_down):
    w = w_in[l]
    w_nat = jnp.concatenate([w[:, :_OFF_DQ], w[:, _OFF_DK:_OFF_DV], w[:, _OFF_CONV:]], axis=1).astype(BF16)
    w_t = jnp.concatenate([w[:, _OFF_DQ:_OFF_DK], w[:, _OFF_DV:_OFF_CONV]], axis=1).T.astype(BF16)
    router = jnp.zeros((D_MODEL, LANES), F32)
    router = router.at[:, :N_EXPERTS].set(router_expert_w[l]).at[:, N_EXPERTS:N_EXPERTS + N_GROUPS].set(
        router_group_w[l])
    r_b = jnp.zeros((1, LANES), F32).at[0, :N_EXPERTS].set(router_expert_b[l]).at[
        0, N_EXPERTS:N_EXPERTS + N_GROUPS].set(router_group_b[l])
    return dict(
        norm1=norm1_w[l][None, :], w_nat=w_nat, w_t=w_t,
        lg=jax.nn.log_sigmoid(ret_decay_logit[l].astype(F32)),
        gn_w=ret_gn_w[l][None, :].astype(F32),
        lam=diff_lambda[l].astype(F32), subln=diff_subln_w[l][:, None].astype(F32),
        conv_w=conv_w[l].astype(F32), w_out=w_out[l].astype(BF16),
        norm2=norm2_w[l][None, :], r_w=router, r_b=r_b,
        wg=expert_w_gate[l].astype(BF16), wu=expert_w_up[l].astype(BF16), wd=expert_w_down[l].astype(BF16),
    )


def _trunk(x, layers, final_w, slopes, pieces):
    B, S, D = x.shape
    tm = _tile(S, 512)
    tq = _tile(S, 512)
    tk = _tile(S, 256)
    t_moe = _tile(B * S, 1024)
    aux = _position_features(S)
    for l, lw in enumerate(layers):
        rqkv, rg, dk, dqvt, cv, kn = _in_proj(x, lw["norm1"], lw["w_nat"], lw["w_t"], tm)
        ret = _retention(lw["lg"], rqkv, rg, lw["gn_w"])
        diff = _diff_attention(slopes, pieces, lw["lam"], lw["subln"], kn, dqvt, dk, aux, l, tq, tk)
        x = _out_proj(x, ret, diff, cv, lw["conv_w"], lw["w_out"], tm)
        last = l == len(layers) - 1
        x = _moe(x.reshape(B * S, D), lw["norm2"], lw["r_w"], lw["r_b"], lw["wg"], lw["wu"],
                 lw["wd"], final_w, last, t_moe).reshape(B, S, D)
    return x


def kernel(x_prompt, x_sample, norm1_w, w_in, ret_decay_logit, ret_gn_w, diff_lambda, diff_subln_w, conv_w, w_out,
           norm2_w, router_group_w, router_group_b, router_expert_w, router_expert_b, expert_w_gate, expert_w_up,
           expert_w_down, final_norm_w):
    depth = w_in.shape[0]
    layers = [
        _prep_layer(l, norm1_w, w_in, ret_decay_logit, ret_gn_w, diff_lambda, diff_subln_w, conv_w, w_out, norm2_w,
                    router_group_w, router_group_b, router_expert_w, router_expert_b, expert_w_gate, expert_w_up,
                    expert_w_down)
        for l in range(depth)
    ]
    final_w = final_norm_w[None, :].astype(F32)
    slopes = (np.float32(2.0) ** (np.float32(-8.0) * np.arange(1, DIFF_HEADS + 1, dtype=np.float32)
                                  / np.float32(DIFF_HEADS))) * np.float32(LOG2E)
    s_hi = slopes.astype(BF16).astype(np.float32)
    s_mid = (slopes - s_hi).astype(BF16).astype(np.float32)
    s_lo = (slopes - s_hi - s_mid).astype(BF16).astype(np.float32)
    pieces = jnp.asarray(np.stack([s_hi, s_mid, s_lo], axis=1).reshape(-1))
    slopes = jnp.asarray(slopes)
    return (_trunk(x_prompt, layers, final_w, slopes, pieces), _trunk(x_sample, layers, final_w, slopes, pieces))
```

```python
import functools
import math

import jax
import jax.numpy as jnp
import numpy as np
from jax import lax
from jax.experimental import pallas as pl
from jax.experimental.pallas import tpu as pltpu

F32 = jnp.float32
BF16 = jnp.bfloat16

D_MODEL = 1024
RET_HEADS = 6
RET_HEAD_DIM = 64
RET_WIDTH = RET_HEADS * RET_HEAD_DIM
DIFF_HEADS = 6
DIFF_QK_DIM = 32
DIFF_V_DIM = 64
DIFF_WIDTH = DIFF_HEADS * DIFF_V_DIM
CONV_CHANNELS = 256
CONV_WIDTH = 3
N_GROUPS = 4
EXPERTS_PER_GROUP = 4
N_EXPERTS = 16
EXPERT_FF = 512
EPS = 1e-6

LANES = 128
HEAD_PAIRS = RET_HEADS // 2
RET_CHUNK = 128
SUBLANES = 8
VMEM_LIMIT = 56 * 1024 * 1024
LOG2E = math.log2(math.e)
NEG_BIG = -1e30

_OFF_RG = 3 * RET_WIDTH
_OFF_DQ = 4 * RET_WIDTH
_OFF_DK = _OFF_DQ + DIFF_WIDTH
_OFF_DV = _OFF_DK + DIFF_WIDTH
_OFF_CONV = _OFF_DV + DIFF_WIDTH


def _params(*sem):
    return pltpu.CompilerParams(dimension_semantics=sem, vmem_limit_bytes=VMEM_LIMIT)


def _split_hi_lo(x):
    hi = x.astype(BF16)
    lo = (x - hi.astype(F32)).astype(BF16)
    return hi, lo


def _in_proj_kernel(x_ref, nw_ref, w_ref, wt_ref, rqkv_ref, rg_ref, dk_ref, dqvt_ref, cv_ref, kn_ref):
    x = x_ref[0]
    ms = jnp.mean(x * x, axis=-1, keepdims=True)
    h = (x * lax.rsqrt(ms + EPS) * nw_ref[...]).astype(BF16)
    a = jnp.dot(h, w_ref[:, 0:_OFF_DQ], preferred_element_type=F32)
    rqkv_ref[0, :, 0:RET_WIDTH] = a[:, 0:RET_WIDTH].astype(BF16)
    rqkv_ref[0, :, RET_WIDTH:2 * RET_WIDTH] = (a[:, RET_WIDTH:2 * RET_WIDTH] * RET_HEAD_DIM ** -0.5).astype(BF16)
    rqkv_ref[0, :, 2 * RET_WIDTH:3 * RET_WIDTH] = a[:, 2 * RET_WIDTH:3 * RET_WIDTH].astype(BF16)
    rg_ref[0] = a[:, _OFF_RG:_OFF_DQ]
    dk = jnp.dot(h, w_ref[:, _OFF_DQ:_OFF_DQ + DIFF_WIDTH], preferred_element_type=F32).astype(BF16)
    dk_ref[0] = dk
    dkf = dk.astype(F32)
    sq_hi, sq_lo = _split_hi_lo(dkf * dkf)
    grp = (lax.shift_right_logical(lax.broadcasted_iota(jnp.int32, (DIFF_WIDTH, LANES), 0), 5)
           == lax.broadcasted_iota(jnp.int32, (DIFF_WIDTH, LANES), 1))
    grp = jnp.where(grp, 1.0, 0.0).astype(BF16)
    kn2 = jnp.dot(sq_hi, grp, preferred_element_type=F32) + jnp.dot(sq_lo, grp, preferred_element_type=F32)
    kn_ref[0, 0] = jnp.broadcast_to(jnp.max(kn2, axis=0, keepdims=True), (SUBLANES, LANES))
    cv_ref[0] =jnp.dot(h, w_ref[:, _OFF_DQ + DIFF_WIDTH:], preferred_element_type=F32)
    t = lax.dot_general(wt_ref[...], h, (((1,), (1,)), ((), ())), preferred_element_type=F32)
    dqvt_ref[0, 0:DIFF_WIDTH, :] = (t[0:DIFF_WIDTH] * (DIFF_QK_DIM ** -0.5 * LOG2E)).astype(BF16)
    dqvt_ref[0, DIFF_WIDTH:, :] = t[DIFF_WIDTH:].astype(BF16)


def _in_proj(x, nw, w_nat, w_t, tm):
    B, S, D = x.shape
    n_nat = w_nat.shape[1]
    return pl.pallas_call(
        _in_proj_kernel,
        grid=(B, S // tm),
        in_specs=[
            pl.BlockSpec((1, tm, D), lambda b, i: (b, i, 0)),
            pl.BlockSpec((1, D), lambda b, i: (0, 0)),
            pl.BlockSpec((D, n_nat), lambda b, i: (0, 0)),
            pl.BlockSpec((2 * DIFF_WIDTH, D), lambda b, i: (0, 0)),
        ],
        out_specs=[
            pl.BlockSpec((1, tm, 3 * RET_WIDTH), lambda b, i: (b, i, 0)),
            pl.BlockSpec((1, tm, RET_WIDTH), lambda b, i: (b, i, 0)),
            pl.BlockSpec((1, tm, DIFF_WIDTH), lambda b, i: (b, i, 0)),
            pl.BlockSpec((1, 2 * DIFF_WIDTH, tm), lambda b, i: (b, 0, i)),
            pl.BlockSpec((1, tm, 3 * CONV_CHANNELS), lambda b, i: (b, i, 0)),
            pl.BlockSpec((1, 1, SUBLANES, LANES), lambda b, i: (b, i, 0, 0)),
        ],
        out_shape=[
            jax.ShapeDtypeStruct((B, S, 3 * RET_WIDTH), BF16),
            jax.ShapeDtypeStruct((B, S, RET_WIDTH), F32),
            jax.ShapeDtypeStruct((B, S, DIFF_WIDTH), BF16),
            jax.ShapeDtypeStruct((B, 2 * DIFF_WIDTH, S), BF16),
            jax.ShapeDtypeStruct((B, S, 3 * CONV_CHANNELS), F32),
            jax.ShapeDtypeStruct((B, S // tm, SUBLANES, LANES), F32),
        ],
        compiler_params=_params("parallel", "parallel"),
        name="in_proj",
    )(x, nw, w_nat, w_t)


def _pair_lane_value(lg_ref, direction, pair, shape, axis):
    idx = lax.broadcasted_iota(jnp.int32, shape, axis)
    return jnp.where(idx < RET_HEAD_DIM, lg_ref[direction, 2 * pair], lg_ref[direction, 2 * pair + 1])


def _same_head_mask():
    r = lax.broadcasted_iota(jnp.int32, (LANES, LANES), 0)
    c = lax.broadcasted_iota(jnp.int32, (LANES, LANES), 1)
    return (r < RET_HEAD_DIM) == (c < RET_HEAD_DIM)


def _ret_fwd_kernel(lg_ref, qkv_ref, y_ref, state_ref, dmat_ref, qdec_ref, kdec_ref, cdec_ref, *, chunks):
    C = RET_CHUNK

    @pl.when((pl.program_id(0) == 0) & (pl.program_id(1) == 0))
    def _build_tables():
        diff = (lax.broadcasted_iota(jnp.int32, (C, C), 0) - lax.broadcasted_iota(jnp.int32, (C, C), 1)).astype(F32)
        for h in range(RET_HEADS):
            lower = jnp.exp(jnp.maximum(diff, 0.0) * lg_ref[0, h])
            upper = jnp.exp(jnp.maximum(-diff, 0.0) * lg_ref[1, h])
            dmat_ref[h] = jnp.where(diff >= 0, lower, upper)
        pos = lax.broadcasted_iota(jnp.int32, (C, LANES), 0).astype(F32)
        for p in range(HEAD_PAIRS):
            lg_lane = _pair_lane_value(lg_ref, 0, p, (C, LANES), 1)
            qdec_ref[p] = jnp.exp((pos + 1.0) * lg_lane)
            kdec_ref[p] = jnp.exp((C - 1.0 - pos) * lg_lane)
            cdec_ref[p] = jnp.exp(float(C) * _pair_lane_value(lg_ref, 0, p, (LANES, LANES), 0))

    @pl.when(pl.program_id(1) == 0)
    def _reset():
        state_ref[...] = jnp.zeros(state_ref.shape, F32)

    lane = lax.broadcasted_iota(jnp.int32, (C, LANES), 1)
    same_head = _same_head_mask()
    def operands(p, ci):
        rows = slice(ci * C, (ci + 1) * C)
        return (qkv_ref[0, rows, p * LANES:(p + 1) * LANES],
                qkv_ref[0, rows, RET_WIDTH + p * LANES:RET_WIDTH + (p + 1) * LANES],
                qkv_ref[0, rows, 2 * RET_WIDTH + p * LANES:2 * RET_WIDTH + (p + 1) * LANES])

    items = [(p, ci) for p in range(HEAD_PAIRS) for ci in range(chunks)]
    scores, kvs = {}, {}
    for p, ci in items:
        q, k, v = operands(p, ci)
        for h in range(2):
            head_lanes = (lane >= h * RET_HEAD_DIM) & (lane < (h + 1) * RET_HEAD_DIM)
            qm = jnp.where(head_lanes, q, jnp.zeros_like(q))
            scores[p, ci, h] = lax.dot_general(qm, k, (((1,), (1,)), ((), ())), preferred_element_type=F32)
        kd = (k.astype(F32) * kdec_ref[p]).astype(BF16)
        kvs[p, ci] = lax.dot_general(kd, v, (((0,), (0,)), ((), ())), preferred_element_type=F32)
    states = {}
    for p in range(HEAD_PAIRS):
        state = state_ref[p]
        for ci in range(chunks):
            states[p, ci] = state.astype(BF16)
            state = cdec_ref[p] * state + jnp.where(same_head, kvs[p, ci], 0.0)
        state_ref[p] = state
    for p, ci in items:
        q, _, v = operands(p, ci)
        inner = [jnp.dot((scores[p, ci, h] * dmat_ref[2 * p + h]).astype(BF16), v, preferred_element_type=F32)
                 for h in range(2)]
        cross = jnp.dot(q, states[p, ci], preferred_element_type=F32) * qdec_ref[p]
        y_ref[0, ci * C:(ci + 1) * C, p * LANES:(p + 1) * LANES] = (
            jnp.where(lane < RET_HEAD_DIM, inner[0], inner[1]) + cross)


def _ret_bwd_kernel(lg_ref, qkv_ref, y1_ref, g_ref, gnw_ref, o_ref, state_ref, qdec_ref, kdec_ref, cdec_ref, *,
                    chunks):
    C = RET_CHUNK

    @pl.when((pl.program_id(0) == 0) & (pl.program_id(1) == 0))
    def _build_tables():
        pos = lax.broadcasted_iota(jnp.int32, (C, LANES), 0).astype(F32)
        for p in range(HEAD_PAIRS):
            lg_lane = _pair_lane_value(lg_ref, 1, p, (C, LANES), 1)
            qdec_ref[p] = jnp.exp((float(C) - pos) * lg_lane)
            kdec_ref[p] = jnp.exp(pos * lg_lane)
            cdec_ref[p] = jnp.exp(float(C) * _pair_lane_value(lg_ref, 1, p, (LANES, LANES), 0))

    @pl.when(pl.program_id(1) == 0)
    def _reset():
        state_ref[...] = jnp.zeros(state_ref.shape, F32)

    same_head = _same_head_mask()
    head_avg = jnp.where(same_head, 1.0 / RET_HEAD_DIM, 0.0).astype(BF16)

    def head_mean(t):
        hi, lo = _split_hi_lo(t)
        return (jnp.dot(hi, head_avg, preferred_element_type=F32)
                + jnp.dot(lo, head_avg, preferred_element_type=F32))

    items = [(p, ci) for p in range(HEAD_PAIRS) for ci in range(chunks)]
    kvs = {}
    for p, ci in items:
        rows = slice(ci * C, (ci + 1) * C)
        k = qkv_ref[0, rows, RET_WIDTH + p * LANES:RET_WIDTH + (p + 1) * LANES]
        v = qkv_ref[0, rows, 2 * RET_WIDTH + p * LANES:2 * RET_WIDTH + (p + 1) * LANES]
        kd = (k.astype(F32) * kdec_ref[p]).astype(BF16)
        kvs[p, ci] = lax.dot_general(kd, v, (((0,), (0,)), ((), ())), preferred_element_type=F32)
    states = {}
    for p in range(HEAD_PAIRS):
        state = state_ref[p]
        for ci in reversed(range(chunks)):
            states[p, ci] = state.astype(BF16)
            state = cdec_ref[p] * state + jnp.where(same_head, kvs[p, ci], 0.0)
        state_ref[p] = state
    ys = {}
    for p, ci in items:
        rows = slice(ci * C, (ci + 1) * C)
        sl = slice(p * LANES, (p + 1) * LANES)
        cross = jnp.dot(qkv_ref[0, rows, sl], states[p, ci], preferred_element_type=F32)
        ys[p, ci] = y1_ref[0, rows, sl] + cross * qdec_ref[p]
    means = {key: head_mean(y) for key, y in ys.items()}
    devs = {key: ys[key] - means[key] for key in ys}
    variances = {key: head_mean(d * d) for key, d in devs.items()}
    for p, ci in items:
        rows = slice(ci * C, (ci + 1) * C)
        sl = slice(p * LANES, (p + 1) * LANES)
        g = g_ref[0, rows, sl]
        o = devs[p, ci] * lax.rsqrt(variances[p, ci] + EPS) * gnw_ref[:, sl] * (g * jax.nn.sigmoid(g))
        o_ref[0, rows, sl] = o.astype(o_ref.dtype)


def _retention(lg, rqkv, rg, gn_w):
    B, S, _ = rqkv.shape
    chunks = math.gcd(S // RET_CHUNK, 4)
    C = chunks * RET_CHUNK
    n = S // C
    smem = pl.BlockSpec(memory_space=pltpu.SMEM)
    table = pltpu.VMEM((HEAD_PAIRS, RET_CHUNK, LANES), F32)
    state = pltpu.VMEM((HEAD_PAIRS, LANES, LANES), F32)
    y1 = pl.pallas_call(
        functools.partial(_ret_fwd_kernel, chunks=chunks),
        grid=(B, n),
        in_specs=[smem, pl.BlockSpec((1, C, 3 * RET_WIDTH), lambda b, i: (b, i, 0))],
        out_specs=pl.BlockSpec((1, C, RET_WIDTH), lambda b, i: (b, i, 0)),
        out_shape=jax.ShapeDtypeStruct((B, S, RET_WIDTH), F32),
        scratch_shapes=[state, pltpu.VMEM((RET_HEADS, RET_CHUNK, RET_CHUNK), F32), table, table, state],
        compiler_params=_params("arbitrary", "arbitrary"),
        name="ret_fwd",
    )(lg, rqkv)
    rev = lambda b, i: (b, n - 1 - i, 0)
    return pl.pallas_call(
        functools.partial(_ret_bwd_kernel, chunks=chunks),
        grid=(B, n),
        in_specs=[
            smem,
            pl.BlockSpec((1, C, 3 * RET_WIDTH), rev),
            pl.BlockSpec((1, C, RET_WIDTH), rev),
            pl.BlockSpec((1, C, RET_WIDTH), rev),
            pl.BlockSpec((1, RET_WIDTH), lambda b, i: (0, 0)),
        ],
        out_specs=pl.BlockSpec((1, C, RET_WIDTH), rev),
        out_shape=jax.ShapeDtypeStruct((B, S, RET_WIDTH), BF16),
        scratch_shapes=[state, table, table, state],
        compiler_params=_params("arbitrary", "arbitrary"),
        name="ret_bwd",
    )(lg, rqkv, y1, rg, gn_w)


AUX_BLOCK = 256
AUX_ROWS = 16
V_ROWS = DIFF_V_DIM + SUBLANES
GUARD_LOG2 = 60.0
UNDERFLOW_LOG2 = 150.0


def _position_features(seq):
    j = jnp.arange(seq, dtype=jnp.int32)
    ones = jnp.ones((seq,), F32)
    n = (j // AUX_BLOCK).astype(F32)
    jc = (j % AUX_BLOCK - AUX_BLOCK // 2).astype(F32)
    feat = jnp.stack([ones] * 3 + [n] * 3 + [jc] * 3, axis=1)
    return jnp.pad(feat, ((0, 0), (0, LANES - feat.shape[1]))).astype(BF16)


def _split3(x):
    hi = x.astype(BF16).astype(F32)
    r = x - hi
    mid = r.astype(BF16).astype(F32)
    return hi, mid, (r - mid).astype(BF16).astype(F32)


def _attn_kernel(slopes_ref, pieces_ref, lam_ref, sw_ref, kn_ref, qt_ref, k_ref, aux_ref, vt_ref, o_ref,
                 m_ref, acc_ref, rhs_ref, pa_ref, pb_ref, dist_ref, sd_ref, *, seq, tq, tk, lam_init):
    pair = pl.program_id(1)
    qi = pl.program_id(2)
    q0 = qi * tq
    nk = seq // tk
    per = tq // tk
    kd0 = qi * per
    qt = qt_ref[0]
    row = lax.broadcasted_iota(jnp.int32, (LANES, tq), 0)
    q_masked = []
    for hc in range(4):
        lo = hc * DIFF_QK_DIM
        q_masked.append(jnp.where((row >= lo) & (row < lo + DIFF_QK_DIM), qt, jnp.zeros_like(qt)))
    rel = (lax.broadcasted_iota(jnp.int32, (tk, tq), 1) - lax.broadcasted_iota(jnp.int32, (tk, tq), 0)).astype(F32)

    @pl.when(qi == 0)
    def _distance_table():
        d = lax.broadcasted_iota(jnp.int32, (tq, tq), 1) - lax.broadcasted_iota(jnp.int32, (tq, tq), 0)
        dist_ref[...] = jnp.abs(d).astype(F32)

    kn2 = jnp.max(jnp.max(kn_ref[0], axis=0), axis=0, keepdims=True)
    kn_lane = lax.broadcasted_iota(jnp.int32, (1, LANES), 1)
    ipos = (q0 + lax.broadcasted_iota(jnp.int32, (1, tq), 1)).astype(F32)
    aug_row = lax.broadcasted_iota(jnp.int32, (AUX_ROWS, tq), 0)

    def v_rows_with_ones(h, k0, n):
        vt = vt_ref[0, h * DIFF_V_DIM:(h + 1) * DIFF_V_DIM, pl.ds(k0, n)]
        return jnp.concatenate([vt, jnp.ones((SUBLANES, n), BF16)], axis=0)

    kd = pl.multiple_of(q0, tq)
    kdiag = k_ref[0, pl.ds(kd, tq), :]
    for hc in range(4):
        s = jnp.dot(kdiag, q_masked[hc], preferred_element_type=F32)
        sd_ref[hc] = s - dist_ref[...] * slopes_ref[2 * pair + hc // 2]
    for hc in range(4):
        m_ref[hc] = jnp.max(sd_ref[hc], axis=0, keepdims=True)
    probs = [jnp.exp2(sd_ref[hc] - m_ref[hc]).astype(BF16) for hc in range(4)]
    for hc in range(4):
        acc_ref[hc] = jnp.dot(v_rows_with_ones(hc // 2, kd, tq), probs[hc], preferred_element_type=F32)

    for h in range(2):
        slope = slopes_ref[2 * pair + h]

        def v_with_ones(k0, h=h):
            return v_rows_with_ones(h, k0, tk)

        def online_step(kb, carry, h=h, slope=slope, v_with_ones=v_with_ones):
            k0 = pl.multiple_of(kb * tk, tk)
            kblk = k_ref[0, pl.ds(k0, tk), :]
            v_ones = v_with_ones(k0)
            bias = jnp.abs(rel + (q0 - k0).astype(F32)) * slope
            for c in range(2):
                hc = 2 * h + c
                s = jnp.dot(kblk, q_masked[hc], preferred_element_type=F32) - bias
                m_old = m_ref[hc]
                m_new = jnp.maximum(m_old, jnp.max(s, axis=0, keepdims=True))
                p = jnp.exp2(s - m_new).astype(BF16)
                acc_ref[hc] = jnp.exp2(m_old - m_new) * acc_ref[hc] + jnp.dot(v_ones, p, preferred_element_type=F32)
                m_ref[hc] = m_new
            return carry

        excess = None
        for c in range(2):
            hc = 2 * h + c
            qf = q_masked[hc].astype(F32)
            qn2 = jnp.sum(qf * qf, axis=0, keepdims=True)
            kn2_hc = jnp.max(jnp.where(kn_lane == 4 * pair + hc, kn2, 0.0), axis=-1, keepdims=True)
            bound = jnp.sqrt(qn2 * kn2_hc) * 1.01 + 1e-3
            e = jnp.max(bound - m_ref[hc], axis=-1, keepdims=True)
            excess = e if excess is None else jnp.maximum(excess, e)
        reach = (excess + UNDERFLOW_LOG2) / slope
        q0f = q0.astype(F32)
        lo_blk = jnp.clip(jnp.floor((q0f - (tk - 1.0) - reach) / tk), 0.0, float(nk))
        hi_blk = jnp.clip(jnp.floor((reach + q0f + (tq - 1.0)) / tk) + 1.0, 0.0, float(nk))
        lo_blk = jnp.minimum(lo_blk.astype(jnp.int32)[0, 0], kd0)
        hi_blk = jnp.maximum(hi_blk.astype(jnp.int32)[0, 0], kd0 + per)
        fast = jnp.where(excess <= GUARD_LOG2, 1, 0).astype(jnp.int32)[0, 0] == 1

        @pl.when(fast)
        def _single_pass(h=h, slope=slope, v_with_ones=v_with_ones, lo_blk=lo_blk, hi_blk=hi_blk):
            pieces = [pieces_ref[(2 * pair + h) * 3 + i] for i in range(3)]
            for c in range(2):
                hc = 2 * h + c
                for side, sign in ((0, 1.0), (1, -1.0)):
                    const = (-sign) * (slope * ipos) - m_ref[hc] + sign * (0.5 * AUX_BLOCK) * slope
                    c_hi, c_mid, c_lo = _split3(const)
                    aug = jnp.where(aug_row == 0, c_hi,
                                    jnp.where(aug_row == 1, c_mid, jnp.where(aug_row == 2, c_lo, 0.0)))
                    for i in range(3):
                        aug = jnp.where(aug_row == 3 + i, sign * AUX_BLOCK * pieces[i], aug)
                        aug = jnp.where(aug_row == 6 + i, sign * pieces[i], aug)
                    rhs_ref[side * 4 + hc, 0:LANES, :] = q_masked[hc]
                    rhs_ref[side * 4 + hc, LANES:LANES + AUX_ROWS, :] = aug.astype(BF16)
                    rhs_ref[side * 4 + hc, LANES + AUX_ROWS:, :] = jnp.zeros((LANES - AUX_ROWS, tq), BF16)

            odd = (kd0 - lo_blk + hi_blk - kd0 - per) % 2
            lo_even = jnp.where((odd == 1) & (lo_blk > 0), lo_blk - 1, lo_blk)
            hi_even = jnp.where((odd == 1) & (lo_blk == 0), hi_blk + 1, hi_blk)
            n_lo = kd0 - lo_even
            total = n_lo + hi_even - kd0 - per

            def block_of(i):
                return jnp.where(i < n_lo, lo_even + i, kd0 + per + i - n_lo)

            def stage_a(pair_idx, p_ref):
                for half in range(2):
                    i = 2 * pair_idx + half
                    k0 = pl.multiple_of(block_of(i) * tk, tk)
                    side = jnp.where(i < n_lo, 0, 1)
                    lhs = jnp.concatenate([k_ref[0, pl.ds(k0, tk), :], aux_ref[pl.ds(k0, tk), :]], axis=1)
                    for c in range(2):
                        t = jnp.dot(lhs, rhs_ref[side * 4 + 2 * h + c], preferred_element_type=F32)
                        p_ref[2 * half + c] = jnp.exp2(t).astype(BF16)

            def stage_b(pair_idx, p_ref):
                va = v_with_ones(pl.multiple_of(block_of(2 * pair_idx) * tk, tk))
                vb = v_with_ones(pl.multiple_of(block_of(2 * pair_idx + 1) * tk, tk))
                for c in range(2):
                    acc_ref[2 * h + c] += (jnp.dot(va, p_ref[c], preferred_element_type=F32)
                                           + jnp.dot(vb, p_ref[2 + c], preferred_element_type=F32))

            n_pairs = total // 2

            @pl.when(n_pairs > 0)
            def _blocks():
                stage_a(0, pa_ref)

                def pairs_body(count):
                    def body(j, carry):
                        for s in range(0, count, 2):
                            stage_a(count * j + s + 1, pb_ref)
                            stage_b(count * j + s, pa_ref)
                            stage_a(count * j + s + 2, pa_ref)
                            stage_b(count * j + s + 1, pb_ref)
                        return carry
                    return body

                long_iters = (n_pairs - 1) // 4
                lax.fori_loop(0, long_iters, pairs_body(4), 0)
                lax.fori_loop(2 * long_iters, (n_pairs - 1) // 2, pairs_body(2), 0)

                @pl.when(n_pairs % 2 == 1)
                def _last_one():
                    stage_b(n_pairs - 1, pa_ref)

                @pl.when(n_pairs % 2 == 0)
                def _last_two():
                    stage_a(n_pairs - 1, pb_ref)
                    stage_b(n_pairs - 2, pa_ref)
                    stage_b(n_pairs - 1, pb_ref)

        @pl.when(jnp.logical_not(fast))
        def _online(online_step=online_step):
            lax.fori_loop(0, kd0, online_step, 0)
            lax.fori_loop(kd0 + per, nk, online_step, 0)

    lam = lam_ref[...]
    lam_full = (jnp.exp(jnp.sum(lam[0:1] * lam[1:2], axis=-1, keepdims=True))
                - jnp.exp(jnp.sum(lam[2:3] * lam[3:4], axis=-1, keepdims=True)) + lam_init)
    outs = []
    for h in range(2):
        a1 = acc_ref[2 * h]
        a2 = acc_ref[2 * h + 1]
        rows = slice(0, DIFF_V_DIM)
        ones_row = slice(DIFF_V_DIM, DIFF_V_DIM + 1)
        o = a1[rows] / a1[ones_row] - lam_full * (a2[rows] / a2[ones_row])
        o = o * lax.rsqrt(jnp.mean(o * o, axis=0, keepdims=True) + EPS)
        outs.append(o * sw_ref[...] * (1.0 - lam_init))
    o_ref[0] = jnp.concatenate(outs, axis=0).T.astype(o_ref.dtype)


def _diff_attention(slopes, pieces, lam, subln_w, kn, dqvt, dk, aux, layer_idx, tq, tk):
    B, S, _ = dk.shape
    assert tq % tk == 0 and tk <= AUX_BLOCK and S // AUX_BLOCK <= AUX_BLOCK
    lam_init = 0.8 - 0.6 * math.exp(-0.3 * layer_idx)
    kern = functools.partial(_attn_kernel, seq=S, tq=tq, tk=tk, lam_init=lam_init)
    v_block0 = DIFF_WIDTH // LANES
    smem = pl.BlockSpec(memory_space=pltpu.SMEM)
    return pl.pallas_call(
        kern,
        grid=(B, HEAD_PAIRS, S // tq),
        in_specs=[
            smem, smem,
            pl.BlockSpec((4, DIFF_QK_DIM), lambda b, j, i: (0, 0)),
            pl.BlockSpec((DIFF_V_DIM, 1), lambda b, j, i: (0, 0)),
            pl.BlockSpec((1,) + kn.shape[1:], lambda b, j, i: (b, 0, 0, 0)),
            pl.BlockSpec((1, LANES, tq), lambda b, j, i: (b, j, i)),
            pl.BlockSpec((1, S, LANES), lambda b, j, i: (b, 0, j)),
            pl.BlockSpec((S, LANES), lambda b, j, i: (0, 0)),
            pl.BlockSpec((1, LANES, S), lambda b, j, i: (b, v_block0 + j, 0)),
        ],
        out_specs=pl.BlockSpec((1, tq, LANES), lambda b, j, i: (b, i, j)),
        out_shape=jax.ShapeDtypeStruct((B, S, DIFF_WIDTH), BF16),
        scratch_shapes=[pltpu.VMEM((4, 1, tq), F32), pltpu.VMEM((4, V_ROWS, tq), F32),
                        pltpu.VMEM((8, 2 * LANES, tq), BF16),
                        pltpu.VMEM((4, tk, tq), BF16), pltpu.VMEM((4, tk, tq), BF16),
                        pltpu.VMEM((tq, tq), F32), pltpu.VMEM((4, tq, tq), F32)],
        compiler_params=_params("parallel", "parallel", "arbitrary"),
        name="diff_attn",
    )(slopes, pieces, lam, subln_w, kn, dqvt, dk, aux, dqvt)


def _out_proj_kernel(x_ref, ret_ref, diff_ref, cv_ref, prev_ref, next_ref, cw_ref, w_ref, o_ref, *, tm):
    i = pl.program_id(1)
    last = pl.num_programs(1) - 1
    cc = CONV_CHANNELS
    cv = cv_ref[0]
    u = cv[:, 2 * cc:3 * cc] * cv[:, 0:cc]
    prev = prev_ref[0]
    nxt = next_ref[0]
    u_prev = prev[SUBLANES - 1:SUBLANES, 2 * cc:3 * cc] * prev[SUBLANES - 1:SUBLANES, 0:cc]
    u_next = nxt[0:1, 2 * cc:3 * cc] * nxt[0:1, 0:cc]
    u_prev = jnp.where(i == 0, 0.0, u_prev)
    u_next = jnp.where(i == last, 0.0, u_next)
    rows = lax.broadcasted_iota(jnp.int32, (tm, cc), 0)
    u_m1 = jnp.where(rows == 0, u_prev, pltpu.roll(u, 1, 0))
    u_p1 = jnp.where(rows == tm - 1, u_next, pltpu.roll(u, tm - 1, 0))
    y = cw_ref[0:1] * u_m1 + cw_ref[1:2] * u + cw_ref[2:3] * u_p1
    conv = (cv[:, cc:2 * cc] * y).astype(BF16)
    acc = jnp.dot(ret_ref[0], w_ref[0:RET_WIDTH], preferred_element_type=F32)
    acc += jnp.dot(diff_ref[0], w_ref[RET_WIDTH:RET_WIDTH + DIFF_WIDTH], preferred_element_type=F32)
    acc += jnp.dot(conv, w_ref[RET_WIDTH + DIFF_WIDTH:], preferred_element_type=F32)
    o_ref[0] = x_ref[0] + acc


def _out_proj(x, ret, diff, cv, conv_w, w_out, tm):
    B, S, D = x.shape
    per = tm // SUBLANES
    nblk8 = S // SUBLANES
    tile = lambda width: pl.BlockSpec((1, tm, width), lambda b, i: (b, i, 0))
    return pl.pallas_call(
        functools.partial(_out_proj_kernel, tm=tm),
        grid=(B, S // tm),
        in_specs=[
            tile(D), tile(RET_WIDTH), tile(DIFF_WIDTH), tile(3 * CONV_CHANNELS),
            pl.BlockSpec((1, SUBLANES, 3 * CONV_CHANNELS), lambda b, i: (b, jnp.maximum(i * per - 1, 0), 0)),
            pl.BlockSpec((1, SUBLANES, 3 * CONV_CHANNELS),
                         lambda b, i: (b, jnp.minimum((i + 1) * per, nblk8 - 1), 0)),
            pl.BlockSpec((CONV_WIDTH, CONV_CHANNELS), lambda b, i: (0, 0)),
            pl.BlockSpec((D, D), lambda b, i: (0, 0)),
        ],
        out_specs=tile(D),
        out_shape=jax.ShapeDtypeStruct((B, S, D), F32),
        compiler_params=_params("parallel", "parallel"),
        name="out_proj",
    )(x, ret, diff, cv, cv, cv, conv_w, w_out)


_GROUP_LANE0 = N_EXPERTS
MOE_ROW_BLOCK = 256


def _moe_sorted_rows(tm):
    return (tm + N_GROUPS * (MOE_ROW_BLOCK - 1)) // MOE_ROW_BLOCK * MOE_ROW_BLOCK


def _router_gates(logits):
    lane = lax.broadcasted_iota(jnp.int32, logits.shape, 1)
    big = jnp.int32(LANES)
    is_group = (lane >= _GROUP_LANE0) & (lane < _GROUP_LANE0 + N_GROUPS)
    gl = jnp.where(is_group, logits, -jnp.inf)
    g_max = jnp.max(gl, axis=-1, keepdims=True)
    g_idx = jnp.min(jnp.where(gl == g_max, lane - _GROUP_LANE0, big), axis=-1, keepdims=True)
    g_w = 1.0 / jnp.sum(jnp.where(is_group, jnp.exp(gl - g_max), 0.0), axis=-1, keepdims=True)
    in_group = (lane >= g_idx * EXPERTS_PER_GROUP) & (lane < (g_idx + 1) * EXPERTS_PER_GROUP)
    el = jnp.where(in_group, logits, -jnp.inf)
    e1 = jnp.max(el, axis=-1, keepdims=True)
    i1 = jnp.min(jnp.where(el == e1, lane, big), axis=-1, keepdims=True)
    el2 = jnp.where(lane == i1, -jnp.inf, el)
    e2 = jnp.max(el2, axis=-1, keepdims=True)
    i2 = jnp.min(jnp.where(el2 == e2, lane, big), axis=-1, keepdims=True)
    r = jnp.exp(e2 - e1)
    p1 = g_w / (1.0 + r)
    p2 = g_w * r / (1.0 + r)
    return jnp.where(lane == i1, p1, jnp.where(lane == i2, p2, 0.0)), g_idx


def _moe_kernel(x_ref, nw_ref, rw_ref, rb_ref, wg_ref, wu_ref, wd_ref, fw_ref, o_ref,
                xs_ref, gs_ref, ys_ref, dest_ref, ltri_ref, seg_ref, *, final_norm, tm):
    tile = pl.program_id(0)
    e = pl.program_id(1)
    sorted_rows = _moe_sorted_rows(tm)

    @pl.when((tile == 0) & (e == 0))
    def _strictly_lower():
        r = lax.broadcasted_iota(jnp.int32, (tm, tm), 0)
        c = lax.broadcasted_iota(jnp.int32, (tm, tm), 1)
        ltri_ref[...] = jnp.where(c < r, 1.0, 0.0).astype(BF16)

    @pl.when(e == 0)
    def _route_and_sort():
        x = x_ref[...]
        xn = x * lax.rsqrt(jnp.mean(x * x, axis=-1, keepdims=True) + EPS) * nw_ref[...]
        hi, lo = _split_hi_lo(xn)
        r_hi, r_lo = _split_hi_lo(rw_ref[...])
        logits = (jnp.dot(hi, r_hi, preferred_element_type=F32)
                  + jnp.dot(lo, r_hi, preferred_element_type=F32)
                  + jnp.dot(hi, r_lo, preferred_element_type=F32)) + rb_ref[...]
        gates, g_idx = _router_gates(logits)
        lane = lax.broadcasted_iota(jnp.int32, (tm, LANES), 1)
        member = jnp.where(lane == g_idx, 1.0, 0.0)
        earlier = jnp.dot(ltri_ref[...], member.astype(BF16), preferred_element_type=F32)
        rank = jnp.sum(member * earlier, axis=-1, keepdims=True)
        count = jnp.sum(member, axis=0, keepdims=True)
        seg = jnp.floor((count + (MOE_ROW_BLOCK - 1.0)) * (1.0 / MOE_ROW_BLOCK)) * MOE_ROW_BLOCK
        lane1 = lax.broadcasted_iota(jnp.int32, (1, LANES), 1)
        s0, s1, s2 = seg[:, 0:1], seg[:, 1:2], seg[:, 2:3]
        start = jnp.where(lane1 == 1, s0, jnp.where(lane1 == 2, s0 + s1, jnp.where(lane1 == 3, s0 + s1 + s2, 0.0)))
        dest = rank + jnp.sum(member * start, axis=-1, keepdims=True)
        dest_ref[...] = jnp.broadcast_to(dest, (tm, LANES))
        dest_row = dest_ref[...].T[0:1, :]
        row_id = lax.broadcasted_iota(jnp.int32, (sorted_rows, tm), 0).astype(F32)
        perm = jnp.where(row_id == dest_row, 1.0, 0.0).astype(BF16)
        g_hi, g_mid, g_lo = _split3(gates)
        g_pieces = g_hi + pltpu.roll(g_mid, N_EXPERTS, 1) + pltpu.roll(g_lo, 2 * N_EXPERTS, 1)
        moved = jnp.dot(perm, jnp.concatenate([hi, g_pieces.astype(BF16)], axis=1), preferred_element_type=F32)
        xs_ref[...] = moved[:, :D_MODEL].astype(BF16)
        g_moved = moved[:, D_MODEL:]
        gs_ref[...] = (g_moved + pltpu.roll(g_moved, LANES - N_EXPERTS, 1)
                       + pltpu.roll(g_moved, LANES - 2 * N_EXPERTS, 1))
        ys_ref[...] = jnp.zeros(ys_ref.shape, F32)
        start_i = start.astype(jnp.int32)
        blocks_i = (seg * (1.0 / MOE_ROW_BLOCK)).astype(jnp.int32)
        for g in range(N_GROUPS):
            seg_ref[g] = start_i[0, g]
            seg_ref[N_GROUPS + g] = blocks_i[0, g]

    group = e // EXPERTS_PER_GROUP
    first_row = seg_ref[group]

    def row_block(b, carry):
        r0 = pl.multiple_of(first_row + b * MOE_ROW_BLOCK, MOE_ROW_BLOCK)
        xb = xs_ref[pl.ds(r0, MOE_ROW_BLOCK), :]
        gate = jnp.dot(xb, wg_ref[0], preferred_element_type=F32)
        up = jnp.dot(xb, wu_ref[0], preferred_element_type=F32)
        hmid = (gate * jax.nn.sigmoid(gate) * up).astype(BF16)
        gsb = gs_ref[pl.ds(r0, MOE_ROW_BLOCK), :]
        lane = lax.broadcasted_iota(jnp.int32, gsb.shape, 1)
        g_e = jnp.sum(jnp.where(lane == e, gsb, 0.0), axis=-1, keepdims=True)
        ys_ref[pl.ds(r0, MOE_ROW_BLOCK), :] += g_e * jnp.dot(hmid, wd_ref[0], preferred_element_type=F32)
        return carry

    lax.fori_loop(0, seg_ref[N_GROUPS + group], row_block, 0)

    @pl.when(e == pl.num_programs(1) - 1)
    def _finish():
        col_id = lax.broadcasted_iota(jnp.int32, (tm, sorted_rows), 1).astype(F32)
        unperm = jnp.where(col_id == dest_ref[:, 0:1], 1.0, 0.0).astype(BF16)
        y = x_ref[...] + jnp.dot(unperm, ys_ref[...].astype(BF16), preferred_element_type=F32)
        if final_norm:
            y = y * lax.rsqrt(jnp.mean(y * y, axis=-1, keepdims=True) + EPS) * fw_ref[...]
        o_ref[...] = y


def _moe(x, nw, r_w, r_b, wg, wu, wd, final_w, final_norm, tm):
    T, D = x.shape
    const = lambda shape: pl.BlockSpec(shape, lambda i, e: (0,) * len(shape))
    sorted_rows = _moe_sorted_rows(tm)
    return pl.pallas_call(
        functools.partial(_moe_kernel, final_norm=final_norm, tm=tm),
        grid=(T // tm, N_EXPERTS),
        in_specs=[
            pl.BlockSpec((tm, D), lambda i, e: (i, 0)),
            const((1, D)), const((D, LANES)), const((1, LANES)),
            pl.BlockSpec((1, D, EXPERT_FF), lambda i, e: (e, 0, 0)),
            pl.BlockSpec((1, D, EXPERT_FF), lambda i, e: (e, 0, 0)),
            pl.BlockSpec((1, EXPERT_FF, D), lambda i, e: (e, 0, 0)),
            const((1, D)),
        ],
        out_specs=pl.BlockSpec((tm, D), lambda i, e: (i, 0)),
        out_shape=jax.ShapeDtypeStruct((T, D), F32),
        scratch_shapes=[pltpu.VMEM((sorted_rows, D), BF16), pltpu.VMEM((sorted_rows, LANES), F32),
                        pltpu.VMEM((sorted_rows, D), F32), pltpu.VMEM((tm, LANES), F32),
                        pltpu.VMEM((tm, tm), BF16), pltpu.SMEM((2 * N_GROUPS,), jnp.int32)],
        compiler_params=_params("arbitrary", "arbitrary"),
        name="moe",
    )(x, nw, r_w, r_b, wg, wu, wd, final_w)


def _tile(n, pref):
    t = min(n, pref)
    assert n % t == 0, (n, t)
    return t


def _prep_layer(l, norm1_w, w_in, ret_decay_logit, ret_gn_w, diff_lambda, diff_subln_w, conv_w, w_out, norm2_w,
                router_group_w, router_group_b, router_expert_w, router_expert_b, expert_w_gate, expert_w_up,
                expert_w_down):
    w = w_in[l]
    w_nat = jnp.concatenate([w[:, :_OFF_DQ], w[:, _OFF_DK:_OFF_DV], w[:, _OFF_CONV:]], axis=1).astype(BF16)
    w_t = jnp.concatenate([w[:, _OFF_DQ:_OFF_DK], w[:, _OFF_DV:_OFF_CONV]], axis=1).T.astype(BF16)
    router = jnp.zeros((D_MODEL, LANES), F32)
    router = router.at[:, :N_EXPERTS].set(router_expert_w[l]).at[:, N_EXPERTS:N_EXPERTS + N_GROUPS].set(
        router_group_w[l])
    r_b = jnp.zeros((1, LANES), F32).at[0, :N_EXPERTS].set(router_expert_b[l]).at[
        0, N_EXPERTS:N_EXPERTS + N_GROUPS].set(router_group_b[l])
    return dict(
        norm1=norm1_w[l][None, :], w_nat=w_nat, w_t=w_t,
        lg=jax.nn.log_sigmoid(ret_decay_logit[l].astype(F32)),
        gn_w=ret_gn_w[l][None, :].astype(F32),
        lam=diff_lambda[l].astype(F32), subln=diff_subln_w[l][:, None].astype(F32),
        conv_w=conv_w[l].astype(F32), w_out=w_out[l].astype(BF16),
        norm2=norm2_w[l][None, :], r_w=router, r_b=r_b,
        wg=expert_w_gate[l].astype(BF16), wu=expert_w_up[l].astype(BF16), wd=expert_w_down[l].astype(BF16),
    )


def _trunk(x, layers, final_w, slopes, pieces):
    B, S, D = x.shape
    tm = _tile(S, 512)
    tq = _tile(S, 512)
    tk = _tile(S, 256)
    t_moe = _tile(B * S, 1024)
    aux = _position_features(S)
    for l, lw in enumerate(layers):
        rqkv, rg, dk, dqvt, cv, kn = _in_proj(x, lw["norm1"], lw["w_nat"], lw["w_t"], tm)
        ret = _retention(lw["lg"], rqkv, rg, lw["gn_w"])
        diff = _diff_attention(slopes, pieces, lw["lam"], lw["subln"], kn, dqvt, dk, aux, l, tq, tk)
        x = _out_proj(x, ret, diff, cv, lw["conv_w"], lw["w_out"], tm)
        last = l == len(layers) - 1
        x = _moe(x.reshape(B * S, D), lw["norm2"], lw["r_w"], lw["r_b"], lw["wg"], lw["wu"],
                 lw["wd"], final_w, last, t_moe).reshape(B, S, D)
    return x


def kernel(x_prompt, x_sample, norm1_w, w_in, ret_decay_logit, ret_gn_w, diff_lambda, diff_subln_w, conv_w, w_out,
           norm2_w, router_group_w, router_group_b, router_expert_w, router_expert_b, expert_w_gate, expert_w_up,
           expert_w_down, final_norm_w):
    depth = w_in.shape[0]
    layers = [
        _prep_layer(l, norm1_w, w_in, ret_decay_logit, ret_gn_w, diff_lambda, diff_subln_w, conv_w, w_out, norm2_w,
                    router_group_w, router_group_b, router_expert_w, router_expert_b, expert_w_gate, expert_w_up,
                    expert_w_down)
        for l in range(depth)
    ]
    final_w = final_norm_w[None, :].astype(F32)
    slopes = (np.float32(2.0) ** (np.float32(-8.0) * np.arange(1, DIFF_HEADS + 1, dtype=np.float32)
                                  / np.float32(DIFF_HEADS))) * np.float32(LOG2E)
    s_hi = slopes.astype(BF16).astype(np.float32)
    s_mid = (slopes - s_hi).astype(BF16).astype(np.float32)
    s_lo = (slopes - s_hi - s_mid).astype(BF16).astype(np.float32)
    pieces = jnp.asarray(np.stack([s_hi, s_mid, s_lo], axis=1).reshape(-1))
    slopes = jnp.asarray(slopes)
    return (_trunk(x_prompt, layers, final_w, slopes, pieces), _trunk(x_sample, layers, final_w, slopes, pieces))
```

```python
import functools
import math

import jax
import jax.numpy as jnp
import numpy as np
from jax import lax
from jax.experimental import pallas as pl
from jax.experimental.pallas import tpu as pltpu

F32 = jnp.float32
BF16 = jnp.bfloat16

D_MODEL = 1024
RET_HEADS = 6
RET_HEAD_DIM = 64
RET_WIDTH = RET_HEADS * RET_HEAD_DIM
DIFF_HEADS = 6
DIFF_QK_DIM = 32
DIFF_V_DIM = 64
DIFF_WIDTH = DIFF_HEADS * DIFF_V_DIM
CONV_CHANNELS = 256
CONV_WIDTH = 3
N_GROUPS = 4
EXPERTS_PER_GROUP = 4
N_EXPERTS = 16
EXPERT_FF = 512
EPS = 1e-6

LANES = 128
HEAD_PAIRS = RET_HEADS // 2
RET_CHUNK = 128
SUBLANES = 8
VMEM_LIMIT = 56 * 1024 * 1024
LOG2E = math.log2(math.e)
NEG_BIG = -1e30

_OFF_RG = 3 * RET_WIDTH
_OFF_DQ = 4 * RET_WIDTH
_OFF_DK = _OFF_DQ + DIFF_WIDTH
_OFF_DV = _OFF_DK + DIFF_WIDTH
_OFF_CONV = _OFF_DV + DIFF_WIDTH


def _params(*sem):
    return pltpu.CompilerParams(dimension_semantics=sem, vmem_limit_bytes=VMEM_LIMIT)


def _split_hi_lo(x):
    hi = x.astype(BF16)
    lo = (x - hi.astype(F32)).astype(BF16)
    return hi, lo


def _in_proj_kernel(x_ref, nw_ref, w_ref, wt_ref, rqkv_ref, rg_ref, dk_ref, dqvt_ref, cv_ref, kn_ref):
    x = x_ref[0]
    ms = jnp.mean(x * x, axis=-1, keepdims=True)
    h = (x * lax.rsqrt(ms + EPS) * nw_ref[...]).astype(BF16)
    a = jnp.dot(h, w_ref[:, 0:_OFF_DQ], preferred_element_type=F32)
    rqkv_ref[0, :, 0:RET_WIDTH] = a[:, 0:RET_WIDTH].astype(BF16)
    rqkv_ref[0, :, RET_WIDTH:2 * RET_WIDTH] = (a[:, RET_WIDTH:2 * RET_WIDTH] * RET_HEAD_DIM ** -0.5).astype(BF16)
    rqkv_ref[0, :, 2 * RET_WIDTH:3 * RET_WIDTH] = a[:, 2 * RET_WIDTH:3 * RET_WIDTH].astype(BF16)
    rg_ref[0] = a[:, _OFF_RG:_OFF_DQ]
    dk = jnp.dot(h, w_ref[:, _OFF_DQ:_OFF_DQ + DIFF_WIDTH], preferred_element_type=F32).astype(BF16)
    dk_ref[0] = dk
    dkf = dk.astype(F32)
    sq_hi, sq_lo = _split_hi_lo(dkf * dkf)
    grp = (lax.shift_right_logical(lax.broadcasted_iota(jnp.int32, (DIFF_WIDTH, LANES), 0), 5)
           == lax.broadcasted_iota(jnp.int32, (DIFF_WIDTH, LANES), 1))
    grp = jnp.where(grp, 1.0, 0.0).astype(BF16)
    kn2 = jnp.dot(sq_hi, grp, preferred_element_type=F32) + jnp.dot(sq_lo, grp, preferred_element_type=F32)
    kn_ref[0, 0] = jnp.broadcast_to(jnp.max(kn2, axis=0, keepdims=True), (SUBLANES, LANES))
    cv_ref[0] =jnp.dot(h, w_ref[:, _OFF_DQ + DIFF_WIDTH:], preferred_element_type=F32)
    t = lax.dot_general(wt_ref[...], h, (((1,), (1,)), ((), ())), preferred_element_type=F32)
    dqvt_ref[0, 0:DIFF_WIDTH, :] = (t[0:DIFF_WIDTH] * (DIFF_QK_DIM ** -0.5 * LOG2E)).astype(BF16)
    dqvt_ref[0, DIFF_WIDTH:, :] = t[DIFF_WIDTH:].astype(BF16)


def _in_proj(x, nw, w_nat, w_t, tm):
    B, S, D = x.shape
    n_nat = w_nat.shape[1]
    return pl.pallas_call(
        _in_proj_kernel,
        grid=(B, S // tm),
        in_specs=[
            pl.BlockSpec((1, tm, D), lambda b, i: (b, i, 0)),
            pl.BlockSpec((1, D), lambda b, i: (0, 0)),
            pl.BlockSpec((D, n_nat), lambda b, i: (0, 0)),
            pl.BlockSpec((2 * DIFF_WIDTH, D), lambda b, i: (0, 0)),
        ],
        out_specs=[
            pl.BlockSpec((1, tm, 3 * RET_WIDTH), lambda b, i: (b, i, 0)),
            pl.BlockSpec((1, tm, RET_WIDTH), lambda b, i: (b, i, 0)),
            pl.BlockSpec((1, tm, DIFF_WIDTH), lambda b, i: (b, i, 0)),
            pl.BlockSpec((1, 2 * DIFF_WIDTH, tm), lambda b, i: (b, 0, i)),
            pl.BlockSpec((1, tm, 3 * CONV_CHANNELS), lambda b, i: (b, i, 0)),
            pl.BlockSpec((1, 1, SUBLANES, LANES), lambda b, i: (b, i, 0, 0)),
        ],
        out_shape=[
            jax.ShapeDtypeStruct((B, S, 3 * RET_WIDTH), BF16),
            jax.ShapeDtypeStruct((B, S, RET_WIDTH), F32),
            jax.ShapeDtypeStruct((B, S, DIFF_WIDTH), BF16),
            jax.ShapeDtypeStruct((B, 2 * DIFF_WIDTH, S), BF16),
            jax.ShapeDtypeStruct((B, S, 3 * CONV_CHANNELS), F32),
            jax.ShapeDtypeStruct((B, S // tm, SUBLANES, LANES), F32),
        ],
        compiler_params=_params("parallel", "parallel"),
        name="in_proj",
    )(x, nw, w_nat, w_t)


def _pair_lane_value(lg_ref, direction, pair, shape, axis):
    idx = lax.broadcasted_iota(jnp.int32, shape, axis)
    return jnp.where(idx < RET_HEAD_DIM, lg_ref[direction, 2 * pair], lg_ref[direction, 2 * pair + 1])


def _same_head_mask():
    r = lax.broadcasted_iota(jnp.int32, (LANES, LANES), 0)
    c = lax.broadcasted_iota(jnp.int32, (LANES, LANES), 1)
    return (r < RET_HEAD_DIM) == (c < RET_HEAD_DIM)


def _ret_fwd_kernel(lg_ref, qkv_ref, y_ref, state_ref, dmat_ref, qdec_ref, kdec_ref, cdec_ref, *, chunks):
    C = RET_CHUNK

    @pl.when((pl.program_id(0) == 0) & (pl.program_id(1) == 0))
    def _build_tables():
        diff = (lax.broadcasted_iota(jnp.int32, (C, C), 0) - lax.broadcasted_iota(jnp.int32, (C, C), 1)).astype(F32)
        for h in range(RET_HEADS):
            lower = jnp.exp(jnp.maximum(diff, 0.0) * lg_ref[0, h])
            upper = jnp.exp(jnp.maximum(-diff, 0.0) * lg_ref[1, h])
            dmat_ref[h] = jnp.where(diff >= 0, lower, upper)
        pos = lax.broadcasted_iota(jnp.int32, (C, LANES), 0).astype(F32)
        for p in range(HEAD_PAIRS):
            lg_lane = _pair_lane_value(lg_ref, 0, p, (C, LANES), 1)
            qdec_ref[p] = jnp.exp((pos + 1.0) * lg_lane)
            kdec_ref[p] = jnp.exp((C - 1.0 - pos) * lg_lane)
            cdec_ref[p] = jnp.exp(float(C) * _pair_lane_value(lg_ref, 0, p, (LANES, LANES), 0))

    @pl.when(pl.program_id(1) == 0)
    def _reset():
        state_ref[...] = jnp.zeros(state_ref.shape, F32)

    lane = lax.broadcasted_iota(jnp.int32, (C, LANES), 1)
    same_head = _same_head_mask()
    def operands(p, ci):
        rows = slice(ci * C, (ci + 1) * C)
        return (qkv_ref[0, rows, p * LANES:(p + 1) * LANES],
                qkv_ref[0, rows, RET_WIDTH + p * LANES:RET_WIDTH + (p + 1) * LANES],
                qkv_ref[0, rows, 2 * RET_WIDTH + p * LANES:2 * RET_WIDTH + (p + 1) * LANES])

    items = [(p, ci) for p in range(HEAD_PAIRS) for ci in range(chunks)]
    scores, kvs = {}, {}
    for p, ci in items:
        q, k, v = operands(p, ci)
        for h in range(2):
            head_lanes = (lane >= h * RET_HEAD_DIM) & (lane < (h + 1) * RET_HEAD_DIM)
            qm = jnp.where(head_lanes, q, jnp.zeros_like(q))
            scores[p, ci, h] = lax.dot_general(qm, k, (((1,), (1,)), ((), ())), preferred_element_type=F32)
        kd = (k.astype(F32) * kdec_ref[p]).astype(BF16)
        kvs[p, ci] = lax.dot_general(kd, v, (((0,), (0,)), ((), ())), preferred_element_type=F32)
    states = {}
    for p in range(HEAD_PAIRS):
        state = state_ref[p]
        for ci in range(chunks):
            states[p, ci] = state.astype(BF16)
            state = cdec_ref[p] * state + jnp.where(same_head, kvs[p, ci], 0.0)
        state_ref[p] = state
    for p, ci in items:
        q, _, v = operands(p, ci)
        inner = [jnp.dot((scores[p, ci, h] * dmat_ref[2 * p + h]).astype(BF16), v, preferred_element_type=F32)
                 for h in range(2)]
        cross = jnp.dot(q, states[p, ci], preferred_element_type=F32) * qdec_ref[p]
        y_ref[0, ci * C:(ci + 1) * C, p * LANES:(p + 1) * LANES] = (
            jnp.where(lane < RET_HEAD_DIM, inner[0], inner[1]) + cross)


def _ret_bwd_kernel(lg_ref, qkv_ref, y1_ref, g_ref, gnw_ref, o_ref, state_ref, qdec_ref, kdec_ref, cdec_ref, *,
                    chunks):
    C = RET_CHUNK

    @pl.when((pl.program_id(0) == 0) & (pl.program_id(1) == 0))
    def _build_tables():
        pos = lax.broadcasted_iota(jnp.int32, (C, LANES), 0).astype(F32)
        for p in range(HEAD_PAIRS):
            lg_lane = _pair_lane_value(lg_ref, 1, p, (C, LANES), 1)
            qdec_ref[p] = jnp.exp((float(C) - pos) * lg_lane)
            kdec_ref[p] = jnp.exp(pos * lg_lane)
            cdec_ref[p] = jnp.exp(float(C) * _pair_lane_value(lg_ref, 1, p, (LANES, LANES), 0))

    @pl.when(pl.program_id(1) == 0)
    def _reset():
        state_ref[...] = jnp.zeros(state_ref.shape, F32)

    same_head = _same_head_mask()
    head_avg = jnp.where(same_head, 1.0 / RET_HEAD_DIM, 0.0).astype(BF16)

    def head_mean(t):
        hi, lo = _split_hi_lo(t)
        return (jnp.dot(hi, head_avg, preferred_element_type=F32)
                + jnp.dot(lo, head_avg, preferred_element_type=F32))

    items = [(p, ci) for p in range(HEAD_PAIRS) for ci in range(chunks)]
    kvs = {}
    for p, ci in items:
        rows = slice(ci * C, (ci + 1) * C)
        k = qkv_ref[0, rows, RET_WIDTH + p * LANES:RET_WIDTH + (p + 1) * LANES]
        v = qkv_ref[0, rows, 2 * RET_WIDTH + p * LANES:2 * RET_WIDTH + (p + 1) * LANES]
        kd = (k.astype(F32) * kdec_ref[p]).astype(BF16)
        kvs[p, ci] = lax.dot_general(kd, v, (((0,), (0,)), ((), ())), preferred_element_type=F32)
    states = {}
    for p in range(HEAD_PAIRS):
        state = state_ref[p]
        for ci in reversed(range(chunks)):
            states[p, ci] = state.astype(BF16)
            state = cdec_ref[p] * state + jnp.where(same_head, kvs[p, ci], 0.0)
        state_ref[p] = state
    ys = {}
    for p, ci in items:
        rows = slice(ci * C, (ci + 1) * C)
        sl = slice(p * LANES, (p + 1) * LANES)
        cross = jnp.dot(qkv_ref[0, rows, sl], states[p, ci], preferred_element_type=F32)
        ys[p, ci] = y1_ref[0, rows, sl] + cross * qdec_ref[p]
    means = {key: head_mean(y) for key, y in ys.items()}
    devs = {key: ys[key] - means[key] for key in ys}
    variances = {key: head_mean(d * d) for key, d in devs.items()}
    for p, ci in items:
        rows = slice(ci * C, (ci + 1) * C)
        sl = slice(p * LANES, (p + 1) * LANES)
        g = g_ref[0, rows, sl]
        o = devs[p, ci] * lax.rsqrt(variances[p, ci] + EPS) * gnw_ref[:, sl] * (g * jax.nn.sigmoid(g))
        o_ref[0, rows, sl] = o.astype(o_ref.dtype)


def _retention(lg, rqkv, rg, gn_w):
    B, S, _ = rqkv.shape
    chunks = math.gcd(S // RET_CHUNK, 4)
    C = chunks * RET_CHUNK
    n = S // C
    smem = pl.BlockSpec(memory_space=pltpu.SMEM)
    table = pltpu.VMEM((HEAD_PAIRS, RET_CHUNK, LANES), F32)
    state = pltpu.VMEM((HEAD_PAIRS, LANES, LANES), F32)
    y1 = pl.pallas_call(
        functools.partial(_ret_fwd_kernel, chunks=chunks),
        grid=(B, n),
        in_specs=[smem, pl.BlockSpec((1, C, 3 * RET_WIDTH), lambda b, i: (b, i, 0))],
        out_specs=pl.BlockSpec((1, C, RET_WIDTH), lambda b, i: (b, i, 0)),
        out_shape=jax.ShapeDtypeStruct((B, S, RET_WIDTH), F32),
        scratch_shapes=[state, pltpu.VMEM((RET_HEADS, RET_CHUNK, RET_CHUNK), F32), table, table, state],
        compiler_params=_params("arbitrary", "arbitrary"),
        name="ret_fwd",
    )(lg, rqkv)
    rev = lambda b, i: (b, n - 1 - i, 0)
    return pl.pallas_call(
        functools.partial(_ret_bwd_kernel, chunks=chunks),
        grid=(B, n),
        in_specs=[
            smem,
            pl.BlockSpec((1, C, 3 * RET_WIDTH), rev),
            pl.BlockSpec((1, C, RET_WIDTH), rev),
            pl.BlockSpec((1, C, RET_WIDTH), rev),
            pl.BlockSpec((1, RET_WIDTH), lambda b, i: (0, 0)),
        ],
        out_specs=pl.BlockSpec((1, C, RET_WIDTH), rev),
        out_shape=jax.ShapeDtypeStruct((B, S, RET_WIDTH), BF16),
        scratch_shapes=[state, table, table, state],
        compiler_params=_params("arbitrary", "arbitrary"),
        name="ret_bwd",
    )(lg, rqkv, y1, rg, gn_w)


AUX_BLOCK = 256
AUX_ROWS = 16
V_ROWS = DIFF_V_DIM + SUBLANES
GUARD_LOG2 = 60.0
UNDERFLOW_LOG2 = 150.0


def _position_features(seq):
    j = jnp.arange(seq, dtype=jnp.int32)
    ones = jnp.ones((seq,), F32)
    n = (j // AUX_BLOCK).astype(F32)
    jc = (j % AUX_BLOCK - AUX_BLOCK // 2).astype(F32)
    feat = jnp.stack([ones] * 3 + [n] * 3 + [jc] * 3, axis=1)
    return jnp.pad(feat, ((0, 0), (0, LANES - feat.shape[1]))).astype(BF16)


def _split3(x):
    hi = x.astype(BF16).astype(F32)
    r = x - hi
    mid = r.astype(BF16).astype(F32)
    return hi, mid, (r - mid).astype(BF16).astype(F32)


def _attn_kernel(slopes_ref, pieces_ref, lam_ref, sw_ref, kn_ref, qt_ref, k_ref, aux_ref, vt_ref, o_ref,
                 m_ref, acc_ref, rhs_ref, pa_ref, pb_ref, dist_ref, sd_ref, *, seq, tq, tk, lam_init):
    pair = pl.program_id(1)
    qi = pl.program_id(2)
    q0 = qi * tq
    nk = seq // tk
    per = tq // tk
    kd0 = qi * per
    qt = qt_ref[0]
    row = lax.broadcasted_iota(jnp.int32, (LANES, tq), 0)
    q_masked = []
    for hc in range(4):
        lo = hc * DIFF_QK_DIM
        q_masked.append(jnp.where((row >= lo) & (row < lo + DIFF_QK_DIM), qt, jnp.zeros_like(qt)))
    rel = (lax.broadcasted_iota(jnp.int32, (tk, tq), 1) - lax.broadcasted_iota(jnp.int32, (tk, tq), 0)).astype(F32)

    @pl.when(qi == 0)
    def _distance_table():
        d = lax.broadcasted_iota(jnp.int32, (tq, tq), 1) - lax.broadcasted_iota(jnp.int32, (tq, tq), 0)
        dist_ref[...] = jnp.abs(d).astype(F32)

    kn2 = jnp.max(jnp.max(kn_ref[0], axis=0), axis=0, keepdims=True)
    kn_lane = lax.broadcasted_iota(jnp.int32, (1, LANES), 1)
    ipos = (q0 + lax.broadcasted_iota(jnp.int32, (1, tq), 1)).astype(F32)
    aug_row = lax.broadcasted_iota(jnp.int32, (AUX_ROWS, tq), 0)

    def v_rows_with_ones(h, k0, n):
        vt = vt_ref[0, h * DIFF_V_DIM:(h + 1) * DIFF_V_DIM, pl.ds(k0, n)]
        return jnp.concatenate([vt, jnp.ones((SUBLANES, n), BF16)], axis=0)

    kd = pl.multiple_of(q0, tq)
    kdiag = k_ref[0, pl.ds(kd, tq), :]
    for hc in range(4):
        s = jnp.dot(kdiag, q_masked[hc], preferred_element_type=F32)
        sd_ref[hc] = s - dist_ref[...] * slopes_ref[2 * pair + hc // 2]
    for hc in range(4):
        m_ref[hc] = jnp.max(sd_ref[hc], axis=0, keepdims=True)
    probs = [jnp.exp2(sd_ref[hc] - m_ref[hc]).astype(BF16) for hc in range(4)]
    for hc in range(4):
        acc_ref[hc] = jnp.dot(v_rows_with_ones(hc // 2, kd, tq), probs[hc], preferred_element_type=F32)

    for h in range(2):
        slope = slopes_ref[2 * pair + h]

        def v_with_ones(k0, h=h):
            return v_rows_with_ones(h, k0, tk)

        def online_step(kb, carry, h=h, slope=slope, v_with_ones=v_with_ones):
            k0 = pl.multiple_of(kb * tk, tk)
            kblk = k_ref[0, pl.ds(k0, tk), :]
            v_ones = v_with_ones(k0)
            bias = jnp.abs(rel + (q0 - k0).astype(F32)) * slope
            for c in range(2):
                hc = 2 * h + c
                s = jnp.dot(kblk, q_masked[hc], preferred_element_type=F32) - bias
                m_old = m_ref[hc]
                m_new = jnp.maximum(m_old, jnp.max(s, axis=0, keepdims=True))
                p = jnp.exp2(s - m_new).astype(BF16)
                acc_ref[hc] = jnp.exp2(m_old - m_new) * acc_ref[hc] + jnp.dot(v_ones, p, preferred_element_type=F32)
                m_ref[hc] = m_new
            return carry

        excess = None
        for c in range(2):
            hc = 2 * h + c
            qf = q_masked[hc].astype(F32)
            qn2 = jnp.sum(qf * qf, axis=0, keepdims=True)
            kn2_hc = jnp.max(jnp.where(kn_lane == 4 * pair + hc, kn2, 0.0), axis=-1, keepdims=True)
            bound = jnp.sqrt(qn2 * kn2_hc) * 1.01 + 1e-3
            e = jnp.max(bound - m_ref[hc], axis=-1, keepdims=True)
            excess = e if excess is None else jnp.maximum(excess, e)
        reach = (excess + UNDERFLOW_LOG2) / slope
        q0f = q0.astype(F32)
        lo_blk = jnp.clip(jnp.floor((q0f - (tk - 1.0) - reach) / tk), 0.0, float(nk))
        hi_blk = jnp.clip(jnp.floor((reach + q0f + (tq - 1.0)) / tk) + 1.0, 0.0, float(nk))
        lo_blk = jnp.minimum(lo_blk.astype(jnp.int32)[0, 0], kd0)
        hi_blk = jnp.maximum(hi_blk.astype(jnp.int32)[0, 0], kd0 + per)
        fast = jnp.where(excess <= GUARD_LOG2, 1, 0).astype(jnp.int32)[0, 0] == 1

        @pl.when(fast)
        def _single_pass(h=h, slope=slope, v_with_ones=v_with_ones, lo_blk=lo_blk, hi_blk=hi_blk):
            pieces = [pieces_ref[(2 * pair + h) * 3 + i] for i in range(3)]
            for c in range(2):
                hc = 2 * h + c
                for side, sign in ((0, 1.0), (1, -1.0)):
                    const = (-sign) * (slope * ipos) - m_ref[hc] + sign * (0.5 * AUX_BLOCK) * slope
                    c_hi, c_mid, c_lo = _split3(const)
                    aug = jnp.where(aug_row == 0, c_hi,
                                    jnp.where(aug_row == 1, c_mid, jnp.where(aug_row == 2, c_lo, 0.0)))
                    for i in range(3):
                        aug = jnp.where(aug_row == 3 + i, sign * AUX_BLOCK * pieces[i], aug)
                        aug = jnp.where(aug_row == 6 + i, sign * pieces[i], aug)
                    rhs_ref[side * 4 + hc, 0:LANES, :] = q_masked[hc]
                    rhs_ref[side * 4 + hc, LANES:LANES + AUX_ROWS, :] = aug.astype(BF16)
                    rhs_ref[side * 4 + hc, LANES + AUX_ROWS:, :] = jnp.zeros((LANES - AUX_ROWS, tq), BF16)

            odd = (kd0 - lo_blk + hi_blk - kd0 - per) % 2
            lo_even = jnp.where((odd == 1) & (lo_blk > 0), lo_blk - 1, lo_blk)
            hi_even = jnp.where((odd == 1) & (lo_blk == 0), hi_blk + 1, hi_blk)
            n_lo = kd0 - lo_even
            total = n_lo + hi_even - kd0 - per

            def block_of(i):
                return jnp.where(i < n_lo, lo_even + i, kd0 + per + i - n_lo)

            def stage_a(pair_idx, p_ref):
                for half in range(2):
                    i = 2 * pair_idx + half
                    k0 = pl.multiple_of(block_of(i) * tk, tk)
                    side = jnp.where(i < n_lo, 0, 1)
                    lhs = jnp.concatenate([k_ref[0, pl.ds(k0, tk), :], aux_ref[pl.ds(k0, tk), :]], axis=1)
                    for c in range(2):
                        t = jnp.dot(lhs, rhs_ref[side * 4 + 2 * h + c], preferred_element_type=F32)
                        p_ref[2 * half + c] = jnp.exp2(t).astype(BF16)

            def stage_b(pair_idx, p_ref):
                va = v_with_ones(pl.multiple_of(block_of(2 * pair_idx) * tk, tk))
                vb = v_with_ones(pl.multiple_of(block_of(2 * pair_idx + 1) * tk, tk))
                for c in range(2):
                    acc_ref[2 * h + c] += (jnp.dot(va, p_ref[c], preferred_element_type=F32)
                                           + jnp.dot(vb, p_ref[2 + c], preferred_element_type=F32))

            n_pairs = total // 2

            @pl.when(n_pairs > 0)
            def _blocks():
                stage_a(0, pa_ref)

                def pairs_body(count):
                    def body(j, carry):
                        for s in range(0, count, 2):
                            stage_a(count * j + s + 1, pb_ref)
                            stage_b(count * j + s, pa_ref)
                            stage_a(count * j + s + 2, pa_ref)
                            stage_b(count * j + s + 1, pb_ref)
                        return carry
                    return body

                long_iters = (n_pairs - 1) // 4
                lax.fori_loop(0, long_iters, pairs_body(4), 0)
                lax.fori_loop(2 * long_iters, (n_pairs - 1) // 2, pairs_body(2), 0)

                @pl.when(n_pairs % 2 == 1)
                def _last_one():
                    stage_b(n_pairs - 1, pa_ref)

                @pl.when(n_pairs % 2 == 0)
                def _last_two():
                    stage_a(n_pairs - 1, pb_ref)
                    stage_b(n_pairs - 2, pa_ref)
                    stage_b(n_pairs - 1, pb_ref)

        @pl.when(jnp.logical_not(fast))
        def _online(online_step=online_step):
            lax.fori_loop(0, kd0, online_step, 0)
            lax.fori_loop(kd0 + per, nk, online_step, 0)

    lam = lam_ref[...]
    lam_full = (jnp.exp(jnp.sum(lam[0:1] * lam[1:2], axis=-1, keepdims=True))
                - jnp.exp(jnp.sum(lam[2:3] * lam[3:4], axis=-1, keepdims=True)) + lam_init)
    outs = []
    for h in range(2):
        a1 = acc_ref[2 * h]
        a2 = acc_ref[2 * h + 1]
        rows = slice(0, DIFF_V_DIM)
        ones_row = slice(DIFF_V_DIM, DIFF_V_DIM + 1)
        o = a1[rows] / a1[ones_row] - lam_full * (a2[rows] / a2[ones_row])
        o = o * lax.rsqrt(jnp.mean(o * o, axis=0, keepdims=True) + EPS)
        outs.append(o * sw_ref[...] * (1.0 - lam_init))
    o_ref[0] = jnp.concatenate(outs, axis=0).T.astype(o_ref.dtype)


def _diff_attention(slopes, pieces, lam, subln_w, kn, dqvt, dk, aux, layer_idx, tq, tk):
    B, S, _ = dk.shape
    assert tq % tk == 0 and tk <= AUX_BLOCK and S // AUX_BLOCK <= AUX_BLOCK
    lam_init = 0.8 - 0.6 * math.exp(-0.3 * layer_idx)
    kern = functools.partial(_attn_kernel, seq=S, tq=tq, tk=tk, lam_init=lam_init)
    v_block0 = DIFF_WIDTH // LANES
    smem = pl.BlockSpec(memory_space=pltpu.SMEM)
    return pl.pallas_call(
        kern,
        grid=(B, HEAD_PAIRS, S // tq),
        in_specs=[
            smem, smem,
            pl.BlockSpec((4, DIFF_QK_DIM), lambda b, j, i: (0, 0)),
            pl.BlockSpec((DIFF_V_DIM, 1), lambda b, j, i: (0, 0)),
            pl.BlockSpec((1,) + kn.shape[1:], lambda b, j, i: (b, 0, 0, 0)),
            pl.BlockSpec((1, LANES, tq), lambda b, j, i: (b, j, i)),
            pl.BlockSpec((1, S, LANES), lambda b, j, i: (b, 0, j)),
            pl.BlockSpec((S, LANES), lambda b, j, i: (0, 0)),
            pl.BlockSpec((1, LANES, S), lambda b, j, i: (b, v_block0 + j, 0)),
        ],
        out_specs=pl.BlockSpec((1, tq, LANES), lambda b, j, i: (b, i, j)),
        out_shape=jax.ShapeDtypeStruct((B, S, DIFF_WIDTH), BF16),
        scratch_shapes=[pltpu.VMEM((4, 1, tq), F32), pltpu.VMEM((4, V_ROWS, tq), F32),
                        pltpu.VMEM((8, 2 * LANES, tq), BF16),
                        pltpu.VMEM((4, tk, tq), BF16), pltpu.VMEM((4, tk, tq), BF16),
                        pltpu.VMEM((tq, tq), F32), pltpu.VMEM((4, tq, tq), F32)],
        compiler_params=_params("parallel", "parallel", "arbitrary"),
        name="diff_attn",
    )(slopes, pieces, lam, subln_w, kn, dqvt, dk, aux, dqvt)


def _out_proj_kernel(x_ref, ret_ref, diff_ref, cv_ref, prev_ref, next_ref, cw_ref, w_ref, o_ref, *, tm):
    i = pl.program_id(1)
    last = pl.num_programs(1) - 1
    cc = CONV_CHANNELS
    cv = cv_ref[0]
    u = cv[:, 2 * cc:3 * cc] * cv[:, 0:cc]
    prev = prev_ref[0]
    nxt = next_ref[0]
    u_prev = prev[SUBLANES - 1:SUBLANES, 2 * cc:3 * cc] * prev[SUBLANES - 1:SUBLANES, 0:cc]
    u_next = nxt[0:1, 2 * cc:3 * cc] * nxt[0:1, 0:cc]
    u_prev = jnp.where(i == 0, 0.0, u_prev)
    u_next = jnp.where(i == last, 0.0, u_next)
    rows = lax.broadcasted_iota(jnp.int32, (tm, cc), 0)
    u_m1 = jnp.where(rows == 0, u_prev, pltpu.roll(u, 1, 0))
    u_p1 = jnp.where(rows == tm - 1, u_next, pltpu.roll(u, tm - 1, 0))
    y = cw_ref[0:1] * u_m1 + cw_ref[1:2] * u + cw_ref[2:3] * u_p1
    conv = (cv[:, cc:2 * cc] * y).astype(BF16)
    acc = jnp.dot(ret_ref[0], w_ref[0:RET_WIDTH], preferred_element_type=F32)
    acc += jnp.dot(diff_ref[0], w_ref[RET_WIDTH:RET_WIDTH + DIFF_WIDTH], preferred_element_type=F32)
    acc += jnp.dot(conv, w_ref[RET_WIDTH + DIFF_WIDTH:], preferred_element_type=F32)
    o_ref[0] = x_ref[0] + acc


def _out_proj(x, ret, diff, cv, conv_w, w_out, tm):
    B, S, D = x.shape
    per = tm // SUBLANES
    nblk8 = S // SUBLANES
    tile = lambda width: pl.BlockSpec((1, tm, width), lambda b, i: (b, i, 0))
    return pl.pallas_call(
        functools.partial(_out_proj_kernel, tm=tm),
        grid=(B, S // tm),
        in_specs=[
            tile(D), tile(RET_WIDTH), tile(DIFF_WIDTH), tile(3 * CONV_CHANNELS),
            pl.BlockSpec((1, SUBLANES, 3 * CONV_CHANNELS), lambda b, i: (b, jnp.maximum(i * per - 1, 0), 0)),
            pl.BlockSpec((1, SUBLANES, 3 * CONV_CHANNELS),
                         lambda b, i: (b, jnp.minimum((i + 1) * per, nblk8 - 1), 0)),
            pl.BlockSpec((CONV_WIDTH, CONV_CHANNELS), lambda b, i: (0, 0)),
            pl.BlockSpec((D, D), lambda b, i: (0, 0)),
        ],
        out_specs=tile(D),
        out_shape=jax.ShapeDtypeStruct((B, S, D), F32),
        compiler_params=_params("parallel", "parallel"),
        name="out_proj",
    )(x, ret, diff, cv, cv, cv, conv_w, w_out)


_GROUP_LANE0 = N_EXPERTS
MOE_ROW_BLOCK = 256
MOE_EXPERTS_PER_STEP = 2
assert EXPERTS_PER_GROUP % MOE_EXPERTS_PER_STEP == 0


def _moe_sorted_rows(tm):
    return (tm + N_GROUPS * (MOE_ROW_BLOCK - 1)) // MOE_ROW_BLOCK * MOE_ROW_BLOCK


def _router_gates(logits):
    lane = lax.broadcasted_iota(jnp.int32, logits.shape, 1)
    big = jnp.int32(LANES)
    is_group = (lane >= _GROUP_LANE0) & (lane < _GROUP_LANE0 + N_GROUPS)
    gl = jnp.where(is_group, logits, -jnp.inf)
    g_max = jnp.max(gl, axis=-1, keepdims=True)
    g_idx = jnp.min(jnp.where(gl == g_max, lane - _GROUP_LANE0, big), axis=-1, keepdims=True)
    g_w = 1.0 / jnp.sum(jnp.where(is_group, jnp.exp(gl - g_max), 0.0), axis=-1, keepdims=True)
    in_group = (lane >= g_idx * EXPERTS_PER_GROUP) & (lane < (g_idx + 1) * EXPERTS_PER_GROUP)
    el = jnp.where(in_group, logits, -jnp.inf)
    e1 = jnp.max(el, axis=-1, keepdims=True)
    i1 = jnp.min(jnp.where(el == e1, lane, big), axis=-1, keepdims=True)
    el2 = jnp.where(lane == i1, -jnp.inf, el)
    e2 = jnp.max(el2, axis=-1, keepdims=True)
    i2 = jnp.min(jnp.where(el2 == e2, lane, big), axis=-1, keepdims=True)
    r = jnp.exp(e2 - e1)
    p1 = g_w / (1.0 + r)
    p2 = g_w * r / (1.0 + r)
    return jnp.where(lane == i1, p1, jnp.where(lane == i2, p2, 0.0)), g_idx


def _moe_kernel(x_ref, nw_ref, rw_ref, rb_ref, wg_ref, wu_ref, wd_ref, fw_ref, o_ref,
                xs_ref, gs_ref, ys_ref, dest_ref, ltri_ref, seg_ref, *, final_norm, tm):
    tile = pl.program_id(0)
    e = pl.program_id(1)
    sorted_rows = _moe_sorted_rows(tm)

    @pl.when((tile == 0) & (e == 0))
    def _strictly_lower():
        r = lax.broadcasted_iota(jnp.int32, (tm, tm), 0)
        c = lax.broadcasted_iota(jnp.int32, (tm, tm), 1)
        ltri_ref[...] = jnp.where(c < r, 1.0, 0.0).astype(BF16)

    @pl.when(e == 0)
    def _route_and_sort():
        x = x_ref[...]
        xn = x * lax.rsqrt(jnp.mean(x * x, axis=-1, keepdims=True) + EPS) * nw_ref[...]
        hi, lo = _split_hi_lo(xn)
        r_hi, r_lo = _split_hi_lo(rw_ref[...])
        logits = (jnp.dot(hi, r_hi, preferred_element_type=F32)
                  + jnp.dot(lo, r_hi, preferred_element_type=F32)
                  + jnp.dot(hi, r_lo, preferred_element_type=F32)) + rb_ref[...]
        gates, g_idx = _router_gates(logits)
        lane = lax.broadcasted_iota(jnp.int32, (tm, LANES), 1)
        member = jnp.where(lane == g_idx, 1.0, 0.0)
        earlier = jnp.dot(ltri_ref[...], member.astype(BF16), preferred_element_type=F32)
        rank = jnp.sum(member * earlier, axis=-1, keepdims=True)
        count = jnp.sum(member, axis=0, keepdims=True)
        seg = jnp.floor((count + (MOE_ROW_BLOCK - 1.0)) * (1.0 / MOE_ROW_BLOCK)) * MOE_ROW_BLOCK
        lane1 = lax.broadcasted_iota(jnp.int32, (1, LANES), 1)
        s0, s1, s2 = seg[:, 0:1], seg[:, 1:2], seg[:, 2:3]
        start = jnp.where(lane1 == 1, s0, jnp.where(lane1 == 2, s0 + s1, jnp.where(lane1 == 3, s0 + s1 + s2, 0.0)))
        dest = rank + jnp.sum(member * start, axis=-1, keepdims=True)
        dest_ref[...] = jnp.broadcast_to(dest, (tm, LANES))
        dest_row = dest_ref[...].T[0:1, :]
        row_id = lax.broadcasted_iota(jnp.int32, (sorted_rows, tm), 0).astype(F32)
        perm = jnp.where(row_id == dest_row, 1.0, 0.0).astype(BF16)
        g_hi, g_mid, g_lo = _split3(gates)
        g_pieces = g_hi + pltpu.roll(g_mid, N_EXPERTS, 1) + pltpu.roll(g_lo, 2 * N_EXPERTS, 1)
        moved = jnp.dot(perm, jnp.concatenate([hi, g_pieces.astype(BF16)], axis=1), preferred_element_type=F32)
        xs_ref[...] = moved[:, :D_MODEL].astype(BF16)
        g_moved = moved[:, D_MODEL:]
        gs_ref[...] = (g_moved + pltpu.roll(g_moved, LANES - N_EXPERTS, 1)
                       + pltpu.roll(g_moved, LANES - 2 * N_EXPERTS, 1))
        ys_ref[...] = jnp.zeros(ys_ref.shape, F32)
        start_i = start.astype(jnp.int32)
        blocks_i = (seg * (1.0 / MOE_ROW_BLOCK)).astype(jnp.int32)
        for g in range(N_GROUPS):
            seg_ref[g] = start_i[0, g]
            seg_ref[N_GROUPS + g] = blocks_i[0, g]

    group = (e * MOE_EXPERTS_PER_STEP) // EXPERTS_PER_GROUP
    first_row = seg_ref[group]

    def row_block(b, carry):
        r0 = pl.multiple_of(first_row + b * MOE_ROW_BLOCK, MOE_ROW_BLOCK)
        xb = xs_ref[pl.ds(r0, MOE_ROW_BLOCK), :]
        gsb = gs_ref[pl.ds(r0, MOE_ROW_BLOCK), :]
        lane = lax.broadcasted_iota(jnp.int32, gsb.shape, 1)
        experts = range(MOE_EXPERTS_PER_STEP)
        gate = [jnp.dot(xb, wg_ref[j], preferred_element_type=F32) for j in experts]
        up = [jnp.dot(xb, wu_ref[j], preferred_element_type=F32) for j in experts]
        hmid = [(gate[j] * jax.nn.sigmoid(gate[j]) * up[j]).astype(BF16) for j in experts]
        down = [jnp.dot(hmid[j], wd_ref[j], preferred_element_type=F32) for j in experts]
        total = None
        for j in experts:
            g_e = jnp.sum(jnp.where(lane == e * MOE_EXPERTS_PER_STEP + j, gsb, 0.0), axis=-1, keepdims=True)
            total = g_e * down[j] if total is None else total + g_e * down[j]
        ys_ref[pl.ds(r0, MOE_ROW_BLOCK), :] += total
        return carry

    lax.fori_loop(0, seg_ref[N_GROUPS + group], row_block, 0)

    @pl.when(e == pl.num_programs(1) - 1)
    def _finish():
        col_id = lax.broadcasted_iota(jnp.int32, (tm, sorted_rows), 1).astype(F32)
        unperm = jnp.where(col_id == dest_ref[:, 0:1], 1.0, 0.0).astype(BF16)
        y = x_ref[...] + jnp.dot(unperm, ys_ref[...].astype(BF16), preferred_element_type=F32)
        if final_norm:
            y = y * lax.rsqrt(jnp.mean(y * y, axis=-1, keepdims=True) + EPS) * fw_ref[...]
        o_ref[...] = y


def _moe(x, nw, r_w, r_b, wg, wu, wd, final_w, final_norm, tm):
    T, D = x.shape
    const = lambda shape: pl.BlockSpec(shape, lambda i, e: (0,) * len(shape))
    sorted_rows = _moe_sorted_rows(tm)
    return pl.pallas_call(
        functools.partial(_moe_kernel, final_norm=final_norm, tm=tm),
        grid=(T // tm, N_EXPERTS // MOE_EXPERTS_PER_STEP),
        in_specs=[
            pl.BlockSpec((tm, D), lambda i, e: (i, 0)),
            const((1, D)), const((D, LANES)), const((1, LANES)),
            pl.BlockSpec((MOE_EXPERTS_PER_STEP, D, EXPERT_FF), lambda i, e: (e, 0, 0)),
            pl.BlockSpec((MOE_EXPERTS_PER_STEP, D, EXPERT_FF), lambda i, e: (e, 0, 0)),
            pl.BlockSpec((MOE_EXPERTS_PER_STEP, EXPERT_FF, D), lambda i, e: (e, 0, 0)),
            const((1, D)),
        ],
        out_specs=pl.BlockSpec((tm, D), lambda i, e: (i, 0)),
        out_shape=jax.ShapeDtypeStruct((T, D), F32),
        scratch_shapes=[pltpu.VMEM((sorted_rows, D), BF16), pltpu.VMEM((sorted_rows, LANES), F32),
                        pltpu.VMEM((sorted_rows, D), F32), pltpu.VMEM((tm, LANES), F32),
                        pltpu.VMEM((tm, tm), BF16), pltpu.SMEM((2 * N_GROUPS,), jnp.int32)],
        compiler_params=_params("arbitrary", "arbitrary"),
        name="moe",
    )(x, nw, r_w, r_b, wg, wu, wd, final_w)


def _tile(n, pref):
    t = min(n, pref)
    assert n % t == 0, (n, t)
    return t


def _prep_layer(l, norm1_w, w_in, ret_decay_logit, ret_gn_w, diff_lambda, diff_subln_w, conv_w, w_out, norm2_w,
                router_group_w, router_group_b, router_expert_w, router_expert_b, expert_w_gate, expert_w_up,
                expert_w_down):
    w = w_in[l]
    w_nat = jnp.concatenate([w[:, :_OFF_DQ], w[:, _OFF_DK:_OFF_DV], w[:, _OFF_CONV:]], axis=1).astype(BF16)
    w_t = jnp.concatenate([w[:, _OFF_DQ:_OFF_DK], w[:, _OFF_DV:_OFF_CONV]], axis=1).T.astype(BF16)
    router = jnp.zeros((D_MODEL, LANES), F32)
    router = router.at[:, :N_EXPERTS].set(router_expert_w[l]).at[:, N_EXPERTS:N_EXPERTS + N_GROUPS].set(
        router_group_w[l])
    r_b = jnp.zeros((1, LANES), F32).at[0, :N_EXPERTS].set(router_expert_b[l]).at[
        0, N_EXPERTS:N_EXPERTS + N_GROUPS].set(router_group_b[l])
    return dict(
        norm1=norm1_w[l][None, :], w_nat=w_nat, w_t=w_t,
        lg=jax.nn.log_sigmoid(ret_decay_logit[l].astype(F32)),
        gn_w=ret_gn_w[l][None, :].astype(F32),
        lam=diff_lambda[l].astype(F32), subln=diff_subln_w[l][:, None].astype(F32),
        conv_w=conv_w[l].astype(F32), w_out=w_out[l].astype(BF16),
        norm2=norm2_w[l][None, :], r_w=router, r_b=r_b,
        wg=expert_w_gate[l].astype(BF16), wu=expert_w_up[l].astype(BF16), wd=expert_w_down[l].astype(BF16),
    )


def _trunk(x, layers, final_w, slopes, pieces):
    B, S, D = x.shape
    tm = _tile(S, 512)
    tq = _tile(S, 512)
    tk = _tile(S, 256)
    t_moe = _tile(B * S, 1024)
    aux = _position_features(S)
    for l, lw in enumerate(layers):
        rqkv, rg, dk, dqvt, cv, kn = _in_proj(x, lw["norm1"], lw["w_nat"], lw["w_t"], tm)
        ret = _retention(lw["lg"], rqkv, rg, lw["gn_w"])
        diff = _diff_attention(slopes, pieces, lw["lam"], lw["subln"], kn, dqvt, dk, aux, l, tq, tk)
        x = _out_proj(x, ret, diff, cv, lw["conv_w"], lw["w_out"], tm)
        last = l == len(layers) - 1
        x = _moe(x.reshape(B * S, D), lw["norm2"], lw["r_w"], lw["r_b"], lw["wg"], lw["wu"],
                 lw["wd"], final_w, last, t_moe).reshape(B, S, D)
    return x


def kernel(x_prompt, x_sample, norm1_w, w_in, ret_decay_logit, ret_gn_w, diff_lambda, diff_subln_w, conv_w, w_out,
           norm2_w, router_group_w, router_group_b, router_expert_w, router_expert_b, expert_w_gate, expert_w_up,
           expert_w_down, final_norm_w):
    depth = w_in.shape[0]
    layers = [
        _prep_layer(l, norm1_w, w_in, ret_decay_logit, ret_gn_w, diff_lambda, diff_subln_w, conv_w, w_out, norm2_w,
                    router_group_w, router_group_b, router_expert_w, router_expert_b, expert_w_gate, expert_w_up,
                    expert_w_down)
        for l in range(depth)
    ]
    final_w = final_norm_w[None, :].astype(F32)
    slopes = (np.float32(2.0) ** (np.float32(-8.0) * np.arange(1, DIFF_HEADS + 1, dtype=np.float32)
                                  / np.float32(DIFF_HEADS))) * np.float32(LOG2E)
    s_hi = slopes.astype(BF16).astype(np.float32)
    s_mid = (slopes - s_hi).astype(BF16).astype(np.float32)
    s_lo = (slopes - s_hi - s_mid).astype(BF16).astype(np.float32)
    pieces = jnp.asarray(np.stack([s_hi, s_mid, s_lo], axis=1).reshape(-1))
    slopes = jnp.asarray(slopes)
    return (_trunk(x_prompt, layers, final_w, slopes, pieces), _trunk(x_sample, layers, final_w, slopes, pieces))
```

```python
import functools
import math

import jax
import jax.numpy as jnp
import numpy as np
from jax import lax
from jax.experimental import pallas as pl
from jax.experimental.pallas import tpu as pltpu

F32 = jnp.float32
BF16 = jnp.bfloat16

D_MODEL = 1024
RET_HEADS = 6
RET_HEAD_DIM = 64
RET_WIDTH = RET_HEADS * RET_HEAD_DIM
DIFF_HEADS = 6
DIFF_QK_DIM = 32
DIFF_V_DIM = 64
DIFF_WIDTH = DIFF_HEADS * DIFF_V_DIM
CONV_CHANNELS = 256
CONV_WIDTH = 3
N_GROUPS = 4
EXPERTS_PER_GROUP = 4
N_EXPERTS = 16
EXPERT_FF = 512
EPS = 1e-6

LANES = 128
HEAD_PAIRS = RET_HEADS // 2
RET_CHUNK = 128
SUBLANES = 8
VMEM_LIMIT = 56 * 1024 * 1024
LOG2E = math.log2(math.e)
NEG_BIG = -1e30

_OFF_RG = 3 * RET_WIDTH
_OFF_DQ = 4 * RET_WIDTH
_OFF_DK = _OFF_DQ + DIFF_WIDTH
_OFF_DV = _OFF_DK + DIFF_WIDTH
_OFF_CONV = _OFF_DV + DIFF_WIDTH


def _params(*sem):
    return pltpu.CompilerParams(dimension_semantics=sem, vmem_limit_bytes=VMEM_LIMIT)


def _split_hi_lo(x):
    hi = x.astype(BF16)
    lo = (x - hi.astype(F32)).astype(BF16)
    return hi, lo


def _in_proj_kernel(x_ref, nw_ref, w_ref, wt_ref, rqkv_ref, rg_ref, dk_ref, dqvt_ref, cv_ref, kn_ref):
    x = x_ref[0]
    ms = jnp.mean(x * x, axis=-1, keepdims=True)
    h = (x * lax.rsqrt(ms + EPS) * nw_ref[...]).astype(BF16)
    a = jnp.dot(h, w_ref[:, 0:_OFF_DQ], preferred_element_type=F32)
    rqkv_ref[0, :, 0:RET_WIDTH] = a[:, 0:RET_WIDTH].astype(BF16)
    rqkv_ref[0, :, RET_WIDTH:2 * RET_WIDTH] = (a[:, RET_WIDTH:2 * RET_WIDTH] * RET_HEAD_DIM ** -0.5).astype(BF16)
    rqkv_ref[0, :, 2 * RET_WIDTH:3 * RET_WIDTH] = a[:, 2 * RET_WIDTH:3 * RET_WIDTH].astype(BF16)
    rg_ref[0] = a[:, _OFF_RG:_OFF_DQ]
    dk = jnp.dot(h, w_ref[:, _OFF_DQ:_OFF_DQ + DIFF_WIDTH], preferred_element_type=F32).astype(BF16)
    dk_ref[0] = dk
    dkf = dk.astype(F32)
    sq_hi, sq_lo = _split_hi_lo(dkf * dkf)
    grp = (lax.shift_right_logical(lax.broadcasted_iota(jnp.int32, (DIFF_WIDTH, LANES), 0), 5)
           == lax.broadcasted_iota(jnp.int32, (DIFF_WIDTH, LANES), 1))
    grp = jnp.where(grp, 1.0, 0.0).astype(BF16)
    kn2 = jnp.dot(sq_hi, grp, preferred_element_type=F32) + jnp.dot(sq_lo, grp, preferred_element_type=F32)
    kn_ref[0, 0] = jnp.broadcast_to(jnp.max(kn2, axis=0, keepdims=True), (SUBLANES, LANES))
    cv_ref[0] =jnp.dot(h, w_ref[:, _OFF_DQ + DIFF_WIDTH:], preferred_element_type=F32)
    t = lax.dot_general(wt_ref[...], h, (((1,), (1,)), ((), ())), preferred_element_type=F32)
    dqvt_ref[0, 0:DIFF_WIDTH, :] = (t[0:DIFF_WIDTH] * (DIFF_QK_DIM ** -0.5 * LOG2E)).astype(BF16)
    dqvt_ref[0, DIFF_WIDTH:, :] = t[DIFF_WIDTH:].astype(BF16)


def _in_proj(x, nw, w_nat, w_t, tm):
    B, S, D = x.shape
    n_nat = w_nat.shape[1]
    return pl.pallas_call(
        _in_proj_kernel,
        grid=(B, S // tm),
        in_specs=[
            pl.BlockSpec((1, tm, D), lambda b, i: (b, i, 0)),
            pl.BlockSpec((1, D), lambda b, i: (0, 0)),
            pl.BlockSpec((D, n_nat), lambda b, i: (0, 0)),
            pl.BlockSpec((2 * DIFF_WIDTH, D), lambda b, i: (0, 0)),
        ],
        out_specs=[
            pl.BlockSpec((1, tm, 3 * RET_WIDTH), lambda b, i: (b, i, 0)),
            pl.BlockSpec((1, tm, RET_WIDTH), lambda b, i: (b, i, 0)),
            pl.BlockSpec((1, tm, DIFF_WIDTH), lambda b, i: (b, i, 0)),
            pl.BlockSpec((1, 2 * DIFF_WIDTH, tm), lambda b, i: (b, 0, i)),
            pl.BlockSpec((1, tm, 3 * CONV_CHANNELS), lambda b, i: (b, i, 0)),
            pl.BlockSpec((1, 1, SUBLANES, LANES), lambda b, i: (b, i, 0, 0)),
        ],
        out_shape=[
            jax.ShapeDtypeStruct((B, S, 3 * RET_WIDTH), BF16),
            jax.ShapeDtypeStruct((B, S, RET_WIDTH), F32),
            jax.ShapeDtypeStruct((B, S, DIFF_WIDTH), BF16),
            jax.ShapeDtypeStruct((B, 2 * DIFF_WIDTH, S), BF16),
            jax.ShapeDtypeStruct((B, S, 3 * CONV_CHANNELS), F32),
            jax.ShapeDtypeStruct((B, S // tm, SUBLANES, LANES), F32),
        ],
        compiler_params=_params("parallel", "parallel"),
        name="in_proj",
    )(x, nw, w_nat, w_t)


def _pair_lane_value(lg_ref, direction, pair, shape, axis):
    idx = lax.broadcasted_iota(jnp.int32, shape, axis)
    return jnp.where(idx < RET_HEAD_DIM, lg_ref[direction, 2 * pair], lg_ref[direction, 2 * pair + 1])


def _same_head_mask():
    r = lax.broadcasted_iota(jnp.int32, (LANES, LANES), 0)
    c = lax.broadcasted_iota(jnp.int32, (LANES, LANES), 1)
    return (r < RET_HEAD_DIM) == (c < RET_HEAD_DIM)


def _ret_fwd_kernel(lg_ref, qkv_ref, y_ref, state_ref, dmat_ref, qdec_ref, kdec_ref, cdec_ref, *, chunks):
    C = RET_CHUNK

    @pl.when((pl.program_id(0) == 0) & (pl.program_id(1) == 0))
    def _build_tables():
        diff = (lax.broadcasted_iota(jnp.int32, (C, C), 0) - lax.broadcasted_iota(jnp.int32, (C, C), 1)).astype(F32)
        for h in range(RET_HEADS):
            lower = jnp.exp(jnp.maximum(diff, 0.0) * lg_ref[0, h])
            upper = jnp.exp(jnp.maximum(-diff, 0.0) * lg_ref[1, h])
            dmat_ref[h] = jnp.where(diff >= 0, lower, upper)
        pos = lax.broadcasted_iota(jnp.int32, (C, LANES), 0).astype(F32)
        for p in range(HEAD_PAIRS):
            lg_lane = _pair_lane_value(lg_ref, 0, p, (C, LANES), 1)
            qdec_ref[p] = jnp.exp((pos + 1.0) * lg_lane)
            kdec_ref[p] = jnp.exp((C - 1.0 - pos) * lg_lane)
            cdec_ref[p] = jnp.exp(float(C) * _pair_lane_value(lg_ref, 0, p, (LANES, LANES), 0))

    @pl.when(pl.program_id(1) == 0)
    def _reset():
        state_ref[...] = jnp.zeros(state_ref.shape, F32)

    lane = lax.broadcasted_iota(jnp.int32, (C, LANES), 1)
    same_head = _same_head_mask()
    def operands(p, ci):
        rows = slice(ci * C, (ci + 1) * C)
        return (qkv_ref[0, rows, p * LANES:(p + 1) * LANES],
                qkv_ref[0, rows, RET_WIDTH + p * LANES:RET_WIDTH + (p + 1) * LANES],
                qkv_ref[0, rows, 2 * RET_WIDTH + p * LANES:2 * RET_WIDTH + (p + 1) * LANES])

    items = [(p, ci) for p in range(HEAD_PAIRS) for ci in range(chunks)]
    scores, kvs = {}, {}
    for p, ci in items:
        q, k, v = operands(p, ci)
        for h in range(2):
            head_lanes = (lane >= h * RET_HEAD_DIM) & (lane < (h + 1) * RET_HEAD_DIM)
            qm = jnp.where(head_lanes, q, jnp.zeros_like(q))
            scores[p, ci, h] = lax.dot_general(qm, k, (((1,), (1,)), ((), ())), preferred_element_type=F32)
        kd = (k.astype(F32) * kdec_ref[p]).astype(BF16)
        kvs[p, ci] = lax.dot_general(kd, v, (((0,), (0,)), ((), ())), preferred_element_type=F32)
    states = {}
    for p in range(HEAD_PAIRS):
        state = state_ref[p]
        for ci in range(chunks):
            states[p, ci] = state.astype(BF16)
            state = cdec_ref[p] * state + jnp.where(same_head, kvs[p, ci], 0.0)
        state_ref[p] = state
    for p, ci in items:
        q, _, v = operands(p, ci)
        inner = [jnp.dot((scores[p, ci, h] * dmat_ref[2 * p + h]).astype(BF16), v, preferred_element_type=F32)
                 for h in range(2)]
        cross = jnp.dot(q, states[p, ci], preferred_element_type=F32) * qdec_ref[p]
        y_ref[0, ci * C:(ci + 1) * C, p * LANES:(p + 1) * LANES] = (
            jnp.where(lane < RET_HEAD_DIM, inner[0], inner[1]) + cross)


def _ret_bwd_kernel(lg_ref, qkv_ref, y1_ref, g_ref, gnw_ref, o_ref, state_ref, qdec_ref, kdec_ref, cdec_ref, *,
                    chunks):
    C = RET_CHUNK

    @pl.when((pl.program_id(0) == 0) & (pl.program_id(1) == 0))
    def _build_tables():
        pos = lax.broadcasted_iota(jnp.int32, (C, LANES), 0).astype(F32)
        for p in range(HEAD_PAIRS):
            lg_lane = _pair_lane_value(lg_ref, 1, p, (C, LANES), 1)
            qdec_ref[p] = jnp.exp((float(C) - pos) * lg_lane)
            kdec_ref[p] = jnp.exp(pos * lg_lane)
            cdec_ref[p] = jnp.exp(float(C) * _pair_lane_value(lg_ref, 1, p, (LANES, LANES), 0))

    @pl.when(pl.program_id(1) == 0)
    def _reset():
        state_ref[...] = jnp.zeros(state_ref.shape, F32)

    same_head = _same_head_mask()
    head_avg = jnp.where(same_head, 1.0 / RET_HEAD_DIM, 0.0).astype(BF16)

    def head_mean(t):
        hi, lo = _split_hi_lo(t)
        return (jnp.dot(hi, head_avg, preferred_element_type=F32)
                + jnp.dot(lo, head_avg, preferred_element_type=F32))

    items = [(p, ci) for p in range(HEAD_PAIRS) for ci in range(chunks)]
    kvs = {}
    for p, ci in items:
        rows = slice(ci * C, (ci + 1) * C)
        k = qkv_ref[0, rows, RET_WIDTH + p * LANES:RET_WIDTH + (p + 1) * LANES]
        v = qkv_ref[0, rows, 2 * RET_WIDTH + p * LANES:2 * RET_WIDTH + (p + 1) * LANES]
        kd = (k.astype(F32) * kdec_ref[p]).astype(BF16)
        kvs[p, ci] = lax.dot_general(kd, v, (((0,), (0,)), ((), ())), preferred_element_type=F32)
    states = {}
    for p in range(HEAD_PAIRS):
        state = state_ref[p]
        for ci in reversed(range(chunks)):
            states[p, ci] = state.astype(BF16)
            state = cdec_ref[p] * state + jnp.where(same_head, kvs[p, ci], 0.0)
        state_ref[p] = state
    ys = {}
    for p, ci in items:
        rows = slice(ci * C, (ci + 1) * C)
        sl = slice(p * LANES, (p + 1) * LANES)
        cross = jnp.dot(qkv_ref[0, rows, sl], states[p, ci], preferred_element_type=F32)
        ys[p, ci] = y1_ref[0, rows, sl] + cross * qdec_ref[p]
    means = {key: head_mean(y) for key, y in ys.items()}
    devs = {key: ys[key] - means[key] for key in ys}
    variances = {key: head_mean(d * d) for key, d in devs.items()}
    for p, ci in items:
        rows = slice(ci * C, (ci + 1) * C)
        sl = slice(p * LANES, (p + 1) * LANES)
        g = g_ref[0, rows, sl]
        o = devs[p, ci] * lax.rsqrt(variances[p, ci] + EPS) * gnw_ref[:, sl] * (g * jax.nn.sigmoid(g))
        o_ref[0, rows, sl] = o.astype(o_ref.dtype)


def _retention(lg, rqkv, rg, gn_w):
    B, S, _ = rqkv.shape
    chunks = math.gcd(S // RET_CHUNK, 4)
    C = chunks * RET_CHUNK
    n = S // C
    smem = pl.BlockSpec(memory_space=pltpu.SMEM)
    table = pltpu.VMEM((HEAD_PAIRS, RET_CHUNK, LANES), F32)
    state = pltpu.VMEM((HEAD_PAIRS, LANES, LANES), F32)
    y1 = pl.pallas_call(
        functools.partial(_ret_fwd_kernel, chunks=chunks),
        grid=(B, n),
        in_specs=[smem, pl.BlockSpec((1, C, 3 * RET_WIDTH), lambda b, i: (b, i, 0))],
        out_specs=pl.BlockSpec((1, C, RET_WIDTH), lambda b, i: (b, i, 0)),
        out_shape=jax.ShapeDtypeStruct((B, S, RET_WIDTH), F32),
        scratch_shapes=[state, pltpu.VMEM((RET_HEADS, RET_CHUNK, RET_CHUNK), F32), table, table, state],
        compiler_params=_params("arbitrary", "arbitrary"),
        name="ret_fwd",
    )(lg, rqkv)
    rev = lambda b, i: (b, n - 1 - i, 0)
    return pl.pallas_call(
        functools.partial(_ret_bwd_kernel, chunks=chunks),
        grid=(B, n),
        in_specs=[
            smem,
            pl.BlockSpec((1, C, 3 * RET_WIDTH), rev),
            pl.BlockSpec((1, C, RET_WIDTH), rev),
            pl.BlockSpec((1, C, RET_WIDTH), rev),
            pl.BlockSpec((1, RET_WIDTH), lambda b, i: (0, 0)),
        ],
        out_specs=pl.BlockSpec((1, C, RET_WIDTH), rev),
        out_shape=jax.ShapeDtypeStruct((B, S, RET_WIDTH), BF16),
        scratch_shapes=[state, table, table, state],
        compiler_params=_params("arbitrary", "arbitrary"),
        name="ret_bwd",
    )(lg, rqkv, y1, rg, gn_w)


AUX_BLOCK = 256
AUX_ROWS = 16
V_ROWS = DIFF_V_DIM + SUBLANES
GUARD_LOG2 = 60.0
UNDERFLOW_LOG2 = 150.0


def _position_features(seq):
    j = jnp.arange(seq, dtype=jnp.int32)
    ones = jnp.ones((seq,), F32)
    n = (j // AUX_BLOCK).astype(F32)
    jc = (j % AUX_BLOCK - AUX_BLOCK // 2).astype(F32)
    feat = jnp.stack([ones] * 3 + [n] * 3 + [jc] * 3, axis=1)
    return jnp.pad(feat, ((0, 0), (0, LANES - feat.shape[1]))).astype(BF16)


def _split3(x):
    hi = x.astype(BF16).astype(F32)
    r = x - hi
    mid = r.astype(BF16).astype(F32)
    return hi, mid, (r - mid).astype(BF16).astype(F32)


def _attn_kernel_per_head(slopes_ref, pieces_ref, lam_ref, sw_ref, kn_ref, qt_ref, k_ref, aux_ref, vt_ref, o_ref,
                          m_ref, acc_ref, rhs_ref, pa_ref, pb_ref, dist_ref, sd_ref, *, seq, tq, tk, lam_init):
    pair = pl.program_id(1)
    qi = pl.program_id(2)
    q0 = qi * tq
    nk = seq // tk
    per = tq // tk
    kd0 = qi * per
    qt = qt_ref[0]
    row = lax.broadcasted_iota(jnp.int32, (LANES, tq), 0)
    q_masked = []
    for hc in range(4):
        lo = hc * DIFF_QK_DIM
        q_masked.append(jnp.where((row >= lo) & (row < lo + DIFF_QK_DIM), qt, jnp.zeros_like(qt)))
    rel = (lax.broadcasted_iota(jnp.int32, (tk, tq), 1) - lax.broadcasted_iota(jnp.int32, (tk, tq), 0)).astype(F32)

    @pl.when(qi == 0)
    def _distance_table():
        d = lax.broadcasted_iota(jnp.int32, (tq, tq), 1) - lax.broadcasted_iota(jnp.int32, (tq, tq), 0)
        dist_ref[...] = jnp.abs(d).astype(F32)

    kn2 = jnp.max(jnp.max(kn_ref[0], axis=0), axis=0, keepdims=True)
    kn_lane = lax.broadcasted_iota(jnp.int32, (1, LANES), 1)
    ipos = (q0 + lax.broadcasted_iota(jnp.int32, (1, tq), 1)).astype(F32)
    aug_row = lax.broadcasted_iota(jnp.int32, (AUX_ROWS, tq), 0)

    def v_rows_with_ones(h, k0, n):
        vt = vt_ref[0, h * DIFF_V_DIM:(h + 1) * DIFF_V_DIM, pl.ds(k0, n)]
        return jnp.concatenate([vt, jnp.ones((SUBLANES, n), BF16)], axis=0)

    kd = pl.multiple_of(q0, tq)
    kdiag = k_ref[0, pl.ds(kd, tq), :]
    for hc in range(4):
        s = jnp.dot(kdiag, q_masked[hc], preferred_element_type=F32)
        sd_ref[hc] = s - dist_ref[...] * slopes_ref[2 * pair + hc // 2]
    for hc in range(4):
        m_ref[hc] = jnp.max(sd_ref[hc], axis=0, keepdims=True)
    probs = [jnp.exp2(sd_ref[hc] - m_ref[hc]).astype(BF16) for hc in range(4)]
    for hc in range(4):
        acc_ref[hc] = jnp.dot(v_rows_with_ones(hc // 2, kd, tq), probs[hc], preferred_element_type=F32)

    for h in range(2):
        slope = slopes_ref[2 * pair + h]

        def v_with_ones(k0, h=h):
            return v_rows_with_ones(h, k0, tk)

        def online_step(kb, carry, h=h, slope=slope, v_with_ones=v_with_ones):
            k0 = pl.multiple_of(kb * tk, tk)
            kblk = k_ref[0, pl.ds(k0, tk), :]
            v_ones = v_with_ones(k0)
            bias = jnp.abs(rel + (q0 - k0).astype(F32)) * slope
            for c in range(2):
                hc = 2 * h + c
                s = jnp.dot(kblk, q_masked[hc], preferred_element_type=F32) - bias
                m_old = m_ref[hc]
                m_new = jnp.maximum(m_old, jnp.max(s, axis=0, keepdims=True))
                p = jnp.exp2(s - m_new).astype(BF16)
                acc_ref[hc] = jnp.exp2(m_old - m_new) * acc_ref[hc] + jnp.dot(v_ones, p, preferred_element_type=F32)
                m_ref[hc] = m_new
            return carry

        excess = None
        for c in range(2):
            hc = 2 * h + c
            qf = q_masked[hc].astype(F32)
            qn2 = jnp.sum(qf * qf, axis=0, keepdims=True)
            kn2_hc = jnp.max(jnp.where(kn_lane == 4 * pair + hc, kn2, 0.0), axis=-1, keepdims=True)
            bound = jnp.sqrt(qn2 * kn2_hc) * 1.01 + 1e-3
            e = jnp.max(bound - m_ref[hc], axis=-1, keepdims=True)
            excess = e if excess is None else jnp.maximum(excess, e)
        reach = (excess + UNDERFLOW_LOG2) / slope
        q0f = q0.astype(F32)
        lo_blk = jnp.clip(jnp.floor((q0f - (tk - 1.0) - reach) / tk), 0.0, float(nk))
        hi_blk = jnp.clip(jnp.floor((reach + q0f + (tq - 1.0)) / tk) + 1.0, 0.0, float(nk))
        lo_blk = jnp.minimum(lo_blk.astype(jnp.int32)[0, 0], kd0)
        hi_blk = jnp.maximum(hi_blk.astype(jnp.int32)[0, 0], kd0 + per)
        fast = jnp.where(excess <= GUARD_LOG2, 1, 0).astype(jnp.int32)[0, 0] == 1

        @pl.when(fast)
        def _single_pass(h=h, slope=slope, v_with_ones=v_with_ones, lo_blk=lo_blk, hi_blk=hi_blk):
            pieces = [pieces_ref[(2 * pair + h) * 3 + i] for i in range(3)]
            for c in range(2):
                hc = 2 * h + c
                for side, sign in ((0, 1.0), (1, -1.0)):
                    const = (-sign) * (slope * ipos) - m_ref[hc] + sign * (0.5 * AUX_BLOCK) * slope
                    c_hi, c_mid, c_lo = _split3(const)
                    aug = jnp.where(aug_row == 0, c_hi,
                                    jnp.where(aug_row == 1, c_mid, jnp.where(aug_row == 2, c_lo, 0.0)))
                    for i in range(3):
                        aug = jnp.where(aug_row == 3 + i, sign * AUX_BLOCK * pieces[i], aug)
                        aug = jnp.where(aug_row == 6 + i, sign * pieces[i], aug)
                    rhs_ref[side * 4 + hc, 0:LANES, :] = q_masked[hc]
                    rhs_ref[side * 4 + hc, LANES:LANES + AUX_ROWS, :] = aug.astype(BF16)
                    rhs_ref[side * 4 + hc, LANES + AUX_ROWS:, :] = jnp.zeros((LANES - AUX_ROWS, tq), BF16)

            odd = (kd0 - lo_blk + hi_blk - kd0 - per) % 2
            lo_even = jnp.where((odd == 1) & (lo_blk > 0), lo_blk - 1, lo_blk)
            hi_even = jnp.where((odd == 1) & (lo_blk == 0), hi_blk + 1, hi_blk)
            n_lo = kd0 - lo_even
            total = n_lo + hi_even - kd0 - per

            def block_of(i):
                return jnp.where(i < n_lo, lo_even + i, kd0 + per + i - n_lo)

            def stage_a(pair_idx, p_ref):
                for half in range(2):
                    i = 2 * pair_idx + half
                    k0 = pl.multiple_of(block_of(i) * tk, tk)
                    side = jnp.where(i < n_lo, 0, 1)
                    lhs = jnp.concatenate([k_ref[0, pl.ds(k0, tk), :], aux_ref[pl.ds(k0, tk), :]], axis=1)
                    for c in range(2):
                        t = jnp.dot(lhs, rhs_ref[side * 4 + 2 * h + c], preferred_element_type=F32)
                        p_ref[2 * half + c] = jnp.exp2(t).astype(BF16)

            def stage_b(pair_idx, p_ref):
                va = v_with_ones(pl.multiple_of(block_of(2 * pair_idx) * tk, tk))
                vb = v_with_ones(pl.multiple_of(block_of(2 * pair_idx + 1) * tk, tk))
                for c in range(2):
                    acc_ref[2 * h + c] += (jnp.dot(va, p_ref[c], preferred_element_type=F32)
                                           + jnp.dot(vb, p_ref[2 + c], preferred_element_type=F32))

            n_pairs = total // 2

            @pl.when(n_pairs > 0)
            def _blocks():
                stage_a(0, pa_ref)

                def pairs_body(count):
                    def body(j, carry):
                        for s in range(0, count, 2):
                            stage_a(count * j + s + 1, pb_ref)
                            stage_b(count * j + s, pa_ref)
                            stage_a(count * j + s + 2, pa_ref)
                            stage_b(count * j + s + 1, pb_ref)
                        return carry
                    return body

                long_iters = (n_pairs - 1) // 4
                lax.fori_loop(0, long_iters, pairs_body(4), 0)
                lax.fori_loop(2 * long_iters, (n_pairs - 1) // 2, pairs_body(2), 0)

                @pl.when(n_pairs % 2 == 1)
                def _last_one():
                    stage_b(n_pairs - 1, pa_ref)

                @pl.when(n_pairs % 2 == 0)
                def _last_two():
                    stage_a(n_pairs - 1, pb_ref)
                    stage_b(n_pairs - 2, pa_ref)
                    stage_b(n_pairs - 1, pb_ref)

        @pl.when(jnp.logical_not(fast))
        def _online(online_step=online_step):
            lax.fori_loop(0, kd0, online_step, 0)
            lax.fori_loop(kd0 + per, nk, online_step, 0)

    lam = lam_ref[...]
    lam_full = (jnp.exp(jnp.sum(lam[0:1] * lam[1:2], axis=-1, keepdims=True))
                - jnp.exp(jnp.sum(lam[2:3] * lam[3:4], axis=-1, keepdims=True)) + lam_init)
    outs = []
    for h in range(2):
        a1 = acc_ref[2 * h]
        a2 = acc_ref[2 * h + 1]
        rows = slice(0, DIFF_V_DIM)
        ones_row = slice(DIFF_V_DIM, DIFF_V_DIM + 1)
        o = a1[rows] / a1[ones_row] - lam_full * (a2[rows] / a2[ones_row])
        o = o * lax.rsqrt(jnp.mean(o * o, axis=0, keepdims=True) + EPS)
        outs.append(o * sw_ref[...] * (1.0 - lam_init))
    o_ref[0] = jnp.concatenate(outs, axis=0).T.astype(o_ref.dtype)


def _attn_kernel(slopes_ref, pieces_ref, lam_ref, sw_ref, kn_ref, qt_ref, k_ref, aux_ref, vt_ref, o_ref,
                 m_ref, acc_ref, rhs_ref, pa_ref, pb_ref, dist_ref, sd_ref, *, seq, tq, tk, lam_init):
    pair = pl.program_id(1)
    qi = pl.program_id(2)
    q0 = qi * tq
    nk = seq // tk
    per = tq // tk
    kd0 = qi * per
    qt = qt_ref[0]
    row = lax.broadcasted_iota(jnp.int32, (LANES, tq), 0)
    q_masked = []
    for hc in range(4):
        lo = hc * DIFF_QK_DIM
        q_masked.append(jnp.where((row >= lo) & (row < lo + DIFF_QK_DIM), qt, jnp.zeros_like(qt)))

    @pl.when(qi == 0)
    def _distance_table():
        d = lax.broadcasted_iota(jnp.int32, (tq, tq), 1) - lax.broadcasted_iota(jnp.int32, (tq, tq), 0)
        dist_ref[...] = jnp.abs(d).astype(F32)

    kd = pl.multiple_of(q0, tq)
    kdiag = k_ref[0, pl.ds(kd, tq), :]
    self_prod = qt.astype(F32) * kdiag.astype(F32).T
    shifts = [jnp.sum(self_prod[hc * DIFF_QK_DIM:(hc + 1) * DIFF_QK_DIM], axis=0, keepdims=True) for hc in range(4)]

    kn2 = jnp.max(jnp.max(kn_ref[0], axis=0), axis=0, keepdims=True)
    kn_lane = lax.broadcasted_iota(jnp.int32, (1, LANES), 1)
    excess = []
    for h in range(2):
        worst = None
        for c in range(2):
            hc = 2 * h + c
            qf = q_masked[hc].astype(F32)
            qn2 = jnp.sum(qf * qf, axis=0, keepdims=True)
            kn2_hc = jnp.max(jnp.where(kn_lane == 4 * pair + hc, kn2, 0.0), axis=-1, keepdims=True)
            bound = jnp.sqrt(qn2 * kn2_hc) * 1.01 + 1e-3
            e = jnp.max(bound - shifts[hc], axis=-1, keepdims=True)
            worst = e if worst is None else jnp.maximum(worst, e)
        excess.append(worst)
    single_pass = jnp.where(jnp.maximum(excess[0], excess[1]) <= GUARD_LOG2, 1, 0).astype(jnp.int32)[0, 0] == 1

    def v_rows_with_ones(h, k0, n):
        vt = vt_ref[0, h * DIFF_V_DIM:(h + 1) * DIFF_V_DIM, pl.ds(k0, n)]
        return jnp.concatenate([vt, jnp.ones((SUBLANES, n), BF16)], axis=0)

    def keys_with_features(k0):
        return jnp.concatenate([k_ref[0, pl.ds(k0, tk), :], aux_ref[pl.ds(k0, tk), :]], axis=1)

    @pl.when(single_pass)
    def _single_pass():
        ipos = (q0 + lax.broadcasted_iota(jnp.int32, (1, tq), 1)).astype(F32)
        aug_row = lax.broadcasted_iota(jnp.int32, (AUX_ROWS, tq), 0)
        q0f = q0.astype(F32)
        lo_even, n_lo, totals = [], [], []
        for h in range(2):
            slope = slopes_ref[2 * pair + h]
            pieces = [pieces_ref[(2 * pair + h) * 3 + i] for i in range(3)]
            for c in range(2):
                hc = 2 * h + c
                for side, sign in ((0, 1.0), (1, -1.0), (2, 0.0)):
                    const = (-sign) * (slope * ipos) - shifts[hc] + sign * (0.5 * AUX_BLOCK) * slope
                    c_hi, c_mid, c_lo = _split3(const)
                    aug = jnp.where(aug_row == 0, c_hi,
                                    jnp.where(aug_row == 1, c_mid, jnp.where(aug_row == 2, c_lo, 0.0)))
                    for i in range(3):
                        aug = jnp.where(aug_row == 3 + i, sign * AUX_BLOCK * pieces[i], aug)
                        aug = jnp.where(aug_row == 6 + i, sign * pieces[i], aug)
                    rhs_ref[side * 4 + hc, 0:LANES, :] = q_masked[hc]
                    rhs_ref[side * 4 + hc, LANES:LANES + AUX_ROWS, :] = aug.astype(BF16)
                    rhs_ref[side * 4 + hc, LANES + AUX_ROWS:, :] = jnp.zeros((LANES - AUX_ROWS, tq), BF16)
            reach = (excess[h] + UNDERFLOW_LOG2) / slope
            lo_blk = jnp.clip(jnp.floor((q0f - (tk - 1.0) - reach) / tk), 0.0, float(nk))
            hi_blk = jnp.clip(jnp.floor((reach + q0f + (tq - 1.0)) / tk) + 1.0, 0.0, float(nk))
            lo_blk = jnp.minimum(lo_blk.astype(jnp.int32)[0, 0], kd0)
            hi_blk = jnp.maximum(hi_blk.astype(jnp.int32)[0, 0], kd0 + per)
            odd = (kd0 - lo_blk + hi_blk - kd0 - per) % 2
            lo_h = jnp.where((odd == 1) & (lo_blk > 0), lo_blk - 1, lo_blk)
            hi_h = jnp.where((odd == 1) & (lo_blk == 0), hi_blk + 1, hi_blk)
            lo_even.append(lo_h)
            n_lo.append(kd0 - lo_h)
            totals.append(kd0 - lo_h + hi_h - kd0 - per)
        pairs_h0 = totals[0] // 2
        n_pairs = pairs_h0 + totals[1] // 2

        def locate(x, half):
            head = jnp.where(x >= pairs_h0, 1, 0)
            i = 2 * (x - head * pairs_h0) + half
            before = jnp.where(head == 1, n_lo[1], n_lo[0])
            first = jnp.where(head == 1, lo_even[1], lo_even[0])
            blk = jnp.where(i < before, first + i, kd0 + per + i - before)
            return head, blk, jnp.where(i < before, 0, 1)

        def stage_a(x, p_ref):
            for half in range(2):
                head, blk, side = locate(x, half)
                lhs = keys_with_features(pl.multiple_of(blk * tk, tk))
                for c in range(2):
                    t = jnp.dot(lhs, rhs_ref[side * 4 + 2 * head + c], preferred_element_type=F32)
                    p_ref[2 * half + c] = jnp.exp2(t).astype(BF16)

        def stage_b(x, p_ref):
            head, blk_a, _ = locate(x, 0)
            _, blk_b, _ = locate(x, 1)
            rows = pl.ds(pl.multiple_of(head * DIFF_V_DIM, DIFF_V_DIM), DIFF_V_DIM)
            ones = jnp.ones((SUBLANES, tk), BF16)
            va = jnp.concatenate([vt_ref[0, rows, pl.ds(pl.multiple_of(blk_a * tk, tk), tk)], ones], axis=0)
            vb = jnp.concatenate([vt_ref[0, rows, pl.ds(pl.multiple_of(blk_b * tk, tk), tk)], ones], axis=0)
            for c in range(2):
                acc_ref[2 * head + c] += (jnp.dot(va, p_ref[c], preferred_element_type=F32)
                                          + jnp.dot(vb, p_ref[2 + c], preferred_element_type=F32))

        def diag_a(h, p_ref):
            for half in range(2):
                lhs = keys_with_features(pl.multiple_of(q0 + half * tk, tk))
                bias = dist_ref[half * tk:(half + 1) * tk, :] * slopes_ref[2 * pair + h]
                for c in range(2):
                    t = jnp.dot(lhs, rhs_ref[8 + 2 * h + c], preferred_element_type=F32) - bias
                    p_ref[2 * half + c] = jnp.exp2(t).astype(BF16)

        def diag_b(h, p_ref):
            va = v_rows_with_ones(h, pl.multiple_of(q0, tk), tk)
            vb = v_rows_with_ones(h, pl.multiple_of(q0 + tk, tk), tk)
            for c in range(2):
                acc_ref[2 * h + c] = (jnp.dot(va, p_ref[c], preferred_element_type=F32)
                                      + jnp.dot(vb, p_ref[2 + c], preferred_element_type=F32))

        diag_a(0, pa_ref)
        diag_a(1, pb_ref)
        diag_b(0, pa_ref)

        @pl.when(n_pairs == 0)
        def _diagonal_only():
            diag_b(1, pb_ref)

        @pl.when(n_pairs > 0)
        def _blocks():
            stage_a(0, pa_ref)
            diag_b(1, pb_ref)

            def pairs_body(count):
                def body(j, carry):
                    for s in range(0, count, 2):
                        stage_a(count * j + s + 1, pb_ref)
                        stage_b(count * j + s, pa_ref)
                        stage_a(count * j + s + 2, pa_ref)
                        stage_b(count * j + s + 1, pb_ref)
                    return carry
                return body

            long_iters = (n_pairs - 1) // 4
            lax.fori_loop(0, long_iters, pairs_body(4), 0)
            lax.fori_loop(2 * long_iters, (n_pairs - 1) // 2, pairs_body(2), 0)

            @pl.when(n_pairs % 2 == 1)
            def _last_one():
                stage_b(n_pairs - 1, pa_ref)

            @pl.when(n_pairs % 2 == 0)
            def _last_two():
                stage_a(n_pairs - 1, pb_ref)
                stage_b(n_pairs - 2, pa_ref)
                stage_b(n_pairs - 1, pb_ref)

    @pl.when(jnp.logical_not(single_pass))
    def _online():
        for hc in range(4):
            s = jnp.dot(kdiag, q_masked[hc], preferred_element_type=F32)
            sd_ref[hc] = s - dist_ref[...] * slopes_ref[2 * pair + hc // 2]
        for hc in range(4):
            m_ref[hc] = jnp.max(sd_ref[hc], axis=0, keepdims=True)
        probs = [jnp.exp2(sd_ref[hc] - m_ref[hc]).astype(BF16) for hc in range(4)]
        for hc in range(4):
            acc_ref[hc] = jnp.dot(v_rows_with_ones(hc // 2, kd, tq), probs[hc], preferred_element_type=F32)
        rel = (lax.broadcasted_iota(jnp.int32, (tk, tq), 1)
               - lax.broadcasted_iota(jnp.int32, (tk, tq), 0)).astype(F32)
        for h in range(2):
            def online_step(kb, carry, h=h):
                k0 = pl.multiple_of(kb * tk, tk)
                kblk = k_ref[0, pl.ds(k0, tk), :]
                v_ones = v_rows_with_ones(h, k0, tk)
                bias = jnp.abs(rel + (q0 - k0).astype(F32)) * slopes_ref[2 * pair + h]
                for c in range(2):
                    hc = 2 * h + c
                    s = jnp.dot(kblk, q_masked[hc], preferred_element_type=F32) - bias
                    m_old = m_ref[hc]
                    m_new = jnp.maximum(m_old, jnp.max(s, axis=0, keepdims=True))
                    p = jnp.exp2(s - m_new).astype(BF16)
                    acc_ref[hc] = (jnp.exp2(m_old - m_new) * acc_ref[hc]
                                   + jnp.dot(v_ones, p, preferred_element_type=F32))
                    m_ref[hc] = m_new
                return carry

            lax.fori_loop(0, kd0, online_step, 0)
            lax.fori_loop(kd0 + per, nk, online_step, 0)

    lam = lam_ref[...]
    lam_full = (jnp.exp(jnp.sum(lam[0:1] * lam[1:2], axis=-1, keepdims=True))
                - jnp.exp(jnp.sum(lam[2:3] * lam[3:4], axis=-1, keepdims=True)) + lam_init)
    outs = []
    for h in range(2):
        a1 = acc_ref[2 * h]
        a2 = acc_ref[2 * h + 1]
        rows = slice(0, DIFF_V_DIM)
        ones_row = slice(DIFF_V_DIM, DIFF_V_DIM + 1)
        o = a1[rows] / a1[ones_row] - lam_full * (a2[rows] / a2[ones_row])
        o = o * lax.rsqrt(jnp.mean(o * o, axis=0, keepdims=True) + EPS)
        outs.append(o * sw_ref[...] * (1.0 - lam_init))
    o_ref[0] = jnp.concatenate(outs, axis=0).T.astype(o_ref.dtype)


def _diff_attention(slopes, pieces, lam, subln_w, kn, dqvt, dk, aux, layer_idx, tq, tk):
    B, S, _ = dk.shape
    assert tq == 2 * tk and tk <= AUX_BLOCK and S // AUX_BLOCK <= AUX_BLOCK
    lam_init = 0.8 - 0.6 * math.exp(-0.3 * layer_idx)
    kern = functools.partial(_attn_kernel, seq=S, tq=tq, tk=tk, lam_init=lam_init)
    v_block0 = DIFF_WIDTH // LANES
    smem = pl.BlockSpec(memory_space=pltpu.SMEM)
    return pl.pallas_call(
        kern,
        grid=(B, HEAD_PAIRS, S // tq),
        in_specs=[
            smem, smem,
            pl.BlockSpec((4, DIFF_QK_DIM), lambda b, j, i: (0, 0)),
            pl.BlockSpec((DIFF_V_DIM, 1), lambda b, j, i: (0, 0)),
            pl.BlockSpec((1,) + kn.shape[1:], lambda b, j, i: (b, 0, 0, 0)),
            pl.BlockSpec((1, LANES, tq), lambda b, j, i: (b, j, i)),
            pl.BlockSpec((1, S, LANES), lambda b, j, i: (b, 0, j)),
            pl.BlockSpec((S, LANES), lambda b, j, i: (0, 0)),
            pl.BlockSpec((1, LANES, S), lambda b, j, i: (b, v_block0 + j, 0)),
        ],
        out_specs=pl.BlockSpec((1, tq, LANES), lambda b, j, i: (b, i, j)),
        out_shape=jax.ShapeDtypeStruct((B, S, DIFF_WIDTH), BF16),
        scratch_shapes=[pltpu.VMEM((4, 1, tq), F32), pltpu.VMEM((4, V_ROWS, tq), F32),
                        pltpu.VMEM((12, 2 * LANES, tq), BF16),
                        pltpu.VMEM((4, tk, tq), BF16), pltpu.VMEM((4, tk, tq), BF16),
                        pltpu.VMEM((tq, tq), F32), pltpu.VMEM((4, tq, tq), F32)],
        compiler_params=_params("parallel", "parallel", "arbitrary"),
        name="diff_attn",
    )(slopes, pieces, lam, subln_w, kn, dqvt, dk, aux, dqvt)


def _out_proj_kernel(x_ref, ret_ref, diff_ref, cv_ref, prev_ref, next_ref, cw_ref, w_ref, o_ref, *, tm):
    i = pl.program_id(1)
    last = pl.num_programs(1) - 1
    cc = CONV_CHANNELS
    cv = cv_ref[0]
    u = cv[:, 2 * cc:3 * cc] * cv[:, 0:cc]
    prev = prev_ref[0]
    nxt = next_ref[0]
    u_prev = prev[SUBLANES - 1:SUBLANES, 2 * cc:3 * cc] * prev[SUBLANES - 1:SUBLANES, 0:cc]
    u_next = nxt[0:1, 2 * cc:3 * cc] * nxt[0:1, 0:cc]
    u_prev = jnp.where(i == 0, 0.0, u_prev)
    u_next = jnp.where(i == last, 0.0, u_next)
    rows = lax.broadcasted_iota(jnp.int32, (tm, cc), 0)
    u_m1 = jnp.where(rows == 0, u_prev, pltpu.roll(u, 1, 0))
    u_p1 = jnp.where(rows == tm - 1, u_next, pltpu.roll(u, tm - 1, 0))
    y = cw_ref[0:1] * u_m1 + cw_ref[1:2] * u + cw_ref[2:3] * u_p1
    conv = (cv[:, cc:2 * cc] * y).astype(BF16)
    acc = jnp.dot(ret_ref[0], w_ref[0:RET_WIDTH], preferred_element_type=F32)
    acc += jnp.dot(diff_ref[0], w_ref[RET_WIDTH:RET_WIDTH + DIFF_WIDTH], preferred_element_type=F32)
    acc += jnp.dot(conv, w_ref[RET_WIDTH + DIFF_WIDTH:], preferred_element_type=F32)
    o_ref[0] = x_ref[0] + acc


def _out_proj(x, ret, diff, cv, conv_w, w_out, tm):
    B, S, D = x.shape
    per = tm // SUBLANES
    nblk8 = S // SUBLANES
    tile = lambda width: pl.BlockSpec((1, tm, width), lambda b, i: (b, i, 0))
    return pl.pallas_call(
        functools.partial(_out_proj_kernel, tm=tm),
        grid=(B, S // tm),
        in_specs=[
            tile(D), tile(RET_WIDTH), tile(DIFF_WIDTH), tile(3 * CONV_CHANNELS),
            pl.BlockSpec((1, SUBLANES, 3 * CONV_CHANNELS), lambda b, i: (b, jnp.maximum(i * per - 1, 0), 0)),
            pl.BlockSpec((1, SUBLANES, 3 * CONV_CHANNELS),
                         lambda b, i: (b, jnp.minimum((i + 1) * per, nblk8 - 1), 0)),
            pl.BlockSpec((CONV_WIDTH, CONV_CHANNELS), lambda b, i: (0, 0)),
            pl.BlockSpec((D, D), lambda b, i: (0, 0)),
        ],
        out_specs=tile(D),
        out_shape=jax.ShapeDtypeStruct((B, S, D), F32),
        compiler_params=_params("parallel", "parallel"),
        name="out_proj",
    )(x, ret, diff, cv, cv, cv, conv_w, w_out)


_GROUP_LANE0 = N_EXPERTS
MOE_ROW_BLOCK = 256
MOE_EXPERTS_PER_STEP = 2
assert EXPERTS_PER_GROUP % MOE_EXPERTS_PER_STEP == 0


def _moe_sorted_rows(tm):
    return (tm + N_GROUPS * (MOE_ROW_BLOCK - 1)) // MOE_ROW_BLOCK * MOE_ROW_BLOCK


def _router_gates(logits):
    lane = lax.broadcasted_iota(jnp.int32, logits.shape, 1)
    big = jnp.int32(LANES)
    is_group = (lane >= _GROUP_LANE0) & (lane < _GROUP_LANE0 + N_GROUPS)
    gl = jnp.where(is_group, logits, -jnp.inf)
    g_max = jnp.max(gl, axis=-1, keepdims=True)
    g_idx = jnp.min(jnp.where(gl == g_max, lane - _GROUP_LANE0, big), axis=-1, keepdims=True)
    g_w = 1.0 / jnp.sum(jnp.where(is_group, jnp.exp(gl - g_max), 0.0), axis=-1, keepdims=True)
    in_group = (lane >= g_idx * EXPERTS_PER_GROUP) & (lane < (g_idx + 1) * EXPERTS_PER_GROUP)
    el = jnp.where(in_group, logits, -jnp.inf)
    e1 = jnp.max(el, axis=-1, keepdims=True)
    i1 = jnp.min(jnp.where(el == e1, lane, big), axis=-1, keepdims=True)
    el2 = jnp.where(lane == i1, -jnp.inf, el)
    e2 = jnp.max(el2, axis=-1, keepdims=True)
    i2 = jnp.min(jnp.where(el2 == e2, lane, big), axis=-1, keepdims=True)
    r = jnp.exp(e2 - e1)
    p1 = g_w / (1.0 + r)
    p2 = g_w * r / (1.0 + r)
    return jnp.where(lane == i1, p1, jnp.where(lane == i2, p2, 0.0)), g_idx


def _moe_kernel(x_ref, nw_ref, rw_ref, rb_ref, wg_ref, wu_ref, wd_ref, fw_ref, o_ref,
                xs_ref, gs_ref, ys_ref, dest_ref, ltri_ref, seg_ref, *, final_norm, tm):
    tile = pl.program_id(0)
    e = pl.program_id(1)
    sorted_rows = _moe_sorted_rows(tm)

    @pl.when((tile == 0) & (e == 0))
    def _strictly_lower():
        r = lax.broadcasted_iota(jnp.int32, (tm, tm), 0)
        c = lax.broadcasted_iota(jnp.int32, (tm, tm), 1)
        ltri_ref[...] = jnp.where(c < r, 1.0, 0.0).astype(BF16)

    @pl.when(e == 0)
    def _route_and_sort():
        x = x_ref[...]
        xn = x * lax.rsqrt(jnp.mean(x * x, axis=-1, keepdims=True) + EPS) * nw_ref[...]
        hi, lo = _split_hi_lo(xn)
        r_hi, r_lo = _split_hi_lo(rw_ref[...])
        logits = (jnp.dot(hi, r_hi, preferred_element_type=F32)
                  + jnp.dot(lo, r_hi, preferred_element_type=F32)
                  + jnp.dot(hi, r_lo, preferred_element_type=F32)) + rb_ref[...]
        gates, g_idx = _router_gates(logits)
        lane = lax.broadcasted_iota(jnp.int32, (tm, LANES), 1)
        member = jnp.where(lane == g_idx, 1.0, 0.0)
        earlier = jnp.dot(ltri_ref[...], member.astype(BF16), preferred_element_type=F32)
        rank = jnp.sum(member * earlier, axis=-1, keepdims=True)
        count = jnp.sum(member, axis=0, keepdims=True)
        seg = jnp.floor((count + (MOE_ROW_BLOCK - 1.0)) * (1.0 / MOE_ROW_BLOCK)) * MOE_ROW_BLOCK
        lane1 = lax.broadcasted_iota(jnp.int32, (1, LANES), 1)
        s0, s1, s2 = seg[:, 0:1], seg[:, 1:2], seg[:, 2:3]
        start = jnp.where(lane1 == 1, s0, jnp.where(lane1 == 2, s0 + s1, jnp.where(lane1 == 3, s0 + s1 + s2, 0.0)))
        dest = rank + jnp.sum(member * start, axis=-1, keepdims=True)
        dest_ref[...] = jnp.broadcast_to(dest, (tm, LANES))
        dest_row = dest_ref[...].T[0:1, :]
        row_id = lax.broadcasted_iota(jnp.int32, (sorted_rows, tm), 0).astype(F32)
        perm = jnp.where(row_id == dest_row, 1.0, 0.0).astype(BF16)
        g_hi, g_mid, g_lo = _split3(gates)
        g_pieces = g_hi + pltpu.roll(g_mid, N_EXPERTS, 1) + pltpu.roll(g_lo, 2 * N_EXPERTS, 1)
        moved = jnp.dot(perm, jnp.concatenate([hi, g_pieces.astype(BF16)], axis=1), preferred_element_type=F32)
        xs_ref[...] = moved[:, :D_MODEL].astype(BF16)
        g_moved = moved[:, D_MODEL:]
        gs_ref[...] = (g_moved + pltpu.roll(g_moved, LANES - N_EXPERTS, 1)
                       + pltpu.roll(g_moved, LANES - 2 * N_EXPERTS, 1))
        ys_ref[...] = jnp.zeros(ys_ref.shape, F32)
        start_i = start.astype(jnp.int32)
        blocks_i = (seg * (1.0 / MOE_ROW_BLOCK)).astype(jnp.int32)
        for g in range(N_GROUPS):
            seg_ref[g] = start_i[0, g]
            seg_ref[N_GROUPS + g] = blocks_i[0, g]

    group = (e * MOE_EXPERTS_PER_STEP) // EXPERTS_PER_GROUP
    first_row = seg_ref[group]

    def row_block(b, carry):
        r0 = pl.multiple_of(first_row + b * MOE_ROW_BLOCK, MOE_ROW_BLOCK)
        xb = xs_ref[pl.ds(r0, MOE_ROW_BLOCK), :]
        gsb = gs_ref[pl.ds(r0, MOE_ROW_BLOCK), :]
        lane = lax.broadcasted_iota(jnp.int32, gsb.shape, 1)
        experts = range(MOE_EXPERTS_PER_STEP)
        gate = [jnp.dot(xb, wg_ref[j], preferred_element_type=F32) for j in experts]
        up = [jnp.dot(xb, wu_ref[j], preferred_element_type=F32) for j in experts]
        hmid = [(gate[j] * jax.nn.sigmoid(gate[j]) * up[j]).astype(BF16) for j in experts]
        down = [jnp.dot(hmid[j], wd_ref[j], preferred_element_type=F32) for j in experts]
        total = None
        for j in experts:
            g_e = jnp.sum(jnp.where(lane == e * MOE_EXPERTS_PER_STEP + j, gsb, 0.0), axis=-1, keepdims=True)
            total = g_e * down[j] if total is None else total + g_e * down[j]
        ys_ref[pl.ds(r0, MOE_ROW_BLOCK), :] += total
        return carry

    lax.fori_loop(0, seg_ref[N_GROUPS + group], row_block, 0)

    @pl.when(e == pl.num_programs(1) - 1)
    def _finish():
        col_id = lax.broadcasted_iota(jnp.int32, (tm, sorted_rows), 1).astype(F32)
        unperm = jnp.where(col_id == dest_ref[:, 0:1], 1.0, 0.0).astype(BF16)
        y = x_ref[...] + jnp.dot(unperm, ys_ref[...].astype(BF16), preferred_element_type=F32)
        if final_norm:
            y = y * lax.rsqrt(jnp.mean(y * y, axis=-1, keepdims=True) + EPS) * fw_ref[...]
        o_ref[...] = y


def _moe(x, nw, r_w, r_b, wg, wu, wd, final_w, final_norm, tm):
    T, D = x.shape
    const = lambda shape: pl.BlockSpec(shape, lambda i, e: (0,) * len(shape))
    sorted_rows = _moe_sorted_rows(tm)
    return pl.pallas_call(
        functools.partial(_moe_kernel, final_norm=final_norm, tm=tm),
        grid=(T // tm, N_EXPERTS // MOE_EXPERTS_PER_STEP),
        in_specs=[
            pl.BlockSpec((tm, D), lambda i, e: (i, 0)),
            const((1, D)), const((D, LANES)), const((1, LANES)),
            pl.BlockSpec((MOE_EXPERTS_PER_STEP, D, EXPERT_FF), lambda i, e: (e, 0, 0)),
            pl.BlockSpec((MOE_EXPERTS_PER_STEP, D, EXPERT_FF), lambda i, e: (e, 0, 0)),
            pl.BlockSpec((MOE_EXPERTS_PER_STEP, EXPERT_FF, D), lambda i, e: (e, 0, 0)),
            const((1, D)),
        ],
        out_specs=pl.BlockSpec((tm, D), lambda i, e: (i, 0)),
        out_shape=jax.ShapeDtypeStruct((T, D), F32),
        scratch_shapes=[pltpu.VMEM((sorted_rows, D), BF16), pltpu.VMEM((sorted_rows, LANES), F32),
                        pltpu.VMEM((sorted_rows, D), F32), pltpu.VMEM((tm, LANES), F32),
                        pltpu.VMEM((tm, tm), BF16), pltpu.SMEM((2 * N_GROUPS,), jnp.int32)],
        compiler_params=_params("arbitrary", "arbitrary"),
        name="moe",
    )(x, nw, r_w, r_b, wg, wu, wd, final_w)


def _tile(n, pref):
    t = min(n, pref)
    assert n % t == 0, (n, t)
    return t


def _prep_layer(l, norm1_w, w_in, ret_decay_logit, ret_gn_w, diff_lambda, diff_subln_w, conv_w, w_out, norm2_w,
                router_group_w, router_group_b, router_expert_w, router_expert_b, expert_w_gate, expert_w_up,
                expert_w_down):
    w = w_in[l]
    w_nat = jnp.concatenate([w[:, :_OFF_DQ], w[:, _OFF_DK:_OFF_DV], w[:, _OFF_CONV:]], axis=1).astype(BF16)
    w_t = jnp.concatenate([w[:, _OFF_DQ:_OFF_DK], w[:, _OFF_DV:_OFF_CONV]], axis=1).T.astype(BF16)
    router = jnp.zeros((D_MODEL, LANES), F32)
    router = router.at[:, :N_EXPERTS].set(router_expert_w[l]).at[:, N_EXPERTS:N_EXPERTS + N_GROUPS].set(
        router_group_w[l])
    r_b = jnp.zeros((1, LANES), F32).at[0, :N_EXPERTS].set(router_expert_b[l]).at[
        0, N_EXPERTS:N_EXPERTS + N_GROUPS].set(router_group_b[l])
    return dict(
        norm1=norm1_w[l][None, :], w_nat=w_nat, w_t=w_t,
        lg=jax.nn.log_sigmoid(ret_decay_logit[l].astype(F32)),
        gn_w=ret_gn_w[l][None, :].astype(F32),
        lam=diff_lambda[l].astype(F32), subln=diff_subln_w[l][:, None].astype(F32),
        conv_w=conv_w[l].astype(F32), w_out=w_out[l].astype(BF16),
        norm2=norm2_w[l][None, :], r_w=router, r_b=r_b,
        wg=expert_w_gate[l].astype(BF16), wu=expert_w_up[l].astype(BF16), wd=expert_w_down[l].astype(BF16),
    )


def _trunk(x, layers, final_w, slopes, pieces):
    B, S, D = x.shape
    tm = _tile(S, 512)
    tq = _tile(S, 512)
    tk = tq // 2
    t_moe = _tile(B * S, 1024)
    aux = _position_features(S)
    for l, lw in enumerate(layers):
        rqkv, rg, dk, dqvt, cv, kn = _in_proj(x, lw["norm1"], lw["w_nat"], lw["w_t"], tm)
        ret = _retention(lw["lg"], rqkv, rg, lw["gn_w"])
        diff = _diff_attention(slopes, pieces, lw["lam"], lw["subln"], kn, dqvt, dk, aux, l, tq, tk)
        x = _out_proj(x, ret, diff, cv, lw["conv_w"], lw["w_out"], tm)
        last = l == len(layers) - 1
        x = _moe(x.reshape(B * S, D), lw["norm2"], lw["r_w"], lw["r_b"], lw["wg"], lw["wu"],
                 lw["wd"], final_w, last, t_moe).reshape(B, S, D)
    return x


def kernel(x_prompt, x_sample, norm1_w, w_in, ret_decay_logit, ret_gn_w, diff_lambda, diff_subln_w, conv_w, w_out,
           norm2_w, router_group_w, router_group_b, router_expert_w, router_expert_b, expert_w_gate, expert_w_up,
           expert_w_down, final_norm_w):
    depth = w_in.shape[0]
    layers = [
        _prep_layer(l, norm1_w, w_in, ret_decay_logit, ret_gn_w, diff_lambda, diff_subln_w, conv_w, w_out, norm2_w,
                    router_group_w, router_group_b, router_expert_w, router_expert_b, expert_w_gate, expert_w_up,
                    expert_w_down)
        for l in range(depth)
    ]
    final_w = final_norm_w[None, :].astype(F32)
    slopes = (np.float32(2.0) ** (np.float32(-8.0) * np.arange(1, DIFF_HEADS + 1, dtype=np.float32)
                                  / np.float32(DIFF_HEADS))) * np.float32(LOG2E)
    s_hi = slopes.astype(BF16).astype(np.float32)
    s_mid = (slopes - s_hi).astype(BF16).astype(np.float32)
    s_lo = (slopes - s_hi - s_mid).astype(BF16).astype(np.float32)
    pieces = jnp.asarray(np.stack([s_hi, s_mid, s_lo], axis=1).reshape(-1))
    slopes = jnp.asarray(slopes)
    return (_trunk(x_prompt, layers, final_w, slopes, pieces), _trunk(x_sample, layers, final_w, slopes, pieces))
```

```python
import functools
import math

import jax
import jax.numpy as jnp
import numpy as np
from jax import lax
from jax.experimental import pallas as pl
from jax.experimental.pallas import tpu as pltpu

F32 = jnp.float32
BF16 = jnp.bfloat16

D_MODEL = 1024
RET_HEADS = 6
RET_HEAD_DIM = 64
RET_WIDTH = RET_HEADS * RET_HEAD_DIM
DIFF_HEADS = 6
DIFF_QK_DIM = 32
DIFF_V_DIM = 64
DIFF_WIDTH = DIFF_HEADS * DIFF_V_DIM
CONV_CHANNELS = 256
CONV_WIDTH = 3
N_GROUPS = 4
EXPERTS_PER_GROUP = 4
N_EXPERTS = 16
EXPERT_FF = 512
EPS = 1e-6

LANES = 128
HEAD_PAIRS = RET_HEADS // 2
RET_CHUNK = 128
SUBLANES = 8
VMEM_LIMIT = 56 * 1024 * 1024
LOG2E = math.log2(math.e)
NEG_BIG = -1e30

_OFF_RG = 3 * RET_WIDTH
_OFF_DQ = 4 * RET_WIDTH
_OFF_DK = _OFF_DQ + DIFF_WIDTH
_OFF_DV = _OFF_DK + DIFF_WIDTH
_OFF_CONV = _OFF_DV + DIFF_WIDTH


def _params(*sem):
    return pltpu.CompilerParams(dimension_semantics=sem, vmem_limit_bytes=VMEM_LIMIT)


def _split_hi_lo(x):
    hi = x.astype(BF16)
    lo = (x - hi.astype(F32)).astype(BF16)
    return hi, lo


def _in_proj_kernel(x_ref, nw_ref, w_ref, wt_ref, rqkv_ref, rg_ref, dk_ref, dqvt_ref, cv_ref, kn_ref):
    x = x_ref[0]
    ms = jnp.mean(x * x, axis=-1, keepdims=True)
    h = (x * lax.rsqrt(ms + EPS) * nw_ref[...]).astype(BF16)
    a = jnp.dot(h, w_ref[:, 0:_OFF_DQ], preferred_element_type=F32)
    rqkv_ref[0, :, 0:RET_WIDTH] = a[:, 0:RET_WIDTH].astype(BF16)
    rqkv_ref[0, :, RET_WIDTH:2 * RET_WIDTH] = (a[:, RET_WIDTH:2 * RET_WIDTH] * RET_HEAD_DIM ** -0.5).astype(BF16)
    rqkv_ref[0, :, 2 * RET_WIDTH:3 * RET_WIDTH] = a[:, 2 * RET_WIDTH:3 * RET_WIDTH].astype(BF16)
    rg_ref[0] = a[:, _OFF_RG:_OFF_DQ]
    dk = jnp.dot(h, w_ref[:, _OFF_DQ:_OFF_DQ + DIFF_WIDTH], preferred_element_type=F32).astype(BF16)
    dk_ref[0] = dk
    dkf = dk.astype(F32)
    sq_hi, sq_lo = _split_hi_lo(dkf * dkf)
    grp = (lax.shift_right_logical(lax.broadcasted_iota(jnp.int32, (DIFF_WIDTH, LANES), 0), 5)
           == lax.broadcasted_iota(jnp.int32, (DIFF_WIDTH, LANES), 1))
    grp = jnp.where(grp, 1.0, 0.0).astype(BF16)
    kn2 = jnp.dot(sq_hi, grp, preferred_element_type=F32) + jnp.dot(sq_lo, grp, preferred_element_type=F32)
    kn_ref[0, 0] = jnp.broadcast_to(jnp.max(kn2, axis=0, keepdims=True), (SUBLANES, LANES))
    cv_ref[0] =jnp.dot(h, w_ref[:, _OFF_DQ + DIFF_WIDTH:], preferred_element_type=F32)
    t = lax.dot_general(wt_ref[...], h, (((1,), (1,)), ((), ())), preferred_element_type=F32)
    dqvt_ref[0, 0:DIFF_WIDTH, :] = (t[0:DIFF_WIDTH] * (DIFF_QK_DIM ** -0.5 * LOG2E)).astype(BF16)
    dqvt_ref[0, DIFF_WIDTH:, :] = t[DIFF_WIDTH:].astype(BF16)


def _in_proj(x, nw, w_nat, w_t, tm):
    B, S, D = x.shape
    n_nat = w_nat.shape[1]
    return pl.pallas_call(
        _in_proj_kernel,
        grid=(B, S // tm),
        in_specs=[
            pl.BlockSpec((1, tm, D), lambda b, i: (b, i, 0)),
            pl.BlockSpec((1, D), lambda b, i: (0, 0)),
            pl.BlockSpec((D, n_nat), lambda b, i: (0, 0)),
            pl.BlockSpec((2 * DIFF_WIDTH, D), lambda b, i: (0, 0)),
        ],
        out_specs=[
            pl.BlockSpec((1, tm, 3 * RET_WIDTH), lambda b, i: (b, i, 0)),
            pl.BlockSpec((1, tm, RET_WIDTH), lambda b, i: (b, i, 0)),
            pl.BlockSpec((1, tm, DIFF_WIDTH), lambda b, i: (b, i, 0)),
            pl.BlockSpec((1, 2 * DIFF_WIDTH, tm), lambda b, i: (b, 0, i)),
            pl.BlockSpec((1, tm, 3 * CONV_CHANNELS), lambda b, i: (b, i, 0)),
            pl.BlockSpec((1, 1, SUBLANES, LANES), lambda b, i: (b, i, 0, 0)),
        ],
        out_shape=[
            jax.ShapeDtypeStruct((B, S, 3 * RET_WIDTH), BF16),
            jax.ShapeDtypeStruct((B, S, RET_WIDTH), F32),
            jax.ShapeDtypeStruct((B, S, DIFF_WIDTH), BF16),
            jax.ShapeDtypeStruct((B, 2 * DIFF_WIDTH, S), BF16),
            jax.ShapeDtypeStruct((B, S, 3 * CONV_CHANNELS), F32),
            jax.ShapeDtypeStruct((B, S // tm, SUBLANES, LANES), F32),
        ],
        compiler_params=_params("parallel", "parallel"),
        name="in_proj",
    )(x, nw, w_nat, w_t)


def _pair_lane_value(lg_ref, direction, pair, shape, axis):
    idx = lax.broadcasted_iota(jnp.int32, shape, axis)
    return jnp.where(idx < RET_HEAD_DIM, lg_ref[direction, 2 * pair], lg_ref[direction, 2 * pair + 1])


def _same_head_mask():
    r = lax.broadcasted_iota(jnp.int32, (LANES, LANES), 0)
    c = lax.broadcasted_iota(jnp.int32, (LANES, LANES), 1)
    return (r < RET_HEAD_DIM) == (c < RET_HEAD_DIM)


def _ret_fwd_kernel(lg_ref, qkv_ref, y_ref, state_ref, dmat_ref, qdec_ref, kdec_ref, cdec_ref, *, chunks):
    C = RET_CHUNK

    @pl.when((pl.program_id(0) == 0) & (pl.program_id(1) == 0))
    def _build_tables():
        diff = (lax.broadcasted_iota(jnp.int32, (C, C), 0) - lax.broadcasted_iota(jnp.int32, (C, C), 1)).astype(F32)
        for h in range(RET_HEADS):
            lower = jnp.exp(jnp.maximum(diff, 0.0) * lg_ref[0, h])
            upper = jnp.exp(jnp.maximum(-diff, 0.0) * lg_ref[1, h])
            dmat_ref[h] = jnp.where(diff >= 0, lower, upper)
        pos = lax.broadcasted_iota(jnp.int32, (C, LANES), 0).astype(F32)
        for p in range(HEAD_PAIRS):
            lg_lane = _pair_lane_value(lg_ref, 0, p, (C, LANES), 1)
            qdec_ref[p] = jnp.exp((pos + 1.0) * lg_lane)
            kdec_ref[p] = jnp.exp((C - 1.0 - pos) * lg_lane)
            cdec_ref[p] = jnp.exp(float(C) * _pair_lane_value(lg_ref, 0, p, (LANES, LANES), 0))

    @pl.when(pl.program_id(1) == 0)
    def _reset():
        state_ref[...] = jnp.zeros(state_ref.shape, F32)

    lane = lax.broadcasted_iota(jnp.int32, (C, LANES), 1)
    same_head = _same_head_mask()
    def operands(p, ci):
        rows = slice(ci * C, (ci + 1) * C)
        return (qkv_ref[0, rows, p * LANES:(p + 1) * LANES],
                qkv_ref[0, rows, RET_WIDTH + p * LANES:RET_WIDTH + (p + 1) * LANES],
                qkv_ref[0, rows, 2 * RET_WIDTH + p * LANES:2 * RET_WIDTH + (p + 1) * LANES])

    items = [(p, ci) for p in range(HEAD_PAIRS) for ci in range(chunks)]
    scores, kvs = {}, {}
    for p, ci in items:
        q, k, v = operands(p, ci)
        for h in range(2):
            head_lanes = (lane >= h * RET_HEAD_DIM) & (lane < (h + 1) * RET_HEAD_DIM)
            qm = jnp.where(head_lanes, q, jnp.zeros_like(q))
            scores[p, ci, h] = lax.dot_general(qm, k, (((1,), (1,)), ((), ())), preferred_element_type=F32)
        kd = (k.astype(F32) * kdec_ref[p]).astype(BF16)
        kvs[p, ci] = lax.dot_general(kd, v, (((0,), (0,)), ((), ())), preferred_element_type=F32)
    states = {}
    for p in range(HEAD_PAIRS):
        state = state_ref[p]
        for ci in range(chunks):
            states[p, ci] = state.astype(BF16)
            state = cdec_ref[p] * state + jnp.where(same_head, kvs[p, ci], 0.0)
        state_ref[p] = state
    for p, ci in items:
        q, _, v = operands(p, ci)
        inner = [jnp.dot((scores[p, ci, h] * dmat_ref[2 * p + h]).astype(BF16), v, preferred_element_type=F32)
                 for h in range(2)]
        cross = jnp.dot(q, states[p, ci], preferred_element_type=F32) * qdec_ref[p]
        y_ref[0, ci * C:(ci + 1) * C, p * LANES:(p + 1) * LANES] = (
            jnp.where(lane < RET_HEAD_DIM, inner[0], inner[1]) + cross)


def _ret_bwd_kernel(lg_ref, qkv_ref, y1_ref, g_ref, gnw_ref, o_ref, state_ref, qdec_ref, kdec_ref, cdec_ref, *,
                    chunks):
    C = RET_CHUNK

    @pl.when((pl.program_id(0) == 0) & (pl.program_id(1) == 0))
    def _build_tables():
        pos = lax.broadcasted_iota(jnp.int32, (C, LANES), 0).astype(F32)
        for p in range(HEAD_PAIRS):
            lg_lane = _pair_lane_value(lg_ref, 1, p, (C, LANES), 1)
            qdec_ref[p] = jnp.exp((float(C) - pos) * lg_lane)
            kdec_ref[p] = jnp.exp(pos * lg_lane)
            cdec_ref[p] = jnp.exp(float(C) * _pair_lane_value(lg_ref, 1, p, (LANES, LANES), 0))

    @pl.when(pl.program_id(1) == 0)
    def _reset():
        state_ref[...] = jnp.zeros(state_ref.shape, F32)

    same_head = _same_head_mask()
    head_avg = jnp.where(same_head, 1.0 / RET_HEAD_DIM, 0.0).astype(BF16)

    def head_mean(t):
        hi, lo = _split_hi_lo(t)
        return (jnp.dot(hi, head_avg, preferred_element_type=F32)
                + jnp.dot(lo, head_avg, preferred_element_type=F32))

    items = [(p, ci) for p in range(HEAD_PAIRS) for ci in range(chunks)]
    kvs = {}
    for p, ci in items:
        rows = slice(ci * C, (ci + 1) * C)
        k = qkv_ref[0, rows, RET_WIDTH + p * LANES:RET_WIDTH + (p + 1) * LANES]
        v = qkv_ref[0, rows, 2 * RET_WIDTH + p * LANES:2 * RET_WIDTH + (p + 1) * LANES]
        kd = (k.astype(F32) * kdec_ref[p]).astype(BF16)
        kvs[p, ci] = lax.dot_general(kd, v, (((0,), (0,)), ((), ())), preferred_element_type=F32)
    states = {}
    for p in range(HEAD_PAIRS):
        state = state_ref[p]
        for ci in reversed(range(chunks)):
            states[p, ci] = state.astype(BF16)
            state = cdec_ref[p] * state + jnp.where(same_head, kvs[p, ci], 0.0)
        state_ref[p] = state
    ys = {}
    for p, ci in items:
        rows = slice(ci * C, (ci + 1) * C)
        sl = slice(p * LANES, (p + 1) * LANES)
        cross = jnp.dot(qkv_ref[0, rows, sl], states[p, ci], preferred_element_type=F32)
        ys[p, ci] = y1_ref[0, rows, sl] + cross * qdec_ref[p]
    means = {key: head_mean(y) for key, y in ys.items()}
    devs = {key: ys[key] - means[key] for key in ys}
    variances = {key: head_mean(d * d) for key, d in devs.items()}
    for p, ci in items:
        rows = slice(ci * C, (ci + 1) * C)
        sl = slice(p * LANES, (p + 1) * LANES)
        g = g_ref[0, rows, sl]
        o = devs[p, ci] * lax.rsqrt(variances[p, ci] + EPS) * gnw_ref[:, sl] * (g * jax.nn.sigmoid(g))
        o_ref[0, rows, sl] = o.astype(o_ref.dtype)


def _retention(lg, rqkv, rg, gn_w):
    B, S, _ = rqkv.shape
    chunks = math.gcd(S // RET_CHUNK, 4)
    C = chunks * RET_CHUNK
    n = S // C
    smem = pl.BlockSpec(memory_space=pltpu.SMEM)
    table = pltpu.VMEM((HEAD_PAIRS, RET_CHUNK, LANES), F32)
    state = pltpu.VMEM((HEAD_PAIRS, LANES, LANES), F32)
    y1 = pl.pallas_call(
        functools.partial(_ret_fwd_kernel, chunks=chunks),
        grid=(B, n),
        in_specs=[smem, pl.BlockSpec((1, C, 3 * RET_WIDTH), lambda b, i: (b, i, 0))],
        out_specs=pl.BlockSpec((1, C, RET_WIDTH), lambda b, i: (b, i, 0)),
        out_shape=jax.ShapeDtypeStruct((B, S, RET_WIDTH), F32),
        scratch_shapes=[state, pltpu.VMEM((RET_HEADS, RET_CHUNK, RET_CHUNK), F32), table, table, state],
        compiler_params=_params("arbitrary", "arbitrary"),
        name="ret_fwd",
    )(lg, rqkv)
    rev = lambda b, i: (b, n - 1 - i, 0)
    return pl.pallas_call(
        functools.partial(_ret_bwd_kernel, chunks=chunks),
        grid=(B, n),
        in_specs=[
            smem,
            pl.BlockSpec((1, C, 3 * RET_WIDTH), rev),
            pl.BlockSpec((1, C, RET_WIDTH), rev),
            pl.BlockSpec((1, C, RET_WIDTH), rev),
            pl.BlockSpec((1, RET_WIDTH), lambda b, i: (0, 0)),
        ],
        out_specs=pl.BlockSpec((1, C, RET_WIDTH), rev),
        out_shape=jax.ShapeDtypeStruct((B, S, RET_WIDTH), BF16),
        scratch_shapes=[state, table, table, state],
        compiler_params=_params("arbitrary", "arbitrary"),
        name="ret_bwd",
    )(lg, rqkv, y1, rg, gn_w)


AUX_BLOCK = 256
AUX_ROWS = 16
V_ROWS = DIFF_V_DIM + SUBLANES
GUARD_LOG2 = 60.0
UNDERFLOW_LOG2 = 150.0


def _position_features(seq):
    j = jnp.arange(seq, dtype=jnp.int32)
    ones = jnp.ones((seq,), F32)
    n = (j // AUX_BLOCK).astype(F32)
    jc = (j % AUX_BLOCK - AUX_BLOCK // 2).astype(F32)
    feat = jnp.stack([ones] * 3 + [n] * 3 + [jc] * 3, axis=1)
    return jnp.pad(feat, ((0, 0), (0, LANES - feat.shape[1]))).astype(BF16)


def _split3(x):
    hi = x.astype(BF16).astype(F32)
    r = x - hi
    mid = r.astype(BF16).astype(F32)
    return hi, mid, (r - mid).astype(BF16).astype(F32)


def _attn_kernel(slopes_ref, pieces_ref, lam_ref, sw_ref, kn_ref, qt_ref, k_ref, aux_ref, vt_ref, o_ref,
                 m_ref, acc_ref, qm_ref, aug_ref, pa_ref, pb_ref, dist_ref, sd_ref, *, seq, tq, tk, lam_init):
    pair = pl.program_id(1)
    qi = pl.program_id(2)
    q0 = qi * tq
    nk = seq // tk
    per = tq // tk
    kd0 = qi * per
    qt = qt_ref[0]
    row = lax.broadcasted_iota(jnp.int32, (LANES, tq), 0)
    q_masked = []
    for hc in range(4):
        lo = hc * DIFF_QK_DIM
        q_masked.append(jnp.where((row >= lo) & (row < lo + DIFF_QK_DIM), qt, jnp.zeros_like(qt)))

    @pl.when(qi == 0)
    def _distance_table():
        d = lax.broadcasted_iota(jnp.int32, (tq, tq), 1) - lax.broadcasted_iota(jnp.int32, (tq, tq), 0)
        dist_ref[...] = jnp.abs(d).astype(F32)

    kd = pl.multiple_of(q0, tq)
    kdiag = k_ref[0, pl.ds(kd, tq), :]
    self_prod = qt.astype(F32) * kdiag.astype(F32).T
    shifts = [jnp.sum(self_prod[hc * DIFF_QK_DIM:(hc + 1) * DIFF_QK_DIM], axis=0, keepdims=True) for hc in range(4)]

    kn2 = jnp.max(jnp.max(kn_ref[0], axis=0), axis=0, keepdims=True)
    kn_lane = lax.broadcasted_iota(jnp.int32, (1, LANES), 1)
    excess = []
    for h in range(2):
        worst = None
        for c in range(2):
            hc = 2 * h + c
            qf = q_masked[hc].astype(F32)
            qn2 = jnp.sum(qf * qf, axis=0, keepdims=True)
            kn2_hc = jnp.max(jnp.where(kn_lane == 4 * pair + hc, kn2, 0.0), axis=-1, keepdims=True)
            bound = jnp.sqrt(qn2 * kn2_hc) * 1.01 + 1e-3
            e = jnp.max(bound - shifts[hc], axis=-1, keepdims=True)
            worst = e if worst is None else jnp.maximum(worst, e)
        excess.append(worst)
    single_pass = jnp.where(jnp.maximum(excess[0], excess[1]) <= GUARD_LOG2, 1, 0).astype(jnp.int32)[0, 0] == 1

    def v_rows_with_ones(h, k0, n):
        vt = vt_ref[0, h * DIFF_V_DIM:(h + 1) * DIFF_V_DIM, pl.ds(k0, n)]
        return jnp.concatenate([vt, jnp.ones((SUBLANES, n), BF16)], axis=0)

    def keys_with_features(k0):
        return jnp.concatenate([k_ref[0, pl.ds(k0, tk), :], aux_ref[pl.ds(k0, tk), :]], axis=1)

    def queries_with_features(side, hc):
        return jnp.concatenate([qm_ref[hc], aug_ref[side * 4 + hc],
                                jnp.zeros((LANES - AUX_ROWS, tq), BF16)], axis=0)

    @pl.when(single_pass)
    def _single_pass():
        ipos = (q0 + lax.broadcasted_iota(jnp.int32, (1, tq), 1)).astype(F32)
        aug_row = lax.broadcasted_iota(jnp.int32, (AUX_ROWS, tq), 0)
        q0f = q0.astype(F32)
        lo_even, n_lo, totals = [], [], []
        for h in range(2):
            slope = slopes_ref[2 * pair + h]
            pieces = [pieces_ref[(2 * pair + h) * 3 + i] for i in range(3)]
            for c in range(2):
                hc = 2 * h + c
                for side, sign in ((0, 1.0), (1, -1.0), (2, 0.0)):
                    const = (-sign) * (slope * ipos) - shifts[hc] + sign * (0.5 * AUX_BLOCK) * slope
                    c_hi, c_mid, c_lo = _split3(const)
                    aug = jnp.where(aug_row == 0, c_hi,
                                    jnp.where(aug_row == 1, c_mid, jnp.where(aug_row == 2, c_lo, 0.0)))
                    for i in range(3):
                        aug = jnp.where(aug_row == 3 + i, sign * AUX_BLOCK * pieces[i], aug)
                        aug = jnp.where(aug_row == 6 + i, sign * pieces[i], aug)
                    aug_ref[side * 4 + hc] = aug.astype(BF16)
                qm_ref[hc] = q_masked[hc]
            reach = (excess[h] + UNDERFLOW_LOG2) / slope
            lo_blk = jnp.clip(jnp.floor((q0f - (tk - 1.0) - reach) / tk), 0.0, float(nk))
            hi_blk = jnp.clip(jnp.floor((reach + q0f + (tq - 1.0)) / tk) + 1.0, 0.0, float(nk))
            lo_blk = jnp.minimum(lo_blk.astype(jnp.int32)[0, 0], kd0)
            hi_blk = jnp.maximum(hi_blk.astype(jnp.int32)[0, 0], kd0 + per)
            odd = (kd0 - lo_blk + hi_blk - kd0 - per) % 2
            lo_h = jnp.where((odd == 1) & (lo_blk > 0), lo_blk - 1, lo_blk)
            hi_h = jnp.where((odd == 1) & (lo_blk == 0), hi_blk + 1, hi_blk)
            lo_even.append(lo_h)
            n_lo.append(kd0 - lo_h)
            totals.append(kd0 - lo_h + hi_h - kd0 - per)
        pairs_h0 = totals[0] // 2
        n_pairs = pairs_h0 + totals[1] // 2

        def locate(x, half):
            head = jnp.where(x >= pairs_h0, 1, 0)
            i = 2 * (x - head * pairs_h0) + half
            before = jnp.where(head == 1, n_lo[1], n_lo[0])
            first = jnp.where(head == 1, lo_even[1], lo_even[0])
            blk = jnp.where(i < before, first + i, kd0 + per + i - before)
            return head, blk, jnp.where(i < before, 0, 1)

        def stage_a(x, p_ref):
            for half in range(2):
                head, blk, side = locate(x, half)
                lhs = keys_with_features(pl.multiple_of(blk * tk, tk))
                for c in range(2):
                    t = jnp.dot(lhs, queries_with_features(side, 2 * head + c), preferred_element_type=F32)
                    p_ref[2 * half + c] = jnp.exp2(t).astype(BF16)

        def stage_b(x, p_ref):
            head, blk_a, _ = locate(x, 0)
            _, blk_b, _ = locate(x, 1)
            rows = pl.ds(pl.multiple_of(head * DIFF_V_DIM, DIFF_V_DIM), DIFF_V_DIM)
            ones = jnp.ones((SUBLANES, tk), BF16)
            va = jnp.concatenate([vt_ref[0, rows, pl.ds(pl.multiple_of(blk_a * tk, tk), tk)], ones], axis=0)
            vb = jnp.concatenate([vt_ref[0, rows, pl.ds(pl.multiple_of(blk_b * tk, tk), tk)], ones], axis=0)
            for c in range(2):
                acc_ref[2 * head + c] += (jnp.dot(va, p_ref[c], preferred_element_type=F32)
                                          + jnp.dot(vb, p_ref[2 + c], preferred_element_type=F32))

        def diag_a(h, p_ref):
            for half in range(2):
                lhs = keys_with_features(pl.multiple_of(q0 + half * tk, tk))
                bias = dist_ref[half * tk:(half + 1) * tk, :] * slopes_ref[2 * pair + h]
                for c in range(2):
                    t = jnp.dot(lhs, queries_with_features(2, 2 * h + c), preferred_element_type=F32) - bias
                    p_ref[2 * half + c] = jnp.exp2(t).astype(BF16)

        def diag_b(h, p_ref):
            va = v_rows_with_ones(h, pl.multiple_of(q0, tk), tk)
            vb = v_rows_with_ones(h, pl.multiple_of(q0 + tk, tk), tk)
            for c in range(2):
                acc_ref[2 * h + c] = (jnp.dot(va, p_ref[c], preferred_element_type=F32)
                                      + jnp.dot(vb, p_ref[2 + c], preferred_element_type=F32))

        diag_a(0, pa_ref)
        diag_a(1, pb_ref)
        diag_b(0, pa_ref)

        @pl.when(n_pairs == 0)
        def _diagonal_only():
            diag_b(1, pb_ref)

        @pl.when(n_pairs > 0)
        def _blocks():
            stage_a(0, pa_ref)
            diag_b(1, pb_ref)

            def pairs_body(count):
                def body(j, carry):
                    for s in range(0, count, 2):
                        stage_a(count * j + s + 1, pb_ref)
                        stage_b(count * j + s, pa_ref)
                        stage_a(count * j + s + 2, pa_ref)
                        stage_b(count * j + s + 1, pb_ref)
                    return carry
                return body

            long_iters = (n_pairs - 1) // 4
            lax.fori_loop(0, long_iters, pairs_body(4), 0)
            lax.fori_loop(2 * long_iters, (n_pairs - 1) // 2, pairs_body(2), 0)

            @pl.when(n_pairs % 2 == 1)
            def _last_one():
                stage_b(n_pairs - 1, pa_ref)

            @pl.when(n_pairs % 2 == 0)
            def _last_two():
                stage_a(n_pairs - 1, pb_ref)
                stage_b(n_pairs - 2, pa_ref)
                stage_b(n_pairs - 1, pb_ref)

    @pl.when(jnp.logical_not(single_pass))
    def _online():
        for hc in range(4):
            s = jnp.dot(kdiag, q_masked[hc], preferred_element_type=F32)
            sd_ref[hc] = s - dist_ref[...] * slopes_ref[2 * pair + hc // 2]
        for hc in range(4):
            m_ref[hc] = jnp.max(sd_ref[hc], axis=0, keepdims=True)
        probs = [jnp.exp2(sd_ref[hc] - m_ref[hc]).astype(BF16) for hc in range(4)]
        for hc in range(4):
            acc_ref[hc] = jnp.dot(v_rows_with_ones(hc // 2, kd, tq), probs[hc], preferred_element_type=F32)
        rel = (lax.broadcasted_iota(jnp.int32, (tk, tq), 1)
               - lax.broadcasted_iota(jnp.int32, (tk, tq), 0)).astype(F32)
        for h in range(2):
            def online_step(kb, carry, h=h):
                k0 = pl.multiple_of(kb * tk, tk)
                kblk = k_ref[0, pl.ds(k0, tk), :]
                v_ones = v_rows_with_ones(h, k0, tk)
                bias = jnp.abs(rel + (q0 - k0).astype(F32)) * slopes_ref[2 * pair + h]
                for c in range(2):
                    hc = 2 * h + c
                    s = jnp.dot(kblk, q_masked[hc], preferred_element_type=F32) - bias
                    m_old = m_ref[hc]
                    m_new = jnp.maximum(m_old, jnp.max(s, axis=0, keepdims=True))
                    p = jnp.exp2(s - m_new).astype(BF16)
                    acc_ref[hc] = (jnp.exp2(m_old - m_new) * acc_ref[hc]
                                   + jnp.dot(v_ones, p, preferred_element_type=F32))
                    m_ref[hc] = m_new
                return carry

            lax.fori_loop(0, kd0, online_step, 0)
            lax.fori_loop(kd0 + per, nk, online_step, 0)

    lam = lam_ref[...]
    lam_full = (jnp.exp(jnp.sum(lam[0:1] * lam[1:2], axis=-1, keepdims=True))
                - jnp.exp(jnp.sum(lam[2:3] * lam[3:4], axis=-1, keepdims=True)) + lam_init)
    outs = []
    for h in range(2):
        a1 = acc_ref[2 * h]
        a2 = acc_ref[2 * h + 1]
        rows = slice(0, DIFF_V_DIM)
        ones_row = slice(DIFF_V_DIM, DIFF_V_DIM + 1)
        o = a1[rows] / a1[ones_row] - lam_full * (a2[rows] / a2[ones_row])
        o = o * lax.rsqrt(jnp.mean(o * o, axis=0, keepdims=True) + EPS)
        outs.append(o * sw_ref[...] * (1.0 - lam_init))
    o_ref[0] = jnp.concatenate(outs, axis=0).T.astype(o_ref.dtype)


def _diff_attention(slopes, pieces, lam, subln_w, kn, dqvt, dk, aux, layer_idx, tq, tk):
    B, S, _ = dk.shape
    assert tq == 2 * tk and tk <= AUX_BLOCK and S // AUX_BLOCK <= AUX_BLOCK
    lam_init = 0.8 - 0.6 * math.exp(-0.3 * layer_idx)
    kern = functools.partial(_attn_kernel, seq=S, tq=tq, tk=tk, lam_init=lam_init)
    v_block0 = DIFF_WIDTH // LANES
    smem = pl.BlockSpec(memory_space=pltpu.SMEM)
    return pl.pallas_call(
        kern,
        grid=(B, HEAD_PAIRS, S // tq),
        in_specs=[
            smem, smem,
            pl.BlockSpec((4, DIFF_QK_DIM), lambda b, j, i: (0, 0)),
            pl.BlockSpec((DIFF_V_DIM, 1), lambda b, j, i: (0, 0)),
            pl.BlockSpec((1,) + kn.shape[1:], lambda b, j, i: (b, 0, 0, 0)),
            pl.BlockSpec((1, LANES, tq), lambda b, j, i: (b, j, i)),
            pl.BlockSpec((1, S, LANES), lambda b, j, i: (b, 0, j)),
            pl.BlockSpec((S, LANES), lambda b, j, i: (0, 0)),
            pl.BlockSpec((1, LANES, S), lambda b, j, i: (b, v_block0 + j, 0)),
        ],
        out_specs=pl.BlockSpec((1, tq, LANES), lambda b, j, i: (b, i, j)),
        out_shape=jax.ShapeDtypeStruct((B, S, DIFF_WIDTH), BF16),
        scratch_shapes=[pltpu.VMEM((4, 1, tq), F32), pltpu.VMEM((4, V_ROWS, tq), F32),
                        pltpu.VMEM((4, LANES, tq), BF16), pltpu.VMEM((12, AUX_ROWS, tq), BF16),
                        pltpu.VMEM((4, tk, tq), BF16), pltpu.VMEM((4, tk, tq), BF16),
                        pltpu.VMEM((tq, tq), F32), pltpu.VMEM((4, tq, tq), F32)],
        compiler_params=_params("parallel", "parallel", "arbitrary"),
        name="diff_attn",
    )(slopes, pieces, lam, subln_w, kn, dqvt, dk, aux, dqvt)


def _out_proj_kernel(x_ref, ret_ref, diff_ref, cv_ref, prev_ref, next_ref, cw_ref, w_ref, o_ref, *, tm):
    i = pl.program_id(1)
    last = pl.num_programs(1) - 1
    cc = CONV_CHANNELS
    cv = cv_ref[0]
    u = cv[:, 2 * cc:3 * cc] * cv[:, 0:cc]
    prev = prev_ref[0]
    nxt = next_ref[0]
    u_prev = prev[SUBLANES - 1:SUBLANES, 2 * cc:3 * cc] * prev[SUBLANES - 1:SUBLANES, 0:cc]
    u_next = nxt[0:1, 2 * cc:3 * cc] * nxt[0:1, 0:cc]
    u_prev = jnp.where(i == 0, 0.0, u_prev)
    u_next = jnp.where(i == last, 0.0, u_next)
    rows = lax.broadcasted_iota(jnp.int32, (tm, cc), 0)
    u_m1 = jnp.where(rows == 0, u_prev, pltpu.roll(u, 1, 0))
    u_p1 = jnp.where(rows == tm - 1, u_next, pltpu.roll(u, tm - 1, 0))
    y = cw_ref[0:1] * u_m1 + cw_ref[1:2] * u + cw_ref[2:3] * u_p1
    conv = (cv[:, cc:2 * cc] * y).astype(BF16)
    acc = jnp.dot(ret_ref[0], w_ref[0:RET_WIDTH], preferred_element_type=F32)
    acc += jnp.dot(diff_ref[0], w_ref[RET_WIDTH:RET_WIDTH + DIFF_WIDTH], preferred_element_type=F32)
    acc += jnp.dot(conv, w_ref[RET_WIDTH + DIFF_WIDTH:], preferred_element_type=F32)
    o_ref[0] = x_ref[0] + acc


def _out_proj(x, ret, diff, cv, conv_w, w_out, tm):
    B, S, D = x.shape
    per = tm // SUBLANES
    nblk8 = S // SUBLANES
    tile = lambda width: pl.BlockSpec((1, tm, width), lambda b, i: (b, i, 0))
    return pl.pallas_call(
        functools.partial(_out_proj_kernel, tm=tm),
        grid=(B, S // tm),
        in_specs=[
            tile(D), tile(RET_WIDTH), tile(DIFF_WIDTH), tile(3 * CONV_CHANNELS),
            pl.BlockSpec((1, SUBLANES, 3 * CONV_CHANNELS), lambda b, i: (b, jnp.maximum(i * per - 1, 0), 0)),
            pl.BlockSpec((1, SUBLANES, 3 * CONV_CHANNELS),
                         lambda b, i: (b, jnp.minimum((i + 1) * per, nblk8 - 1), 0)),
            pl.BlockSpec((CONV_WIDTH, CONV_CHANNELS), lambda b, i: (0, 0)),
            pl.BlockSpec((D, D), lambda b, i: (0, 0)),
        ],
        out_specs=tile(D),
        out_shape=jax.ShapeDtypeStruct((B, S, D), F32),
        compiler_params=_params("parallel", "parallel"),
        name="out_proj",
    )(x, ret, diff, cv, cv, cv, conv_w, w_out)


_GROUP_LANE0 = N_EXPERTS
MOE_ROW_BLOCK = 256
MOE_EXPERTS_PER_STEP = 2
assert EXPERTS_PER_GROUP % MOE_EXPERTS_PER_STEP == 0


def _moe_sorted_rows(tm):
    return (tm + N_GROUPS * (MOE_ROW_BLOCK - 1)) // MOE_ROW_BLOCK * MOE_ROW_BLOCK


def _router_gates(logits):
    lane = lax.broadcasted_iota(jnp.int32, logits.shape, 1)
    big = jnp.int32(LANES)
    is_group = (lane >= _GROUP_LANE0) & (lane < _GROUP_LANE0 + N_GROUPS)
    gl = jnp.where(is_group, logits, -jnp.inf)
    g_max = jnp.max(gl, axis=-1, keepdims=True)
    g_idx = jnp.min(jnp.where(gl == g_max, lane - _GROUP_LANE0, big), axis=-1, keepdims=True)
    g_w = 1.0 / jnp.sum(jnp.where(is_group, jnp.exp(gl - g_max), 0.0), axis=-1, keepdims=True)
    in_group = (lane >= g_idx * EXPERTS_PER_GROUP) & (lane < (g_idx + 1) * EXPERTS_PER_GROUP)
    el = jnp.where(in_group, logits, -jnp.inf)
    e1 = jnp.max(el, axis=-1, keepdims=True)
    i1 = jnp.min(jnp.where(el == e1, lane, big), axis=-1, keepdims=True)
    el2 = jnp.where(lane == i1, -jnp.inf, el)
    e2 = jnp.max(el2, axis=-1, keepdims=True)
    i2 = jnp.min(jnp.where(el2 == e2, lane, big), axis=-1, keepdims=True)
    r = jnp.exp(e2 - e1)
    p1 = g_w / (1.0 + r)
    p2 = g_w * r / (1.0 + r)
    return jnp.where(lane == i1, p1, jnp.where(lane == i2, p2, 0.0)), g_idx


def _moe_kernel(x_ref, nw_ref, rw_ref, rb_ref, wg_ref, wu_ref, wd_ref, fw_ref, o_ref,
                xs_ref, gs_ref, ys_ref, dest_ref, ltri_ref, seg_ref, *, final_norm, tm):
    tile = pl.program_id(0)
    e = pl.program_id(1)
    sorted_rows = _moe_sorted_rows(tm)

    @pl.when((tile == 0) & (e == 0))
    def _strictly_lower():
        r = lax.broadcasted_iota(jnp.int32, (tm, tm), 0)
        c = lax.broadcasted_iota(jnp.int32, (tm, tm), 1)
        ltri_ref[...] = jnp.where(c < r, 1.0, 0.0).astype(BF16)

    @pl.when(e == 0)
    def _route_and_sort():
        x = x_ref[...]
        xn = x * lax.rsqrt(jnp.mean(x * x, axis=-1, keepdims=True) + EPS) * nw_ref[...]
        hi, lo = _split_hi_lo(xn)
        r_hi, r_lo = _split_hi_lo(rw_ref[...])
        logits = (jnp.dot(hi, r_hi, preferred_element_type=F32)
                  + jnp.dot(lo, r_hi, preferred_element_type=F32)
                  + jnp.dot(hi, r_lo, preferred_element_type=F32)) + rb_ref[...]
        gates, g_idx = _router_gates(logits)
        lane = lax.broadcasted_iota(jnp.int32, (tm, LANES), 1)
        member = jnp.where(lane == g_idx, 1.0, 0.0)
        earlier = jnp.dot(ltri_ref[...], member.astype(BF16), preferred_element_type=F32)
        rank = jnp.sum(member * earlier, axis=-1, keepdims=True)
        count = jnp.sum(member, axis=0, keepdims=True)
        seg = jnp.floor((count + (MOE_ROW_BLOCK - 1.0)) * (1.0 / MOE_ROW_BLOCK)) * MOE_ROW_BLOCK
        lane1 = lax.broadcasted_iota(jnp.int32, (1, LANES), 1)
        s0, s1, s2 = seg[:, 0:1], seg[:, 1:2], seg[:, 2:3]
        start = jnp.where(lane1 == 1, s0, jnp.where(lane1 == 2, s0 + s1, jnp.where(lane1 == 3, s0 + s1 + s2, 0.0)))
        dest = rank + jnp.sum(member * start, axis=-1, keepdims=True)
        dest_ref[...] = jnp.broadcast_to(dest, (tm, LANES))
        dest_row = dest_ref[...].T[0:1, :]
        row_id = lax.broadcasted_iota(jnp.int32, (sorted_rows, tm), 0).astype(F32)
        perm = jnp.where(row_id == dest_row, 1.0, 0.0).astype(BF16)
        g_hi, g_mid, g_lo = _split3(gates)
        g_pieces = g_hi + pltpu.roll(g_mid, N_EXPERTS, 1) + pltpu.roll(g_lo, 2 * N_EXPERTS, 1)
        moved = jnp.dot(perm, jnp.concatenate([hi, g_pieces.astype(BF16)], axis=1), preferred_element_type=F32)
        xs_ref[...] = moved[:, :D_MODEL].astype(BF16)
        g_moved = moved[:, D_MODEL:]
        gs_ref[...] = (g_moved + pltpu.roll(g_moved, LANES - N_EXPERTS, 1)
                       + pltpu.roll(g_moved, LANES - 2 * N_EXPERTS, 1))
        ys_ref[...] = jnp.zeros(ys_ref.shape, F32)
        start_i = start.astype(jnp.int32)
        blocks_i = (seg * (1.0 / MOE_ROW_BLOCK)).astype(jnp.int32)
        for g in range(N_GROUPS):
            seg_ref[g] = start_i[0, g]
            seg_ref[N_GROUPS + g] = blocks_i[0, g]

    group = (e * MOE_EXPERTS_PER_STEP) // EXPERTS_PER_GROUP
    first_row = seg_ref[group]

    def row_block(b, carry):
        r0 = pl.multiple_of(first_row + b * MOE_ROW_BLOCK, MOE_ROW_BLOCK)
        xb = xs_ref[pl.ds(r0, MOE_ROW_BLOCK), :]
        gsb = gs_ref[pl.ds(r0, MOE_ROW_BLOCK), :]
        lane = lax.broadcasted_iota(jnp.int32, gsb.shape, 1)
        experts = range(MOE_EXPERTS_PER_STEP)
        gate = [jnp.dot(xb, wg_ref[j], preferred_element_type=F32) for j in experts]
        up = [jnp.dot(xb, wu_ref[j], preferred_element_type=F32) for j in experts]
        hmid = [(gate[j] * jax.nn.sigmoid(gate[j]) * up[j]).astype(BF16) for j in experts]
        down = [jnp.dot(hmid[j], wd_ref[j], preferred_element_type=F32) for j in experts]
        total = None
        for j in experts:
            g_e = jnp.sum(jnp.where(lane == e * MOE_EXPERTS_PER_STEP + j, gsb, 0.0), axis=-1, keepdims=True)
            total = g_e * down[j] if total is None else total + g_e * down[j]
        ys_ref[pl.ds(r0, MOE_ROW_BLOCK), :] += total
        return carry

    lax.fori_loop(0, seg_ref[N_GROUPS + group], row_block, 0)

    @pl.when(e == pl.num_programs(1) - 1)
    def _finish():
        col_id = lax.broadcasted_iota(jnp.int32, (tm, sorted_rows), 1).astype(F32)
        unperm = jnp.where(col_id == dest_ref[:, 0:1], 1.0, 0.0).astype(BF16)
        y = x_ref[...] + jnp.dot(unperm, ys_ref[...].astype(BF16), preferred_element_type=F32)
        if final_norm:
            y = y * lax.rsqrt(jnp.mean(y * y, axis=-1, keepdims=True) + EPS) * fw_ref[...]
        o_ref[...] = y


def _moe(x, nw, r_w, r_b, wg, wu, wd, final_w, final_norm, tm):
    T, D = x.shape
    const = lambda shape: pl.BlockSpec(shape, lambda i, e: (0,) * len(shape))
    sorted_rows = _moe_sorted_rows(tm)
    return pl.pallas_call(
        functools.partial(_moe_kernel, final_norm=final_norm, tm=tm),
        grid=(T // tm, N_EXPERTS // MOE_EXPERTS_PER_STEP),
        in_specs=[
            pl.BlockSpec((tm, D), lambda i, e: (i, 0)),
            const((1, D)), const((D, LANES)), const((1, LANES)),
            pl.BlockSpec((MOE_EXPERTS_PER_STEP, D, EXPERT_FF), lambda i, e: (e, 0, 0)),
            pl.BlockSpec((MOE_EXPERTS_PER_STEP, D, EXPERT_FF), lambda i, e: (e, 0, 0)),
            pl.BlockSpec((MOE_EXPERTS_PER_STEP, EXPERT_FF, D), lambda i, e: (e, 0, 0)),
            const((1, D)),
        ],
        out_specs=pl.BlockSpec((tm, D), lambda i, e: (i, 0)),
        out_shape=jax.ShapeDtypeStruct((T, D), F32),
        scratch_shapes=[pltpu.VMEM((sorted_rows, D), BF16), pltpu.VMEM((sorted_rows, LANES), F32),
                        pltpu.VMEM((sorted_rows, D), F32), pltpu.VMEM((tm, LANES), F32),
                        pltpu.VMEM((tm, tm), BF16), pltpu.SMEM((2 * N_GROUPS,), jnp.int32)],
        compiler_params=_params("arbitrary", "arbitrary"),
        name="moe",
    )(x, nw, r_w, r_b, wg, wu, wd, final_w)


def _tile(n, pref):
    t = min(n, pref)
    assert n % t == 0, (n, t)
    return t


def _prep_layer(l, norm1_w, w_in, ret_decay_logit, ret_gn_w, diff_lambda, diff_subln_w, conv_w, w_out, norm2_w,
                router_group_w, router_group_b, router_expert_w, router_expert_b, expert_w_gate, expert_w_up,
                expert_w_down):
    w = w_in[l]
    w_nat = jnp.concatenate([w[:, :_OFF_DQ], w[:, _OFF_DK:_OFF_DV], w[:, _OFF_CONV:]], axis=1).astype(BF16)
    w_t = jnp.concatenate([w[:, _OFF_DQ:_OFF_DK], w[:, _OFF_DV:_OFF_CONV]], axis=1).T.astype(BF16)
    router = jnp.zeros((D_MODEL, LANES), F32)
    router = router.at[:, :N_EXPERTS].set(router_expert_w[l]).at[:, N_EXPERTS:N_EXPERTS + N_GROUPS].set(
        router_group_w[l])
    r_b = jnp.zeros((1, LANES), F32).at[0, :N_EXPERTS].set(router_expert_b[l]).at[
        0, N_EXPERTS:N_EXPERTS + N_GROUPS].set(router_group_b[l])
    return dict(
        norm1=norm1_w[l][None, :], w_nat=w_nat, w_t=w_t,
        lg=jax.nn.log_sigmoid(ret_decay_logit[l].astype(F32)),
        gn_w=ret_gn_w[l][None, :].astype(F32),
        lam=diff_lambda[l].astype(F32), subln=diff_subln_w[l][:, None].astype(F32),
        conv_w=conv_w[l].astype(F32), w_out=w_out[l].astype(BF16),
        norm2=norm2_w[l][None, :], r_w=router, r_b=r_b,
        wg=expert_w_gate[l].astype(BF16), wu=expert_w_up[l].astype(BF16), wd=expert_w_down[l].astype(BF16),
    )


def _trunk(x, layers, final_w, slopes, pieces):
    B, S, D = x.shape
    tm = _tile(S, 512)
    tq = _tile(S, 512)
    tk = tq // 2
    t_moe = _tile(B * S, 1024)
    aux = _position_features(S)
    for l, lw in enumerate(layers):
        rqkv, rg, dk, dqvt, cv, kn = _in_proj(x, lw["norm1"], lw["w_nat"], lw["w_t"], tm)
        ret = _retention(lw["lg"], rqkv, rg, lw["gn_w"])
        diff = _diff_attention(slopes, pieces, lw["lam"], lw["subln"], kn, dqvt, dk, aux, l, tq, tk)
        x = _out_proj(x, ret, diff, cv, lw["conv_w"], lw["w_out"], tm)
        last = l == len(layers) - 1
        x = _moe(x.reshape(B * S, D), lw["norm2"], lw["r_w"], lw["r_b"], lw["wg"], lw["wu"],
                 lw["wd"], final_w, last, t_moe).reshape(B, S, D)
    return x


def kernel(x_prompt, x_sample, norm1_w, w_in, ret_decay_logit, ret_gn_w, diff_lambda, diff_subln_w, conv_w, w_out,
           norm2_w, router_group_w, router_group_b, router_expert_w, router_expert_b, expert_w_gate, expert_w_up,
           expert_w_down, final_norm_w):
    depth = w_in.shape[0]
    layers = [
        _prep_layer(l, norm1_w, w_in, ret_decay_logit, ret_gn_w, diff_lambda, diff_subln_w, conv_w, w_out, norm2_w,
                    router_group_w, router_group_b, router_expert_w, router_expert_b, expert_w_gate, expert_w_up,
                    expert_w_down)
        for l in range(depth)
    ]
    final_w = final_norm_w[None, :].astype(F32)
    slopes = (np.float32(2.0) ** (np.float32(-8.0) * np.arange(1, DIFF_HEADS + 1, dtype=np.float32)
                                  / np.float32(DIFF_HEADS))) * np.float32(LOG2E)
    s_hi = slopes.astype(BF16).astype(np.float32)
    s_mid = (slopes - s_hi).astype(BF16).astype(np.float32)
    s_lo = (slopes - s_hi - s_mid).astype(BF16).astype(np.float32)
    pieces = jnp.asarray(np.stack([s_hi, s_mid, s_lo], axis=1).reshape(-1))
    slopes = jnp.asarray(slopes)
    return (_trunk(x_prompt, layers, final_w, slopes, pieces), _trunk(x_sample, layers, final_w, slopes, pieces))
```

```python
import functools
import math

import jax
import jax.numpy as jnp
import numpy as np
from jax import lax
from jax.experimental import pallas as pl
from jax.experimental.pallas import tpu as pltpu

F32 = jnp.float32
BF16 = jnp.bfloat16

D_MODEL = 1024
RET_HEADS = 6
RET_HEAD_DIM = 64
RET_WIDTH = RET_HEADS * RET_HEAD_DIM
DIFF_HEADS = 6
DIFF_QK_DIM = 32
DIFF_V_DIM = 64
DIFF_WIDTH = DIFF_HEADS * DIFF_V_DIM
CONV_CHANNELS = 256
CONV_WIDTH = 3
N_GROUPS = 4
EXPERTS_PER_GROUP = 4
N_EXPERTS = 16
EXPERT_FF = 512
EPS = 1e-6

LANES = 128
HEAD_PAIRS = RET_HEADS // 2
RET_CHUNK = 128
SUBLANES = 8
VMEM_LIMIT = 56 * 1024 * 1024
LOG2E = math.log2(math.e)
NEG_BIG = -1e30

_OFF_RG = 3 * RET_WIDTH
_OFF_DQ = 4 * RET_WIDTH
_OFF_DK = _OFF_DQ + DIFF_WIDTH
_OFF_DV = _OFF_DK + DIFF_WIDTH
_OFF_CONV = _OFF_DV + DIFF_WIDTH


def _params(*sem):
    return pltpu.CompilerParams(dimension_semantics=sem, vmem_limit_bytes=VMEM_LIMIT)


def _split_hi_lo(x):
    hi = x.astype(BF16)
    lo = (x - hi.astype(F32)).astype(BF16)
    return hi, lo


def _in_proj_kernel(x_ref, nw_ref, w_ref, wt_ref, rqkv_ref, rg_ref, dk_ref, dqvt_ref, cv_ref, kn_ref):
    x = x_ref[0]
    ms = jnp.mean(x * x, axis=-1, keepdims=True)
    h = (x * lax.rsqrt(ms + EPS) * nw_ref[...]).astype(BF16)
    a = jnp.dot(h, w_ref[:, 0:_OFF_DQ], preferred_element_type=F32)
    rqkv_ref[0, :, 0:RET_WIDTH] = a[:, 0:RET_WIDTH].astype(BF16)
    rqkv_ref[0, :, RET_WIDTH:2 * RET_WIDTH] = (a[:, RET_WIDTH:2 * RET_WIDTH] * RET_HEAD_DIM ** -0.5).astype(BF16)
    rqkv_ref[0, :, 2 * RET_WIDTH:3 * RET_WIDTH] = a[:, 2 * RET_WIDTH:3 * RET_WIDTH].astype(BF16)
    rg_ref[0] = a[:, _OFF_RG:_OFF_DQ]
    dk = jnp.dot(h, w_ref[:, _OFF_DQ:_OFF_DQ + DIFF_WIDTH], preferred_element_type=F32).astype(BF16)
    dk_ref[0] = dk
    dkf = dk.astype(F32)
    sq_hi, sq_lo = _split_hi_lo(dkf * dkf)
    grp = (lax.shift_right_logical(lax.broadcasted_iota(jnp.int32, (DIFF_WIDTH, LANES), 0),
                                   DIFF_QK_DIM.bit_length() - 1)
           == lax.broadcasted_iota(jnp.int32, (DIFF_WIDTH, LANES), 1))
    grp = jnp.where(grp, 1.0, 0.0).astype(BF16)
    kn2 = jnp.dot(sq_hi, grp, preferred_element_type=F32) + jnp.dot(sq_lo, grp, preferred_element_type=F32)
    kn_ref[0, 0] = jnp.broadcast_to(jnp.max(kn2, axis=0, keepdims=True), (SUBLANES, LANES))
    cv_ref[0] =jnp.dot(h, w_ref[:, _OFF_DQ + DIFF_WIDTH:], preferred_element_type=F32)
    t = lax.dot_general(wt_ref[...], h, (((1,), (1,)), ((), ())), preferred_element_type=F32)
    dqvt_ref[0, 0:DIFF_WIDTH, :] = (t[0:DIFF_WIDTH] * (DIFF_QK_DIM ** -0.5 * LOG2E)).astype(BF16)
    dqvt_ref[0, DIFF_WIDTH:, :] = t[DIFF_WIDTH:].astype(BF16)


def _in_proj(x, nw, w_nat, w_t, tm):
    B, S, D = x.shape
    n_nat = w_nat.shape[1]
    return pl.pallas_call(
        _in_proj_kernel,
        grid=(B, S // tm),
        in_specs=[
            pl.BlockSpec((1, tm, D), lambda b, i: (b, i, 0)),
            pl.BlockSpec((1, D), lambda b, i: (0, 0)),
            pl.BlockSpec((D, n_nat), lambda b, i: (0, 0)),
            pl.BlockSpec((2 * DIFF_WIDTH, D), lambda b, i: (0, 0)),
        ],
        out_specs=[
            pl.BlockSpec((1, tm, 3 * RET_WIDTH), lambda b, i: (b, i, 0)),
            pl.BlockSpec((1, tm, RET_WIDTH), lambda b, i: (b, i, 0)),
            pl.BlockSpec((1, tm, DIFF_WIDTH), lambda b, i: (b, i, 0)),
            pl.BlockSpec((1, 2 * DIFF_WIDTH, tm), lambda b, i: (b, 0, i)),
            pl.BlockSpec((1, tm, 3 * CONV_CHANNELS), lambda b, i: (b, i, 0)),
            pl.BlockSpec((1, 1, SUBLANES, LANES), lambda b, i: (b, i, 0, 0)),
        ],
        out_shape=[
            jax.ShapeDtypeStruct((B, S, 3 * RET_WIDTH), BF16),
            jax.ShapeDtypeStruct((B, S, RET_WIDTH), F32),
            jax.ShapeDtypeStruct((B, S, DIFF_WIDTH), BF16),
            jax.ShapeDtypeStruct((B, 2 * DIFF_WIDTH, S), BF16),
            jax.ShapeDtypeStruct((B, S, 3 * CONV_CHANNELS), F32),
            jax.ShapeDtypeStruct((B, S // tm, SUBLANES, LANES), F32),
        ],
        compiler_params=_params("parallel", "parallel"),
        name="in_proj",
    )(x, nw, w_nat, w_t)


def _pair_lane_value(lg_ref, direction, pair, shape, axis):
    idx = lax.broadcasted_iota(jnp.int32, shape, axis)
    return jnp.where(idx < RET_HEAD_DIM, lg_ref[direction, 2 * pair], lg_ref[direction, 2 * pair + 1])


def _same_head_mask():
    r = lax.broadcasted_iota(jnp.int32, (LANES, LANES), 0)
    c = lax.broadcasted_iota(jnp.int32, (LANES, LANES), 1)
    return (r < RET_HEAD_DIM) == (c < RET_HEAD_DIM)


def _ret_fwd_kernel(lg_ref, qkv_ref, y_ref, state_ref, dmat_ref, qdec_ref, kdec_ref, cdec_ref, *, chunks):
    C = RET_CHUNK

    @pl.when((pl.program_id(0) == 0) & (pl.program_id(1) == 0))
    def _build_tables():
        diff = (lax.broadcasted_iota(jnp.int32, (C, C), 0) - lax.broadcasted_iota(jnp.int32, (C, C), 1)).astype(F32)
        for h in range(RET_HEADS):
            lower = jnp.exp(jnp.maximum(diff, 0.0) * lg_ref[0, h])
            upper = jnp.exp(jnp.maximum(-diff, 0.0) * lg_ref[1, h])
            dmat_ref[h] = jnp.where(diff >= 0, lower, upper)
        pos = lax.broadcasted_iota(jnp.int32, (C, LANES), 0).astype(F32)
        for p in range(HEAD_PAIRS):
            lg_lane = _pair_lane_value(lg_ref, 0, p, (C, LANES), 1)
            qdec_ref[p] = jnp.exp((pos + 1.0) * lg_lane)
            kdec_ref[p] = jnp.exp((C - 1.0 - pos) * lg_lane)
            cdec_ref[p] = jnp.exp(float(C) * _pair_lane_value(lg_ref, 0, p, (LANES, LANES), 0))

    @pl.when(pl.program_id(1) == 0)
    def _reset():
        state_ref[...] = jnp.zeros(state_ref.shape, F32)

    lane = lax.broadcasted_iota(jnp.int32, (C, LANES), 1)
    same_head = _same_head_mask()
    def operands(p, ci):
        rows = slice(ci * C, (ci + 1) * C)
        return (qkv_ref[0, rows, p * LANES:(p + 1) * LANES],
                qkv_ref[0, rows, RET_WIDTH + p * LANES:RET_WIDTH + (p + 1) * LANES],
                qkv_ref[0, rows, 2 * RET_WIDTH + p * LANES:2 * RET_WIDTH + (p + 1) * LANES])

    items = [(p, ci) for p in range(HEAD_PAIRS) for ci in range(chunks)]
    scores, kvs = {}, {}
    for p, ci in items:
        q, k, v = operands(p, ci)
        for h in range(2):
            head_lanes = (lane >= h * RET_HEAD_DIM) & (lane < (h + 1) * RET_HEAD_DIM)
            qm = jnp.where(head_lanes, q, jnp.zeros_like(q))
            scores[p, ci, h] = lax.dot_general(qm, k, (((1,), (1,)), ((), ())), preferred_element_type=F32)
        kd = (k.astype(F32) * kdec_ref[p]).astype(BF16)
        kvs[p, ci] = lax.dot_general(kd, v, (((0,), (0,)), ((), ())), preferred_element_type=F32)
    states = {}
    for p in range(HEAD_PAIRS):
        state = state_ref[p]
        for ci in range(chunks):
            states[p, ci] = state.astype(BF16)
            state = cdec_ref[p] * state + jnp.where(same_head, kvs[p, ci], 0.0)
        state_ref[p] = state
    for p, ci in items:
        q, _, v = operands(p, ci)
        inner = [jnp.dot((scores[p, ci, h] * dmat_ref[2 * p + h]).astype(BF16), v, preferred_element_type=F32)
                 for h in range(2)]
        cross = jnp.dot(q, states[p, ci], preferred_element_type=F32) * qdec_ref[p]
        y_ref[0, ci * C:(ci + 1) * C, p * LANES:(p + 1) * LANES] = (
            jnp.where(lane < RET_HEAD_DIM, inner[0], inner[1]) + cross)


def _ret_bwd_kernel(lg_ref, qkv_ref, y1_ref, g_ref, gnw_ref, o_ref, state_ref, qdec_ref, kdec_ref, cdec_ref, *,
                    chunks):
    C = RET_CHUNK

    @pl.when((pl.program_id(0) == 0) & (pl.program_id(1) == 0))
    def _build_tables():
        pos = lax.broadcasted_iota(jnp.int32, (C, LANES), 0).astype(F32)
        for p in range(HEAD_PAIRS):
            lg_lane = _pair_lane_value(lg_ref, 1, p, (C, LANES), 1)
            qdec_ref[p] = jnp.exp((float(C) - pos) * lg_lane)
            kdec_ref[p] = jnp.exp(pos * lg_lane)
            cdec_ref[p] = jnp.exp(float(C) * _pair_lane_value(lg_ref, 1, p, (LANES, LANES), 0))

    @pl.when(pl.program_id(1) == 0)
    def _reset():
        state_ref[...] = jnp.zeros(state_ref.shape, F32)

    same_head = _same_head_mask()
    head_avg = jnp.where(same_head, 1.0 / RET_HEAD_DIM, 0.0).astype(BF16)

    def head_mean(t):
        hi, lo = _split_hi_lo(t)
        return (jnp.dot(hi, head_avg, preferred_element_type=F32)
                + jnp.dot(lo, head_avg, preferred_element_type=F32))

    items = [(p, ci) for p in range(HEAD_PAIRS) for ci in range(chunks)]
    kvs = {}
    for p, ci in items:
        rows = slice(ci * C, (ci + 1) * C)
        k = qkv_ref[0, rows, RET_WIDTH + p * LANES:RET_WIDTH + (p + 1) * LANES]
        v = qkv_ref[0, rows, 2 * RET_WIDTH + p * LANES:2 * RET_WIDTH + (p + 1) * LANES]
        kd = (k.astype(F32) * kdec_ref[p]).astype(BF16)
        kvs[p, ci] = lax.dot_general(kd, v, (((0,), (0,)), ((), ())), preferred_element_type=F32)
    states = {}
    for p in range(HEAD_PAIRS):
        state = state_ref[p]
        for ci in reversed(range(chunks)):
            states[p, ci] = state.astype(BF16)
            state = cdec_ref[p] * state + jnp.where(same_head, kvs[p, ci], 0.0)
        state_ref[p] = state
    ys = {}
    for p, ci in items:
        rows = slice(ci * C, (ci + 1) * C)
        sl = slice(p * LANES, (p + 1) * LANES)
        cross = jnp.dot(qkv_ref[0, rows, sl], states[p, ci], preferred_element_type=F32)
        ys[p, ci] = y1_ref[0, rows, sl] + cross * qdec_ref[p]
    means = {key: head_mean(y) for key, y in ys.items()}
    devs = {key: ys[key] - means[key] for key in ys}
    variances = {key: head_mean(d * d) for key, d in devs.items()}
    for p, ci in items:
        rows = slice(ci * C, (ci + 1) * C)
        sl = slice(p * LANES, (p + 1) * LANES)
        g = g_ref[0, rows, sl]
        o = devs[p, ci] * lax.rsqrt(variances[p, ci] + EPS) * gnw_ref[:, sl] * (g * jax.nn.sigmoid(g))
        o_ref[0, rows, sl] = o.astype(o_ref.dtype)


def _retention(lg, rqkv, rg, gn_w):
    B, S, _ = rqkv.shape
    chunks = math.gcd(S // RET_CHUNK, 4)
    C = chunks * RET_CHUNK
    n = S // C
    smem = pl.BlockSpec(memory_space=pltpu.SMEM)
    table = pltpu.VMEM((HEAD_PAIRS, RET_CHUNK, LANES), F32)
    state = pltpu.VMEM((HEAD_PAIRS, LANES, LANES), F32)
    y1 = pl.pallas_call(
        functools.partial(_ret_fwd_kernel, chunks=chunks),
        grid=(B, n),
        in_specs=[smem, pl.BlockSpec((1, C, 3 * RET_WIDTH), lambda b, i: (b, i, 0))],
        out_specs=pl.BlockSpec((1, C, RET_WIDTH), lambda b, i: (b, i, 0)),
        out_shape=jax.ShapeDtypeStruct((B, S, RET_WIDTH), F32),
        scratch_shapes=[state, pltpu.VMEM((RET_HEADS, RET_CHUNK, RET_CHUNK), F32), table, table, state],
        compiler_params=_params("arbitrary", "arbitrary"),
        name="ret_fwd",
    )(lg, rqkv)
    rev = lambda b, i: (b, n - 1 - i, 0)
    return pl.pallas_call(
        functools.partial(_ret_bwd_kernel, chunks=chunks),
        grid=(B, n),
        in_specs=[
            smem,
            pl.BlockSpec((1, C, 3 * RET_WIDTH), rev),
            pl.BlockSpec((1, C, RET_WIDTH), rev),
            pl.BlockSpec((1, C, RET_WIDTH), rev),
            pl.BlockSpec((1, RET_WIDTH), lambda b, i: (0, 0)),
        ],
        out_specs=pl.BlockSpec((1, C, RET_WIDTH), rev),
        out_shape=jax.ShapeDtypeStruct((B, S, RET_WIDTH), BF16),
        scratch_shapes=[state, table, table, state],
        compiler_params=_params("arbitrary", "arbitrary"),
        name="ret_bwd",
    )(lg, rqkv, y1, rg, gn_w)


AUX_BLOCK = 256
AUX_ROWS = 16
V_ROWS = DIFF_V_DIM + SUBLANES
GUARD_LOG2 = 60.0
BOUND_SLACK = 1.01
LONG_BODY_PAIRS = 8
UNDERFLOW_LOG2 = 150.0


def _position_features(seq):
    j = jnp.arange(seq, dtype=jnp.int32)
    ones = jnp.ones((seq,), F32)
    n = (j // AUX_BLOCK).astype(F32)
    jc = (j % AUX_BLOCK - AUX_BLOCK // 2).astype(F32)
    feat = jnp.stack([ones] * 3 + [n] * 3 + [jc] * 3, axis=1)
    return jnp.pad(feat, ((0, 0), (0, LANES - feat.shape[1]))).astype(BF16)


def _split3(x):
    hi = x.astype(BF16).astype(F32)
    r = x - hi
    mid = r.astype(BF16).astype(F32)
    return hi, mid, (r - mid).astype(BF16).astype(F32)


def _attn_kernel(slopes_ref, pieces_ref, lam_ref, sw_ref, kn_ref, qt_ref, k_ref, aux_ref, vt_ref, o_ref,
                 m_ref, acc_ref, qm_ref, aug_ref, pa_ref, pb_ref, dist_ref, *, seq, tq, tk, lam_init):
    pair = pl.program_id(1)
    qi = pl.program_id(2)
    q0 = qi * tq
    nk = seq // tk
    per = tq // tk
    kd0 = qi * per
    qt = qt_ref[0]
    row = lax.broadcasted_iota(jnp.int32, (LANES, tq), 0)
    q_masked = []
    for hc in range(4):
        lo = hc * DIFF_QK_DIM
        q_masked.append(jnp.where((row >= lo) & (row < lo + DIFF_QK_DIM), qt, jnp.zeros_like(qt)))

    @pl.when(qi == 0)
    def _distance_table():
        d = lax.broadcasted_iota(jnp.int32, (tq, tq), 1) - lax.broadcasted_iota(jnp.int32, (tq, tq), 0)
        dist_ref[...] = jnp.abs(d).astype(F32)

    kd = pl.multiple_of(q0, tq)
    kdiag = k_ref[0, pl.ds(kd, tq), :]
    self_prod = qt.astype(F32) * kdiag.astype(F32).T
    shifts = [jnp.sum(self_prod[hc * DIFF_QK_DIM:(hc + 1) * DIFF_QK_DIM], axis=0, keepdims=True) for hc in range(4)]

    kn2 = jnp.max(jnp.max(kn_ref[0], axis=0), axis=0, keepdims=True)
    kn_lane = lax.broadcasted_iota(jnp.int32, (1, LANES), 1)
    excess = []
    for h in range(2):
        worst = None
        for c in range(2):
            hc = 2 * h + c
            qf = q_masked[hc].astype(F32)
            qn2 = jnp.sum(qf * qf, axis=0, keepdims=True)
            kn2_hc = jnp.max(jnp.where(kn_lane == 4 * pair + hc, kn2, 0.0), axis=-1, keepdims=True)
            bound = jnp.sqrt(qn2 * kn2_hc) * BOUND_SLACK + 1e-3
            e = jnp.max(bound - shifts[hc], axis=-1, keepdims=True)
            worst = e if worst is None else jnp.maximum(worst, e)
        excess.append(worst)
    single_pass = jnp.where(jnp.maximum(excess[0], excess[1]) <= GUARD_LOG2, 1, 0).astype(jnp.int32)[0, 0] == 1

    def v_rows_with_ones(h, k0, n):
        vt = vt_ref[0, h * DIFF_V_DIM:(h + 1) * DIFF_V_DIM, pl.ds(k0, n)]
        return jnp.concatenate([vt, jnp.ones((SUBLANES, n), BF16)], axis=0)

    def keys_with_features(k0):
        return jnp.concatenate([k_ref[0, pl.ds(k0, tk), :], aux_ref[pl.ds(k0, tk), :]], axis=1)

    def queries_with_features(side, hc):
        return jnp.concatenate([qm_ref[hc], aug_ref[side * 4 + hc],
                                jnp.zeros((LANES - AUX_ROWS, tq), BF16)], axis=0)

    @pl.when(single_pass)
    def _single_pass():
        ipos = (q0 + lax.broadcasted_iota(jnp.int32, (1, tq), 1)).astype(F32)
        aug_row = lax.broadcasted_iota(jnp.int32, (AUX_ROWS, tq), 0)
        q0f = q0.astype(F32)
        lo_even, n_lo, totals = [], [], []
        for h in range(2):
            slope = slopes_ref[2 * pair + h]
            pieces = [pieces_ref[(2 * pair + h) * 3 + i] for i in range(3)]
            for c in range(2):
                hc = 2 * h + c
                for side, sign in ((0, 1.0), (1, -1.0), (2, 0.0)):
                    const = (-sign) * (slope * ipos) - shifts[hc] + sign * (0.5 * AUX_BLOCK) * slope
                    c_hi, c_mid, c_lo = _split3(const)
                    aug = jnp.where(aug_row == 0, c_hi,
                                    jnp.where(aug_row == 1, c_mid, jnp.where(aug_row == 2, c_lo, 0.0)))
                    for i in range(3):
                        aug = jnp.where(aug_row == 3 + i, sign * AUX_BLOCK * pieces[i], aug)
                        aug = jnp.where(aug_row == 6 + i, sign * pieces[i], aug)
                    aug_ref[side * 4 + hc] = aug.astype(BF16)
                qm_ref[hc] = q_masked[hc]
            reach = (excess[h] + UNDERFLOW_LOG2) / slope
            lo_blk = jnp.clip(jnp.floor((q0f - (tk - 1.0) - reach) / tk), 0.0, float(nk))
            hi_blk = jnp.clip(jnp.floor((reach + q0f + (tq - 1.0)) / tk) + 1.0, 0.0, float(nk))
            lo_blk = jnp.minimum(lo_blk.astype(jnp.int32)[0, 0], kd0)
            hi_blk = jnp.maximum(hi_blk.astype(jnp.int32)[0, 0], kd0 + per)
            odd = (kd0 - lo_blk + hi_blk - kd0 - per) % 2
            lo_h = jnp.where((odd == 1) & (lo_blk > 0), lo_blk - 1, lo_blk)
            hi_h = jnp.where((odd == 1) & (lo_blk == 0), hi_blk + 1, hi_blk)
            lo_even.append(lo_h)
            n_lo.append(kd0 - lo_h)
            totals.append(kd0 - lo_h + hi_h - kd0 - per)
        pairs_h0 = totals[0] // 2
        n_pairs = pairs_h0 + totals[1] // 2

        def locate(x, half):
            head = jnp.where(x >= pairs_h0, 1, 0)
            i = 2 * (x - head * pairs_h0) + half
            before = jnp.where(head == 1, n_lo[1], n_lo[0])
            first = jnp.where(head == 1, lo_even[1], lo_even[0])
            blk = jnp.where(i < before, first + i, kd0 + per + i - before)
            return head, blk, jnp.where(i < before, 0, 1)

        def stage_a(x, p_ref):
            for half in range(2):
                head, blk, side = locate(x, half)
                lhs = keys_with_features(pl.multiple_of(blk * tk, tk))
                for c in range(2):
                    t = jnp.dot(lhs, queries_with_features(side, 2 * head + c), preferred_element_type=F32)
                    p_ref[2 * half + c] = jnp.exp2(t).astype(BF16)

        def stage_b(x, p_ref):
            head, blk_a, _ = locate(x, 0)
            _, blk_b, _ = locate(x, 1)
            rows = pl.ds(pl.multiple_of(head * DIFF_V_DIM, DIFF_V_DIM), DIFF_V_DIM)
            ones = jnp.ones((SUBLANES, tk), BF16)
            va = jnp.concatenate([vt_ref[0, rows, pl.ds(pl.multiple_of(blk_a * tk, tk), tk)], ones], axis=0)
            vb = jnp.concatenate([vt_ref[0, rows, pl.ds(pl.multiple_of(blk_b * tk, tk), tk)], ones], axis=0)
            for c in range(2):
                acc_ref[2 * head + c] += (jnp.dot(va, p_ref[c], preferred_element_type=F32)
                                          + jnp.dot(vb, p_ref[2 + c], preferred_element_type=F32))

        def diag_a(h, p_ref):
            for half in range(2):
                lhs = keys_with_features(pl.multiple_of(q0 + half * tk, tk))
                bias = dist_ref[half * tk:(half + 1) * tk, :] * slopes_ref[2 * pair + h]
                for c in range(2):
                    t = jnp.dot(lhs, queries_with_features(2, 2 * h + c), preferred_element_type=F32) - bias
                    p_ref[2 * half + c] = jnp.exp2(t).astype(BF16)

        def diag_b(h, p_ref):
            va = v_rows_with_ones(h, pl.multiple_of(q0, tk), tk)
            vb = v_rows_with_ones(h, pl.multiple_of(q0 + tk, tk), tk)
            for c in range(2):
                acc_ref[2 * h + c] = (jnp.dot(va, p_ref[c], preferred_element_type=F32)
                                      + jnp.dot(vb, p_ref[2 + c], preferred_element_type=F32))

        diag_a(0, pa_ref)
        diag_a(1, pb_ref)
        diag_b(0, pa_ref)

        @pl.when(n_pairs == 0)
        def _diagonal_only():
            diag_b(1, pb_ref)

        @pl.when(n_pairs > 0)
        def _blocks():
            stage_a(0, pa_ref)
            diag_b(1, pb_ref)

            def pairs_body(count):
                def body(j, carry):
                    for s in range(0, count, 2):
                        stage_a(count * j + s + 1, pb_ref)
                        stage_b(count * j + s, pa_ref)
                        stage_a(count * j + s + 2, pa_ref)
                        stage_b(count * j + s + 1, pb_ref)
                    return carry
                return body

            long_iters = (n_pairs - 1) // LONG_BODY_PAIRS
            lax.fori_loop(0, long_iters, pairs_body(LONG_BODY_PAIRS), 0)
            lax.fori_loop(LONG_BODY_PAIRS // 2 * long_iters, (n_pairs - 1) // 2, pairs_body(2), 0)

            @pl.when(n_pairs % 2 == 1)
            def _last_one():
                stage_b(n_pairs - 1, pa_ref)

            @pl.when(n_pairs % 2 == 0)
            def _last_two():
                stage_a(n_pairs - 1, pb_ref)
                stage_b(n_pairs - 2, pa_ref)
                stage_b(n_pairs - 1, pb_ref)

    @pl.when(jnp.logical_not(single_pass))
    def _online():
        m_ref[...] = jnp.full(m_ref.shape, NEG_BIG, F32)
        acc_ref[...] = jnp.zeros(acc_ref.shape, F32)
        rel = (lax.broadcasted_iota(jnp.int32, (tk, tq), 1)
               - lax.broadcasted_iota(jnp.int32, (tk, tq), 0)).astype(F32)
        for h in range(2):
            def online_step(kb, carry, h=h):
                k0 = pl.multiple_of(kb * tk, tk)
                kblk = k_ref[0, pl.ds(k0, tk), :]
                v_ones = v_rows_with_ones(h, k0, tk)
                bias = jnp.abs(rel + (q0 - k0).astype(F32)) * slopes_ref[2 * pair + h]
                for c in range(2):
                    hc = 2 * h + c
                    s = jnp.dot(kblk, q_masked[hc], preferred_element_type=F32) - bias
                    m_old = m_ref[hc]
                    m_new = jnp.maximum(m_old, jnp.max(s, axis=0, keepdims=True))
                    p = jnp.exp2(s - m_new).astype(BF16)
                    acc_ref[hc] = (jnp.exp2(m_old - m_new) * acc_ref[hc]
                                   + jnp.dot(v_ones, p, preferred_element_type=F32))
                    m_ref[hc] = m_new
                return carry

            lax.fori_loop(0, nk, online_step, 0)

    lam = lam_ref[...]
    lam_full = (jnp.exp(jnp.sum(lam[0:1] * lam[1:2], axis=-1, keepdims=True))
                - jnp.exp(jnp.sum(lam[2:3] * lam[3:4], axis=-1, keepdims=True)) + lam_init)
    outs = []
    for h in range(2):
        a1 = acc_ref[2 * h]
        a2 = acc_ref[2 * h + 1]
        rows = slice(0, DIFF_V_DIM)
        ones_row = slice(DIFF_V_DIM, DIFF_V_DIM + 1)
        o = a1[rows] / a1[ones_row] - lam_full * (a2[rows] / a2[ones_row])
        o = o * lax.rsqrt(jnp.mean(o * o, axis=0, keepdims=True) + EPS)
        outs.append(o * sw_ref[...] * (1.0 - lam_init))
    o_ref[0] = jnp.concatenate(outs, axis=0).T.astype(o_ref.dtype)


def _diff_attention(slopes, pieces, lam, subln_w, kn, dqvt, dk, aux, layer_idx, tq, tk):
    B, S, _ = dk.shape
    assert tq == 2 * tk and tk <= AUX_BLOCK and S // AUX_BLOCK <= AUX_BLOCK
    lam_init = 0.8 - 0.6 * math.exp(-0.3 * layer_idx)
    kern = functools.partial(_attn_kernel, seq=S, tq=tq, tk=tk, lam_init=lam_init)
    v_block0 = DIFF_WIDTH // LANES
    smem = pl.BlockSpec(memory_space=pltpu.SMEM)
    return pl.pallas_call(
        kern,
        grid=(B, HEAD_PAIRS, S // tq),
        in_specs=[
            smem, smem,
            pl.BlockSpec((4, DIFF_QK_DIM), lambda b, j, i: (0, 0)),
            pl.BlockSpec((DIFF_V_DIM, 1), lambda b, j, i: (0, 0)),
            pl.BlockSpec((1,) + kn.shape[1:], lambda b, j, i: (b, 0, 0, 0)),
            pl.BlockSpec((1, LANES, tq), lambda b, j, i: (b, j, i)),
            pl.BlockSpec((1, S, LANES), lambda b, j, i: (b, 0, j)),
            pl.BlockSpec((S, LANES), lambda b, j, i: (0, 0)),
            pl.BlockSpec((1, LANES, S), lambda b, j, i: (b, v_block0 + j, 0)),
        ],
        out_specs=pl.BlockSpec((1, tq, LANES), lambda b, j, i: (b, i, j)),
        out_shape=jax.ShapeDtypeStruct((B, S, DIFF_WIDTH), BF16),
        scratch_shapes=[pltpu.VMEM((4, 1, tq), F32), pltpu.VMEM((4, V_ROWS, tq), F32),
                        pltpu.VMEM((4, LANES, tq), BF16), pltpu.VMEM((12, AUX_ROWS, tq), BF16),
                        pltpu.VMEM((4, tk, tq), BF16), pltpu.VMEM((4, tk, tq), BF16),
                        pltpu.VMEM((tq, tq), F32)],
        compiler_params=_params("parallel", "parallel", "arbitrary"),
        name="diff_attn",
    )(slopes, pieces, lam, subln_w, kn, dqvt, dk, aux, dqvt)


def _out_proj_kernel(x_ref, ret_ref, diff_ref, cv_ref, prev_ref, next_ref, cw_ref, w_ref, o_ref, *, tm):
    i = pl.program_id(1)
    last = pl.num_programs(1) - 1
    cc = CONV_CHANNELS
    cv = cv_ref[0]
    u = cv[:, 2 * cc:3 * cc] * cv[:, 0:cc]
    prev = prev_ref[0]
    nxt = next_ref[0]
    u_prev = prev[SUBLANES - 1:SUBLANES, 2 * cc:3 * cc] * prev[SUBLANES - 1:SUBLANES, 0:cc]
    u_next = nxt[0:1, 2 * cc:3 * cc] * nxt[0:1, 0:cc]
    u_prev = jnp.where(i == 0, 0.0, u_prev)
    u_next = jnp.where(i == last, 0.0, u_next)
    rows = lax.broadcasted_iota(jnp.int32, (tm, cc), 0)
    u_m1 = jnp.where(rows == 0, u_prev, pltpu.roll(u, 1, 0))
    u_p1 = jnp.where(rows == tm - 1, u_next, pltpu.roll(u, tm - 1, 0))
    y = cw_ref[0:1] * u_m1 + cw_ref[1:2] * u + cw_ref[2:3] * u_p1
    conv = (cv[:, cc:2 * cc] * y).astype(BF16)
    acc = jnp.dot(ret_ref[0], w_ref[0:RET_WIDTH], preferred_element_type=F32)
    acc += jnp.dot(diff_ref[0], w_ref[RET_WIDTH:RET_WIDTH + DIFF_WIDTH], preferred_element_type=F32)
    acc += jnp.dot(conv, w_ref[RET_WIDTH + DIFF_WIDTH:], preferred_element_type=F32)
    o_ref[0] = x_ref[0] + acc


def _out_proj(x, ret, diff, cv, conv_w, w_out, tm):
    B, S, D = x.shape
    per = tm // SUBLANES
    nblk8 = S // SUBLANES
    tile = lambda width: pl.BlockSpec((1, tm, width), lambda b, i: (b, i, 0))
    return pl.pallas_call(
        functools.partial(_out_proj_kernel, tm=tm),
        grid=(B, S // tm),
        in_specs=[
            tile(D), tile(RET_WIDTH), tile(DIFF_WIDTH), tile(3 * CONV_CHANNELS),
            pl.BlockSpec((1, SUBLANES, 3 * CONV_CHANNELS), lambda b, i: (b, jnp.maximum(i * per - 1, 0), 0)),
            pl.BlockSpec((1, SUBLANES, 3 * CONV_CHANNELS),
                         lambda b, i: (b, jnp.minimum((i + 1) * per, nblk8 - 1), 0)),
            pl.BlockSpec((CONV_WIDTH, CONV_CHANNELS), lambda b, i: (0, 0)),
            pl.BlockSpec((D, D), lambda b, i: (0, 0)),
        ],
        out_specs=tile(D),
        out_shape=jax.ShapeDtypeStruct((B, S, D), F32),
        compiler_params=_params("parallel", "parallel"),
        name="out_proj",
    )(x, ret, diff, cv, cv, cv, conv_w, w_out)


_GROUP_LANE0 = N_EXPERTS
MOE_ROW_BLOCK = 256
MOE_EXPERTS_PER_STEP = 2
assert EXPERTS_PER_GROUP % MOE_EXPERTS_PER_STEP == 0


def _moe_sorted_rows(tm):
    return (tm + N_GROUPS * (MOE_ROW_BLOCK - 1)) // MOE_ROW_BLOCK * MOE_ROW_BLOCK


def _router_gates(logits):
    lane = lax.broadcasted_iota(jnp.int32, logits.shape, 1)
    big = jnp.int32(LANES)
    is_group = (lane >= _GROUP_LANE0) & (lane < _GROUP_LANE0 + N_GROUPS)
    gl = jnp.where(is_group, logits, -jnp.inf)
    g_max = jnp.max(gl, axis=-1, keepdims=True)
    g_idx = jnp.min(jnp.where(gl == g_max, lane - _GROUP_LANE0, big), axis=-1, keepdims=True)
    g_w = 1.0 / jnp.sum(jnp.where(is_group, jnp.exp(gl - g_max), 0.0), axis=-1, keepdims=True)
    in_group = (lane >= g_idx * EXPERTS_PER_GROUP) & (lane < (g_idx + 1) * EXPERTS_PER_GROUP)
    el = jnp.where(in_group, logits, -jnp.inf)
    e1 = jnp.max(el, axis=-1, keepdims=True)
    i1 = jnp.min(jnp.where(el == e1, lane, big), axis=-1, keepdims=True)
    el2 = jnp.where(lane == i1, -jnp.inf, el)
    e2 = jnp.max(el2, axis=-1, keepdims=True)
    i2 = jnp.min(jnp.where(el2 == e2, lane, big), axis=-1, keepdims=True)
    r = jnp.exp(e2 - e1)
    p1 = g_w / (1.0 + r)
    p2 = g_w * r / (1.0 + r)
    return jnp.where(lane == i1, p1, jnp.where(lane == i2, p2, 0.0)), g_idx


def _moe_kernel(x_ref, nw_ref, rw_ref, rb_ref, wg_ref, wu_ref, wd_ref, fw_ref, o_ref,
                xs_ref, gs_ref, ys_ref, dest_ref, ltri_ref, seg_ref, *, final_norm, tm):
    tile = pl.program_id(0)
    e = pl.program_id(1)
    sorted_rows = _moe_sorted_rows(tm)

    @pl.when((tile == 0) & (e == 0))
    def _strictly_lower():
        r = lax.broadcasted_iota(jnp.int32, (tm, tm), 0)
        c = lax.broadcasted_iota(jnp.int32, (tm, tm), 1)
        ltri_ref[...] = jnp.where(c < r, 1.0, 0.0).astype(BF16)

    @pl.when(e == 0)
    def _route_and_sort():
        x = x_ref[...]
        xn = x * lax.rsqrt(jnp.mean(x * x, axis=-1, keepdims=True) + EPS) * nw_ref[...]
        hi, lo = _split_hi_lo(xn)
        r_hi, r_lo = _split_hi_lo(rw_ref[...])
        logits = (jnp.dot(hi, r_hi, preferred_element_type=F32)
                  + jnp.dot(lo, r_hi, preferred_element_type=F32)
                  + jnp.dot(hi, r_lo, preferred_element_type=F32)) + rb_ref[...]
        gates, g_idx = _router_gates(logits)
        lane = lax.broadcasted_iota(jnp.int32, (tm, LANES), 1)
        member = jnp.where(lane == g_idx, 1.0, 0.0)
        earlier = jnp.dot(ltri_ref[...], member.astype(BF16), preferred_element_type=F32)
        rank = jnp.sum(member * earlier, axis=-1, keepdims=True)
        count = jnp.sum(member, axis=0, keepdims=True)
        seg = jnp.floor((count + (MOE_ROW_BLOCK - 1.0)) * (1.0 / MOE_ROW_BLOCK)) * MOE_ROW_BLOCK
        lane1 = lax.broadcasted_iota(jnp.int32, (1, LANES), 1)
        s0, s1, s2 = seg[:, 0:1], seg[:, 1:2], seg[:, 2:3]
        start = jnp.where(lane1 == 1, s0, jnp.where(lane1 == 2, s0 + s1, jnp.where(lane1 == 3, s0 + s1 + s2, 0.0)))
        dest = rank + jnp.sum(member * start, axis=-1, keepdims=True)
        dest_ref[...] = jnp.broadcast_to(dest, (tm, LANES))
        dest_row = dest_ref[...].T[0:1, :]
        row_id = lax.broadcasted_iota(jnp.int32, (sorted_rows, tm), 0).astype(F32)
        perm = jnp.where(row_id == dest_row, 1.0, 0.0).astype(BF16)
        g_hi, g_mid, g_lo = _split3(gates)
        g_pieces = g_hi + pltpu.roll(g_mid, N_EXPERTS, 1) + pltpu.roll(g_lo, 2 * N_EXPERTS, 1)
        moved = jnp.dot(perm, jnp.concatenate([hi, g_pieces.astype(BF16)], axis=1), preferred_element_type=F32)
        xs_ref[...] = moved[:, :D_MODEL].astype(BF16)
        g_moved = moved[:, D_MODEL:]
        gs_ref[...] = (g_moved + pltpu.roll(g_moved, LANES - N_EXPERTS, 1)
                       + pltpu.roll(g_moved, LANES - 2 * N_EXPERTS, 1))
        ys_ref[...] = jnp.zeros(ys_ref.shape, F32)
        start_i = start.astype(jnp.int32)
        blocks_i = (seg * (1.0 / MOE_ROW_BLOCK)).astype(jnp.int32)
        for g in range(N_GROUPS):
            seg_ref[g] = start_i[0, g]
            seg_ref[N_GROUPS + g] = blocks_i[0, g]

    group = (e * MOE_EXPERTS_PER_STEP) // EXPERTS_PER_GROUP
    first_row = seg_ref[group]

    def row_block(b, carry):
        r0 = pl.multiple_of(first_row + b * MOE_ROW_BLOCK, MOE_ROW_BLOCK)
        xb = xs_ref[pl.ds(r0, MOE_ROW_BLOCK), :]
        gsb = gs_ref[pl.ds(r0, MOE_ROW_BLOCK), :]
        lane = lax.broadcasted_iota(jnp.int32, gsb.shape, 1)
        experts = range(MOE_EXPERTS_PER_STEP)
        gate = [jnp.dot(xb, wg_ref[j], preferred_element_type=F32) for j in experts]
        up = [jnp.dot(xb, wu_ref[j], preferred_element_type=F32) for j in experts]
        hmid = [(gate[j] * jax.nn.sigmoid(gate[j]) * up[j]).astype(BF16) for j in experts]
        down = [jnp.dot(hmid[j], wd_ref[j], preferred_element_type=F32) for j in experts]
        total = None
        for j in experts:
            g_e = jnp.sum(jnp.where(lane == e * MOE_EXPERTS_PER_STEP + j, gsb, 0.0), axis=-1, keepdims=True)
            total = g_e * down[j] if total is None else total + g_e * down[j]
        ys_ref[pl.ds(r0, MOE_ROW_BLOCK), :] += total
        return carry

    lax.fori_loop(0, seg_ref[N_GROUPS + group], row_block, 0)

    @pl.when(e == pl.num_programs(1) - 1)
    def _finish():
        col_id = lax.broadcasted_iota(jnp.int32, (tm, sorted_rows), 1).astype(F32)
        unperm = jnp.where(col_id == dest_ref[:, 0:1], 1.0, 0.0).astype(BF16)
        y = x_ref[...] + jnp.dot(unperm, ys_ref[...].astype(BF16), preferred_element_type=F32)
        if final_norm:
            y = y * lax.rsqrt(jnp.mean(y * y, axis=-1, keepdims=True) + EPS) * fw_ref[...]
        o_ref[...] = y


def _moe(x, nw, r_w, r_b, wg, wu, wd, final_w, final_norm, tm):
    T, D = x.shape
    const = lambda shape: pl.BlockSpec(shape, lambda i, e: (0,) * len(shape))
    sorted_rows = _moe_sorted_rows(tm)
    return pl.pallas_call(
        functools.partial(_moe_kernel, final_norm=final_norm, tm=tm),
        grid=(T // tm, N_EXPERTS // MOE_EXPERTS_PER_STEP),
        in_specs=[
            pl.BlockSpec((tm, D), lambda i, e: (i, 0)),
            const((1, D)), const((D, LANES)), const((1, LANES)),
            pl.BlockSpec((MOE_EXPERTS_PER_STEP, D, EXPERT_FF), lambda i, e: (e, 0, 0)),
            pl.BlockSpec((MOE_EXPERTS_PER_STEP, D, EXPERT_FF), lambda i, e: (e, 0, 0)),
            pl.BlockSpec((MOE_EXPERTS_PER_STEP, EXPERT_FF, D), lambda i, e: (e, 0, 0)),
            const((1, D)),
        ],
        out_specs=pl.BlockSpec((tm, D), lambda i, e: (i, 0)),
        out_shape=jax.ShapeDtypeStruct((T, D), F32),
        scratch_shapes=[pltpu.VMEM((sorted_rows, D), BF16), pltpu.VMEM((sorted_rows, LANES), F32),
                        pltpu.VMEM((sorted_rows, D), F32), pltpu.VMEM((tm, LANES), F32),
                        pltpu.VMEM((tm, tm), BF16), pltpu.SMEM((2 * N_GROUPS,), jnp.int32)],
        compiler_params=_params("arbitrary", "arbitrary"),
        name="moe",
    )(x, nw, r_w, r_b, wg, wu, wd, final_w)


def _tile(n, pref):
    t = min(n, pref)
    assert n % t == 0, (n, t)
    return t


def _prep_layer(l, norm1_w, w_in, ret_decay_logit, ret_gn_w, diff_lambda, diff_subln_w, conv_w, w_out, norm2_w,
                router_group_w, router_group_b, router_expert_w, router_expert_b, expert_w_gate, expert_w_up,
                expert_w_down):
    w = w_in[l]
    w_nat = jnp.concatenate([w[:, :_OFF_DQ], w[:, _OFF_DK:_OFF_DV], w[:, _OFF_CONV:]], axis=1).astype(BF16)
    w_t = jnp.concatenate([w[:, _OFF_DQ:_OFF_DK], w[:, _OFF_DV:_OFF_CONV]], axis=1).T.astype(BF16)
    router = jnp.zeros((D_MODEL, LANES), F32)
    router = router.at[:, :N_EXPERTS].set(router_expert_w[l]).at[:, N_EXPERTS:N_EXPERTS + N_GROUPS].set(
        router_group_w[l])
    r_b = jnp.zeros((1, LANES), F32).at[0, :N_EXPERTS].set(router_expert_b[l]).at[
        0, N_EXPERTS:N_EXPERTS + N_GROUPS].set(router_group_b[l])
    return dict(
        norm1=norm1_w[l][None, :], w_nat=w_nat, w_t=w_t,
        lg=jax.nn.log_sigmoid(ret_decay_logit[l].astype(F32)),
        gn_w=ret_gn_w[l][None, :].astype(F32),
        lam=diff_lambda[l].astype(F32), subln=diff_subln_w[l][:, None].astype(F32),
        conv_w=conv_w[l].astype(F32), w_out=w_out[l].astype(BF16),
        norm2=norm2_w[l][None, :], r_w=router, r_b=r_b,
        wg=expert_w_gate[l].astype(BF16), wu=expert_w_up[l].astype(BF16), wd=expert_w_down[l].astype(BF16),
    )


def _trunk(x, layers, final_w, slopes, pieces):
    B, S, D = x.shape
    tm = _tile(S, 512)
    tq = _tile(S, 512)
    tk = tq // 2
    t_moe = _tile(B * S, 1024)
    aux = _position_features(S)
    for l, lw in enumerate(layers):
        rqkv, rg, dk, dqvt, cv, kn = _in_proj(x, lw["norm1"], lw["w_nat"], lw["w_t"], tm)
        ret = _retention(lw["lg"], rqkv, rg, lw["gn_w"])
        diff = _diff_attention(slopes, pieces, lw["lam"], lw["subln"], kn, dqvt, dk, aux, l, tq, tk)
        x = _out_proj(x, ret, diff, cv, lw["conv_w"], lw["w_out"], tm)
        last = l == len(layers) - 1
        x = _moe(x.reshape(B * S, D), lw["norm2"], lw["r_w"], lw["r_b"], lw["wg"], lw["wu"],
                 lw["wd"], final_w, last, t_moe).reshape(B, S, D)
    return x


def kernel(x_prompt, x_sample, norm1_w, w_in, ret_decay_logit, ret_gn_w, diff_lambda, diff_subln_w, conv_w, w_out,
           norm2_w, router_group_w, router_group_b, router_expert_w, router_expert_b, expert_w_gate, expert_w_up,
           expert_w_down, final_norm_w):
    depth = w_in.shape[0]
    layers = [
        _prep_layer(l, norm1_w, w_in, ret_decay_logit, ret_gn_w, diff_lambda, diff_subln_w, conv_w, w_out, norm2_w,
                    router_group_w, router_group_b, router_expert_w, router_expert_b, expert_w_gate, expert_w_up,
                    expert_w_down)
        for l in range(depth)
    ]
    final_w = final_norm_w[None, :].astype(F32)
    slopes = (np.float32(2.0) ** (np.float32(-8.0) * np.arange(1, DIFF_HEADS + 1, dtype=np.float32)
                                  / np.float32(DIFF_HEADS))) * np.float32(LOG2E)
    s_hi = slopes.astype(BF16).astype(np.float32)
    s_mid = (slopes - s_hi).astype(BF16).astype(np.float32)
    s_lo = (slopes - s_hi - s_mid).astype(BF16).astype(np.float32)
    pieces = jnp.asarray(np.stack([s_hi, s_mid, s_lo], axis=1).reshape(-1))
    slopes = jnp.asarray(slopes)
    return (_trunk(x_prompt, layers, final_w, slopes, pieces), _trunk(x_sample, layers, final_w, slopes, pieces))
```

```python
import functools
import math

import jax
import jax.numpy as jnp
import numpy as np
from jax import lax
from jax.experimental import pallas as pl
from jax.experimental.pallas import tpu as pltpu

F32 = jnp.float32
BF16 = jnp.bfloat16

D_MODEL = 1024
RET_HEADS = 6
RET_HEAD_DIM = 64
RET_WIDTH = RET_HEADS * RET_HEAD_DIM
DIFF_HEADS = 6
DIFF_QK_DIM = 32
DIFF_V_DIM = 64
DIFF_WIDTH = DIFF_HEADS * DIFF_V_DIM
CONV_CHANNELS = 256
CONV_WIDTH = 3
N_GROUPS = 4
EXPERTS_PER_GROUP = 4
N_EXPERTS = 16
EXPERT_FF = 512
EPS = 1e-6

LANES = 128
HEAD_PAIRS = RET_HEADS // 2
RET_CHUNK = 128
SUBLANES = 8
VMEM_LIMIT = 56 * 1024 * 1024
LOG2E = math.log2(math.e)
NEG_BIG = -1e30

_OFF_RG = 3 * RET_WIDTH
_OFF_DQ = 4 * RET_WIDTH
_OFF_DK = _OFF_DQ + DIFF_WIDTH
_OFF_DV = _OFF_DK + DIFF_WIDTH
_OFF_CONV = _OFF_DV + DIFF_WIDTH


def _params(*sem):
    return pltpu.CompilerParams(dimension_semantics=sem, vmem_limit_bytes=VMEM_LIMIT)


def _split_hi_lo(x):
    hi = x.astype(BF16)
    lo = (x - hi.astype(F32)).astype(BF16)
    return hi, lo


def _in_proj_kernel(x_ref, nw_ref, w_ref, wt_ref, rqkv_ref, rg_ref, dk_ref, dqvt_ref, cv_ref, kn_ref):
    x = x_ref[0]
    ms = jnp.mean(x * x, axis=-1, keepdims=True)
    h = (x * lax.rsqrt(ms + EPS) * nw_ref[...]).astype(BF16)
    a = jnp.dot(h, w_ref[:, 0:_OFF_DQ], preferred_element_type=F32)
    rqkv_ref[0, :, 0:RET_WIDTH] = a[:, 0:RET_WIDTH].astype(BF16)
    rqkv_ref[0, :, RET_WIDTH:2 * RET_WIDTH] = (a[:, RET_WIDTH:2 * RET_WIDTH] * RET_HEAD_DIM ** -0.5).astype(BF16)
    rqkv_ref[0, :, 2 * RET_WIDTH:3 * RET_WIDTH] = a[:, 2 * RET_WIDTH:3 * RET_WIDTH].astype(BF16)
    rg_ref[0] = a[:, _OFF_RG:_OFF_DQ]
    dk = jnp.dot(h, w_ref[:, _OFF_DQ:_OFF_DQ + DIFF_WIDTH], preferred_element_type=F32).astype(BF16)
    dk_ref[0] = dk
    dkf = dk.astype(F32)
    sq_hi, sq_lo = _split_hi_lo(dkf * dkf)
    grp = (lax.shift_right_logical(lax.broadcasted_iota(jnp.int32, (DIFF_WIDTH, LANES), 0),
                                   DIFF_QK_DIM.bit_length() - 1)
           == lax.broadcasted_iota(jnp.int32, (DIFF_WIDTH, LANES), 1))
    grp = jnp.where(grp, 1.0, 0.0).astype(BF16)
    kn2 = jnp.dot(sq_hi, grp, preferred_element_type=F32) + jnp.dot(sq_lo, grp, preferred_element_type=F32)
    kn_ref[0, 0] = jnp.broadcast_to(jnp.max(kn2, axis=0, keepdims=True), (SUBLANES, LANES))
    cv_ref[0] =jnp.dot(h, w_ref[:, _OFF_DQ + DIFF_WIDTH:], preferred_element_type=F32)
    t = lax.dot_general(wt_ref[...], h, (((1,), (1,)), ((), ())), preferred_element_type=F32)
    dqvt_ref[0, 0:DIFF_WIDTH, :] = (t[0:DIFF_WIDTH] * (DIFF_QK_DIM ** -0.5 * LOG2E)).astype(BF16)
    dqvt_ref[0, DIFF_WIDTH:, :] = t[DIFF_WIDTH:].astype(BF16)


def _in_proj(x, nw, w_nat, w_t, tm):
    B, S, D = x.shape
    n_nat = w_nat.shape[1]
    return pl.pallas_call(
        _in_proj_kernel,
        grid=(B, S // tm),
        in_specs=[
            pl.BlockSpec((1, tm, D), lambda b, i: (b, i, 0)),
            pl.BlockSpec((1, D), lambda b, i: (0, 0)),
            pl.BlockSpec((D, n_nat), lambda b, i: (0, 0)),
            pl.BlockSpec((2 * DIFF_WIDTH, D), lambda b, i: (0, 0)),
        ],
        out_specs=[
            pl.BlockSpec((1, tm, 3 * RET_WIDTH), lambda b, i: (b, i, 0)),
            pl.BlockSpec((1, tm, RET_WIDTH), lambda b, i: (b, i, 0)),
            pl.BlockSpec((1, tm, DIFF_WIDTH), lambda b, i: (b, i, 0)),
            pl.BlockSpec((1, 2 * DIFF_WIDTH, tm), lambda b, i: (b, 0, i)),
            pl.BlockSpec((1, tm, 3 * CONV_CHANNELS), lambda b, i: (b, i, 0)),
            pl.BlockSpec((1, 1, SUBLANES, LANES), lambda b, i: (b, i, 0, 0)),
        ],
        out_shape=[
            jax.ShapeDtypeStruct((B, S, 3 * RET_WIDTH), BF16),
            jax.ShapeDtypeStruct((B, S, RET_WIDTH), F32),
            jax.ShapeDtypeStruct((B, S, DIFF_WIDTH), BF16),
            jax.ShapeDtypeStruct((B, 2 * DIFF_WIDTH, S), BF16),
            jax.ShapeDtypeStruct((B, S, 3 * CONV_CHANNELS), F32),
            jax.ShapeDtypeStruct((B, S // tm, SUBLANES, LANES), F32),
        ],
        compiler_params=_params("parallel", "parallel"),
        name="in_proj",
    )(x, nw, w_nat, w_t)


def _pair_lane_value(lg_ref, direction, pair, shape, axis):
    idx = lax.broadcasted_iota(jnp.int32, shape, axis)
    return jnp.where(idx < RET_HEAD_DIM, lg_ref[direction, 2 * pair], lg_ref[direction, 2 * pair + 1])


def _same_head_mask():
    r = lax.broadcasted_iota(jnp.int32, (LANES, LANES), 0)
    c = lax.broadcasted_iota(jnp.int32, (LANES, LANES), 1)
    return (r < RET_HEAD_DIM) == (c < RET_HEAD_DIM)


def _ret_fwd_kernel(lg_ref, qkv_ref, y_ref, state_ref, dmat_ref, qdec_ref, kdec_ref, cdec_ref, *, chunks):
    C = RET_CHUNK

    @pl.when((pl.program_id(0) == 0) & (pl.program_id(1) == 0))
    def _build_tables():
        diff = (lax.broadcasted_iota(jnp.int32, (C, C), 0) - lax.broadcasted_iota(jnp.int32, (C, C), 1)).astype(F32)
        for h in range(RET_HEADS):
            lower = jnp.exp(jnp.maximum(diff, 0.0) * lg_ref[0, h])
            upper = jnp.exp(jnp.maximum(-diff, 0.0) * lg_ref[1, h])
            dmat_ref[h] = jnp.where(diff >= 0, lower, upper)
        pos = lax.broadcasted_iota(jnp.int32, (C, LANES), 0).astype(F32)
        for p in range(HEAD_PAIRS):
            lg_lane = _pair_lane_value(lg_ref, 0, p, (C, LANES), 1)
            qdec_ref[p] = jnp.exp((pos + 1.0) * lg_lane)
            kdec_ref[p] = jnp.exp((C - 1.0 - pos) * lg_lane)
            cdec_ref[p] = jnp.exp(float(C) * _pair_lane_value(lg_ref, 0, p, (LANES, LANES), 0))

    @pl.when(pl.program_id(1) == 0)
    def _reset():
        state_ref[...] = jnp.zeros(state_ref.shape, F32)

    lane = lax.broadcasted_iota(jnp.int32, (C, LANES), 1)
    same_head = _same_head_mask()
    def operands(p, ci):
        rows = slice(ci * C, (ci + 1) * C)
        return (qkv_ref[0, rows, p * LANES:(p + 1) * LANES],
                qkv_ref[0, rows, RET_WIDTH + p * LANES:RET_WIDTH + (p + 1) * LANES],
                qkv_ref[0, rows, 2 * RET_WIDTH + p * LANES:2 * RET_WIDTH + (p + 1) * LANES])

    items = [(p, ci) for p in range(HEAD_PAIRS) for ci in range(chunks)]
    scores, kvs = {}, {}
    for p, ci in items:
        q, k, v = operands(p, ci)
        for h in range(2):
            head_lanes = (lane >= h * RET_HEAD_DIM) & (lane < (h + 1) * RET_HEAD_DIM)
            qm = jnp.where(head_lanes, q, jnp.zeros_like(q))
            scores[p, ci, h] = lax.dot_general(qm, k, (((1,), (1,)), ((), ())), preferred_element_type=F32)
        kd = (k.astype(F32) * kdec_ref[p]).astype(BF16)
        kvs[p, ci] = lax.dot_general(kd, v, (((0,), (0,)), ((), ())), preferred_element_type=F32)
    states = {}
    for p in range(HEAD_PAIRS):
        state = state_ref[p]
        for ci in range(chunks):
            states[p, ci] = state.astype(BF16)
            state = cdec_ref[p] * state + jnp.where(same_head, kvs[p, ci], 0.0)
        state_ref[p] = state
    for p, ci in items:
        q, _, v = operands(p, ci)
        inner = [jnp.dot((scores[p, ci, h] * dmat_ref[2 * p + h]).astype(BF16), v, preferred_element_type=F32)
                 for h in range(2)]
        cross = jnp.dot(q, states[p, ci], preferred_element_type=F32) * qdec_ref[p]
        y_ref[0, ci * C:(ci + 1) * C, p * LANES:(p + 1) * LANES] = (
            jnp.where(lane < RET_HEAD_DIM, inner[0], inner[1]) + cross)


def _ret_bwd_kernel(lg_ref, qkv_ref, y1_ref, g_ref, gnw_ref, o_ref, state_ref, qdec_ref, kdec_ref, cdec_ref, *,
                    chunks):
    C = RET_CHUNK

    @pl.when((pl.program_id(0) == 0) & (pl.program_id(1) == 0))
    def _build_tables():
        pos = lax.broadcasted_iota(jnp.int32, (C, LANES), 0).astype(F32)
        for p in range(HEAD_PAIRS):
            lg_lane = _pair_lane_value(lg_ref, 1, p, (C, LANES), 1)
            qdec_ref[p] = jnp.exp((float(C) - pos) * lg_lane)
            kdec_ref[p] = jnp.exp(pos * lg_lane)
            cdec_ref[p] = jnp.exp(float(C) * _pair_lane_value(lg_ref, 1, p, (LANES, LANES), 0))

    @pl.when(pl.program_id(1) == 0)
    def _reset():
        state_ref[...] = jnp.zeros(state_ref.shape, F32)

    same_head = _same_head_mask()
    head_avg = jnp.where(same_head, 1.0 / RET_HEAD_DIM, 0.0).astype(BF16)

    def head_mean(t):
        hi, lo = _split_hi_lo(t)
        return (jnp.dot(hi, head_avg, preferred_element_type=F32)
                + jnp.dot(lo, head_avg, preferred_element_type=F32))

    items = [(p, ci) for p in range(HEAD_PAIRS) for ci in range(chunks)]
    kvs = {}
    for p, ci in items:
        rows = slice(ci * C, (ci + 1) * C)
        k = qkv_ref[0, rows, RET_WIDTH + p * LANES:RET_WIDTH + (p + 1) * LANES]
        v = qkv_ref[0, rows, 2 * RET_WIDTH + p * LANES:2 * RET_WIDTH + (p + 1) * LANES]
        kd = (k.astype(F32) * kdec_ref[p]).astype(BF16)
        kvs[p, ci] = lax.dot_general(kd, v, (((0,), (0,)), ((), ())), preferred_element_type=F32)
    states = {}
    for p in range(HEAD_PAIRS):
        state = state_ref[p]
        for ci in reversed(range(chunks)):
            states[p, ci] = state.astype(BF16)
            state = cdec_ref[p] * state + jnp.where(same_head, kvs[p, ci], 0.0)
        state_ref[p] = state
    ys = {}
    for p, ci in items:
        rows = slice(ci * C, (ci + 1) * C)
        sl = slice(p * LANES, (p + 1) * LANES)
        cross = jnp.dot(qkv_ref[0, rows, sl], states[p, ci], preferred_element_type=F32)
        ys[p, ci] = y1_ref[0, rows, sl] + cross * qdec_ref[p]
    means = {key: head_mean(y) for key, y in ys.items()}
    devs = {key: ys[key] - means[key] for key in ys}
    variances = {key: head_mean(d * d) for key, d in devs.items()}
    for p, ci in items:
        rows = slice(ci * C, (ci + 1) * C)
        sl = slice(p * LANES, (p + 1) * LANES)
        g = g_ref[0, rows, sl]
        o = devs[p, ci] * lax.rsqrt(variances[p, ci] + EPS) * gnw_ref[:, sl] * (g * jax.nn.sigmoid(g))
        o_ref[0, rows, sl] = o.astype(o_ref.dtype)


def _retention(lg, rqkv, rg, gn_w):
    B, S, _ = rqkv.shape
    chunks = math.gcd(S // RET_CHUNK, 8)
    C = chunks * RET_CHUNK
    n = S // C
    smem = pl.BlockSpec(memory_space=pltpu.SMEM)
    table = pltpu.VMEM((HEAD_PAIRS, RET_CHUNK, LANES), F32)
    state = pltpu.VMEM((HEAD_PAIRS, LANES, LANES), F32)
    y1 = pl.pallas_call(
        functools.partial(_ret_fwd_kernel, chunks=chunks),
        grid=(B, n),
        in_specs=[smem, pl.BlockSpec((1, C, 3 * RET_WIDTH), lambda b, i: (b, i, 0))],
        out_specs=pl.BlockSpec((1, C, RET_WIDTH), lambda b, i: (b, i, 0)),
        out_shape=jax.ShapeDtypeStruct((B, S, RET_WIDTH), F32),
        scratch_shapes=[state, pltpu.VMEM((RET_HEADS, RET_CHUNK, RET_CHUNK), F32), table, table, state],
        compiler_params=_params("arbitrary", "arbitrary"),
        name="ret_fwd",
    )(lg, rqkv)
    rev = lambda b, i: (b, n - 1 - i, 0)
    return pl.pallas_call(
        functools.partial(_ret_bwd_kernel, chunks=chunks),
        grid=(B, n),
        in_specs=[
            smem,
            pl.BlockSpec((1, C, 3 * RET_WIDTH), rev),
            pl.BlockSpec((1, C, RET_WIDTH), rev),
            pl.BlockSpec((1, C, RET_WIDTH), rev),
            pl.BlockSpec((1, RET_WIDTH), lambda b, i: (0, 0)),
        ],
        out_specs=pl.BlockSpec((1, C, RET_WIDTH), rev),
        out_shape=jax.ShapeDtypeStruct((B, S, RET_WIDTH), BF16),
        scratch_shapes=[state, table, table, state],
        compiler_params=_params("arbitrary", "arbitrary"),
        name="ret_bwd",
    )(lg, rqkv, y1, rg, gn_w)


AUX_BLOCK = 256
AUX_ROWS = 16
V_ROWS = DIFF_V_DIM + SUBLANES
GUARD_LOG2 = 60.0
BOUND_SLACK = 1.01
LONG_BODY_PAIRS = 8
UNDERFLOW_LOG2 = 150.0


def _position_features(seq):
    j = jnp.arange(seq, dtype=jnp.int32)
    ones = jnp.ones((seq,), F32)
    n = (j // AUX_BLOCK).astype(F32)
    jc = (j % AUX_BLOCK - AUX_BLOCK // 2).astype(F32)
    feat = jnp.stack([ones] * 3 + [n] * 3 + [jc] * 3, axis=1)
    return jnp.pad(feat, ((0, 0), (0, LANES - feat.shape[1]))).astype(BF16)


def _split3(x):
    hi = x.astype(BF16).astype(F32)
    r = x - hi
    mid = r.astype(BF16).astype(F32)
    return hi, mid, (r - mid).astype(BF16).astype(F32)


def _attn_kernel(slopes_ref, pieces_ref, lam_ref, sw_ref, kn_ref, qt_ref, k_ref, aux_ref, vt_ref, o_ref,
                 m_ref, acc_ref, qm_ref, aug_ref, pa_ref, pb_ref, dist_ref, *, seq, tq, tk, lam_init):
    pair = pl.program_id(1)
    qi = pl.program_id(2)
    q0 = qi * tq
    nk = seq // tk
    per = tq // tk
    kd0 = qi * per
    qt = qt_ref[0]
    row = lax.broadcasted_iota(jnp.int32, (LANES, tq), 0)
    q_masked = []
    for hc in range(4):
        lo = hc * DIFF_QK_DIM
        q_masked.append(jnp.where((row >= lo) & (row < lo + DIFF_QK_DIM), qt, jnp.zeros_like(qt)))

    @pl.when(qi == 0)
    def _distance_table():
        d = lax.broadcasted_iota(jnp.int32, (tq, tq), 1) - lax.broadcasted_iota(jnp.int32, (tq, tq), 0)
        dist_ref[...] = jnp.abs(d).astype(F32)

    kd = pl.multiple_of(q0, tq)
    kdiag = k_ref[0, pl.ds(kd, tq), :]
    self_prod = qt.astype(F32) * kdiag.astype(F32).T
    shifts = [jnp.sum(self_prod[hc * DIFF_QK_DIM:(hc + 1) * DIFF_QK_DIM], axis=0, keepdims=True) for hc in range(4)]

    kn2 = jnp.max(jnp.max(kn_ref[0], axis=0), axis=0, keepdims=True)
    kn_lane = lax.broadcasted_iota(jnp.int32, (1, LANES), 1)
    excess = []
    for h in range(2):
        worst = None
        for c in range(2):
            hc = 2 * h + c
            qf = q_masked[hc].astype(F32)
            qn2 = jnp.sum(qf * qf, axis=0, keepdims=True)
            kn2_hc = jnp.max(jnp.where(kn_lane == 4 * pair + hc, kn2, 0.0), axis=-1, keepdims=True)
            bound = jnp.sqrt(qn2 * kn2_hc) * BOUND_SLACK + 1e-3
            e = jnp.max(bound - shifts[hc], axis=-1, keepdims=True)
            worst = e if worst is None else jnp.maximum(worst, e)
        excess.append(worst)
    single_pass = jnp.where(jnp.maximum(excess[0], excess[1]) <= GUARD_LOG2, 1, 0).astype(jnp.int32)[0, 0] == 1

    def v_rows_with_ones(h, k0, n):
        vt = vt_ref[0, h * DIFF_V_DIM:(h + 1) * DIFF_V_DIM, pl.ds(k0, n)]
        return jnp.concatenate([vt, jnp.ones((SUBLANES, n), BF16)], axis=0)

    def keys_with_features(k0):
        return jnp.concatenate([k_ref[0, pl.ds(k0, tk), :], aux_ref[pl.ds(k0, tk), :]], axis=1)

    def queries_with_features(side, hc):
        return jnp.concatenate([qm_ref[hc], aug_ref[side * 4 + hc],
                                jnp.zeros((LANES - AUX_ROWS, tq), BF16)], axis=0)

    @pl.when(single_pass)
    def _single_pass():
        ipos = (q0 + lax.broadcasted_iota(jnp.int32, (1, tq), 1)).astype(F32)
        aug_row = lax.broadcasted_iota(jnp.int32, (AUX_ROWS, tq), 0)
        q0f = q0.astype(F32)
        lo_even, n_lo, totals = [], [], []
        for h in range(2):
            slope = slopes_ref[2 * pair + h]
            pieces = [pieces_ref[(2 * pair + h) * 3 + i] for i in range(3)]
            for c in range(2):
                hc = 2 * h + c
                for side, sign in ((0, 1.0), (1, -1.0), (2, 0.0)):
                    const = (-sign) * (slope * ipos) - shifts[hc] + sign * (0.5 * AUX_BLOCK) * slope
                    c_hi, c_mid, c_lo = _split3(const)
                    aug = jnp.where(aug_row == 0, c_hi,
                                    jnp.where(aug_row == 1, c_mid, jnp.where(aug_row == 2, c_lo, 0.0)))
                    for i in range(3):
                        aug = jnp.where(aug_row == 3 + i, sign * AUX_BLOCK * pieces[i], aug)
                        aug = jnp.where(aug_row == 6 + i, sign * pieces[i], aug)
                    aug_ref[side * 4 + hc] = aug.astype(BF16)
                qm_ref[hc] = q_masked[hc]
            reach = (excess[h] + UNDERFLOW_LOG2) / slope
            lo_blk = jnp.clip(jnp.floor((q0f - (tk - 1.0) - reach) / tk), 0.0, float(nk))
            hi_blk = jnp.clip(jnp.floor((reach + q0f + (tq - 1.0)) / tk) + 1.0, 0.0, float(nk))
            lo_blk = jnp.minimum(lo_blk.astype(jnp.int32)[0, 0], kd0)
            hi_blk = jnp.maximum(hi_blk.astype(jnp.int32)[0, 0], kd0 + per)
            odd = (kd0 - lo_blk + hi_blk - kd0 - per) % 2
            lo_h = jnp.where((odd == 1) & (lo_blk > 0), lo_blk - 1, lo_blk)
            hi_h = jnp.where((odd == 1) & (lo_blk == 0), hi_blk + 1, hi_blk)
            lo_even.append(lo_h)
            n_lo.append(kd0 - lo_h)
            totals.append(kd0 - lo_h + hi_h - kd0 - per)
        pairs_h0 = totals[0] // 2
        n_pairs = pairs_h0 + totals[1] // 2

        def locate(x, half):
            head = jnp.where(x >= pairs_h0, 1, 0)
            i = 2 * (x - head * pairs_h0) + half
            before = jnp.where(head == 1, n_lo[1], n_lo[0])
            first = jnp.where(head == 1, lo_even[1], lo_even[0])
            blk = jnp.where(i < before, first + i, kd0 + per + i - before)
            return head, blk, jnp.where(i < before, 0, 1)

        def stage_a(x, p_ref):
            for half in range(2):
                head, blk, side = locate(x, half)
                lhs = keys_with_features(pl.multiple_of(blk * tk, tk))
                for c in range(2):
                    t = jnp.dot(lhs, queries_with_features(side, 2 * head + c), preferred_element_type=F32)
                    p_ref[2 * half + c] = jnp.exp2(t).astype(BF16)

        def stage_b(x, p_ref):
            head, blk_a, _ = locate(x, 0)
            _, blk_b, _ = locate(x, 1)
            rows = pl.ds(pl.multiple_of(head * DIFF_V_DIM, DIFF_V_DIM), DIFF_V_DIM)
            ones = jnp.ones((SUBLANES, tk), BF16)
            va = jnp.concatenate([vt_ref[0, rows, pl.ds(pl.multiple_of(blk_a * tk, tk), tk)], ones], axis=0)
            vb = jnp.concatenate([vt_ref[0, rows, pl.ds(pl.multiple_of(blk_b * tk, tk), tk)], ones], axis=0)
            for c in range(2):
                acc_ref[2 * head + c] += (jnp.dot(va, p_ref[c], preferred_element_type=F32)
                                          + jnp.dot(vb, p_ref[2 + c], preferred_element_type=F32))

        def diag_a(h, p_ref):
            for half in range(2):
                lhs = keys_with_features(pl.multiple_of(q0 + half * tk, tk))
                bias = dist_ref[half * tk:(half + 1) * tk, :] * slopes_ref[2 * pair + h]
                for c in range(2):
                    t = jnp.dot(lhs, queries_with_features(2, 2 * h + c), preferred_element_type=F32) - bias
                    p_ref[2 * half + c] = jnp.exp2(t).astype(BF16)

        def diag_b(h, p_ref):
            va = v_rows_with_ones(h, pl.multiple_of(q0, tk), tk)
            vb = v_rows_with_ones(h, pl.multiple_of(q0 + tk, tk), tk)
            for c in range(2):
                acc_ref[2 * h + c] = (jnp.dot(va, p_ref[c], preferred_element_type=F32)
                                      + jnp.dot(vb, p_ref[2 + c], preferred_element_type=F32))

        diag_a(0, pa_ref)
        diag_a(1, pb_ref)
        diag_b(0, pa_ref)

        @pl.when(n_pairs == 0)
        def _diagonal_only():
            diag_b(1, pb_ref)

        @pl.when(n_pairs > 0)
        def _blocks():
            stage_a(0, pa_ref)
            diag_b(1, pb_ref)

            def pairs_body(count):
                def body(j, carry):
                    for s in range(0, count, 2):
                        stage_a(count * j + s + 1, pb_ref)
                        stage_b(count * j + s, pa_ref)
                        stage_a(count * j + s + 2, pa_ref)
                        stage_b(count * j + s + 1, pb_ref)
                    return carry
                return body

            long_iters = (n_pairs - 1) // LONG_BODY_PAIRS
            lax.fori_loop(0, long_iters, pairs_body(LONG_BODY_PAIRS), 0)
            lax.fori_loop(LONG_BODY_PAIRS // 2 * long_iters, (n_pairs - 1) // 2, pairs_body(2), 0)

            @pl.when(n_pairs % 2 == 1)
            def _last_one():
                stage_b(n_pairs - 1, pa_ref)

            @pl.when(n_pairs % 2 == 0)
            def _last_two():
                stage_a(n_pairs - 1, pb_ref)
                stage_b(n_pairs - 2, pa_ref)
                stage_b(n_pairs - 1, pb_ref)

    @pl.when(jnp.logical_not(single_pass))
    def _online():
        m_ref[...] = jnp.full(m_ref.shape, NEG_BIG, F32)
        acc_ref[...] = jnp.zeros(acc_ref.shape, F32)
        rel = (lax.broadcasted_iota(jnp.int32, (tk, tq), 1)
               - lax.broadcasted_iota(jnp.int32, (tk, tq), 0)).astype(F32)
        for h in range(2):
            def online_step(kb, carry, h=h):
                k0 = pl.multiple_of(kb * tk, tk)
                kblk = k_ref[0, pl.ds(k0, tk), :]
                v_ones = v_rows_with_ones(h, k0, tk)
                bias = jnp.abs(rel + (q0 - k0).astype(F32)) * slopes_ref[2 * pair + h]
                for c in range(2):
                    hc = 2 * h + c
                    s = jnp.dot(kblk, q_masked[hc], preferred_element_type=F32) - bias
                    m_old = m_ref[hc]
                    m_new = jnp.maximum(m_old, jnp.max(s, axis=0, keepdims=True))
                    p = jnp.exp2(s - m_new).astype(BF16)
                    acc_ref[hc] = (jnp.exp2(m_old - m_new) * acc_ref[hc]
                                   + jnp.dot(v_ones, p, preferred_element_type=F32))
                    m_ref[hc] = m_new
                return carry

            lax.fori_loop(0, nk, online_step, 0)

    lam = lam_ref[...]
    lam_full = (jnp.exp(jnp.sum(lam[0:1] * lam[1:2], axis=-1, keepdims=True))
                - jnp.exp(jnp.sum(lam[2:3] * lam[3:4], axis=-1, keepdims=True)) + lam_init)
    outs = []
    for h in range(2):
        a1 = acc_ref[2 * h]
        a2 = acc_ref[2 * h + 1]
        rows = slice(0, DIFF_V_DIM)
        ones_row = slice(DIFF_V_DIM, DIFF_V_DIM + 1)
        o = a1[rows] / a1[ones_row] - lam_full * (a2[rows] / a2[ones_row])
        o = o * lax.rsqrt(jnp.mean(o * o, axis=0, keepdims=True) + EPS)
        outs.append(o * sw_ref[...] * (1.0 - lam_init))
    o_ref[0] = jnp.concatenate(outs, axis=0).T.astype(o_ref.dtype)


def _diff_attention(slopes, pieces, lam, subln_w, kn, dqvt, dk, aux, layer_idx, tq, tk):
    B, S, _ = dk.shape
    assert tq == 2 * tk and tk <= AUX_BLOCK and S // AUX_BLOCK <= AUX_BLOCK
    lam_init = 0.8 - 0.6 * math.exp(-0.3 * layer_idx)
    kern = functools.partial(_attn_kernel, seq=S, tq=tq, tk=tk, lam_init=lam_init)
    v_block0 = DIFF_WIDTH // LANES
    smem = pl.BlockSpec(memory_space=pltpu.SMEM)
    return pl.pallas_call(
        kern,
        grid=(B, HEAD_PAIRS, S // tq),
        in_specs=[
            smem, smem,
            pl.BlockSpec((4, DIFF_QK_DIM), lambda b, j, i: (0, 0)),
            pl.BlockSpec((DIFF_V_DIM, 1), lambda b, j, i: (0, 0)),
            pl.BlockSpec((1,) + kn.shape[1:], lambda b, j, i: (b, 0, 0, 0)),
            pl.BlockSpec((1, LANES, tq), lambda b, j, i: (b, j, i)),
            pl.BlockSpec((1, S, LANES), lambda b, j, i: (b, 0, j)),
            pl.BlockSpec((S, LANES), lambda b, j, i: (0, 0)),
            pl.BlockSpec((1, LANES, S), lambda b, j, i: (b, v_block0 + j, 0)),
        ],
        out_specs=pl.BlockSpec((1, tq, LANES), lambda b, j, i: (b, i, j)),
        out_shape=jax.ShapeDtypeStruct((B, S, DIFF_WIDTH), BF16),
        scratch_shapes=[pltpu.VMEM((4, 1, tq), F32), pltpu.VMEM((4, V_ROWS, tq), F32),
                        pltpu.VMEM((4, LANES, tq), BF16), pltpu.VMEM((12, AUX_ROWS, tq), BF16),
                        pltpu.VMEM((4, tk, tq), BF16), pltpu.VMEM((4, tk, tq), BF16),
                        pltpu.VMEM((tq, tq), F32)],
        compiler_params=_params("parallel", "parallel", "arbitrary"),
        name="diff_attn",
    )(slopes, pieces, lam, subln_w, kn, dqvt, dk, aux, dqvt)


def _out_proj_kernel(x_ref, ret_ref, diff_ref, cv_ref, prev_ref, next_ref, cw_ref, w_ref, o_ref, *, tm):
    i = pl.program_id(1)
    last = pl.num_programs(1) - 1
    cc = CONV_CHANNELS
    cv = cv_ref[0]
    u = cv[:, 2 * cc:3 * cc] * cv[:, 0:cc]
    prev = prev_ref[0]
    nxt = next_ref[0]
    u_prev = prev[SUBLANES - 1:SUBLANES, 2 * cc:3 * cc] * prev[SUBLANES - 1:SUBLANES, 0:cc]
    u_next = nxt[0:1, 2 * cc:3 * cc] * nxt[0:1, 0:cc]
    u_prev = jnp.where(i == 0, 0.0, u_prev)
    u_next = jnp.where(i == last, 0.0, u_next)
    rows = lax.broadcasted_iota(jnp.int32, (tm, cc), 0)
    u_m1 = jnp.where(rows == 0, u_prev, pltpu.roll(u, 1, 0))
    u_p1 = jnp.where(rows == tm - 1, u_next, pltpu.roll(u, tm - 1, 0))
    y = cw_ref[0:1] * u_m1 + cw_ref[1:2] * u + cw_ref[2:3] * u_p1
    conv = (cv[:, cc:2 * cc] * y).astype(BF16)
    acc = jnp.dot(ret_ref[0], w_ref[0:RET_WIDTH], preferred_element_type=F32)
    acc += jnp.dot(diff_ref[0], w_ref[RET_WIDTH:RET_WIDTH + DIFF_WIDTH], preferred_element_type=F32)
    acc += jnp.dot(conv, w_ref[RET_WIDTH + DIFF_WIDTH:], preferred_element_type=F32)
    o_ref[0] = x_ref[0] + acc


def _out_proj(x, ret, diff, cv, conv_w, w_out, tm):
    B, S, D = x.shape
    per = tm // SUBLANES
    nblk8 = S // SUBLANES
    tile = lambda width: pl.BlockSpec((1, tm, width), lambda b, i: (b, i, 0))
    return pl.pallas_call(
        functools.partial(_out_proj_kernel, tm=tm),
        grid=(B, S // tm),
        in_specs=[
            tile(D), tile(RET_WIDTH), tile(DIFF_WIDTH), tile(3 * CONV_CHANNELS),
            pl.BlockSpec((1, SUBLANES, 3 * CONV_CHANNELS), lambda b, i: (b, jnp.maximum(i * per - 1, 0), 0)),
            pl.BlockSpec((1, SUBLANES, 3 * CONV_CHANNELS),
                         lambda b, i: (b, jnp.minimum((i + 1) * per, nblk8 - 1), 0)),
            pl.BlockSpec((CONV_WIDTH, CONV_CHANNELS), lambda b, i: (0, 0)),
            pl.BlockSpec((D, D), lambda b, i: (0, 0)),
        ],
        out_specs=tile(D),
        out_shape=jax.ShapeDtypeStruct((B, S, D), F32),
        compiler_params=_params("parallel", "parallel"),
        name="out_proj",
    )(x, ret, diff, cv, cv, cv, conv_w, w_out)


_GROUP_LANE0 = N_EXPERTS
MOE_ROW_BLOCK = 256
MOE_EXPERTS_PER_STEP = 2
assert EXPERTS_PER_GROUP % MOE_EXPERTS_PER_STEP == 0


def _moe_sorted_rows(tm):
    return (tm + N_GROUPS * (MOE_ROW_BLOCK - 1)) // MOE_ROW_BLOCK * MOE_ROW_BLOCK


def _router_gates(logits):
    tm = logits.shape[0]
    lt = logits.T
    big = jnp.int32(LANES)
    grow = lax.broadcasted_iota(jnp.int32, (SUBLANES, tm), 0)
    gl = jnp.where(grow < N_GROUPS, lt[_GROUP_LANE0:_GROUP_LANE0 + SUBLANES], -jnp.inf)
    g_max = jnp.max(gl, axis=0, keepdims=True)
    g_idx = jnp.min(jnp.where(gl == g_max, grow, big), axis=0, keepdims=True)
    g_w = 1.0 / jnp.sum(jnp.exp(gl - g_max), axis=0, keepdims=True)
    erow = lax.broadcasted_iota(jnp.int32, (N_EXPERTS, tm), 0)
    in_group = lax.shift_right_logical(erow, EXPERTS_PER_GROUP.bit_length() - 1) == g_idx
    el = jnp.where(in_group, lt[0:N_EXPERTS], -jnp.inf)
    e1 = jnp.max(el, axis=0, keepdims=True)
    i1 = jnp.min(jnp.where(el == e1, erow, big), axis=0, keepdims=True)
    el2 = jnp.where(erow == i1, -jnp.inf, el)
    e2 = jnp.max(el2, axis=0, keepdims=True)
    i2 = jnp.min(jnp.where(el2 == e2, erow, big), axis=0, keepdims=True)
    r = jnp.exp(e2 - e1)
    p1 = g_w / (1.0 + r)
    p2 = g_w * r / (1.0 + r)
    gates_t = jnp.where(erow == i1, p1, jnp.where(erow == i2, p2, 0.0))
    member_t = jnp.where(grow == g_idx, 1.0, 0.0)
    rest = jnp.zeros((LANES - N_EXPERTS - SUBLANES, tm), F32)
    return jnp.concatenate([gates_t, member_t, rest], axis=0).T


def _moe_kernel(x_ref, nw_ref, rw_ref, rb_ref, wg_ref, wu_ref, wd_ref, fw_ref, o_ref,
                xs_ref, gs_ref, ys_ref, dest_ref, ltri_ref, seg_ref, *, final_norm, tm):
    tile = pl.program_id(0)
    e = pl.program_id(1)
    sorted_rows = _moe_sorted_rows(tm)

    @pl.when((tile == 0) & (e == 0))
    def _strictly_lower():
        r = lax.broadcasted_iota(jnp.int32, (tm, tm), 0)
        c = lax.broadcasted_iota(jnp.int32, (tm, tm), 1)
        ltri_ref[...] = jnp.where(c < r, 1.0, 0.0).astype(BF16)

    @pl.when(e == 0)
    def _route_and_sort():
        x = x_ref[...]
        xn = x * lax.rsqrt(jnp.mean(x * x, axis=-1, keepdims=True) + EPS) * nw_ref[...]
        hi, lo = _split_hi_lo(xn)
        r_hi, r_lo = _split_hi_lo(rw_ref[...])
        logits = (jnp.dot(hi, r_hi, preferred_element_type=F32)
                  + jnp.dot(lo, r_hi, preferred_element_type=F32)
                  + jnp.dot(hi, r_lo, preferred_element_type=F32)) + rb_ref[...]
        routed = _router_gates(logits)
        lane = lax.broadcasted_iota(jnp.int32, (tm, LANES), 1)
        gates = jnp.where(lane < N_EXPERTS, routed, 0.0)
        member = jnp.where(lane >= _GROUP_LANE0, routed, 0.0)
        earlier = jnp.dot(ltri_ref[...], member.astype(BF16), preferred_element_type=F32)
        rank = jnp.sum(member * earlier, axis=-1, keepdims=True)
        count = jnp.sum(member, axis=0, keepdims=True)
        seg = jnp.floor((count + (MOE_ROW_BLOCK - 1.0)) * (1.0 / MOE_ROW_BLOCK)) * MOE_ROW_BLOCK
        lane1 = lax.broadcasted_iota(jnp.int32, (1, LANES), 1)
        g0 = _GROUP_LANE0
        s0, s1, s2 = seg[:, g0:g0 + 1], seg[:, g0 + 1:g0 + 2], seg[:, g0 + 2:g0 + 3]
        start = jnp.where(lane1 == g0 + 1, s0,
                          jnp.where(lane1 == g0 + 2, s0 + s1, jnp.where(lane1 == g0 + 3, s0 + s1 + s2, 0.0)))
        dest = rank + jnp.sum(member * start, axis=-1, keepdims=True)
        dest_ref[...] = jnp.broadcast_to(dest, (tm, LANES))
        dest_row = dest_ref[...].T[0:1, :]
        row_id = lax.broadcasted_iota(jnp.int32, (sorted_rows, tm), 0).astype(F32)
        perm = jnp.where(row_id == dest_row, 1.0, 0.0).astype(BF16)
        g_hi, g_mid, g_lo = _split3(gates)
        g_pieces = g_hi + pltpu.roll(g_mid, N_EXPERTS, 1) + pltpu.roll(g_lo, 2 * N_EXPERTS, 1)
        moved = jnp.dot(perm, jnp.concatenate([hi, g_pieces.astype(BF16)], axis=1), preferred_element_type=F32)
        xs_ref[...] = moved[:, :D_MODEL].astype(BF16)
        g_moved = moved[:, D_MODEL:]
        gs_ref[...] = (g_moved + pltpu.roll(g_moved, LANES - N_EXPERTS, 1)
                       + pltpu.roll(g_moved, LANES - 2 * N_EXPERTS, 1))
        ys_ref[...] = jnp.zeros(ys_ref.shape, F32)
        start_i = start.astype(jnp.int32)
        blocks_i = (seg * (1.0 / MOE_ROW_BLOCK)).astype(jnp.int32)
        for g in range(N_GROUPS):
            seg_ref[g] = start_i[0, _GROUP_LANE0 + g]
            seg_ref[N_GROUPS + g] = blocks_i[0, _GROUP_LANE0 + g]

    group = (e * MOE_EXPERTS_PER_STEP) // EXPERTS_PER_GROUP
    first_row = seg_ref[group]

    def row_block(b, carry):
        r0 = pl.multiple_of(first_row + b * MOE_ROW_BLOCK, MOE_ROW_BLOCK)
        xb = xs_ref[pl.ds(r0, MOE_ROW_BLOCK), :]
        gsb = gs_ref[pl.ds(r0, MOE_ROW_BLOCK), :]
        lane = lax.broadcasted_iota(jnp.int32, gsb.shape, 1)
        experts = range(MOE_EXPERTS_PER_STEP)
        gate = [jnp.dot(xb, wg_ref[j], preferred_element_type=F32) for j in experts]
        up = [jnp.dot(xb, wu_ref[j], preferred_element_type=F32) for j in experts]
        hmid = [(gate[j] * jax.nn.sigmoid(gate[j]) * up[j]).astype(BF16) for j in experts]
        down = [jnp.dot(hmid[j], wd_ref[j], preferred_element_type=F32) for j in experts]
        total = None
        for j in experts:
            g_e = jnp.sum(jnp.where(lane == e * MOE_EXPERTS_PER_STEP + j, gsb, 0.0), axis=-1, keepdims=True)
            total = g_e * down[j] if total is None else total + g_e * down[j]
        ys_ref[pl.ds(r0, MOE_ROW_BLOCK), :] += total
        return carry

    lax.fori_loop(0, seg_ref[N_GROUPS + group], row_block, 0)

    @pl.when(e == pl.num_programs(1) - 1)
    def _finish():
        col_id = lax.broadcasted_iota(jnp.int32, (tm, sorted_rows), 1).astype(F32)
        unperm = jnp.where(col_id == dest_ref[:, 0:1], 1.0, 0.0).astype(BF16)
        y = x_ref[...] + jnp.dot(unperm, ys_ref[...].astype(BF16), preferred_element_type=F32)
        if final_norm:
            y = y * lax.rsqrt(jnp.mean(y * y, axis=-1, keepdims=True) + EPS) * fw_ref[...]
        o_ref[...] = y


def _moe(x, nw, r_w, r_b, wg, wu, wd, final_w, final_norm, tm):
    T, D = x.shape
    const = lambda shape: pl.BlockSpec(shape, lambda i, e: (0,) * len(shape))
    sorted_rows = _moe_sorted_rows(tm)
    return pl.pallas_call(
        functools.partial(_moe_kernel, final_norm=final_norm, tm=tm),
        grid=(T // tm, N_EXPERTS // MOE_EXPERTS_PER_STEP),
        in_specs=[
            pl.BlockSpec((tm, D), lambda i, e: (i, 0)),
            const((1, D)), const((D, LANES)), const((1, LANES)),
            pl.BlockSpec((MOE_EXPERTS_PER_STEP, D, EXPERT_FF), lambda i, e: (e, 0, 0)),
            pl.BlockSpec((MOE_EXPERTS_PER_STEP, D, EXPERT_FF), lambda i, e: (e, 0, 0)),
            pl.BlockSpec((MOE_EXPERTS_PER_STEP, EXPERT_FF, D), lambda i, e: (e, 0, 0)),
            const((1, D)),
        ],
        out_specs=pl.BlockSpec((tm, D), lambda i, e: (i, 0)),
        out_shape=jax.ShapeDtypeStruct((T, D), F32),
        scratch_shapes=[pltpu.VMEM((sorted_rows, D), BF16), pltpu.VMEM((sorted_rows, LANES), F32),
                        pltpu.VMEM((sorted_rows, D), F32), pltpu.VMEM((tm, LANES), F32),
                        pltpu.VMEM((tm, tm), BF16), pltpu.SMEM((2 * N_GROUPS,), jnp.int32)],
        compiler_params=_params("arbitrary", "arbitrary"),
        name="moe",
    )(x, nw, r_w, r_b, wg, wu, wd, final_w)


def _tile(n, pref):
    t = min(n, pref)
    assert n % t == 0, (n, t)
    return t


def _prep_layer(l, norm1_w, w_in, ret_decay_logit, ret_gn_w, diff_lambda, diff_subln_w, conv_w, w_out, norm2_w,
                router_group_w, router_group_b, router_expert_w, router_expert_b, expert_w_gate, expert_w_up,
                expert_w_down):
    w = w_in[l]
    w_nat = jnp.concatenate([w[:, :_OFF_DQ], w[:, _OFF_DK:_OFF_DV], w[:, _OFF_CONV:]], axis=1).astype(BF16)
    w_t = jnp.concatenate([w[:, _OFF_DQ:_OFF_DK], w[:, _OFF_DV:_OFF_CONV]], axis=1).T.astype(BF16)
    router = jnp.zeros((D_MODEL, LANES), F32)
    router = router.at[:, :N_EXPERTS].set(router_expert_w[l]).at[:, N_EXPERTS:N_EXPERTS + N_GROUPS].set(
        router_group_w[l])
    r_b = jnp.zeros((1, LANES), F32).at[0, :N_EXPERTS].set(router_expert_b[l]).at[
        0, N_EXPERTS:N_EXPERTS + N_GROUPS].set(router_group_b[l])
    return dict(
        norm1=norm1_w[l][None, :], w_nat=w_nat, w_t=w_t,
        lg=jax.nn.log_sigmoid(ret_decay_logit[l].astype(F32)),
        gn_w=ret_gn_w[l][None, :].astype(F32),
        lam=diff_lambda[l].astype(F32), subln=diff_subln_w[l][:, None].astype(F32),
        conv_w=conv_w[l].astype(F32), w_out=w_out[l].astype(BF16),
        norm2=norm2_w[l][None, :], r_w=router, r_b=r_b,
        wg=expert_w_gate[l].astype(BF16), wu=expert_w_up[l].astype(BF16), wd=expert_w_down[l].astype(BF16),
    )


def _trunk(x, layers, final_w, slopes, pieces):
    B, S, D = x.shape
    tm = _tile(S, 1024)
    tq = _tile(S, 512)
    tk = tq // 2
    t_moe = _tile(B * S, 1024)
    aux = _position_features(S)
    for l, lw in enumerate(layers):
        rqkv, rg, dk, dqvt, cv, kn = _in_proj(x, lw["norm1"], lw["w_nat"], lw["w_t"], tm)
        ret = _retention(lw["lg"], rqkv, rg, lw["gn_w"])
        diff = _diff_attention(slopes, pieces, lw["lam"], lw["subln"], kn, dqvt, dk, aux, l, tq, tk)
        x = _out_proj(x, ret, diff, cv, lw["conv_w"], lw["w_out"], tm)
        last = l == len(layers) - 1
        x = _moe(x.reshape(B * S, D), lw["norm2"], lw["r_w"], lw["r_b"], lw["wg"], lw["wu"],
                 lw["wd"], final_w, last, t_moe).reshape(B, S, D)
    return x


def kernel(x_prompt, x_sample, norm1_w, w_in, ret_decay_logit, ret_gn_w, diff_lambda, diff_subln_w, conv_w, w_out,
           norm2_w, router_group_w, router_group_b, router_expert_w, router_expert_b, expert_w_gate, expert_w_up,
           expert_w_down, final_norm_w):
    depth = w_in.shape[0]
    layers = [
        _prep_layer(l, norm1_w, w_in, ret_decay_logit, ret_gn_w, diff_lambda, diff_subln_w, conv_w, w_out, norm2_w,
                    router_group_w, router_group_b, router_expert_w, router_expert_b, expert_w_gate, expert_w_up,
                    expert_w_down)
        for l in range(depth)
    ]
    final_w = final_norm_w[None, :].astype(F32)
    slopes = (np.float32(2.0) ** (np.float32(-8.0) * np.arange(1, DIFF_HEADS + 1, dtype=np.float32)
                                  / np.float32(DIFF_HEADS))) * np.float32(LOG2E)
    s_hi = slopes.astype(BF16).astype(np.float32)
    s_mid = (slopes - s_hi).astype(BF16).astype(np.float32)
    s_lo = (slopes - s_hi - s_mid).astype(BF16).astype(np.float32)
    pieces = jnp.asarray(np.stack([s_hi, s_mid, s_lo], axis=1).reshape(-1))
    slopes = jnp.asarray(slopes)
    return (_trunk(x_prompt, layers, final_w, slopes, pieces), _trunk(x_sample, layers, final_w, slopes, pieces))
```

```python
import functools
import math

import jax
import jax.numpy as jnp
import numpy as np
from jax import lax
from jax.experimental import pallas as pl
from jax.experimental.pallas import tpu as pltpu

F32 = jnp.float32
BF16 = jnp.bfloat16

D_MODEL = 1024
RET_HEADS = 6
RET_HEAD_DIM = 64
RET_WIDTH = RET_HEADS * RET_HEAD_DIM
DIFF_HEADS = 6
DIFF_QK_DIM = 32
DIFF_V_DIM = 64
DIFF_WIDTH = DIFF_HEADS * DIFF_V_DIM
CONV_CHANNELS = 256
CONV_WIDTH = 3
N_GROUPS = 4
EXPERTS_PER_GROUP = 4
N_EXPERTS = 16
EXPERT_FF = 512
EPS = 1e-6

LANES = 128
HEAD_PAIRS = RET_HEADS // 2
RET_CHUNK = 128
SUBLANES = 8
VMEM_LIMIT = 56 * 1024 * 1024
LOG2E = math.log2(math.e)
NEG_BIG = -1e30

_OFF_RG = 3 * RET_WIDTH
_OFF_DQ = 4 * RET_WIDTH
_OFF_DK = _OFF_DQ + DIFF_WIDTH
_OFF_DV = _OFF_DK + DIFF_WIDTH
_OFF_CONV = _OFF_DV + DIFF_WIDTH


def _params(*sem):
    return pltpu.CompilerParams(dimension_semantics=sem, vmem_limit_bytes=VMEM_LIMIT)


def _split_hi_lo(x):
    hi = x.astype(BF16)
    lo = (x - hi.astype(F32)).astype(BF16)
    return hi, lo


def _in_proj_kernel(x_ref, nw_ref, w_ref, wt_ref, rqkv_ref, rg_ref, dk_ref, dqvt_ref, cv_ref, kn_ref):
    x = x_ref[0]
    ms = jnp.mean(x * x, axis=-1, keepdims=True)
    h = (x * lax.rsqrt(ms + EPS) * nw_ref[...]).astype(BF16)
    a = jnp.dot(h, w_ref[:, 0:_OFF_DQ], preferred_element_type=F32)
    rqkv_ref[0, :, 0:RET_WIDTH] = a[:, 0:RET_WIDTH].astype(BF16)
    rqkv_ref[0, :, RET_WIDTH:2 * RET_WIDTH] = (a[:, RET_WIDTH:2 * RET_WIDTH] * RET_HEAD_DIM ** -0.5).astype(BF16)
    rqkv_ref[0, :, 2 * RET_WIDTH:3 * RET_WIDTH] = a[:, 2 * RET_WIDTH:3 * RET_WIDTH].astype(BF16)
    rg_ref[0] = a[:, _OFF_RG:_OFF_DQ]
    dk = jnp.dot(h, w_ref[:, _OFF_DQ:_OFF_DQ + DIFF_WIDTH], preferred_element_type=F32).astype(BF16)
    dk_ref[0] = dk
    dkf = dk.astype(F32)
    sq_hi, sq_lo = _split_hi_lo(dkf * dkf)
    grp = (lax.shift_right_logical(lax.broadcasted_iota(jnp.int32, (DIFF_WIDTH, LANES), 0),
                                   DIFF_QK_DIM.bit_length() - 1)
           == lax.broadcasted_iota(jnp.int32, (DIFF_WIDTH, LANES), 1))
    grp = jnp.where(grp, 1.0, 0.0).astype(BF16)
    kn2 = jnp.dot(sq_hi, grp, preferred_element_type=F32) + jnp.dot(sq_lo, grp, preferred_element_type=F32)
    kn_ref[0, 0] = jnp.broadcast_to(jnp.max(kn2, axis=0, keepdims=True), (SUBLANES, LANES))
    cv_ref[0] =jnp.dot(h, w_ref[:, _OFF_DQ + DIFF_WIDTH:], preferred_element_type=F32)
    t = lax.dot_general(wt_ref[...], h, (((1,), (1,)), ((), ())), preferred_element_type=F32)
    dqvt_ref[0, 0:DIFF_WIDTH, :] = (t[0:DIFF_WIDTH] * (DIFF_QK_DIM ** -0.5 * LOG2E)).astype(BF16)
    dqvt_ref[0, DIFF_WIDTH:, :] = t[DIFF_WIDTH:].astype(BF16)


def _in_proj(x, nw, w_nat, w_t, tm):
    B, S, D = x.shape
    n_nat = w_nat.shape[1]
    return pl.pallas_call(
        _in_proj_kernel,
        grid=(B, S // tm),
        in_specs=[
            pl.BlockSpec((1, tm, D), lambda b, i: (b, i, 0)),
            pl.BlockSpec((1, D), lambda b, i: (0, 0)),
            pl.BlockSpec((D, n_nat), lambda b, i: (0, 0)),
            pl.BlockSpec((2 * DIFF_WIDTH, D), lambda b, i: (0, 0)),
        ],
        out_specs=[
            pl.BlockSpec((1, tm, 3 * RET_WIDTH), lambda b, i: (b, i, 0)),
            pl.BlockSpec((1, tm, RET_WIDTH), lambda b, i: (b, i, 0)),
            pl.BlockSpec((1, tm, DIFF_WIDTH), lambda b, i: (b, i, 0)),
            pl.BlockSpec((1, 2 * DIFF_WIDTH, tm), lambda b, i: (b, 0, i)),
            pl.BlockSpec((1, tm, 3 * CONV_CHANNELS), lambda b, i: (b, i, 0)),
            pl.BlockSpec((1, 1, SUBLANES, LANES), lambda b, i: (b, i, 0, 0)),
        ],
        out_shape=[
            jax.ShapeDtypeStruct((B, S, 3 * RET_WIDTH), BF16),
            jax.ShapeDtypeStruct((B, S, RET_WIDTH), F32),
            jax.ShapeDtypeStruct((B, S, DIFF_WIDTH), BF16),
            jax.ShapeDtypeStruct((B, 2 * DIFF_WIDTH, S), BF16),
            jax.ShapeDtypeStruct((B, S, 3 * CONV_CHANNELS), F32),
            jax.ShapeDtypeStruct((B, S // tm, SUBLANES, LANES), F32),
        ],
        compiler_params=_params("parallel", "parallel"),
        name="in_proj",
    )(x, nw, w_nat, w_t)


def _pair_lane_value(lg_ref, direction, pair, shape, axis):
    idx = lax.broadcasted_iota(jnp.int32, shape, axis)
    return jnp.where(idx < RET_HEAD_DIM, lg_ref[direction, 2 * pair], lg_ref[direction, 2 * pair + 1])


def _same_head_mask():
    r = lax.broadcasted_iota(jnp.int32, (LANES, LANES), 0)
    c = lax.broadcasted_iota(jnp.int32, (LANES, LANES), 1)
    return (r < RET_HEAD_DIM) == (c < RET_HEAD_DIM)


def _ret_fwd_kernel(lg_ref, qkv_ref, y_ref, state_ref, dmat_ref, qdec_ref, kdec_ref, cdec_ref, *, chunks):
    C = RET_CHUNK

    @pl.when((pl.program_id(0) == 0) & (pl.program_id(1) == 0))
    def _build_tables():
        diff = (lax.broadcasted_iota(jnp.int32, (C, C), 0) - lax.broadcasted_iota(jnp.int32, (C, C), 1)).astype(F32)
        for h in range(RET_HEADS):
            lower = jnp.exp(jnp.maximum(diff, 0.0) * lg_ref[0, h])
            upper = jnp.exp(jnp.maximum(-diff, 0.0) * lg_ref[1, h])
            dmat_ref[h] = jnp.where(diff >= 0, lower, upper)
        pos = lax.broadcasted_iota(jnp.int32, (C, LANES), 0).astype(F32)
        for p in range(HEAD_PAIRS):
            lg_lane = _pair_lane_value(lg_ref, 0, p, (C, LANES), 1)
            qdec_ref[p] = jnp.exp((pos + 1.0) * lg_lane)
            kdec_ref[p] = jnp.exp((C - 1.0 - pos) * lg_lane)
            cdec_ref[p] = jnp.exp(float(C) * _pair_lane_value(lg_ref, 0, p, (LANES, LANES), 0))

    @pl.when(pl.program_id(1) == 0)
    def _reset():
        state_ref[...] = jnp.zeros(state_ref.shape, F32)

    lane = lax.broadcasted_iota(jnp.int32, (C, LANES), 1)
    same_head = _same_head_mask()
    def operands(p, ci):
        rows = slice(ci * C, (ci + 1) * C)
        return (qkv_ref[0, rows, p * LANES:(p + 1) * LANES],
                qkv_ref[0, rows, RET_WIDTH + p * LANES:RET_WIDTH + (p + 1) * LANES],
                qkv_ref[0, rows, 2 * RET_WIDTH + p * LANES:2 * RET_WIDTH + (p + 1) * LANES])

    items = [(p, ci) for p in range(HEAD_PAIRS) for ci in range(chunks)]
    scores, kvs = {}, {}
    for p, ci in items:
        q, k, v = operands(p, ci)
        for h in range(2):
            head_lanes = (lane >= h * RET_HEAD_DIM) & (lane < (h + 1) * RET_HEAD_DIM)
            qm = jnp.where(head_lanes, q, jnp.zeros_like(q))
            scores[p, ci, h] = lax.dot_general(qm, k, (((1,), (1,)), ((), ())), preferred_element_type=F32)
        kd = (k.astype(F32) * kdec_ref[p]).astype(BF16)
        kvs[p, ci] = lax.dot_general(kd, v, (((0,), (0,)), ((), ())), preferred_element_type=F32)
    states = {}
    for p in range(HEAD_PAIRS):
        state = state_ref[p]
        for ci in range(chunks):
            states[p, ci] = state.astype(BF16)
            state = cdec_ref[p] * state + jnp.where(same_head, kvs[p, ci], 0.0)
        state_ref[p] = state
    for p, ci in items:
        q, _, v = operands(p, ci)
        inner = [jnp.dot((scores[p, ci, h] * dmat_ref[2 * p + h]).astype(BF16), v, preferred_element_type=F32)
                 for h in range(2)]
        cross = jnp.dot(q, states[p, ci], preferred_element_type=F32) * qdec_ref[p]
        y_ref[0, ci * C:(ci + 1) * C, p * LANES:(p + 1) * LANES] = (
            jnp.where(lane < RET_HEAD_DIM, inner[0], inner[1]) + cross)


def _ret_bwd_kernel(lg_ref, qkv_ref, y1_ref, g_ref, gnw_ref, o_ref, state_ref, qdec_ref, kdec_ref, cdec_ref, *,
                    chunks):
    C = RET_CHUNK

    @pl.when((pl.program_id(0) == 0) & (pl.program_id(1) == 0))
    def _build_tables():
        pos = lax.broadcasted_iota(jnp.int32, (C, LANES), 0).astype(F32)
        for p in range(HEAD_PAIRS):
            lg_lane = _pair_lane_value(lg_ref, 1, p, (C, LANES), 1)
            qdec_ref[p] = jnp.exp((float(C) - pos) * lg_lane)
            kdec_ref[p] = jnp.exp(pos * lg_lane)
            cdec_ref[p] = jnp.exp(float(C) * _pair_lane_value(lg_ref, 1, p, (LANES, LANES), 0))

    @pl.when(pl.program_id(1) == 0)
    def _reset():
        state_ref[...] = jnp.zeros(state_ref.shape, F32)

    same_head = _same_head_mask()
    head_avg = jnp.where(same_head, 1.0 / RET_HEAD_DIM, 0.0).astype(BF16)

    def head_mean(t):
        hi, lo = _split_hi_lo(t)
        return (jnp.dot(hi, head_avg, preferred_element_type=F32)
                + jnp.dot(lo, head_avg, preferred_element_type=F32))

    items = [(p, ci) for p in range(HEAD_PAIRS) for ci in range(chunks)]
    kvs = {}
    for p, ci in items:
        rows = slice(ci * C, (ci + 1) * C)
        k = qkv_ref[0, rows, RET_WIDTH + p * LANES:RET_WIDTH + (p + 1) * LANES]
        v = qkv_ref[0, rows, 2 * RET_WIDTH + p * LANES:2 * RET_WIDTH + (p + 1) * LANES]
        kd = (k.astype(F32) * kdec_ref[p]).astype(BF16)
        kvs[p, ci] = lax.dot_general(kd, v, (((0,), (0,)), ((), ())), preferred_element_type=F32)
    states = {}
    for p in range(HEAD_PAIRS):
        state = state_ref[p]
        for ci in reversed(range(chunks)):
            states[p, ci] = state.astype(BF16)
            state = cdec_ref[p] * state + jnp.where(same_head, kvs[p, ci], 0.0)
        state_ref[p] = state
    ys = {}
    for p, ci in items:
        rows = slice(ci * C, (ci + 1) * C)
        sl = slice(p * LANES, (p + 1) * LANES)
        cross = jnp.dot(qkv_ref[0, rows, sl], states[p, ci], preferred_element_type=F32)
        ys[p, ci] = y1_ref[0, rows, sl] + cross * qdec_ref[p]
    means = {key: head_mean(y) for key, y in ys.items()}
    devs = {key: ys[key] - means[key] for key in ys}
    variances = {key: head_mean(d * d) for key, d in devs.items()}
    for p, ci in items:
        rows = slice(ci * C, (ci + 1) * C)
        sl = slice(p * LANES, (p + 1) * LANES)
        g = g_ref[0, rows, sl]
        o = devs[p, ci] * lax.rsqrt(variances[p, ci] + EPS) * gnw_ref[:, sl] * (g * jax.nn.sigmoid(g))
        o_ref[0, rows, sl] = o.astype(o_ref.dtype)


def _retention(lg, rqkv, rg, gn_w):
    B, S, _ = rqkv.shape
    chunks = math.gcd(S // RET_CHUNK, 8)
    C = chunks * RET_CHUNK
    n = S // C
    smem = pl.BlockSpec(memory_space=pltpu.SMEM)
    table = pltpu.VMEM((HEAD_PAIRS, RET_CHUNK, LANES), F32)
    state = pltpu.VMEM((HEAD_PAIRS, LANES, LANES), F32)
    y1 = pl.pallas_call(
        functools.partial(_ret_fwd_kernel, chunks=chunks),
        grid=(B, n),
        in_specs=[smem, pl.BlockSpec((1, C, 3 * RET_WIDTH), lambda b, i: (b, i, 0))],
        out_specs=pl.BlockSpec((1, C, RET_WIDTH), lambda b, i: (b, i, 0)),
        out_shape=jax.ShapeDtypeStruct((B, S, RET_WIDTH), F32),
        scratch_shapes=[state, pltpu.VMEM((RET_HEADS, RET_CHUNK, RET_CHUNK), F32), table, table, state],
        compiler_params=_params("arbitrary", "arbitrary"),
        name="ret_fwd",
    )(lg, rqkv)
    rev = lambda b, i: (b, n - 1 - i, 0)
    return pl.pallas_call(
        functools.partial(_ret_bwd_kernel, chunks=chunks),
        grid=(B, n),
        in_specs=[
            smem,
            pl.BlockSpec((1, C, 3 * RET_WIDTH), rev),
            pl.BlockSpec((1, C, RET_WIDTH), rev),
            pl.BlockSpec((1, C, RET_WIDTH), rev),
            pl.BlockSpec((1, RET_WIDTH), lambda b, i: (0, 0)),
        ],
        out_specs=pl.BlockSpec((1, C, RET_WIDTH), rev),
        out_shape=jax.ShapeDtypeStruct((B, S, RET_WIDTH), BF16),
        scratch_shapes=[state, table, table, state],
        compiler_params=_params("arbitrary", "arbitrary"),
        name="ret_bwd",
    )(lg, rqkv, y1, rg, gn_w)


AUX_BLOCK = 256
AUX_ROWS = 16
V_ROWS = DIFF_V_DIM + SUBLANES
GUARD_LOG2 = 60.0
BOUND_SLACK = 1.01
LONG_BODY_PAIRS = 8
UNDERFLOW_LOG2 = 150.0


def _position_features(seq):
    j = jnp.arange(seq, dtype=jnp.int32)
    ones = jnp.ones((seq,), F32)
    n = (j // AUX_BLOCK).astype(F32)
    jc = (j % AUX_BLOCK - AUX_BLOCK // 2).astype(F32)
    feat = jnp.stack([ones] * 3 + [n] * 3 + [jc] * 3, axis=1)
    return jnp.pad(feat, ((0, 0), (0, LANES - feat.shape[1]))).astype(BF16)


def _split3(x):
    hi = x.astype(BF16).astype(F32)
    r = x - hi
    mid = r.astype(BF16).astype(F32)
    return hi, mid, (r - mid).astype(BF16).astype(F32)


def _attn_kernel(slopes_ref, pieces_ref, lam_ref, sw_ref, kn_ref, qt_ref, k_ref, aux_ref, vt_ref, o_ref,
                 m_ref, acc_ref, qm_ref, aug_ref, pa_ref, pb_ref, dist_ref, *, seq, tq, tk, lam_init):
    pair = pl.program_id(1)
    qi = pl.program_id(2)
    q0 = qi * tq
    nk = seq // tk
    per = tq // tk
    kd0 = qi * per
    qt = qt_ref[0]
    row = lax.broadcasted_iota(jnp.int32, (LANES, tq), 0)
    q_masked = []
    for hc in range(4):
        lo = hc * DIFF_QK_DIM
        q_masked.append(jnp.where((row >= lo) & (row < lo + DIFF_QK_DIM), qt, jnp.zeros_like(qt)))

    @pl.when(qi == 0)
    def _distance_table():
        d = lax.broadcasted_iota(jnp.int32, (tq, tq), 1) - lax.broadcasted_iota(jnp.int32, (tq, tq), 0)
        dist_ref[...] = jnp.abs(d).astype(F32)

    kd = pl.multiple_of(q0, tq)
    kdiag = k_ref[0, pl.ds(kd, tq), :]
    self_prod = qt.astype(F32) * kdiag.astype(F32).T
    shifts = [jnp.sum(self_prod[hc * DIFF_QK_DIM:(hc + 1) * DIFF_QK_DIM], axis=0, keepdims=True) for hc in range(4)]

    kn2 = jnp.max(jnp.max(kn_ref[0], axis=0), axis=0, keepdims=True)
    kn_lane = lax.broadcasted_iota(jnp.int32, (1, LANES), 1)
    excess = []
    for h in range(2):
        worst = None
        for c in range(2):
            hc = 2 * h + c
            qf = q_masked[hc].astype(F32)
            qn2 = jnp.sum(qf * qf, axis=0, keepdims=True)
            kn2_hc = jnp.max(jnp.where(kn_lane == 4 * pair + hc, kn2, 0.0), axis=-1, keepdims=True)
            bound = jnp.sqrt(qn2 * kn2_hc) * BOUND_SLACK + 1e-3
            e = jnp.max(bound - shifts[hc], axis=-1, keepdims=True)
            worst = e if worst is None else jnp.maximum(worst, e)
        excess.append(worst)
    single_pass = jnp.where(jnp.maximum(excess[0], excess[1]) <= GUARD_LOG2, 1, 0).astype(jnp.int32)[0, 0] == 1

    def v_rows_with_ones(h, k0, n):
        vt = vt_ref[0, h * DIFF_V_DIM:(h + 1) * DIFF_V_DIM, pl.ds(k0, n)]
        return jnp.concatenate([vt, jnp.ones((SUBLANES, n), BF16)], axis=0)

    def keys_with_features(k0):
        return jnp.concatenate([k_ref[0, pl.ds(k0, tk), :], aux_ref[pl.ds(k0, tk), :]], axis=1)

    def queries_with_features(side, hc):
        return jnp.concatenate([qm_ref[hc], aug_ref[side * 4 + hc],
                                jnp.zeros((LANES - AUX_ROWS, tq), BF16)], axis=0)

    @pl.when(single_pass)
    def _single_pass():
        ipos = (q0 + lax.broadcasted_iota(jnp.int32, (1, tq), 1)).astype(F32)
        aug_row = lax.broadcasted_iota(jnp.int32, (AUX_ROWS, tq), 0)
        q0f = q0.astype(F32)
        lo_even, n_lo, totals = [], [], []
        for h in range(2):
            slope = slopes_ref[2 * pair + h]
            pieces = [pieces_ref[(2 * pair + h) * 3 + i] for i in range(3)]
            for c in range(2):
                hc = 2 * h + c
                for side, sign in ((0, 1.0), (1, -1.0), (2, 0.0)):
                    const = (-sign) * (slope * ipos) - shifts[hc] + sign * (0.5 * AUX_BLOCK) * slope
                    c_hi, c_mid, c_lo = _split3(const)
                    aug = jnp.where(aug_row == 0, c_hi,
                                    jnp.where(aug_row == 1, c_mid, jnp.where(aug_row == 2, c_lo, 0.0)))
                    for i in range(3):
                        aug = jnp.where(aug_row == 3 + i, sign * AUX_BLOCK * pieces[i], aug)
                        aug = jnp.where(aug_row == 6 + i, sign * pieces[i], aug)
                    aug_ref[side * 4 + hc] = aug.astype(BF16)
                qm_ref[hc] = q_masked[hc]
            reach = (excess[h] + UNDERFLOW_LOG2) / slope
            lo_blk = jnp.clip(jnp.floor((q0f - (tk - 1.0) - reach) / tk), 0.0, float(nk))
            hi_blk = jnp.clip(jnp.floor((reach + q0f + (tq - 1.0)) / tk) + 1.0, 0.0, float(nk))
            lo_blk = jnp.minimum(lo_blk.astype(jnp.int32)[0, 0], kd0)
            hi_blk = jnp.maximum(hi_blk.astype(jnp.int32)[0, 0], kd0 + per)
            odd = (kd0 - lo_blk + hi_blk - kd0 - per) % 2
            lo_h = jnp.where((odd == 1) & (lo_blk > 0), lo_blk - 1, lo_blk)
            hi_h = jnp.where((odd == 1) & (lo_blk == 0), hi_blk + 1, hi_blk)
            lo_even.append(lo_h)
            n_lo.append(kd0 - lo_h)
            totals.append(kd0 - lo_h + hi_h - kd0 - per)
        pairs_h0 = totals[0] // 2
        n_pairs = pairs_h0 + totals[1] // 2

        def locate(x, half):
            head = jnp.where(x >= pairs_h0, 1, 0)
            i = 2 * (x - head * pairs_h0) + half
            before = jnp.where(head == 1, n_lo[1], n_lo[0])
            first = jnp.where(head == 1, lo_even[1], lo_even[0])
            blk = jnp.where(i < before, first + i, kd0 + per + i - before)
            return head, blk, jnp.where(i < before, 0, 1)

        def stage_a(x, p_ref):
            for half in range(2):
                head, blk, side = locate(x, half)
                lhs = keys_with_features(pl.multiple_of(blk * tk, tk))
                for c in range(2):
                    t = jnp.dot(lhs, queries_with_features(side, 2 * head + c), preferred_element_type=F32)
                    p_ref[2 * half + c] = jnp.exp2(t).astype(BF16)

        def stage_b(x, p_ref):
            head, blk_a, _ = locate(x, 0)
            _, blk_b, _ = locate(x, 1)
            rows = pl.ds(pl.multiple_of(head * DIFF_V_DIM, DIFF_V_DIM), DIFF_V_DIM)
            ones = jnp.ones((SUBLANES, tk), BF16)
            va = jnp.concatenate([vt_ref[0, rows, pl.ds(pl.multiple_of(blk_a * tk, tk), tk)], ones], axis=0)
            vb = jnp.concatenate([vt_ref[0, rows, pl.ds(pl.multiple_of(blk_b * tk, tk), tk)], ones], axis=0)
            for c in range(2):
                acc_ref[2 * head + c] += (jnp.dot(va, p_ref[c], preferred_element_type=F32)
                                          + jnp.dot(vb, p_ref[2 + c], preferred_element_type=F32))

        def diag_a(h, p_ref):
            for half in range(2):
                lhs = keys_with_features(pl.multiple_of(q0 + half * tk, tk))
                bias = dist_ref[half * tk:(half + 1) * tk, :] * slopes_ref[2 * pair + h]
                for c in range(2):
                    t = jnp.dot(lhs, queries_with_features(2, 2 * h + c), preferred_element_type=F32) - bias
                    p_ref[2 * half + c] = jnp.exp2(t).astype(BF16)

        def diag_b(h, p_ref):
            va = v_rows_with_ones(h, pl.multiple_of(q0, tk), tk)
            vb = v_rows_with_ones(h, pl.multiple_of(q0 + tk, tk), tk)
            for c in range(2):
                acc_ref[2 * h + c] = (jnp.dot(va, p_ref[c], preferred_element_type=F32)
                                      + jnp.dot(vb, p_ref[2 + c], preferred_element_type=F32))

        diag_a(0, pa_ref)
        diag_a(1, pb_ref)
        diag_b(0, pa_ref)

        @pl.when(n_pairs == 0)
        def _diagonal_only():
            diag_b(1, pb_ref)

        @pl.when(n_pairs > 0)
        def _blocks():
            stage_a(0, pa_ref)
            diag_b(1, pb_ref)

            def pairs_body(count):
                def body(j, carry):
                    for s in range(0, count, 2):
                        stage_a(count * j + s + 1, pb_ref)
                        stage_b(count * j + s, pa_ref)
                        stage_a(count * j + s + 2, pa_ref)
                        stage_b(count * j + s + 1, pb_ref)
                    return carry
                return body

            long_iters = (n_pairs - 1) // LONG_BODY_PAIRS
            lax.fori_loop(0, long_iters, pairs_body(LONG_BODY_PAIRS), 0)
            lax.fori_loop(LONG_BODY_PAIRS // 2 * long_iters, (n_pairs - 1) // 2, pairs_body(2), 0)

            @pl.when(n_pairs % 2 == 1)
            def _last_one():
                stage_b(n_pairs - 1, pa_ref)

            @pl.when(n_pairs % 2 == 0)
            def _last_two():
                stage_a(n_pairs - 1, pb_ref)
                stage_b(n_pairs - 2, pa_ref)
                stage_b(n_pairs - 1, pb_ref)

    @pl.when(jnp.logical_not(single_pass))
    def _online():
        m_ref[...] = jnp.full(m_ref.shape, NEG_BIG, F32)
        acc_ref[...] = jnp.zeros(acc_ref.shape, F32)
        rel = (lax.broadcasted_iota(jnp.int32, (tk, tq), 1)
               - lax.broadcasted_iota(jnp.int32, (tk, tq), 0)).astype(F32)
        for h in range(2):
            def online_step(kb, carry, h=h):
                k0 = pl.multiple_of(kb * tk, tk)
                kblk = k_ref[0, pl.ds(k0, tk), :]
                v_ones = v_rows_with_ones(h, k0, tk)
                bias = jnp.abs(rel + (q0 - k0).astype(F32)) * slopes_ref[2 * pair + h]
                for c in range(2):
                    hc = 2 * h + c
                    s = jnp.dot(kblk, q_masked[hc], preferred_element_type=F32) - bias
                    m_old = m_ref[hc]
                    m_new = jnp.maximum(m_old, jnp.max(s, axis=0, keepdims=True))
                    p = jnp.exp2(s - m_new).astype(BF16)
                    acc_ref[hc] = (jnp.exp2(m_old - m_new) * acc_ref[hc]
                                   + jnp.dot(v_ones, p, preferred_element_type=F32))
                    m_ref[hc] = m_new
                return carry

            lax.fori_loop(0, nk, online_step, 0)

    lam = lam_ref[...]
    lam_full = (jnp.exp(jnp.sum(lam[0:1] * lam[1:2], axis=-1, keepdims=True))
                - jnp.exp(jnp.sum(lam[2:3] * lam[3:4], axis=-1, keepdims=True)) + lam_init)
    outs = []
    for h in range(2):
        a1 = acc_ref[2 * h]
        a2 = acc_ref[2 * h + 1]
        rows = slice(0, DIFF_V_DIM)
        ones_row = slice(DIFF_V_DIM, DIFF_V_DIM + 1)
        o = a1[rows] / a1[ones_row] - lam_full * (a2[rows] / a2[ones_row])
        o = o * lax.rsqrt(jnp.mean(o * o, axis=0, keepdims=True) + EPS)
        outs.append(o * sw_ref[...] * (1.0 - lam_init))
    o_ref[0] = jnp.concatenate(outs, axis=0).T.astype(o_ref.dtype)


def _diff_attention(slopes, pieces, lam, subln_w, kn, dqvt, dk, aux, layer_idx, tq, tk):
    B, S, _ = dk.shape
    assert tq == 2 * tk and tk <= AUX_BLOCK and S // AUX_BLOCK <= AUX_BLOCK
    lam_init = 0.8 - 0.6 * math.exp(-0.3 * layer_idx)
    kern = functools.partial(_attn_kernel, seq=S, tq=tq, tk=tk, lam_init=lam_init)
    v_block0 = DIFF_WIDTH // LANES
    smem = pl.BlockSpec(memory_space=pltpu.SMEM)
    return pl.pallas_call(
        kern,
        grid=(B, HEAD_PAIRS, S // tq),
        in_specs=[
            smem, smem,
            pl.BlockSpec((4, DIFF_QK_DIM), lambda b, j, i: (0, 0)),
            pl.BlockSpec((DIFF_V_DIM, 1), lambda b, j, i: (0, 0)),
            pl.BlockSpec((1,) + kn.shape[1:], lambda b, j, i: (b, 0, 0, 0)),
            pl.BlockSpec((1, LANES, tq), lambda b, j, i: (b, j, i)),
            pl.BlockSpec((1, S, LANES), lambda b, j, i: (b, 0, j)),
            pl.BlockSpec((S, LANES), lambda b, j, i: (0, 0)),
            pl.BlockSpec((1, LANES, S), lambda b, j, i: (b, v_block0 + j, 0)),
        ],
        out_specs=pl.BlockSpec((1, tq, LANES), lambda b, j, i: (b, i, j)),
        out_shape=jax.ShapeDtypeStruct((B, S, DIFF_WIDTH), BF16),
        scratch_shapes=[pltpu.VMEM((4, 1, tq), F32), pltpu.VMEM((4, V_ROWS, tq), F32),
                        pltpu.VMEM((4, LANES, tq), BF16), pltpu.VMEM((12, AUX_ROWS, tq), BF16),
                        pltpu.VMEM((4, tk, tq), BF16), pltpu.VMEM((4, tk, tq), BF16),
                        pltpu.VMEM((tq, tq), F32)],
        compiler_params=_params("parallel", "parallel", "arbitrary"),
        name="diff_attn",
    )(slopes, pieces, lam, subln_w, kn, dqvt, dk, aux, dqvt)


def _out_proj_kernel(x_ref, ret_ref, diff_ref, cv_ref, prev_ref, next_ref, cw_ref, w_ref, o_ref, *, tm):
    i = pl.program_id(1)
    last = pl.num_programs(1) - 1
    cc = CONV_CHANNELS
    cv = cv_ref[0]
    u = cv[:, 2 * cc:3 * cc] * cv[:, 0:cc]
    prev = prev_ref[0]
    nxt = next_ref[0]
    u_prev = prev[SUBLANES - 1:SUBLANES, 2 * cc:3 * cc] * prev[SUBLANES - 1:SUBLANES, 0:cc]
    u_next = nxt[0:1, 2 * cc:3 * cc] * nxt[0:1, 0:cc]
    u_prev = jnp.where(i == 0, 0.0, u_prev)
    u_next = jnp.where(i == last, 0.0, u_next)
    rows = lax.broadcasted_iota(jnp.int32, (tm, cc), 0)
    u_m1 = jnp.where(rows == 0, u_prev, pltpu.roll(u, 1, 0))
    u_p1 = jnp.where(rows == tm - 1, u_next, pltpu.roll(u, tm - 1, 0))
    y = cw_ref[0:1] * u_m1 + cw_ref[1:2] * u + cw_ref[2:3] * u_p1
    conv = (cv[:, cc:2 * cc] * y).astype(BF16)
    acc = jnp.dot(ret_ref[0], w_ref[0:RET_WIDTH], preferred_element_type=F32)
    acc += jnp.dot(diff_ref[0], w_ref[RET_WIDTH:RET_WIDTH + DIFF_WIDTH], preferred_element_type=F32)
    acc += jnp.dot(conv, w_ref[RET_WIDTH + DIFF_WIDTH:], preferred_element_type=F32)
    o_ref[0] = x_ref[0] + acc


def _out_proj(x, ret, diff, cv, conv_w, w_out, tm):
    B, S, D = x.shape
    per = tm // SUBLANES
    nblk8 = S // SUBLANES
    tile = lambda width: pl.BlockSpec((1, tm, width), lambda b, i: (b, i, 0))
    return pl.pallas_call(
        functools.partial(_out_proj_kernel, tm=tm),
        grid=(B, S // tm),
        in_specs=[
            tile(D), tile(RET_WIDTH), tile(DIFF_WIDTH), tile(3 * CONV_CHANNELS),
            pl.BlockSpec((1, SUBLANES, 3 * CONV_CHANNELS), lambda b, i: (b, jnp.maximum(i * per - 1, 0), 0)),
            pl.BlockSpec((1, SUBLANES, 3 * CONV_CHANNELS),
                         lambda b, i: (b, jnp.minimum((i + 1) * per, nblk8 - 1), 0)),
            pl.BlockSpec((CONV_WIDTH, CONV_CHANNELS), lambda b, i: (0, 0)),
            pl.BlockSpec((D, D), lambda b, i: (0, 0)),
        ],
        out_specs=tile(D),
        out_shape=jax.ShapeDtypeStruct((B, S, D), F32),
        compiler_params=_params("parallel", "parallel"),
        name="out_proj",
    )(x, ret, diff, cv, cv, cv, conv_w, w_out)


_GROUP_LANE0 = N_EXPERTS
MOE_ROW_BLOCK = 256
MOE_LONG_BLOCK = 320
MOE_ROW_ALIGN = math.gcd(MOE_ROW_BLOCK, MOE_LONG_BLOCK)
MOE_EXPERTS_PER_STEP = 2
assert EXPERTS_PER_GROUP % MOE_EXPERTS_PER_STEP == 0


def _moe_sorted_rows(tm):
    assert MOE_ROW_BLOCK < MOE_LONG_BLOCK <= 2 * MOE_ROW_BLOCK
    return (tm + N_GROUPS * (MOE_ROW_BLOCK - 1)) // MOE_ROW_BLOCK * MOE_ROW_BLOCK


def _router_gates(logits):
    tm = logits.shape[0]
    lt = logits.T
    big = jnp.int32(LANES)
    grow = lax.broadcasted_iota(jnp.int32, (SUBLANES, tm), 0)
    gl = jnp.where(grow < N_GROUPS, lt[_GROUP_LANE0:_GROUP_LANE0 + SUBLANES], -jnp.inf)
    g_max = jnp.max(gl, axis=0, keepdims=True)
    g_idx = jnp.min(jnp.where(gl == g_max, grow, big), axis=0, keepdims=True)
    g_w = 1.0 / jnp.sum(jnp.exp(gl - g_max), axis=0, keepdims=True)
    erow = lax.broadcasted_iota(jnp.int32, (N_EXPERTS, tm), 0)
    in_group = lax.shift_right_logical(erow, EXPERTS_PER_GROUP.bit_length() - 1) == g_idx
    el = jnp.where(in_group, lt[0:N_EXPERTS], -jnp.inf)
    e1 = jnp.max(el, axis=0, keepdims=True)
    i1 = jnp.min(jnp.where(el == e1, erow, big), axis=0, keepdims=True)
    el2 = jnp.where(erow == i1, -jnp.inf, el)
    e2 = jnp.max(el2, axis=0, keepdims=True)
    i2 = jnp.min(jnp.where(el2 == e2, erow, big), axis=0, keepdims=True)
    r = jnp.exp(e2 - e1)
    p1 = g_w / (1.0 + r)
    p2 = g_w * r / (1.0 + r)
    gates_t = jnp.where(erow == i1, p1, jnp.where(erow == i2, p2, 0.0))
    member_t = jnp.where(grow == g_idx, 1.0, 0.0)
    rest = jnp.zeros((LANES - N_EXPERTS - SUBLANES, tm), F32)
    return jnp.concatenate([gates_t, member_t, rest], axis=0).T


def _moe_kernel(x_ref, nw_ref, rw_ref, rb_ref, wg_ref, wu_ref, wd_ref, fw_ref, o_ref,
                xs_ref, gs_ref, ys_ref, dest_ref, ltri_ref, seg_ref, *, final_norm, tm):
    tile = pl.program_id(0)
    e = pl.program_id(1)
    sorted_rows = _moe_sorted_rows(tm)

    @pl.when((tile == 0) & (e == 0))
    def _strictly_lower():
        r = lax.broadcasted_iota(jnp.int32, (tm, tm), 0)
        c = lax.broadcasted_iota(jnp.int32, (tm, tm), 1)
        ltri_ref[...] = jnp.where(c < r, 1.0, 0.0).astype(BF16)

    @pl.when(e == 0)
    def _route_and_sort():
        x = x_ref[...]
        xn = x * lax.rsqrt(jnp.mean(x * x, axis=-1, keepdims=True) + EPS) * nw_ref[...]
        hi, lo = _split_hi_lo(xn)
        r_hi, r_lo = _split_hi_lo(rw_ref[...])
        logits = (jnp.dot(hi, r_hi, preferred_element_type=F32)
                  + jnp.dot(lo, r_hi, preferred_element_type=F32)
                  + jnp.dot(hi, r_lo, preferred_element_type=F32)) + rb_ref[...]
        routed = _router_gates(logits)
        lane = lax.broadcasted_iota(jnp.int32, (tm, LANES), 1)
        gates = jnp.where(lane < N_EXPERTS, routed, 0.0)
        member = jnp.where(lane >= _GROUP_LANE0, routed, 0.0)
        earlier = jnp.dot(ltri_ref[...], member.astype(BF16), preferred_element_type=F32)
        rank = jnp.sum(member * earlier, axis=-1, keepdims=True)
        count = jnp.sum(member, axis=0, keepdims=True)
        seg = jnp.floor((count + (MOE_ROW_BLOCK - 1.0)) * (1.0 / MOE_ROW_BLOCK)) * MOE_ROW_BLOCK
        long_one = (count > MOE_ROW_BLOCK) & (count <= MOE_LONG_BLOCK)
        seg = jnp.where(long_one, float(MOE_LONG_BLOCK), seg)
        lane1 = lax.broadcasted_iota(jnp.int32, (1, LANES), 1)
        g0 = _GROUP_LANE0
        s0, s1, s2 = seg[:, g0:g0 + 1], seg[:, g0 + 1:g0 + 2], seg[:, g0 + 2:g0 + 3]
        start = jnp.where(lane1 == g0 + 1, s0,
                          jnp.where(lane1 == g0 + 2, s0 + s1, jnp.where(lane1 == g0 + 3, s0 + s1 + s2, 0.0)))
        dest = rank + jnp.sum(member * start, axis=-1, keepdims=True)
        dest_ref[...] = jnp.broadcast_to(dest, (tm, LANES))
        dest_row = dest_ref[...].T[0:1, :]
        row_id = lax.broadcasted_iota(jnp.int32, (sorted_rows, tm), 0).astype(F32)
        perm = jnp.where(row_id == dest_row, 1.0, 0.0).astype(BF16)
        g_hi, g_mid, g_lo = _split3(gates)
        g_pieces = g_hi + pltpu.roll(g_mid, N_EXPERTS, 1) + pltpu.roll(g_lo, 2 * N_EXPERTS, 1)
        moved = jnp.dot(perm, jnp.concatenate([hi, g_pieces.astype(BF16)], axis=1), preferred_element_type=F32)
        xs_ref[...] = moved[:, :D_MODEL].astype(BF16)
        g_moved = moved[:, D_MODEL:]
        gs_ref[...] = (g_moved + pltpu.roll(g_moved, LANES - N_EXPERTS, 1)
                       + pltpu.roll(g_moved, LANES - 2 * N_EXPERTS, 1))
        ys_ref[...] = jnp.zeros(ys_ref.shape, F32)
        start_i = start.astype(jnp.int32)
        blocks_i = jnp.where(long_one, 0.0, seg * (1.0 / MOE_ROW_BLOCK)).astype(jnp.int32)
        long_i = jnp.where(long_one, 1, 0).astype(jnp.int32)
        for g in range(N_GROUPS):
            seg_ref[g] = start_i[0, _GROUP_LANE0 + g]
            seg_ref[N_GROUPS + g] = blocks_i[0, _GROUP_LANE0 + g]
            seg_ref[2 * N_GROUPS + g] = long_i[0, _GROUP_LANE0 + g]

    group = (e * MOE_EXPERTS_PER_STEP) // EXPERTS_PER_GROUP
    first_row = seg_ref[group]

    def expert_rows(r0, rows):
        r0 = pl.multiple_of(r0, MOE_ROW_ALIGN)
        xb = xs_ref[pl.ds(r0, rows), :]
        gsb = gs_ref[pl.ds(r0, rows), :]
        lane = lax.broadcasted_iota(jnp.int32, gsb.shape, 1)
        experts = range(MOE_EXPERTS_PER_STEP)
        gate = [jnp.dot(xb, wg_ref[j], preferred_element_type=F32) for j in experts]
        up = [jnp.dot(xb, wu_ref[j], preferred_element_type=F32) for j in experts]
        hmid = [(gate[j] * jax.nn.sigmoid(gate[j]) * up[j]).astype(BF16) for j in experts]
        down = [jnp.dot(hmid[j], wd_ref[j], preferred_element_type=F32) for j in experts]
        total = None
        for j in experts:
            g_e = jnp.sum(jnp.where(lane == e * MOE_EXPERTS_PER_STEP + j, gsb, 0.0), axis=-1, keepdims=True)
            total = g_e * down[j] if total is None else total + g_e * down[j]
        ys_ref[pl.ds(r0, rows), :] += total

    def row_block(b, carry):
        expert_rows(first_row + b * MOE_ROW_BLOCK, MOE_ROW_BLOCK)
        return carry

    lax.fori_loop(0, seg_ref[N_GROUPS + group], row_block, 0)

    @pl.when(seg_ref[2 * N_GROUPS + group] == 1)
    def _one_long_block():
        expert_rows(first_row, MOE_LONG_BLOCK)

    @pl.when(e == pl.num_programs(1) - 1)
    def _finish():
        col_id = lax.broadcasted_iota(jnp.int32, (tm, sorted_rows), 1).astype(F32)
        unperm = jnp.where(col_id == dest_ref[:, 0:1], 1.0, 0.0).astype(BF16)
        y = x_ref[...] + jnp.dot(unperm, ys_ref[...].astype(BF16), preferred_element_type=F32)
        if final_norm:
            y = y * lax.rsqrt(jnp.mean(y * y, axis=-1, keepdims=True) + EPS) * fw_ref[...]
        o_ref[...] = y


def _moe(x, nw, r_w, r_b, wg, wu, wd, final_w, final_norm, tm):
    T, D = x.shape
    const = lambda shape: pl.BlockSpec(shape, lambda i, e: (0,) * len(shape))
    sorted_rows = _moe_sorted_rows(tm)
    return pl.pallas_call(
        functools.partial(_moe_kernel, final_norm=final_norm, tm=tm),
        grid=(T // tm, N_EXPERTS // MOE_EXPERTS_PER_STEP),
        in_specs=[
            pl.BlockSpec((tm, D), lambda i, e: (i, 0)),
            const((1, D)), const((D, LANES)), const((1, LANES)),
            pl.BlockSpec((MOE_EXPERTS_PER_STEP, D, EXPERT_FF), lambda i, e: (e, 0, 0)),
            pl.BlockSpec((MOE_EXPERTS_PER_STEP, D, EXPERT_FF), lambda i, e: (e, 0, 0)),
            pl.BlockSpec((MOE_EXPERTS_PER_STEP, EXPERT_FF, D), lambda i, e: (e, 0, 0)),
            const((1, D)),
        ],
        out_specs=pl.BlockSpec((tm, D), lambda i, e: (i, 0)),
        out_shape=jax.ShapeDtypeStruct((T, D), F32),
        scratch_shapes=[pltpu.VMEM((sorted_rows, D), BF16), pltpu.VMEM((sorted_rows, LANES), F32),
                        pltpu.VMEM((sorted_rows, D), F32), pltpu.VMEM((tm, LANES), F32),
                        pltpu.VMEM((tm, tm), BF16), pltpu.SMEM((3 * N_GROUPS,), jnp.int32)],
        compiler_params=_params("arbitrary", "arbitrary"),
        name="moe",
    )(x, nw, r_w, r_b, wg, wu, wd, final_w)


def _tile(n, pref):
    t = min(n, pref)
    assert n % t == 0, (n, t)
    return t


def _prep_layer(l, norm1_w, w_in, ret_decay_logit, ret_gn_w, diff_lambda, diff_subln_w, conv_w, w_out, norm2_w,
                router_group_w, router_group_b, router_expert_w, router_expert_b, expert_w_gate, expert_w_up,
                expert_w_down):
    w = w_in[l]
    w_nat = jnp.concatenate([w[:, :_OFF_DQ], w[:, _OFF_DK:_OFF_DV], w[:, _OFF_CONV:]], axis=1).astype(BF16)
    w_t = jnp.concatenate([w[:, _OFF_DQ:_OFF_DK], w[:, _OFF_DV:_OFF_CONV]], axis=1).T.astype(BF16)
    router = jnp.zeros((D_MODEL, LANES), F32)
    router = router.at[:, :N_EXPERTS].set(router_expert_w[l]).at[:, N_EXPERTS:N_EXPERTS + N_GROUPS].set(
        router_group_w[l])
    r_b = jnp.zeros((1, LANES), F32).at[0, :N_EXPERTS].set(router_expert_b[l]).at[
        0, N_EXPERTS:N_EXPERTS + N_GROUPS].set(router_group_b[l])
    return dict(
        norm1=norm1_w[l][None, :], w_nat=w_nat, w_t=w_t,
        lg=jax.nn.log_sigmoid(ret_decay_logit[l].astype(F32)),
        gn_w=ret_gn_w[l][None, :].astype(F32),
        lam=diff_lambda[l].astype(F32), subln=diff_subln_w[l][:, None].astype(F32),
        conv_w=conv_w[l].astype(F32), w_out=w_out[l].astype(BF16),
        norm2=norm2_w[l][None, :], r_w=router, r_b=r_b,
        wg=expert_w_gate[l].astype(BF16), wu=expert_w_up[l].astype(BF16), wd=expert_w_down[l].astype(BF16),
    )


def _trunk(x, layers, final_w, slopes, pieces):
    B, S, D = x.shape
    tm = _tile(S, 1024)
    tq = _tile(S, 512)
    tk = tq // 2
    t_moe = _tile(B * S, 1024)
    aux = _position_features(S)
    for l, lw in enumerate(layers):
        rqkv, rg, dk, dqvt, cv, kn = _in_proj(x, lw["norm1"], lw["w_nat"], lw["w_t"], tm)
        ret = _retention(lw["lg"], rqkv, rg, lw["gn_w"])
        diff = _diff_attention(slopes, pieces, lw["lam"], lw["subln"], kn, dqvt, dk, aux, l, tq, tk)
        x = _out_proj(x, ret, diff, cv, lw["conv_w"], lw["w_out"], tm)
        last = l == len(layers) - 1
        x = _moe(x.reshape(B * S, D), lw["norm2"], lw["r_w"], lw["r_b"], lw["wg"], lw["wu"],
                 lw["wd"], final_w, last, t_moe).reshape(B, S, D)
    return x


def kernel(x_prompt, x_sample, norm1_w, w_in, ret_decay_logit, ret_gn_w, diff_lambda, diff_subln_w, conv_w, w_out,
           norm2_w, router_group_w, router_group_b, router_expert_w, router_expert_b, expert_w_gate, expert_w_up,
           expert_w_down, final_norm_w):
    depth = w_in.shape[0]
    layers = [
        _prep_layer(l, norm1_w, w_in, ret_decay_logit, ret_gn_w, diff_lambda, diff_subln_w, conv_w, w_out, norm2_w,
                    router_group_w, router_group_b, router_expert_w, router_expert_b, expert_w_gate, expert_w_up,
                    expert_w_down)
        for l in range(depth)
    ]
    final_w = final_norm_w[None, :].astype(F32)
    slopes = (np.float32(2.0) ** (np.float32(-8.0) * np.arange(1, DIFF_HEADS + 1, dtype=np.float32)
                                  / np.float32(DIFF_HEADS))) * np.float32(LOG2E)
    s_hi = slopes.astype(BF16).astype(np.float32)
    s_mid = (slopes - s_hi).astype(BF16).astype(np.float32)
    s_lo = (slopes - s_hi - s_mid).astype(BF16).astype(np.float32)
    pieces = jnp.asarray(np.stack([s_hi, s_mid, s_lo], axis=1).reshape(-1))
    slopes = jnp.asarray(slopes)
    return (_trunk(x_prompt, layers, final_w, slopes, pieces), _trunk(x_sample, layers, final_w, slopes, pieces))
```

```python
import functools
import math

import jax
import jax.numpy as jnp
import numpy as np
from jax import lax
from jax.experimental import pallas as pl
from jax.experimental.pallas import tpu as pltpu

F32 = jnp.float32
BF16 = jnp.bfloat16

D_MODEL = 1024
RET_HEADS = 6
RET_HEAD_DIM = 64
RET_WIDTH = RET_HEADS * RET_HEAD_DIM
DIFF_HEADS = 6
DIFF_QK_DIM = 32
DIFF_V_DIM = 64
DIFF_WIDTH = DIFF_HEADS * DIFF_V_DIM
CONV_CHANNELS = 256
CONV_WIDTH = 3
N_GROUPS = 4
EXPERTS_PER_GROUP = 4
N_EXPERTS = 16
EXPERT_FF = 512
EPS = 1e-6

LANES = 128
HEAD_PAIRS = RET_HEADS // 2
RET_CHUNK = 128
SUBLANES = 8
VMEM_LIMIT = 56 * 1024 * 1024
LOG2E = math.log2(math.e)
NEG_BIG = -1e30

_OFF_RG = 3 * RET_WIDTH
_OFF_DQ = 4 * RET_WIDTH
_OFF_DK = _OFF_DQ + DIFF_WIDTH
_OFF_DV = _OFF_DK + DIFF_WIDTH
_OFF_CONV = _OFF_DV + DIFF_WIDTH


def _params(*sem):
    return pltpu.CompilerParams(dimension_semantics=sem, vmem_limit_bytes=VMEM_LIMIT)


def _split_hi_lo(x):
    hi = x.astype(BF16)
    lo = (x - hi.astype(F32)).astype(BF16)
    return hi, lo


def _in_proj_kernel(x_ref, nw_ref, w_ref, wt_ref, rqkv_ref, rg_ref, dk_ref, dqvt_ref, cv_ref, kn_ref):
    x = x_ref[0]
    ms = jnp.mean(x * x, axis=-1, keepdims=True)
    h = (x * lax.rsqrt(ms + EPS) * nw_ref[...]).astype(BF16)
    a = jnp.dot(h, w_ref[:, 0:_OFF_DQ], preferred_element_type=F32)
    rqkv_ref[0, :, 0:RET_WIDTH] = a[:, 0:RET_WIDTH].astype(BF16)
    rqkv_ref[0, :, RET_WIDTH:2 * RET_WIDTH] = (a[:, RET_WIDTH:2 * RET_WIDTH] * RET_HEAD_DIM ** -0.5).astype(BF16)
    rqkv_ref[0, :, 2 * RET_WIDTH:3 * RET_WIDTH] = a[:, 2 * RET_WIDTH:3 * RET_WIDTH].astype(BF16)
    rg_ref[0] = a[:, _OFF_RG:_OFF_DQ]
    dk = jnp.dot(h, w_ref[:, _OFF_DQ:_OFF_DQ + DIFF_WIDTH], preferred_element_type=F32).astype(BF16)
    dk_ref[0] = dk
    dkf = dk.astype(F32)
    sq_hi, sq_lo = _split_hi_lo(dkf * dkf)
    grp = (lax.shift_right_logical(lax.broadcasted_iota(jnp.int32, (DIFF_WIDTH, LANES), 0),
                                   DIFF_QK_DIM.bit_length() - 1)
           == lax.broadcasted_iota(jnp.int32, (DIFF_WIDTH, LANES), 1))
    grp = jnp.where(grp, 1.0, 0.0).astype(BF16)
    kn2 = jnp.dot(sq_hi, grp, preferred_element_type=F32) + jnp.dot(sq_lo, grp, preferred_element_type=F32)
    kn_ref[0, 0] = jnp.broadcast_to(jnp.max(kn2, axis=0, keepdims=True), (SUBLANES, LANES))
    cv_ref[0] =jnp.dot(h, w_ref[:, _OFF_DQ + DIFF_WIDTH:], preferred_element_type=F32)
    t = lax.dot_general(wt_ref[...], h, (((1,), (1,)), ((), ())), preferred_element_type=F32)
    dqvt_ref[0, 0:DIFF_WIDTH, :] = (t[0:DIFF_WIDTH] * (DIFF_QK_DIM ** -0.5 * LOG2E)).astype(BF16)
    dqvt_ref[0, DIFF_WIDTH:, :] = t[DIFF_WIDTH:].astype(BF16)


def _in_proj(x, nw, w_nat, w_t, tm):
    B, S, D = x.shape
    n_nat = w_nat.shape[1]
    return pl.pallas_call(
        _in_proj_kernel,
        grid=(B, S // tm),
        in_specs=[
            pl.BlockSpec((1, tm, D), lambda b, i: (b, i, 0)),
            pl.BlockSpec((1, D), lambda b, i: (0, 0)),
            pl.BlockSpec((D, n_nat), lambda b, i: (0, 0)),
            pl.BlockSpec((2 * DIFF_WIDTH, D), lambda b, i: (0, 0)),
        ],
        out_specs=[
            pl.BlockSpec((1, tm, 3 * RET_WIDTH), lambda b, i: (b, i, 0)),
            pl.BlockSpec((1, tm, RET_WIDTH), lambda b, i: (b, i, 0)),
            pl.BlockSpec((1, tm, DIFF_WIDTH), lambda b, i: (b, i, 0)),
            pl.BlockSpec((1, 2 * DIFF_WIDTH, tm), lambda b, i: (b, 0, i)),
            pl.BlockSpec((1, tm, 3 * CONV_CHANNELS), lambda b, i: (b, i, 0)),
            pl.BlockSpec((1, 1, SUBLANES, LANES), lambda b, i: (b, i, 0, 0)),
        ],
        out_shape=[
            jax.ShapeDtypeStruct((B, S, 3 * RET_WIDTH), BF16),
            jax.ShapeDtypeStruct((B, S, RET_WIDTH), F32),
            jax.ShapeDtypeStruct((B, S, DIFF_WIDTH), BF16),
            jax.ShapeDtypeStruct((B, 2 * DIFF_WIDTH, S), BF16),
            jax.ShapeDtypeStruct((B, S, 3 * CONV_CHANNELS), F32),
            jax.ShapeDtypeStruct((B, S // tm, SUBLANES, LANES), F32),
        ],
        compiler_params=_params("parallel", "parallel"),
        name="in_proj",
    )(x, nw, w_nat, w_t)


def _pair_lane_value(lg_ref, direction, pair, shape, axis):
    idx = lax.broadcasted_iota(jnp.int32, shape, axis)
    return jnp.where(idx < RET_HEAD_DIM, lg_ref[direction, 2 * pair], lg_ref[direction, 2 * pair + 1])


def _same_head_mask():
    r = lax.broadcasted_iota(jnp.int32, (LANES, LANES), 0)
    c = lax.broadcasted_iota(jnp.int32, (LANES, LANES), 1)
    return (r < RET_HEAD_DIM) == (c < RET_HEAD_DIM)


def _ret_fwd_kernel(lg_ref, qkv_ref, y_ref, state_ref, dmat_ref, qdec_ref, kdec_ref, cdec_ref, *, chunks):
    C = RET_CHUNK

    @pl.when((pl.program_id(0) == 0) & (pl.program_id(1) == 0))
    def _build_tables():
        diff = (lax.broadcasted_iota(jnp.int32, (C, C), 0) - lax.broadcasted_iota(jnp.int32, (C, C), 1)).astype(F32)
        for h in range(RET_HEADS):
            lower = jnp.exp(jnp.maximum(diff, 0.0) * lg_ref[0, h])
            upper = jnp.exp(jnp.maximum(-diff, 0.0) * lg_ref[1, h])
            dmat_ref[h] = jnp.where(diff >= 0, lower, upper)
        pos = lax.broadcasted_iota(jnp.int32, (C, LANES), 0).astype(F32)
        for p in range(HEAD_PAIRS):
            lg_lane = _pair_lane_value(lg_ref, 0, p, (C, LANES), 1)
            qdec_ref[p] = jnp.exp((pos + 1.0) * lg_lane)
            kdec_ref[p] = jnp.exp((C - 1.0 - pos) * lg_lane)
            cdec_ref[p] = jnp.exp(float(C) * _pair_lane_value(lg_ref, 0, p, (LANES, LANES), 0))

    @pl.when(pl.program_id(1) == 0)
    def _reset():
        state_ref[...] = jnp.zeros(state_ref.shape, F32)

    lane = lax.broadcasted_iota(jnp.int32, (C, LANES), 1)
    same_head = _same_head_mask()
    def operands(p, ci):
        rows = slice(ci * C, (ci + 1) * C)
        return (qkv_ref[0, rows, p * LANES:(p + 1) * LANES],
                qkv_ref[0, rows, RET_WIDTH + p * LANES:RET_WIDTH + (p + 1) * LANES],
                qkv_ref[0, rows, 2 * RET_WIDTH + p * LANES:2 * RET_WIDTH + (p + 1) * LANES])

    items = [(p, ci) for p in range(HEAD_PAIRS) for ci in range(chunks)]
    scores, kvs = {}, {}
    for p, ci in items:
        q, k, v = operands(p, ci)
        for h in range(2):
            head_lanes = (lane >= h * RET_HEAD_DIM) & (lane < (h + 1) * RET_HEAD_DIM)
            qm = jnp.where(head_lanes, q, jnp.zeros_like(q))
            scores[p, ci, h] = lax.dot_general(qm, k, (((1,), (1,)), ((), ())), preferred_element_type=F32)
        kd = (k.astype(F32) * kdec_ref[p]).astype(BF16)
        kvs[p, ci] = lax.dot_general(kd, v, (((0,), (0,)), ((), ())), preferred_element_type=F32)
    states = {}
    for p in range(HEAD_PAIRS):
        state = state_ref[p]
        for ci in range(chunks):
            states[p, ci] = state.astype(BF16)
            state = cdec_ref[p] * state + jnp.where(same_head, kvs[p, ci], 0.0)
        state_ref[p] = state
    for p, ci in items:
        q, _, v = operands(p, ci)
        inner = [jnp.dot((scores[p, ci, h] * dmat_ref[2 * p + h]).astype(BF16), v, preferred_element_type=F32)
                 for h in range(2)]
        cross = jnp.dot(q, states[p, ci], preferred_element_type=F32) * qdec_ref[p]
        y_ref[0, ci * C:(ci + 1) * C, p * LANES:(p + 1) * LANES] = (
            jnp.where(lane < RET_HEAD_DIM, inner[0], inner[1]) + cross)


def _ret_bwd_kernel(lg_ref, qkv_ref, y1_ref, g_ref, gnw_ref, o_ref, state_ref, qdec_ref, kdec_ref, cdec_ref, *,
                    chunks):
    C = RET_CHUNK

    @pl.when((pl.program_id(0) == 0) & (pl.program_id(1) == 0))
    def _build_tables():
        pos = lax.broadcasted_iota(jnp.int32, (C, LANES), 0).astype(F32)
        for p in range(HEAD_PAIRS):
            lg_lane = _pair_lane_value(lg_ref, 1, p, (C, LANES), 1)
            qdec_ref[p] = jnp.exp((float(C) - pos) * lg_lane)
            kdec_ref[p] = jnp.exp(pos * lg_lane)
            cdec_ref[p] = jnp.exp(float(C) * _pair_lane_value(lg_ref, 1, p, (LANES, LANES), 0))

    @pl.when(pl.program_id(1) == 0)
    def _reset():
        state_ref[...] = jnp.zeros(state_ref.shape, F32)

    same_head = _same_head_mask()
    head_avg = jnp.where(same_head, 1.0 / RET_HEAD_DIM, 0.0).astype(BF16)

    def head_mean(t):
        hi, lo = _split_hi_lo(t)
        return (jnp.dot(hi, head_avg, preferred_element_type=F32)
                + jnp.dot(lo, head_avg, preferred_element_type=F32))

    items = [(p, ci) for p in range(HEAD_PAIRS) for ci in range(chunks)]
    kvs = {}
    for p, ci in items:
        rows = slice(ci * C, (ci + 1) * C)
        k = qkv_ref[0, rows, RET_WIDTH + p * LANES:RET_WIDTH + (p + 1) * LANES]
        v = qkv_ref[0, rows, 2 * RET_WIDTH + p * LANES:2 * RET_WIDTH + (p + 1) * LANES]
        kd = (k.astype(F32) * kdec_ref[p]).astype(BF16)
        kvs[p, ci] = lax.dot_general(kd, v, (((0,), (0,)), ((), ())), preferred_element_type=F32)
    states = {}
    for p in range(HEAD_PAIRS):
        state = state_ref[p]
        for ci in reversed(range(chunks)):
            states[p, ci] = state.astype(BF16)
            state = cdec_ref[p] * state + jnp.where(same_head, kvs[p, ci], 0.0)
        state_ref[p] = state
    ys = {}
    for p, ci in items:
        rows = slice(ci * C, (ci + 1) * C)
        sl = slice(p * LANES, (p + 1) * LANES)
        cross = jnp.dot(qkv_ref[0, rows, sl], states[p, ci], preferred_element_type=F32)
        ys[p, ci] = y1_ref[0, rows, sl] + cross * qdec_ref[p]
    means = {key: head_mean(y) for key, y in ys.items()}
    devs = {key: ys[key] - means[key] for key in ys}
    variances = {key: head_mean(d * d) for key, d in devs.items()}
    for p, ci in items:
        rows = slice(ci * C, (ci + 1) * C)
        sl = slice(p * LANES, (p + 1) * LANES)
        g = g_ref[0, rows, sl]
        o = devs[p, ci] * lax.rsqrt(variances[p, ci] + EPS) * gnw_ref[:, sl] * (g * jax.nn.sigmoid(g))
        o_ref[0, rows, sl] = o.astype(o_ref.dtype)


def _retention(lg, rqkv, rg, gn_w):
    B, S, _ = rqkv.shape
    chunks = math.gcd(S // RET_CHUNK, 8)
    C = chunks * RET_CHUNK
    n = S // C
    smem = pl.BlockSpec(memory_space=pltpu.SMEM)
    table = pltpu.VMEM((HEAD_PAIRS, RET_CHUNK, LANES), F32)
    state = pltpu.VMEM((HEAD_PAIRS, LANES, LANES), F32)
    y1 = pl.pallas_call(
        functools.partial(_ret_fwd_kernel, chunks=chunks),
        grid=(B, n),
        in_specs=[smem, pl.BlockSpec((1, C, 3 * RET_WIDTH), lambda b, i: (b, i, 0))],
        out_specs=pl.BlockSpec((1, C, RET_WIDTH), lambda b, i: (b, i, 0)),
        out_shape=jax.ShapeDtypeStruct((B, S, RET_WIDTH), F32),
        scratch_shapes=[state, pltpu.VMEM((RET_HEADS, RET_CHUNK, RET_CHUNK), F32), table, table, state],
        compiler_params=_params("arbitrary", "arbitrary"),
        name="ret_fwd",
    )(lg, rqkv)
    rev = lambda b, i: (b, n - 1 - i, 0)
    return pl.pallas_call(
        functools.partial(_ret_bwd_kernel, chunks=chunks),
        grid=(B, n),
        in_specs=[
            smem,
            pl.BlockSpec((1, C, 3 * RET_WIDTH), rev),
            pl.BlockSpec((1, C, RET_WIDTH), rev),
            pl.BlockSpec((1, C, RET_WIDTH), rev),
            pl.BlockSpec((1, RET_WIDTH), lambda b, i: (0, 0)),
        ],
        out_specs=pl.BlockSpec((1, C, RET_WIDTH), rev),
        out_shape=jax.ShapeDtypeStruct((B, S, RET_WIDTH), BF16),
        scratch_shapes=[state, table, table, state],
        compiler_params=_params("arbitrary", "arbitrary"),
        name="ret_bwd",
    )(lg, rqkv, y1, rg, gn_w)


AUX_BLOCK = 256
AUX_ROWS = 16
V_ROWS = DIFF_V_DIM + SUBLANES
GUARD_LOG2 = 60.0
BOUND_SLACK = 1.01
LONG_BODY_PAIRS = 8
UNDERFLOW_LOG2 = 150.0


def _position_features(seq):
    j = jnp.arange(seq, dtype=jnp.int32)
    ones = jnp.ones((seq,), F32)
    n = (j // AUX_BLOCK).astype(F32)
    jc = (j % AUX_BLOCK - AUX_BLOCK // 2).astype(F32)
    feat = jnp.stack([ones] * 3 + [n] * 3 + [jc] * 3, axis=1)
    return jnp.pad(feat, ((0, 0), (0, LANES - feat.shape[1]))).astype(BF16)


def _split3(x):
    hi = x.astype(BF16).astype(F32)
    r = x - hi
    mid = r.astype(BF16).astype(F32)
    return hi, mid, (r - mid).astype(BF16).astype(F32)


def _attn_kernel(slopes_ref, pieces_ref, lam_ref, sw_ref, kn_ref, qt_ref, k_ref, aux_ref, vt_ref, o_ref,
                 m_ref, acc_ref, qm_ref, aug_ref, pa_ref, pb_ref, dist_ref, *, seq, tq, tk, lam_init):
    pair = pl.program_id(1)
    qi = pl.program_id(2)
    q0 = qi * tq
    nk = seq // tk
    per = tq // tk
    kd0 = qi * per
    qt = qt_ref[0]
    row = lax.broadcasted_iota(jnp.int32, (LANES, tq), 0)
    q_masked = []
    for hc in range(4):
        lo = hc * DIFF_QK_DIM
        q_masked.append(jnp.where((row >= lo) & (row < lo + DIFF_QK_DIM), qt, jnp.zeros_like(qt)))

    @pl.when(qi == 0)
    def _distance_table():
        d = lax.broadcasted_iota(jnp.int32, (tq, tq), 1) - lax.broadcasted_iota(jnp.int32, (tq, tq), 0)
        dist_ref[...] = jnp.abs(d).astype(F32)

    kd = pl.multiple_of(q0, tq)
    kdiag = k_ref[0, pl.ds(kd, tq), :]
    self_prod = qt.astype(F32) * kdiag.astype(F32).T
    shifts = [jnp.sum(self_prod[hc * DIFF_QK_DIM:(hc + 1) * DIFF_QK_DIM], axis=0, keepdims=True) for hc in range(4)]

    kn2 = jnp.max(jnp.max(kn_ref[0], axis=0), axis=0, keepdims=True)
    kn_lane = lax.broadcasted_iota(jnp.int32, (1, LANES), 1)
    excess = []
    for h in range(2):
        worst = None
        for c in range(2):
            hc = 2 * h + c
            qf = q_masked[hc].astype(F32)
            qn2 = jnp.sum(qf * qf, axis=0, keepdims=True)
            kn2_hc = jnp.max(jnp.where(kn_lane == 4 * pair + hc, kn2, 0.0), axis=-1, keepdims=True)
            bound = jnp.sqrt(qn2 * kn2_hc) * BOUND_SLACK + 1e-3
            e = jnp.max(bound - shifts[hc], axis=-1, keepdims=True)
            worst = e if worst is None else jnp.maximum(worst, e)
        excess.append(worst)
    single_pass = jnp.where(jnp.maximum(excess[0], excess[1]) <= GUARD_LOG2, 1, 0).astype(jnp.int32)[0, 0] == 1

    def v_rows_with_ones(h, k0, n):
        vt = vt_ref[0, h * DIFF_V_DIM:(h + 1) * DIFF_V_DIM, pl.ds(k0, n)]
        return jnp.concatenate([vt, jnp.ones((SUBLANES, n), BF16)], axis=0)

    def keys_with_features(k0):
        return jnp.concatenate([k_ref[0, pl.ds(k0, tk), :], aux_ref[pl.ds(k0, tk), :]], axis=1)

    def queries_with_features(side, hc):
        return jnp.concatenate([qm_ref[hc], aug_ref[side * 4 + hc],
                                jnp.zeros((LANES - AUX_ROWS, tq), BF16)], axis=0)

    @pl.when(single_pass)
    def _single_pass():
        ipos = (q0 + lax.broadcasted_iota(jnp.int32, (1, tq), 1)).astype(F32)
        aug_row = lax.broadcasted_iota(jnp.int32, (AUX_ROWS, tq), 0)
        q0f = q0.astype(F32)
        lo_even, n_lo, totals = [], [], []
        for h in range(2):
            slope = slopes_ref[2 * pair + h]
            pieces = [pieces_ref[(2 * pair + h) * 3 + i] for i in range(3)]
            for c in range(2):
                hc = 2 * h + c
                for side, sign in ((0, 1.0), (1, -1.0), (2, 0.0)):
                    const = (-sign) * (slope * ipos) - shifts[hc] + sign * (0.5 * AUX_BLOCK) * slope
                    c_hi, c_mid, c_lo = _split3(const)
                    aug = jnp.where(aug_row == 0, c_hi,
                                    jnp.where(aug_row == 1, c_mid, jnp.where(aug_row == 2, c_lo, 0.0)))
                    for i in range(3):
                        aug = jnp.where(aug_row == 3 + i, sign * AUX_BLOCK * pieces[i], aug)
                        aug = jnp.where(aug_row == 6 + i, sign * pieces[i], aug)
                    aug_ref[side * 4 + hc] = aug.astype(BF16)
                qm_ref[hc] = q_masked[hc]
            reach = (excess[h] + UNDERFLOW_LOG2) / slope
            lo_blk = jnp.clip(jnp.floor((q0f - (tk - 1.0) - reach) / tk), 0.0, float(nk))
            hi_blk = jnp.clip(jnp.floor((reach + q0f + (tq - 1.0)) / tk) + 1.0, 0.0, float(nk))
            lo_blk = jnp.minimum(lo_blk.astype(jnp.int32)[0, 0], kd0)
            hi_blk = jnp.maximum(hi_blk.astype(jnp.int32)[0, 0], kd0 + per)
            odd = (kd0 - lo_blk + hi_blk - kd0 - per) % 2
            lo_h = jnp.where((odd == 1) & (lo_blk > 0), lo_blk - 1, lo_blk)
            hi_h = jnp.where((odd == 1) & (lo_blk == 0), hi_blk + 1, hi_blk)
            lo_even.append(lo_h)
            n_lo.append(kd0 - lo_h)
            totals.append(kd0 - lo_h + hi_h - kd0 - per)
        pairs_h0 = totals[0] // 2
        n_pairs = pairs_h0 + totals[1] // 2

        def locate(x, half):
            head = jnp.where(x >= pairs_h0, 1, 0)
            i = 2 * (x - head * pairs_h0) + half
            before = jnp.where(head == 1, n_lo[1], n_lo[0])
            first = jnp.where(head == 1, lo_even[1], lo_even[0])
            blk = jnp.where(i < before, first + i, kd0 + per + i - before)
            return head, blk, jnp.where(i < before, 0, 1)

        def stage_a(x, p_ref):
            for half in range(2):
                head, blk, side = locate(x, half)
                lhs = keys_with_features(pl.multiple_of(blk * tk, tk))
                for c in range(2):
                    t = jnp.dot(lhs, queries_with_features(side, 2 * head + c), preferred_element_type=F32)
                    p_ref[2 * half + c] = jnp.exp2(t).astype(BF16)

        def stage_b(x, p_ref):
            head, blk_a, _ = locate(x, 0)
            _, blk_b, _ = locate(x, 1)
            rows = pl.ds(pl.multiple_of(head * DIFF_V_DIM, DIFF_V_DIM), DIFF_V_DIM)
            ones = jnp.ones((SUBLANES, tk), BF16)
            va = jnp.concatenate([vt_ref[0, rows, pl.ds(pl.multiple_of(blk_a * tk, tk), tk)], ones], axis=0)
            vb = jnp.concatenate([vt_ref[0, rows, pl.ds(pl.multiple_of(blk_b * tk, tk), tk)], ones], axis=0)
            for c in range(2):
                acc_ref[2 * head + c] += (jnp.dot(va, p_ref[c], preferred_element_type=F32)
                                          + jnp.dot(vb, p_ref[2 + c], preferred_element_type=F32))

        def diag_a(h, p_ref):
            for half in range(2):
                lhs = keys_with_features(pl.multiple_of(q0 + half * tk, tk))
                bias = dist_ref[half * tk:(half + 1) * tk, :] * slopes_ref[2 * pair + h]
                for c in range(2):
                    t = jnp.dot(lhs, queries_with_features(2, 2 * h + c), preferred_element_type=F32) - bias
                    p_ref[2 * half + c] = jnp.exp2(t).astype(BF16)

        def diag_b(h, p_ref):
            va = v_rows_with_ones(h, pl.multiple_of(q0, tk), tk)
            vb = v_rows_with_ones(h, pl.multiple_of(q0 + tk, tk), tk)
            for c in range(2):
                acc_ref[2 * h + c] = (jnp.dot(va, p_ref[c], preferred_element_type=F32)
                                      + jnp.dot(vb, p_ref[2 + c], preferred_element_type=F32))

        diag_a(0, pa_ref)
        diag_a(1, pb_ref)
        diag_b(0, pa_ref)

        @pl.when(n_pairs == 0)
        def _diagonal_only():
            diag_b(1, pb_ref)

        @pl.when(n_pairs > 0)
        def _blocks():
            stage_a(0, pa_ref)
            diag_b(1, pb_ref)

            def pairs_body(count):
                def body(j, carry):
                    for s in range(0, count, 2):
                        stage_a(count * j + s + 1, pb_ref)
                        stage_b(count * j + s, pa_ref)
                        stage_a(count * j + s + 2, pa_ref)
                        stage_b(count * j + s + 1, pb_ref)
                    return carry
                return body

            long_iters = (n_pairs - 1) // LONG_BODY_PAIRS
            lax.fori_loop(0, long_iters, pairs_body(LONG_BODY_PAIRS), 0)
            lax.fori_loop(LONG_BODY_PAIRS // 2 * long_iters, (n_pairs - 1) // 2, pairs_body(2), 0)

            @pl.when(n_pairs % 2 == 1)
            def _last_one():
                stage_b(n_pairs - 1, pa_ref)

            @pl.when(n_pairs % 2 == 0)
            def _last_two():
                stage_a(n_pairs - 1, pb_ref)
                stage_b(n_pairs - 2, pa_ref)
                stage_b(n_pairs - 1, pb_ref)

    @pl.when(jnp.logical_not(single_pass))
    def _online():
        m_ref[...] = jnp.full(m_ref.shape, NEG_BIG, F32)
        acc_ref[...] = jnp.zeros(acc_ref.shape, F32)
        rel = (lax.broadcasted_iota(jnp.int32, (tk, tq), 1)
               - lax.broadcasted_iota(jnp.int32, (tk, tq), 0)).astype(F32)
        for h in range(2):
            def online_step(kb, carry, h=h):
                k0 = pl.multiple_of(kb * tk, tk)
                kblk = k_ref[0, pl.ds(k0, tk), :]
                v_ones = v_rows_with_ones(h, k0, tk)
                bias = jnp.abs(rel + (q0 - k0).astype(F32)) * slopes_ref[2 * pair + h]
                for c in range(2):
                    hc = 2 * h + c
                    s = jnp.dot(kblk, q_masked[hc], preferred_element_type=F32) - bias
                    m_old = m_ref[hc]
                    m_new = jnp.maximum(m_old, jnp.max(s, axis=0, keepdims=True))
                    p = jnp.exp2(s - m_new).astype(BF16)
                    acc_ref[hc] = (jnp.exp2(m_old - m_new) * acc_ref[hc]
                                   + jnp.dot(v_ones, p, preferred_element_type=F32))
                    m_ref[hc] = m_new
                return carry

            lax.fori_loop(0, nk, online_step, 0)

    lam = lam_ref[...]
    lam_full = (jnp.exp(jnp.sum(lam[0:1] * lam[1:2], axis=-1, keepdims=True))
                - jnp.exp(jnp.sum(lam[2:3] * lam[3:4], axis=-1, keepdims=True)) + lam_init)
    outs = []
    for h in range(2):
        a1 = acc_ref[2 * h]
        a2 = acc_ref[2 * h + 1]
        rows = slice(0, DIFF_V_DIM)
        ones_row = slice(DIFF_V_DIM, DIFF_V_DIM + 1)
        o = a1[rows] / a1[ones_row] - lam_full * (a2[rows] / a2[ones_row])
        o = o * lax.rsqrt(jnp.mean(o * o, axis=0, keepdims=True) + EPS)
        outs.append(o * sw_ref[...] * (1.0 - lam_init))
    o_ref[0] = jnp.concatenate(outs, axis=0).T.astype(o_ref.dtype)


def _diff_attention(slopes, pieces, lam, subln_w, kn, dqvt, dk, aux, layer_idx, tq, tk):
    B, S, _ = dk.shape
    assert tq == 2 * tk and tk <= AUX_BLOCK and S // AUX_BLOCK <= AUX_BLOCK
    lam_init = 0.8 - 0.6 * math.exp(-0.3 * layer_idx)
    kern = functools.partial(_attn_kernel, seq=S, tq=tq, tk=tk, lam_init=lam_init)
    v_block0 = DIFF_WIDTH // LANES
    smem = pl.BlockSpec(memory_space=pltpu.SMEM)
    return pl.pallas_call(
        kern,
        grid=(B, HEAD_PAIRS, S // tq),
        in_specs=[
            smem, smem,
            pl.BlockSpec((4, DIFF_QK_DIM), lambda b, j, i: (0, 0)),
            pl.BlockSpec((DIFF_V_DIM, 1), lambda b, j, i: (0, 0)),
            pl.BlockSpec((1,) + kn.shape[1:], lambda b, j, i: (b, 0, 0, 0)),
            pl.BlockSpec((1, LANES, tq), lambda b, j, i: (b, j, i)),
            pl.BlockSpec((1, S, LANES), lambda b, j, i: (b, 0, j)),
            pl.BlockSpec((S, LANES), lambda b, j, i: (0, 0)),
            pl.BlockSpec((1, LANES, S), lambda b, j, i: (b, v_block0 + j, 0)),
        ],
        out_specs=pl.BlockSpec((1, tq, LANES), lambda b, j, i: (b, i, j)),
        out_shape=jax.ShapeDtypeStruct((B, S, DIFF_WIDTH), BF16),
        scratch_shapes=[pltpu.VMEM((4, 1, tq), F32), pltpu.VMEM((4, V_ROWS, tq), F32),
                        pltpu.VMEM((4, LANES, tq), BF16), pltpu.VMEM((12, AUX_ROWS, tq), BF16),
                        pltpu.VMEM((4, tk, tq), BF16), pltpu.VMEM((4, tk, tq), BF16),
                        pltpu.VMEM((tq, tq), F32)],
        compiler_params=_params("parallel", "parallel", "arbitrary"),
        name="diff_attn",
    )(slopes, pieces, lam, subln_w, kn, dqvt, dk, aux, dqvt)


def _out_proj_kernel(x_ref, ret_ref, diff_ref, cv_ref, prev_ref, next_ref, cw_ref, w_ref, o_ref, *, tm):
    i = pl.program_id(1)
    last = pl.num_programs(1) - 1
    cc = CONV_CHANNELS
    cv = cv_ref[0]
    u = cv[:, 2 * cc:3 * cc] * cv[:, 0:cc]
    prev = prev_ref[0]
    nxt = next_ref[0]
    u_prev = prev[SUBLANES - 1:SUBLANES, 2 * cc:3 * cc] * prev[SUBLANES - 1:SUBLANES, 0:cc]
    u_next = nxt[0:1, 2 * cc:3 * cc] * nxt[0:1, 0:cc]
    u_prev = jnp.where(i == 0, 0.0, u_prev)
    u_next = jnp.where(i == last, 0.0, u_next)
    rows = lax.broadcasted_iota(jnp.int32, (tm, cc), 0)
    u_m1 = jnp.where(rows == 0, u_prev, pltpu.roll(u, 1, 0))
    u_p1 = jnp.where(rows == tm - 1, u_next, pltpu.roll(u, tm - 1, 0))
    y = cw_ref[0:1] * u_m1 + cw_ref[1:2] * u + cw_ref[2:3] * u_p1
    conv = (cv[:, cc:2 * cc] * y).astype(BF16)
    acc = jnp.dot(ret_ref[0], w_ref[0:RET_WIDTH], preferred_element_type=F32)
    acc += jnp.dot(diff_ref[0], w_ref[RET_WIDTH:RET_WIDTH + DIFF_WIDTH], preferred_element_type=F32)
    acc += jnp.dot(conv, w_ref[RET_WIDTH + DIFF_WIDTH:], preferred_element_type=F32)
    o_ref[0] = x_ref[0] + acc


def _out_proj(x, ret, diff, cv, conv_w, w_out, tm):
    B, S, D = x.shape
    per = tm // SUBLANES
    nblk8 = S // SUBLANES
    tile = lambda width: pl.BlockSpec((1, tm, width), lambda b, i: (b, i, 0))
    return pl.pallas_call(
        functools.partial(_out_proj_kernel, tm=tm),
        grid=(B, S // tm),
        in_specs=[
            tile(D), tile(RET_WIDTH), tile(DIFF_WIDTH), tile(3 * CONV_CHANNELS),
            pl.BlockSpec((1, SUBLANES, 3 * CONV_CHANNELS), lambda b, i: (b, jnp.maximum(i * per - 1, 0), 0)),
            pl.BlockSpec((1, SUBLANES, 3 * CONV_CHANNELS),
                         lambda b, i: (b, jnp.minimum((i + 1) * per, nblk8 - 1), 0)),
            pl.BlockSpec((CONV_WIDTH, CONV_CHANNELS), lambda b, i: (0, 0)),
            pl.BlockSpec((D, D), lambda b, i: (0, 0)),
        ],
        out_specs=tile(D),
        out_shape=jax.ShapeDtypeStruct((B, S, D), F32),
        compiler_params=_params("parallel", "parallel"),
        name="out_proj",
    )(x, ret, diff, cv, cv, cv, conv_w, w_out)


_GROUP_LANE0 = N_EXPERTS
MOE_ROW_BLOCK = 256
MOE_LONG_BLOCK = 288
MOE_ROW_ALIGN = math.gcd(MOE_ROW_BLOCK, MOE_LONG_BLOCK)
MOE_EXPERTS_PER_STEP = 2
assert EXPERTS_PER_GROUP % MOE_EXPERTS_PER_STEP == 0


def _moe_sorted_rows(tm):
    assert MOE_ROW_BLOCK < MOE_LONG_BLOCK <= 2 * MOE_ROW_BLOCK
    return (tm + N_GROUPS * (MOE_ROW_BLOCK - 1)) // MOE_ROW_BLOCK * MOE_ROW_BLOCK


def _router_gates(logits):
    tm = logits.shape[0]
    lt = logits.T
    big = jnp.int32(LANES)
    grow = lax.broadcasted_iota(jnp.int32, (SUBLANES, tm), 0)
    gl = jnp.where(grow < N_GROUPS, lt[_GROUP_LANE0:_GROUP_LANE0 + SUBLANES], -jnp.inf)
    g_max = jnp.max(gl, axis=0, keepdims=True)
    g_idx = jnp.min(jnp.where(gl == g_max, grow, big), axis=0, keepdims=True)
    g_w = 1.0 / jnp.sum(jnp.exp(gl - g_max), axis=0, keepdims=True)
    erow = lax.broadcasted_iota(jnp.int32, (N_EXPERTS, tm), 0)
    in_group = lax.shift_right_logical(erow, EXPERTS_PER_GROUP.bit_length() - 1) == g_idx
    el = jnp.where(in_group, lt[0:N_EXPERTS], -jnp.inf)
    e1 = jnp.max(el, axis=0, keepdims=True)
    i1 = jnp.min(jnp.where(el == e1, erow, big), axis=0, keepdims=True)
    el2 = jnp.where(erow == i1, -jnp.inf, el)
    e2 = jnp.max(el2, axis=0, keepdims=True)
    i2 = jnp.min(jnp.where(el2 == e2, erow, big), axis=0, keepdims=True)
    r = jnp.exp(e2 - e1)
    p1 = g_w / (1.0 + r)
    p2 = g_w * r / (1.0 + r)
    gates_t = jnp.where(erow == i1, p1, jnp.where(erow == i2, p2, 0.0))
    member_t = jnp.where(grow == g_idx, 1.0, 0.0)
    rest = jnp.zeros((LANES - N_EXPERTS - SUBLANES, tm), F32)
    return jnp.concatenate([gates_t, member_t, rest], axis=0).T


def _moe_kernel(x_ref, nw_ref, rw_ref, rb_ref, wg_ref, wu_ref, wd_ref, fw_ref, o_ref,
                xs_ref, gs_ref, ys_ref, dest_ref, ltri_ref, seg_ref, *, final_norm, tm):
    tile = pl.program_id(0)
    e = pl.program_id(1)
    sorted_rows = _moe_sorted_rows(tm)

    @pl.when((tile == 0) & (e == 0))
    def _strictly_lower():
        r = lax.broadcasted_iota(jnp.int32, (tm, tm), 0)
        c = lax.broadcasted_iota(jnp.int32, (tm, tm), 1)
        ltri_ref[...] = jnp.where(c < r, 1.0, 0.0).astype(BF16)

    @pl.when(e == 0)
    def _route_and_sort():
        x = x_ref[...]
        xn = x * lax.rsqrt(jnp.mean(x * x, axis=-1, keepdims=True) + EPS) * nw_ref[...]
        hi, lo = _split_hi_lo(xn)
        r_hi, r_lo = _split_hi_lo(rw_ref[...])
        both = jnp.dot(hi, jnp.concatenate([r_hi, r_lo], axis=1), preferred_element_type=F32)
        logits = (both[:, :LANES] + jnp.dot(lo, r_hi, preferred_element_type=F32) + both[:, LANES:]) + rb_ref[...]
        routed = _router_gates(logits)
        lane = lax.broadcasted_iota(jnp.int32, (tm, LANES), 1)
        gates = jnp.where(lane < N_EXPERTS, routed, 0.0)
        member = jnp.where(lane >= _GROUP_LANE0, routed, 0.0)
        earlier = jnp.dot(ltri_ref[...], member.astype(BF16), preferred_element_type=F32)
        rank = jnp.sum(member * earlier, axis=-1, keepdims=True)
        count = jnp.sum(member, axis=0, keepdims=True)
        seg = jnp.floor((count + (MOE_ROW_BLOCK - 1.0)) * (1.0 / MOE_ROW_BLOCK)) * MOE_ROW_BLOCK
        long_one = (count > MOE_ROW_BLOCK) & (count <= MOE_LONG_BLOCK)
        seg = jnp.where(long_one, float(MOE_LONG_BLOCK), seg)
        lane1 = lax.broadcasted_iota(jnp.int32, (1, LANES), 1)
        g0 = _GROUP_LANE0
        s0, s1, s2 = seg[:, g0:g0 + 1], seg[:, g0 + 1:g0 + 2], seg[:, g0 + 2:g0 + 3]
        start = jnp.where(lane1 == g0 + 1, s0,
                          jnp.where(lane1 == g0 + 2, s0 + s1, jnp.where(lane1 == g0 + 3, s0 + s1 + s2, 0.0)))
        dest = rank + jnp.sum(member * start, axis=-1, keepdims=True)
        dest_ref[...] = jnp.broadcast_to(dest, (tm, LANES))
        dest_row = dest_ref[...].T[0:1, :]
        row_id = lax.broadcasted_iota(jnp.int32, (sorted_rows, tm), 0).astype(F32)
        perm = jnp.where(row_id == dest_row, 1.0, 0.0).astype(BF16)
        g_hi, g_mid, g_lo = _split3(gates)
        g_pieces = g_hi + pltpu.roll(g_mid, N_EXPERTS, 1) + pltpu.roll(g_lo, 2 * N_EXPERTS, 1)
        moved = jnp.dot(perm, jnp.concatenate([hi, g_pieces.astype(BF16)], axis=1), preferred_element_type=F32)
        xs_ref[...] = moved[:, :D_MODEL].astype(BF16)
        g_moved = moved[:, D_MODEL:]
        gs_ref[...] = (g_moved + pltpu.roll(g_moved, LANES - N_EXPERTS, 1)
                       + pltpu.roll(g_moved, LANES - 2 * N_EXPERTS, 1))
        ys_ref[...] = jnp.zeros(ys_ref.shape, F32)
        start_i = start.astype(jnp.int32)
        blocks_i = jnp.where(long_one, 0.0, seg * (1.0 / MOE_ROW_BLOCK)).astype(jnp.int32)
        long_i = jnp.where(long_one, 1, 0).astype(jnp.int32)
        for g in range(N_GROUPS):
            seg_ref[g] = start_i[0, _GROUP_LANE0 + g]
            seg_ref[N_GROUPS + g] = blocks_i[0, _GROUP_LANE0 + g]
            seg_ref[2 * N_GROUPS + g] = long_i[0, _GROUP_LANE0 + g]

    group = (e * MOE_EXPERTS_PER_STEP) // EXPERTS_PER_GROUP
    first_row = seg_ref[group]

    def expert_rows(r0, rows):
        r0 = pl.multiple_of(r0, MOE_ROW_ALIGN)
        xb = xs_ref[pl.ds(r0, rows), :]
        gsb = gs_ref[pl.ds(r0, rows), :]
        lane = lax.broadcasted_iota(jnp.int32, gsb.shape, 1)
        experts = range(MOE_EXPERTS_PER_STEP)
        gate = [jnp.dot(xb, wg_ref[j], preferred_element_type=F32) for j in experts]
        up = [jnp.dot(xb, wu_ref[j], preferred_element_type=F32) for j in experts]
        hmid = [(gate[j] * jax.nn.sigmoid(gate[j]) * up[j]).astype(BF16) for j in experts]
        down = [jnp.dot(hmid[j], wd_ref[j], preferred_element_type=F32) for j in experts]
        total = None
        for j in experts:
            g_e = jnp.sum(jnp.where(lane == e * MOE_EXPERTS_PER_STEP + j, gsb, 0.0), axis=-1, keepdims=True)
            total = g_e * down[j] if total is None else total + g_e * down[j]
        ys_ref[pl.ds(r0, rows), :] += total

    def row_block(b, carry):
        expert_rows(first_row + b * MOE_ROW_BLOCK, MOE_ROW_BLOCK)
        return carry

    lax.fori_loop(0, seg_ref[N_GROUPS + group], row_block, 0)

    @pl.when(seg_ref[2 * N_GROUPS + group] == 1)
    def _one_long_block():
        expert_rows(first_row, MOE_LONG_BLOCK)

    @pl.when(e == pl.num_programs(1) - 1)
    def _finish():
        col_id = lax.broadcasted_iota(jnp.int32, (tm, sorted_rows), 1).astype(F32)
        unperm = jnp.where(col_id == dest_ref[:, 0:1], 1.0, 0.0).astype(BF16)
        y = x_ref[...] + jnp.dot(unperm, ys_ref[...].astype(BF16), preferred_element_type=F32)
        if final_norm:
            y = y * lax.rsqrt(jnp.mean(y * y, axis=-1, keepdims=True) + EPS) * fw_ref[...]
        o_ref[...] = y


def _moe(x, nw, r_w, r_b, wg, wu, wd, final_w, final_norm, tm):
    T, D = x.shape
    const = lambda shape: pl.BlockSpec(shape, lambda i, e: (0,) * len(shape))
    sorted_rows = _moe_sorted_rows(tm)
    return pl.pallas_call(
        functools.partial(_moe_kernel, final_norm=final_norm, tm=tm),
        grid=(T // tm, N_EXPERTS // MOE_EXPERTS_PER_STEP),
        in_specs=[
            pl.BlockSpec((tm, D), lambda i, e: (i, 0)),
            const((1, D)), const((D, LANES)), const((1, LANES)),
            pl.BlockSpec((MOE_EXPERTS_PER_STEP, D, EXPERT_FF), lambda i, e: (e, 0, 0)),
            pl.BlockSpec((MOE_EXPERTS_PER_STEP, D, EXPERT_FF), lambda i, e: (e, 0, 0)),
            pl.BlockSpec((MOE_EXPERTS_PER_STEP, EXPERT_FF, D), lambda i, e: (e, 0, 0)),
            const((1, D)),
        ],
        out_specs=pl.BlockSpec((tm, D), lambda i, e: (i, 0)),
        out_shape=jax.ShapeDtypeStruct((T, D), F32),
        scratch_shapes=[pltpu.VMEM((sorted_rows, D), BF16), pltpu.VMEM((sorted_rows, LANES), F32),
                        pltpu.VMEM((sorted_rows, D), F32), pltpu.VMEM((tm, LANES), F32),
                        pltpu.VMEM((tm, tm), BF16), pltpu.SMEM((3 * N_GROUPS,), jnp.int32)],
        compiler_params=_params("arbitrary", "arbitrary"),
        name="moe",
    )(x, nw, r_w, r_b, wg, wu, wd, final_w)


def _tile(n, pref):
    t = min(n, pref)
    assert n % t == 0, (n, t)
    return t


def _prep_layer(l, norm1_w, w_in, ret_decay_logit, ret_gn_w, diff_lambda, diff_subln_w, conv_w, w_out, norm2_w,
                router_group_w, router_group_b, router_expert_w, router_expert_b, expert_w_gate, expert_w_up,
                expert_w_down):
    w = w_in[l]
    w_nat = jnp.concatenate([w[:, :_OFF_DQ], w[:, _OFF_DK:_OFF_DV], w[:, _OFF_CONV:]], axis=1).astype(BF16)
    w_t = jnp.concatenate([w[:, _OFF_DQ:_OFF_DK], w[:, _OFF_DV:_OFF_CONV]], axis=1).T.astype(BF16)
    router = jnp.zeros((D_MODEL, LANES), F32)
    router = router.at[:, :N_EXPERTS].set(router_expert_w[l]).at[:, N_EXPERTS:N_EXPERTS + N_GROUPS].set(
        router_group_w[l])
    r_b = jnp.zeros((1, LANES), F32).at[0, :N_EXPERTS].set(router_expert_b[l]).at[
        0, N_EXPERTS:N_EXPERTS + N_GROUPS].set(router_group_b[l])
    return dict(
        norm1=norm1_w[l][None, :], w_nat=w_nat, w_t=w_t,
        lg=jax.nn.log_sigmoid(ret_decay_logit[l].astype(F32)),
        gn_w=ret_gn_w[l][None, :].astype(F32),
        lam=diff_lambda[l].astype(F32), subln=diff_subln_w[l][:, None].astype(F32),
        conv_w=conv_w[l].astype(F32), w_out=w_out[l].astype(BF16),
        norm2=norm2_w[l][None, :], r_w=router, r_b=r_b,
        wg=expert_w_gate[l].astype(BF16), wu=expert_w_up[l].astype(BF16), wd=expert_w_down[l].astype(BF16),
    )


def _trunk(x, layers, final_w, slopes, pieces):
    B, S, D = x.shape
    tm = _tile(S, 1024)
    tq = _tile(S, 512)
    tk = tq // 2
    t_moe = _tile(B * S, 1024)
    aux = _position_features(S)
    for l, lw in enumerate(layers):
        rqkv, rg, dk, dqvt, cv, kn = _in_proj(x, lw["norm1"], lw["w_nat"], lw["w_t"], tm)
        ret = _retention(lw["lg"], rqkv, rg, lw["gn_w"])
        diff = _diff_attention(slopes, pieces, lw["lam"], lw["subln"], kn, dqvt, dk, aux, l, tq, tk)
        x = _out_proj(x, ret, diff, cv, lw["conv_w"], lw["w_out"], tm)
        last = l == len(layers) - 1
        x = _moe(x.reshape(B * S, D), lw["norm2"], lw["r_w"], lw["r_b"], lw["wg"], lw["wu"],
                 lw["wd"], final_w, last, t_moe).reshape(B, S, D)
    return x


def kernel(x_prompt, x_sample, norm1_w, w_in, ret_decay_logit, ret_gn_w, diff_lambda, diff_subln_w, conv_w, w_out,
           norm2_w, router_group_w, router_group_b, router_expert_w, router_expert_b, expert_w_gate, expert_w_up,
           expert_w_down, final_norm_w):
    depth = w_in.shape[0]
    layers = [
        _prep_layer(l, norm1_w, w_in, ret_decay_logit, ret_gn_w, diff_lambda, diff_subln_w, conv_w, w_out, norm2_w,
                    router_group_w, router_group_b, router_expert_w, router_expert_b, expert_w_gate, expert_w_up,
                    expert_w_down)
        for l in range(depth)
    ]
    final_w = final_norm_w[None, :].astype(F32)
    slopes = (np.float32(2.0) ** (np.float32(-8.0) * np.arange(1, DIFF_HEADS + 1, dtype=np.float32)
                                  / np.float32(DIFF_HEADS))) * np.float32(LOG2E)
    s_hi = slopes.astype(BF16).astype(np.float32)
    s_mid = (slopes - s_hi).astype(BF16).astype(np.float32)
    s_lo = (slopes - s_hi - s_mid).astype(BF16).astype(np.float32)
    pieces = jnp.asarray(np.stack([s_hi, s_mid, s_lo], axis=1).reshape(-1))
    slopes = jnp.asarray(slopes)
    return (_trunk(x_prompt, layers, final_w, slopes, pieces), _trunk(x_sample, layers, final_w, slopes, pieces))
```

```python
import functools
import math

import jax
import jax.numpy as jnp
import numpy as np
from jax import lax
from jax.experimental import pallas as pl
from jax.experimental.pallas import tpu as pltpu

F32 = jnp.float32
BF16 = jnp.bfloat16

D_MODEL = 1024
RET_HEADS = 6
RET_HEAD_DIM = 64
RET_WIDTH = RET_HEADS * RET_HEAD_DIM
DIFF_HEADS = 6
DIFF_QK_DIM = 32
DIFF_V_DIM = 64
DIFF_WIDTH = DIFF_HEADS * DIFF_V_DIM
CONV_CHANNELS = 256
CONV_WIDTH = 3
N_GROUPS = 4
EXPERTS_PER_GROUP = 4
N_EXPERTS = 16
EXPERT_FF = 512
EPS = 1e-6

LANES = 128
HEAD_PAIRS = RET_HEADS // 2
RET_CHUNK = 128
SUBLANES = 8
VMEM_LIMIT = 56 * 1024 * 1024
LOG2E = math.log2(math.e)
NEG_BIG = -1e30

_OFF_RG = 3 * RET_WIDTH
_OFF_DQ = 4 * RET_WIDTH
_OFF_DK = _OFF_DQ + DIFF_WIDTH
_OFF_DV = _OFF_DK + DIFF_WIDTH
_OFF_CONV = _OFF_DV + DIFF_WIDTH


def _params(*sem):
    return pltpu.CompilerParams(dimension_semantics=sem, vmem_limit_bytes=VMEM_LIMIT)


def _split_hi_lo(x):
    hi = x.astype(BF16)
    lo = (x - hi.astype(F32)).astype(BF16)
    return hi, lo


def _in_proj_kernel(x_ref, nw_ref, w_ref, wt_ref, rqkv_ref, rg_ref, dk_ref, dqvt_ref, cv_ref, kn_ref):
    x = x_ref[0]
    ms = jnp.mean(x * x, axis=-1, keepdims=True)
    h = (x * lax.rsqrt(ms + EPS) * nw_ref[...]).astype(BF16)
    a = jnp.dot(h, w_ref[:, 0:_OFF_DQ], preferred_element_type=F32)
    rqkv_ref[0, :, 0:RET_WIDTH] = a[:, 0:RET_WIDTH].astype(BF16)
    rqkv_ref[0, :, RET_WIDTH:2 * RET_WIDTH] = (a[:, RET_WIDTH:2 * RET_WIDTH] * RET_HEAD_DIM ** -0.5).astype(BF16)
    rqkv_ref[0, :, 2 * RET_WIDTH:3 * RET_WIDTH] = a[:, 2 * RET_WIDTH:3 * RET_WIDTH].astype(BF16)
    rg_ref[0] = a[:, _OFF_RG:_OFF_DQ]
    dk = jnp.dot(h, w_ref[:, _OFF_DQ:_OFF_DQ + DIFF_WIDTH], preferred_element_type=F32).astype(BF16)
    dk_ref[0] = dk
    dkf = dk.astype(F32)
    sq_hi, sq_lo = _split_hi_lo(dkf * dkf)
    grp = (lax.shift_right_logical(lax.broadcasted_iota(jnp.int32, (DIFF_WIDTH, LANES), 0),
                                   DIFF_QK_DIM.bit_length() - 1)
           == lax.broadcasted_iota(jnp.int32, (DIFF_WIDTH, LANES), 1))
    grp = jnp.where(grp, 1.0, 0.0).astype(BF16)
    kn2 = jnp.dot(sq_hi, grp, preferred_element_type=F32) + jnp.dot(sq_lo, grp, preferred_element_type=F32)
    kn_ref[0, 0] = jnp.broadcast_to(jnp.max(kn2, axis=0, keepdims=True), (SUBLANES, LANES))
    cv_ref[0] =jnp.dot(h, w_ref[:, _OFF_DQ + DIFF_WIDTH:], preferred_element_type=F32)
    t = lax.dot_general(wt_ref[...], h, (((1,), (1,)), ((), ())), preferred_element_type=F32)
    dqvt_ref[0, 0:DIFF_WIDTH, :] = (t[0:DIFF_WIDTH] * (DIFF_QK_DIM ** -0.5 * LOG2E)).astype(BF16)
    dqvt_ref[0, DIFF_WIDTH:, :] = t[DIFF_WIDTH:].astype(BF16)


def _in_proj(x, nw, w_nat, w_t, tm):
    B, S, D = x.shape
    n_nat = w_nat.shape[1]
    return pl.pallas_call(
        _in_proj_kernel,
        grid=(B, S // tm),
        in_specs=[
            pl.BlockSpec((1, tm, D), lambda b, i: (b, i, 0)),
            pl.BlockSpec((1, D), lambda b, i: (0, 0)),
            pl.BlockSpec((D, n_nat), lambda b, i: (0, 0)),
            pl.BlockSpec((2 * DIFF_WIDTH, D), lambda b, i: (0, 0)),
        ],
        out_specs=[
            pl.BlockSpec((1, tm, 3 * RET_WIDTH), lambda b, i: (b, i, 0)),
            pl.BlockSpec((1, tm, RET_WIDTH), lambda b, i: (b, i, 0)),
            pl.BlockSpec((1, tm, DIFF_WIDTH), lambda b, i: (b, i, 0)),
            pl.BlockSpec((1, 2 * DIFF_WIDTH, tm), lambda b, i: (b, 0, i)),
            pl.BlockSpec((1, tm, 3 * CONV_CHANNELS), lambda b, i: (b, i, 0)),
            pl.BlockSpec((1, 1, SUBLANES, LANES), lambda b, i: (b, i, 0, 0)),
        ],
        out_shape=[
            jax.ShapeDtypeStruct((B, S, 3 * RET_WIDTH), BF16),
            jax.ShapeDtypeStruct((B, S, RET_WIDTH), F32),
            jax.ShapeDtypeStruct((B, S, DIFF_WIDTH), BF16),
            jax.ShapeDtypeStruct((B, 2 * DIFF_WIDTH, S), BF16),
            jax.ShapeDtypeStruct((B, S, 3 * CONV_CHANNELS), F32),
            jax.ShapeDtypeStruct((B, S // tm, SUBLANES, LANES), F32),
        ],
        compiler_params=_params("parallel", "parallel"),
        name="in_proj",
    )(x, nw, w_nat, w_t)


def _pair_lane_value(lg_ref, direction, pair, shape, axis):
    idx = lax.broadcasted_iota(jnp.int32, shape, axis)
    return jnp.where(idx < RET_HEAD_DIM, lg_ref[direction, 2 * pair], lg_ref[direction, 2 * pair + 1])


def _same_head_mask():
    r = lax.broadcasted_iota(jnp.int32, (LANES, LANES), 0)
    c = lax.broadcasted_iota(jnp.int32, (LANES, LANES), 1)
    return (r < RET_HEAD_DIM) == (c < RET_HEAD_DIM)


def _ret_fwd_kernel(lg_ref, qkv_ref, y_ref, state_ref, dmat_ref, qdec_ref, kdec_ref, cdec_ref, *, chunks):
    C = RET_CHUNK

    @pl.when((pl.program_id(0) == 0) & (pl.program_id(1) == 0))
    def _build_tables():
        diff = (lax.broadcasted_iota(jnp.int32, (C, C), 0) - lax.broadcasted_iota(jnp.int32, (C, C), 1)).astype(F32)
        for h in range(RET_HEADS):
            lower = jnp.exp(jnp.maximum(diff, 0.0) * lg_ref[0, h])
            upper = jnp.exp(jnp.maximum(-diff, 0.0) * lg_ref[1, h])
            dmat_ref[h] = jnp.where(diff >= 0, lower, upper)
        pos = lax.broadcasted_iota(jnp.int32, (C, LANES), 0).astype(F32)
        for p in range(HEAD_PAIRS):
            lg_lane = _pair_lane_value(lg_ref, 0, p, (C, LANES), 1)
            qdec_ref[p] = jnp.exp((pos + 1.0) * lg_lane)
            kdec_ref[p] = jnp.exp((C - 1.0 - pos) * lg_lane)
            cdec_ref[p] = jnp.exp(float(C) * _pair_lane_value(lg_ref, 0, p, (LANES, LANES), 0))

    @pl.when(pl.program_id(1) == 0)
    def _reset():
        state_ref[...] = jnp.zeros(state_ref.shape, F32)

    lane = lax.broadcasted_iota(jnp.int32, (C, LANES), 1)
    same_head = _same_head_mask()
    def operands(p, ci):
        rows = slice(ci * C, (ci + 1) * C)
        return (qkv_ref[0, rows, p * LANES:(p + 1) * LANES],
                qkv_ref[0, rows, RET_WIDTH + p * LANES:RET_WIDTH + (p + 1) * LANES],
                qkv_ref[0, rows, 2 * RET_WIDTH + p * LANES:2 * RET_WIDTH + (p + 1) * LANES])

    items = [(p, ci) for p in range(HEAD_PAIRS) for ci in range(chunks)]
    scores, kvs = {}, {}
    for p, ci in items:
        q, k, v = operands(p, ci)
        for h in range(2):
            head_lanes = (lane >= h * RET_HEAD_DIM) & (lane < (h + 1) * RET_HEAD_DIM)
            qm = jnp.where(head_lanes, q, jnp.zeros_like(q))
            scores[p, ci, h] = lax.dot_general(qm, k, (((1,), (1,)), ((), ())), preferred_element_type=F32)
        kd = (k.astype(F32) * kdec_ref[p]).astype(BF16)
        kvs[p, ci] = lax.dot_general(kd, v, (((0,), (0,)), ((), ())), preferred_element_type=F32)
    states = {}
    for p in range(HEAD_PAIRS):
        state = state_ref[p]
        for ci in range(chunks):
            states[p, ci] = state.astype(BF16)
            state = cdec_ref[p] * state + jnp.where(same_head, kvs[p, ci], 0.0)
        state_ref[p] = state
    for p, ci in items:
        q, _, v = operands(p, ci)
        inner = [jnp.dot((scores[p, ci, h] * dmat_ref[2 * p + h]).astype(BF16), v, preferred_element_type=F32)
                 for h in range(2)]
        cross = jnp.dot(q, states[p, ci], preferred_element_type=F32) * qdec_ref[p]
        y_ref[0, ci * C:(ci + 1) * C, p * LANES:(p + 1) * LANES] = (
            jnp.where(lane < RET_HEAD_DIM, inner[0], inner[1]) + cross)


def _ret_bwd_kernel(lg_ref, qkv_ref, y1_ref, g_ref, gnw_ref, o_ref, state_ref, qdec_ref, kdec_ref, cdec_ref, *,
                    chunks):
    C = RET_CHUNK

    @pl.when((pl.program_id(0) == 0) & (pl.program_id(1) == 0))
    def _build_tables():
        pos = lax.broadcasted_iota(jnp.int32, (C, LANES), 0).astype(F32)
        for p in range(HEAD_PAIRS):
            lg_lane = _pair_lane_value(lg_ref, 1, p, (C, LANES), 1)
            qdec_ref[p] = jnp.exp((float(C) - pos) * lg_lane)
            kdec_ref[p] = jnp.exp(pos * lg_lane)
            cdec_ref[p] = jnp.exp(float(C) * _pair_lane_value(lg_ref, 1, p, (LANES, LANES), 0))

    @pl.when(pl.program_id(1) == 0)
    def _reset():
        state_ref[...] = jnp.zeros(state_ref.shape, F32)

    same_head = _same_head_mask()
    head_avg = jnp.where(same_head, 1.0 / RET_HEAD_DIM, 0.0).astype(BF16)

    def head_mean(t):
        hi, lo = _split_hi_lo(t)
        return (jnp.dot(hi, head_avg, preferred_element_type=F32)
                + jnp.dot(lo, head_avg, preferred_element_type=F32))

    items = [(p, ci) for p in range(HEAD_PAIRS) for ci in range(chunks)]
    kvs = {}
    for p, ci in items:
        rows = slice(ci * C, (ci + 1) * C)
        k = qkv_ref[0, rows, RET_WIDTH + p * LANES:RET_WIDTH + (p + 1) * LANES]
        v = qkv_ref[0, rows, 2 * RET_WIDTH + p * LANES:2 * RET_WIDTH + (p + 1) * LANES]
        kd = (k.astype(F32) * kdec_ref[p]).astype(BF16)
        kvs[p, ci] = lax.dot_general(kd, v, (((0,), (0,)), ((), ())), preferred_element_type=F32)
    states = {}
    for p in range(HEAD_PAIRS):
        state = state_ref[p]
        for ci in reversed(range(chunks)):
            states[p, ci] = state.astype(BF16)
            state = cdec_ref[p] * state + jnp.where(same_head, kvs[p, ci], 0.0)
        state_ref[p] = state
    ys = {}
    for p, ci in items:
        rows = slice(ci * C, (ci + 1) * C)
        sl = slice(p * LANES, (p + 1) * LANES)
        cross = jnp.dot(qkv_ref[0, rows, sl], states[p, ci], preferred_element_type=F32)
        ys[p, ci] = y1_ref[0, rows, sl] + cross * qdec_ref[p]
    means = {key: head_mean(y) for key, y in ys.items()}
    devs = {key: ys[key] - means[key] for key in ys}
    variances = {key: head_mean(d * d) for key, d in devs.items()}
    for p, ci in items:
        rows = slice(ci * C, (ci + 1) * C)
        sl = slice(p * LANES, (p + 1) * LANES)
        g = g_ref[0, rows, sl]
        o = devs[p, ci] * lax.rsqrt(variances[p, ci] + EPS) * gnw_ref[:, sl] * (g * jax.nn.sigmoid(g))
        o_ref[0, rows, sl] = o.astype(o_ref.dtype)


def _retention(lg, rqkv, rg, gn_w):
    B, S, _ = rqkv.shape
    chunks = math.gcd(S // RET_CHUNK, 8)
    C = chunks * RET_CHUNK
    n = S // C
    smem = pl.BlockSpec(memory_space=pltpu.SMEM)
    table = pltpu.VMEM((HEAD_PAIRS, RET_CHUNK, LANES), F32)
    state = pltpu.VMEM((HEAD_PAIRS, LANES, LANES), F32)
    y1 = pl.pallas_call(
        functools.partial(_ret_fwd_kernel, chunks=chunks),
        grid=(B, n),
        in_specs=[smem, pl.BlockSpec((1, C, 3 * RET_WIDTH), lambda b, i: (b, i, 0))],
        out_specs=pl.BlockSpec((1, C, RET_WIDTH), lambda b, i: (b, i, 0)),
        out_shape=jax.ShapeDtypeStruct((B, S, RET_WIDTH), F32),
        scratch_shapes=[state, pltpu.VMEM((RET_HEADS, RET_CHUNK, RET_CHUNK), F32), table, table, state],
        compiler_params=_params("arbitrary", "arbitrary"),
        name="ret_fwd",
    )(lg, rqkv)
    rev = lambda b, i: (b, n - 1 - i, 0)
    return pl.pallas_call(
        functools.partial(_ret_bwd_kernel, chunks=chunks),
        grid=(B, n),
        in_specs=[
            smem,
            pl.BlockSpec((1, C, 3 * RET_WIDTH), rev),
            pl.BlockSpec((1, C, RET_WIDTH), rev),
            pl.BlockSpec((1, C, RET_WIDTH), rev),
            pl.BlockSpec((1, RET_WIDTH), lambda b, i: (0, 0)),
        ],
        out_specs=pl.BlockSpec((1, C, RET_WIDTH), rev),
        out_shape=jax.ShapeDtypeStruct((B, S, RET_WIDTH), BF16),
        scratch_shapes=[state, table, table, state],
        compiler_params=_params("arbitrary", "arbitrary"),
        name="ret_bwd",
    )(lg, rqkv, y1, rg, gn_w)


AUX_BLOCK = 256
AUX_ROWS = 16
V_ROWS = DIFF_V_DIM + SUBLANES
GUARD_LOG2 = 60.0
BOUND_SLACK = 1.01
LONG_BODY_PAIRS = 8
UNDERFLOW_LOG2 = 150.0


def _position_features(seq):
    j = jnp.arange(seq, dtype=jnp.int32)
    ones = jnp.ones((seq,), F32)
    n = (j // AUX_BLOCK).astype(F32)
    jc = (j % AUX_BLOCK - AUX_BLOCK // 2).astype(F32)
    feat = jnp.stack([ones] * 3 + [n] * 3 + [jc] * 3, axis=1)
    return jnp.pad(feat, ((0, 0), (0, LANES - feat.shape[1]))).astype(BF16)


def _split3(x):
    hi = x.astype(BF16).astype(F32)
    r = x - hi
    mid = r.astype(BF16).astype(F32)
    return hi, mid, (r - mid).astype(BF16).astype(F32)


def _attn_kernel(slopes_ref, pieces_ref, lam_ref, sw_ref, kn_ref, qt_ref, k_ref, aux_ref, vt_ref, o_ref,
                 m_ref, acc_ref, qm_ref, aug_ref, pa_ref, pb_ref, dist_ref, *, seq, tq, tk, lam_init):
    pair = pl.program_id(1)
    qi = pl.program_id(2)
    q0 = qi * tq
    nk = seq // tk
    per = tq // tk
    kd0 = qi * per
    qt = qt_ref[0]
    row = lax.broadcasted_iota(jnp.int32, (LANES, tq), 0)
    q_masked = []
    for hc in range(4):
        lo = hc * DIFF_QK_DIM
        q_masked.append(jnp.where((row >= lo) & (row < lo + DIFF_QK_DIM), qt, jnp.zeros_like(qt)))

    @pl.when(qi == 0)
    def _distance_table():
        d = lax.broadcasted_iota(jnp.int32, (tq, tq), 1) - lax.broadcasted_iota(jnp.int32, (tq, tq), 0)
        dist_ref[...] = jnp.abs(d).astype(F32)

    kd = pl.multiple_of(q0, tq)
    kdiag = k_ref[0, pl.ds(kd, tq), :]
    self_prod = qt.astype(F32) * kdiag.astype(F32).T
    shifts = [jnp.sum(self_prod[hc * DIFF_QK_DIM:(hc + 1) * DIFF_QK_DIM], axis=0, keepdims=True) for hc in range(4)]

    kn2 = jnp.max(jnp.max(kn_ref[0], axis=0), axis=0, keepdims=True)
    kn_lane = lax.broadcasted_iota(jnp.int32, (1, LANES), 1)
    excess = []
    for h in range(2):
        worst = None
        for c in range(2):
            hc = 2 * h + c
            qf = q_masked[hc].astype(F32)
            qn2 = jnp.sum(qf * qf, axis=0, keepdims=True)
            kn2_hc = jnp.max(jnp.where(kn_lane == 4 * pair + hc, kn2, 0.0), axis=-1, keepdims=True)
            bound = jnp.sqrt(qn2 * kn2_hc) * BOUND_SLACK + 1e-3
            e = jnp.max(bound - shifts[hc], axis=-1, keepdims=True)
            worst = e if worst is None else jnp.maximum(worst, e)
        excess.append(worst)
    single_pass = jnp.where(jnp.maximum(excess[0], excess[1]) <= GUARD_LOG2, 1, 0).astype(jnp.int32)[0, 0] == 1

    def v_rows_with_ones(h, k0, n):
        vt = vt_ref[0, h * DIFF_V_DIM:(h + 1) * DIFF_V_DIM, pl.ds(k0, n)]
        return jnp.concatenate([vt, jnp.ones((SUBLANES, n), BF16)], axis=0)

    def keys_with_features(k0):
        return jnp.concatenate([k_ref[0, pl.ds(k0, tk), :], aux_ref[pl.ds(k0, tk), :]], axis=1)

    def queries_with_features(side, hc):
        return jnp.concatenate([qm_ref[hc], aug_ref[side * 4 + hc],
                                jnp.zeros((LANES - AUX_ROWS, tq), BF16)], axis=0)

    @pl.when(single_pass)
    def _single_pass():
        ipos = (q0 + lax.broadcasted_iota(jnp.int32, (1, tq), 1)).astype(F32)
        aug_row = lax.broadcasted_iota(jnp.int32, (AUX_ROWS, tq), 0)
        q0f = q0.astype(F32)
        lo_even, n_lo, totals = [], [], []
        for h in range(2):
            slope = slopes_ref[2 * pair + h]
            pieces = [pieces_ref[(2 * pair + h) * 3 + i] for i in range(3)]
            for c in range(2):
                hc = 2 * h + c
                for side, sign in ((0, 1.0), (1, -1.0), (2, 0.0)):
                    const = (-sign) * (slope * ipos) - shifts[hc] + sign * (0.5 * AUX_BLOCK) * slope
                    c_hi, c_mid, c_lo = _split3(const)
                    aug = jnp.where(aug_row == 0, c_hi,
                                    jnp.where(aug_row == 1, c_mid, jnp.where(aug_row == 2, c_lo, 0.0)))
                    for i in range(3):
                        aug = jnp.where(aug_row == 3 + i, sign * AUX_BLOCK * pieces[i], aug)
                        aug = jnp.where(aug_row == 6 + i, sign * pieces[i], aug)
                    aug_ref[side * 4 + hc] = aug.astype(BF16)
                qm_ref[hc] = q_masked[hc]
            reach = (excess[h] + UNDERFLOW_LOG2) / slope
            lo_blk = jnp.clip(jnp.floor((q0f - (tk - 1.0) - reach) / tk), 0.0, float(nk))
            hi_blk = jnp.clip(jnp.floor((reach + q0f + (tq - 1.0)) / tk) + 1.0, 0.0, float(nk))
            lo_blk = jnp.minimum(lo_blk.astype(jnp.int32)[0, 0], kd0)
            hi_blk = jnp.maximum(hi_blk.astype(jnp.int32)[0, 0], kd0 + per)
            odd = (kd0 - lo_blk + hi_blk - kd0 - per) % 2
            lo_h = jnp.where((odd == 1) & (lo_blk > 0), lo_blk - 1, lo_blk)
            hi_h = jnp.where((odd == 1) & (lo_blk == 0), hi_blk + 1, hi_blk)
            lo_even.append(lo_h)
            n_lo.append(kd0 - lo_h)
            totals.append(kd0 - lo_h + hi_h - kd0 - per)
        pairs_h0 = totals[0] // 2
        n_pairs = pairs_h0 + totals[1] // 2

        def locate(x, half):
            head = jnp.where(x >= pairs_h0, 1, 0)
            i = 2 * (x - head * pairs_h0) + half
            before = jnp.where(head == 1, n_lo[1], n_lo[0])
            first = jnp.where(head == 1, lo_even[1], lo_even[0])
            blk = jnp.where(i < before, first + i, kd0 + per + i - before)
            return head, blk, jnp.where(i < before, 0, 1)

        def stage_a(x, p_ref):
            for half in range(2):
                head, blk, side = locate(x, half)
                lhs = keys_with_features(pl.multiple_of(blk * tk, tk))
                for c in range(2):
                    t = jnp.dot(lhs, queries_with_features(side, 2 * head + c), preferred_element_type=F32)
                    p_ref[2 * half + c] = jnp.exp2(t).astype(BF16)

        def stage_b(x, p_ref):
            head, blk_a, _ = locate(x, 0)
            _, blk_b, _ = locate(x, 1)
            rows = pl.ds(pl.multiple_of(head * DIFF_V_DIM, DIFF_V_DIM), DIFF_V_DIM)
            ones = jnp.ones((SUBLANES, tk), BF16)
            va = jnp.concatenate([vt_ref[0, rows, pl.ds(pl.multiple_of(blk_a * tk, tk), tk)], ones], axis=0)
            vb = jnp.concatenate([vt_ref[0, rows, pl.ds(pl.multiple_of(blk_b * tk, tk), tk)], ones], axis=0)
            for c in range(2):
                acc_ref[2 * head + c] += (jnp.dot(va, p_ref[c], preferred_element_type=F32)
                                          + jnp.dot(vb, p_ref[2 + c], preferred_element_type=F32))

        def diag_a(h, p_ref):
            for half in range(2):
                lhs = keys_with_features(pl.multiple_of(q0 + half * tk, tk))
                bias = dist_ref[half * tk:(half + 1) * tk, :] * slopes_ref[2 * pair + h]
                for c in range(2):
                    t = jnp.dot(lhs, queries_with_features(2, 2 * h + c), preferred_element_type=F32) - bias
                    p_ref[2 * half + c] = jnp.exp2(t).astype(BF16)

        def diag_b(h, p_ref):
            va = v_rows_with_ones(h, pl.multiple_of(q0, tk), tk)
            vb = v_rows_with_ones(h, pl.multiple_of(q0 + tk, tk), tk)
            for c in range(2):
                acc_ref[2 * h + c] = (jnp.dot(va, p_ref[c], preferred_element_type=F32)
                                      + jnp.dot(vb, p_ref[2 + c], preferred_element_type=F32))

        diag_a(0, pa_ref)
        diag_a(1, pb_ref)
        diag_b(0, pa_ref)

        @pl.when(n_pairs == 0)
        def _diagonal_only():
            diag_b(1, pb_ref)

        @pl.when(n_pairs > 0)
        def _blocks():
            stage_a(0, pa_ref)
            diag_b(1, pb_ref)

            def pairs_body(count):
                def body(j, carry):
                    for s in range(0, count, 2):
                        stage_a(count * j + s + 1, pb_ref)
                        stage_b(count * j + s, pa_ref)
                        stage_a(count * j + s + 2, pa_ref)
                        stage_b(count * j + s + 1, pb_ref)
                    return carry
                return body

            long_iters = (n_pairs - 1) // LONG_BODY_PAIRS
            mid = LONG_BODY_PAIRS // 2
            mid_iters = (n_pairs - 1) // mid
            lax.fori_loop(0, long_iters, pairs_body(LONG_BODY_PAIRS), 0)
            lax.fori_loop(2 * long_iters, mid_iters, pairs_body(mid), 0)
            lax.fori_loop(mid // 2 * mid_iters, (n_pairs - 1) // 2, pairs_body(2), 0)

            @pl.when(n_pairs % 2 == 1)
            def _last_one():
                stage_b(n_pairs - 1, pa_ref)

            @pl.when(n_pairs % 2 == 0)
            def _last_two():
                stage_a(n_pairs - 1, pb_ref)
                stage_b(n_pairs - 2, pa_ref)
                stage_b(n_pairs - 1, pb_ref)

    @pl.when(jnp.logical_not(single_pass))
    def _online():
        m_ref[...] = jnp.full(m_ref.shape, NEG_BIG, F32)
        acc_ref[...] = jnp.zeros(acc_ref.shape, F32)
        rel = (lax.broadcasted_iota(jnp.int32, (tk, tq), 1)
               - lax.broadcasted_iota(jnp.int32, (tk, tq), 0)).astype(F32)
        for h in range(2):
            def online_step(kb, carry, h=h):
                k0 = pl.multiple_of(kb * tk, tk)
                kblk = k_ref[0, pl.ds(k0, tk), :]
                v_ones = v_rows_with_ones(h, k0, tk)
                bias = jnp.abs(rel + (q0 - k0).astype(F32)) * slopes_ref[2 * pair + h]
                for c in range(2):
                    hc = 2 * h + c
                    s = jnp.dot(kblk, q_masked[hc], preferred_element_type=F32) - bias
                    m_old = m_ref[hc]
                    m_new = jnp.maximum(m_old, jnp.max(s, axis=0, keepdims=True))
                    p = jnp.exp2(s - m_new).astype(BF16)
                    acc_ref[hc] = (jnp.exp2(m_old - m_new) * acc_ref[hc]
                                   + jnp.dot(v_ones, p, preferred_element_type=F32))
                    m_ref[hc] = m_new
                return carry

            lax.fori_loop(0, nk, online_step, 0)

    lam = lam_ref[...]
    lam_full = (jnp.exp(jnp.sum(lam[0:1] * lam[1:2], axis=-1, keepdims=True))
                - jnp.exp(jnp.sum(lam[2:3] * lam[3:4], axis=-1, keepdims=True)) + lam_init)
    outs = []
    for h in range(2):
        a1 = acc_ref[2 * h]
        a2 = acc_ref[2 * h + 1]
        rows = slice(0, DIFF_V_DIM)
        ones_row = slice(DIFF_V_DIM, DIFF_V_DIM + 1)
        o = a1[rows] / a1[ones_row] - lam_full * (a2[rows] / a2[ones_row])
        o = o * lax.rsqrt(jnp.mean(o * o, axis=0, keepdims=True) + EPS)
        outs.append(o * sw_ref[...] * (1.0 - lam_init))
    o_ref[0] = jnp.concatenate(outs, axis=0).T.astype(o_ref.dtype)


def _diff_attention(slopes, pieces, lam, subln_w, kn, dqvt, dk, aux, layer_idx, tq, tk):
    B, S, _ = dk.shape
    assert tq == 2 * tk and tk <= AUX_BLOCK and S // AUX_BLOCK <= AUX_BLOCK
    lam_init = 0.8 - 0.6 * math.exp(-0.3 * layer_idx)
    kern = functools.partial(_attn_kernel, seq=S, tq=tq, tk=tk, lam_init=lam_init)
    v_block0 = DIFF_WIDTH // LANES
    smem = pl.BlockSpec(memory_space=pltpu.SMEM)
    return pl.pallas_call(
        kern,
        grid=(B, HEAD_PAIRS, S // tq),
        in_specs=[
            smem, smem,
            pl.BlockSpec((4, DIFF_QK_DIM), lambda b, j, i: (0, 0)),
            pl.BlockSpec((DIFF_V_DIM, 1), lambda b, j, i: (0, 0)),
            pl.BlockSpec((1,) + kn.shape[1:], lambda b, j, i: (b, 0, 0, 0)),
            pl.BlockSpec((1, LANES, tq), lambda b, j, i: (b, j, i)),
            pl.BlockSpec((1, S, LANES), lambda b, j, i: (b, 0, j)),
            pl.BlockSpec((S, LANES), lambda b, j, i: (0, 0)),
            pl.BlockSpec((1, LANES, S), lambda b, j, i: (b, v_block0 + j, 0)),
        ],
        out_specs=pl.BlockSpec((1, tq, LANES), lambda b, j, i: (b, i, j)),
        out_shape=jax.ShapeDtypeStruct((B, S, DIFF_WIDTH), BF16),
        scratch_shapes=[pltpu.VMEM((4, 1, tq), F32), pltpu.VMEM((4, V_ROWS, tq), F32),
                        pltpu.VMEM((4, LANES, tq), BF16), pltpu.VMEM((12, AUX_ROWS, tq), BF16),
                        pltpu.VMEM((4, tk, tq), BF16), pltpu.VMEM((4, tk, tq), BF16),
                        pltpu.VMEM((tq, tq), F32)],
        compiler_params=_params("parallel", "parallel", "arbitrary"),
        name="diff_attn",
    )(slopes, pieces, lam, subln_w, kn, dqvt, dk, aux, dqvt)


def _out_proj_kernel(x_ref, ret_ref, diff_ref, cv_ref, prev_ref, next_ref, cw_ref, w_ref, o_ref, *, tm):
    i = pl.program_id(1)
    last = pl.num_programs(1) - 1
    cc = CONV_CHANNELS
    cv = cv_ref[0]
    u = cv[:, 2 * cc:3 * cc] * cv[:, 0:cc]
    prev = prev_ref[0]
    nxt = next_ref[0]
    u_prev = prev[SUBLANES - 1:SUBLANES, 2 * cc:3 * cc] * prev[SUBLANES - 1:SUBLANES, 0:cc]
    u_next = nxt[0:1, 2 * cc:3 * cc] * nxt[0:1, 0:cc]
    u_prev = jnp.where(i == 0, 0.0, u_prev)
    u_next = jnp.where(i == last, 0.0, u_next)
    rows = lax.broadcasted_iota(jnp.int32, (tm, cc), 0)
    u_m1 = jnp.where(rows == 0, u_prev, pltpu.roll(u, 1, 0))
    u_p1 = jnp.where(rows == tm - 1, u_next, pltpu.roll(u, tm - 1, 0))
    y = cw_ref[0:1] * u_m1 + cw_ref[1:2] * u + cw_ref[2:3] * u_p1
    conv = (cv[:, cc:2 * cc] * y).astype(BF16)
    acc = jnp.dot(ret_ref[0], w_ref[0:RET_WIDTH], preferred_element_type=F32)
    acc += jnp.dot(diff_ref[0], w_ref[RET_WIDTH:RET_WIDTH + DIFF_WIDTH], preferred_element_type=F32)
    acc += jnp.dot(conv, w_ref[RET_WIDTH + DIFF_WIDTH:], preferred_element_type=F32)
    o_ref[0] = x_ref[0] + acc


def _out_proj(x, ret, diff, cv, conv_w, w_out, tm):
    B, S, D = x.shape
    per = tm // SUBLANES
    nblk8 = S // SUBLANES
    tile = lambda width: pl.BlockSpec((1, tm, width), lambda b, i: (b, i, 0))
    return pl.pallas_call(
        functools.partial(_out_proj_kernel, tm=tm),
        grid=(B, S // tm),
        in_specs=[
            tile(D), tile(RET_WIDTH), tile(DIFF_WIDTH), tile(3 * CONV_CHANNELS),
            pl.BlockSpec((1, SUBLANES, 3 * CONV_CHANNELS), lambda b, i: (b, jnp.maximum(i * per - 1, 0), 0)),
            pl.BlockSpec((1, SUBLANES, 3 * CONV_CHANNELS),
                         lambda b, i: (b, jnp.minimum((i + 1) * per, nblk8 - 1), 0)),
            pl.BlockSpec((CONV_WIDTH, CONV_CHANNELS), lambda b, i: (0, 0)),
            pl.BlockSpec((D, D), lambda b, i: (0, 0)),
        ],
        out_specs=tile(D),
        out_shape=jax.ShapeDtypeStruct((B, S, D), F32),
        compiler_params=_params("parallel", "parallel"),
        name="out_proj",
    )(x, ret, diff, cv, cv, cv, conv_w, w_out)


_GROUP_LANE0 = N_EXPERTS
MOE_ROW_BLOCK = 256
MOE_LONG_BLOCK = 288
MOE_ROW_ALIGN = math.gcd(MOE_ROW_BLOCK, MOE_LONG_BLOCK)
MOE_EXPERTS_PER_STEP = 2
assert EXPERTS_PER_GROUP % MOE_EXPERTS_PER_STEP == 0


def _moe_sorted_rows(tm):
    assert MOE_ROW_BLOCK < MOE_LONG_BLOCK <= 2 * MOE_ROW_BLOCK
    return (tm + N_GROUPS * (MOE_ROW_BLOCK - 1)) // MOE_ROW_BLOCK * MOE_ROW_BLOCK


def _router_gates(logits):
    tm = logits.shape[0]
    lt = logits.T
    big = jnp.int32(LANES)
    grow = lax.broadcasted_iota(jnp.int32, (SUBLANES, tm), 0)
    gl = jnp.where(grow < N_GROUPS, lt[_GROUP_LANE0:_GROUP_LANE0 + SUBLANES], -jnp.inf)
    g_max = jnp.max(gl, axis=0, keepdims=True)
    g_idx = jnp.min(jnp.where(gl == g_max, grow, big), axis=0, keepdims=True)
    g_w = 1.0 / jnp.sum(jnp.exp(gl - g_max), axis=0, keepdims=True)
    erow = lax.broadcasted_iota(jnp.int32, (N_EXPERTS, tm), 0)
    in_group = lax.shift_right_logical(erow, EXPERTS_PER_GROUP.bit_length() - 1) == g_idx
    el = jnp.where(in_group, lt[0:N_EXPERTS], -jnp.inf)
    e1 = jnp.max(el, axis=0, keepdims=True)
    i1 = jnp.min(jnp.where(el == e1, erow, big), axis=0, keepdims=True)
    el2 = jnp.where(erow == i1, -jnp.inf, el)
    e2 = jnp.max(el2, axis=0, keepdims=True)
    i2 = jnp.min(jnp.where(el2 == e2, erow, big), axis=0, keepdims=True)
    r = jnp.exp(e2 - e1)
    p1 = g_w / (1.0 + r)
    p2 = g_w * r / (1.0 + r)
    gates_t = jnp.where(erow == i1, p1, jnp.where(erow == i2, p2, 0.0))
    member_t = jnp.where(grow == g_idx, 1.0, 0.0)
    rest = jnp.zeros((LANES - N_EXPERTS - SUBLANES, tm), F32)
    return jnp.concatenate([gates_t, member_t, rest], axis=0).T


def _moe_kernel(x_ref, nw_ref, rw_ref, rb_ref, wg_ref, wu_ref, wd_ref, fw_ref, o_ref,
                xs_ref, gs_ref, ys_ref, dest_ref, ltri_ref, seg_ref, *, final_norm, tm):
    tile = pl.program_id(0)
    e = pl.program_id(1)
    sorted_rows = _moe_sorted_rows(tm)

    @pl.when((tile == 0) & (e == 0))
    def _strictly_lower():
        r = lax.broadcasted_iota(jnp.int32, (tm, tm), 0)
        c = lax.broadcasted_iota(jnp.int32, (tm, tm), 1)
        ltri_ref[...] = jnp.where(c < r, 1.0, 0.0).astype(BF16)

    @pl.when(e == 0)
    def _route_and_sort():
        x = x_ref[...]
        xn = x * lax.rsqrt(jnp.mean(x * x, axis=-1, keepdims=True) + EPS) * nw_ref[...]
        hi, lo = _split_hi_lo(xn)
        r_hi, r_lo = _split_hi_lo(rw_ref[...])
        both = jnp.dot(hi, jnp.concatenate([r_hi, r_lo], axis=1), preferred_element_type=F32)
        logits = (both[:, :LANES] + jnp.dot(lo, r_hi, preferred_element_type=F32) + both[:, LANES:]) + rb_ref[...]
        routed = _router_gates(logits)
        lane = lax.broadcasted_iota(jnp.int32, (tm, LANES), 1)
        gates = jnp.where(lane < N_EXPERTS, routed, 0.0)
        member = jnp.where(lane >= _GROUP_LANE0, routed, 0.0)
        earlier = jnp.dot(ltri_ref[...], member.astype(BF16), preferred_element_type=F32)
        rank = jnp.sum(member * earlier, axis=-1, keepdims=True)
        count = jnp.sum(member, axis=0, keepdims=True)
        seg = jnp.floor((count + (MOE_ROW_BLOCK - 1.0)) * (1.0 / MOE_ROW_BLOCK)) * MOE_ROW_BLOCK
        long_one = (count > MOE_ROW_BLOCK) & (count <= MOE_LONG_BLOCK)
        seg = jnp.where(long_one, float(MOE_LONG_BLOCK), seg)
        lane1 = lax.broadcasted_iota(jnp.int32, (1, LANES), 1)
        g0 = _GROUP_LANE0
        s0, s1, s2 = seg[:, g0:g0 + 1], seg[:, g0 + 1:g0 + 2], seg[:, g0 + 2:g0 + 3]
        start = jnp.where(lane1 == g0 + 1, s0,
                          jnp.where(lane1 == g0 + 2, s0 + s1, jnp.where(lane1 == g0 + 3, s0 + s1 + s2, 0.0)))
        dest = rank + jnp.sum(member * start, axis=-1, keepdims=True)
        dest_ref[...] = jnp.broadcast_to(dest, (tm, LANES))
        dest_row = dest_ref[...].T[0:1, :]
        row_id = lax.broadcasted_iota(jnp.int32, (sorted_rows, tm), 0).astype(F32)
        perm = jnp.where(row_id == dest_row, 1.0, 0.0).astype(BF16)
        g_hi, g_mid, g_lo = _split3(gates)
        g_pieces = g_hi + pltpu.roll(g_mid, N_EXPERTS, 1) + pltpu.roll(g_lo, 2 * N_EXPERTS, 1)
        moved = jnp.dot(perm, jnp.concatenate([hi, g_pieces.astype(BF16)], axis=1), preferred_element_type=F32)
        xs_ref[...] = moved[:, :D_MODEL].astype(BF16)
        g_moved = moved[:, D_MODEL:]
        gs_ref[...] = (g_moved + pltpu.roll(g_moved, LANES - N_EXPERTS, 1)
                       + pltpu.roll(g_moved, LANES - 2 * N_EXPERTS, 1))
        ys_ref[...] = jnp.zeros(ys_ref.shape, F32)
        start_i = start.astype(jnp.int32)
        blocks_i = jnp.where(long_one, 0.0, seg * (1.0 / MOE_ROW_BLOCK)).astype(jnp.int32)
        long_i = jnp.where(long_one, 1, 0).astype(jnp.int32)
        for g in range(N_GROUPS):
            seg_ref[g] = start_i[0, _GROUP_LANE0 + g]
            seg_ref[N_GROUPS + g] = blocks_i[0, _GROUP_LANE0 + g]
            seg_ref[2 * N_GROUPS + g] = long_i[0, _GROUP_LANE0 + g]

    group = (e * MOE_EXPERTS_PER_STEP) // EXPERTS_PER_GROUP
    first_row = seg_ref[group]

    def expert_rows(r0, rows):
        r0 = pl.multiple_of(r0, MOE_ROW_ALIGN)
        xb = xs_ref[pl.ds(r0, rows), :]
        gsb = gs_ref[pl.ds(r0, rows), :]
        lane = lax.broadcasted_iota(jnp.int32, gsb.shape, 1)
        experts = range(MOE_EXPERTS_PER_STEP)
        gate = [jnp.dot(xb, wg_ref[j], preferred_element_type=F32) for j in experts]
        up = [jnp.dot(xb, wu_ref[j], preferred_element_type=F32) for j in experts]
        hmid = [(gate[j] * jax.nn.sigmoid(gate[j]) * up[j]).astype(BF16) for j in experts]
        down = [jnp.dot(hmid[j], wd_ref[j], preferred_element_type=F32) for j in experts]
        total = None
        for j in experts:
            g_e = jnp.sum(jnp.where(lane == e * MOE_EXPERTS_PER_STEP + j, gsb, 0.0), axis=-1, keepdims=True)
            total = g_e * down[j] if total is None else total + g_e * down[j]
        ys_ref[pl.ds(r0, rows), :] += total

    def row_block(b, carry):
        expert_rows(first_row + b * MOE_ROW_BLOCK, MOE_ROW_BLOCK)
        return carry

    lax.fori_loop(0, seg_ref[N_GROUPS + group], row_block, 0)

    @pl.when(seg_ref[2 * N_GROUPS + group] == 1)
    def _one_long_block():
        expert_rows(first_row, MOE_LONG_BLOCK)

    @pl.when(e == pl.num_programs(1) - 1)
    def _finish():
        col_id = lax.broadcasted_iota(jnp.int32, (tm, sorted_rows), 1).astype(F32)
        unperm = jnp.where(col_id == dest_ref[:, 0:1], 1.0, 0.0).astype(BF16)
        y = x_ref[...] + jnp.dot(unperm, ys_ref[...].astype(BF16), preferred_element_type=F32)
        if final_norm:
            y = y * lax.rsqrt(jnp.mean(y * y, axis=-1, keepdims=True) + EPS) * fw_ref[...]
        o_ref[...] = y


def _moe(x, nw, r_w, r_b, wg, wu, wd, final_w, final_norm, tm):
    T, D = x.shape
    const = lambda shape: pl.BlockSpec(shape, lambda i, e: (0,) * len(shape))
    sorted_rows = _moe_sorted_rows(tm)
    return pl.pallas_call(
        functools.partial(_moe_kernel, final_norm=final_norm, tm=tm),
        grid=(T // tm, N_EXPERTS // MOE_EXPERTS_PER_STEP),
        in_specs=[
            pl.BlockSpec((tm, D), lambda i, e: (i, 0)),
            const((1, D)), const((D, LANES)), const((1, LANES)),
            pl.BlockSpec((MOE_EXPERTS_PER_STEP, D, EXPERT_FF), lambda i, e: (e, 0, 0)),
            pl.BlockSpec((MOE_EXPERTS_PER_STEP, D, EXPERT_FF), lambda i, e: (e, 0, 0)),
            pl.BlockSpec((MOE_EXPERTS_PER_STEP, EXPERT_FF, D), lambda i, e: (e, 0, 0)),
            const((1, D)),
        ],
        out_specs=pl.BlockSpec((tm, D), lambda i, e: (i, 0)),
        out_shape=jax.ShapeDtypeStruct((T, D), F32),
        scratch_shapes=[pltpu.VMEM((sorted_rows, D), BF16), pltpu.VMEM((sorted_rows, LANES), F32),
                        pltpu.VMEM((sorted_rows, D), F32), pltpu.VMEM((tm, LANES), F32),
                        pltpu.VMEM((tm, tm), BF16), pltpu.SMEM((3 * N_GROUPS,), jnp.int32)],
        compiler_params=_params("arbitrary", "arbitrary"),
        name="moe",
    )(x, nw, r_w, r_b, wg, wu, wd, final_w)


def _tile(n, pref):
    t = min(n, pref)
    assert n % t == 0, (n, t)
    return t


def _prep_layer(l, norm1_w, w_in, ret_decay_logit, ret_gn_w, diff_lambda, diff_subln_w, conv_w, w_out, norm2_w,
                router_group_w, router_group_b, router_expert_w, router_expert_b, expert_w_gate, expert_w_up,
                expert_w_down):
    w = w_in[l]
    w_nat = jnp.concatenate([w[:, :_OFF_DQ], w[:, _OFF_DK:_OFF_DV], w[:, _OFF_CONV:]], axis=1).astype(BF16)
    w_t = jnp.concatenate([w[:, _OFF_DQ:_OFF_DK], w[:, _OFF_DV:_OFF_CONV]], axis=1).T.astype(BF16)
    router = jnp.zeros((D_MODEL, LANES), F32)
    router = router.at[:, :N_EXPERTS].set(router_expert_w[l]).at[:, N_EXPERTS:N_EXPERTS + N_GROUPS].set(
        router_group_w[l])
    r_b = jnp.zeros((1, LANES), F32).at[0, :N_EXPERTS].set(router_expert_b[l]).at[
        0, N_EXPERTS:N_EXPERTS + N_GROUPS].set(router_group_b[l])
    return dict(
        norm1=norm1_w[l][None, :], w_nat=w_nat, w_t=w_t,
        lg=jax.nn.log_sigmoid(ret_decay_logit[l].astype(F32)),
        gn_w=ret_gn_w[l][None, :].astype(F32),
        lam=diff_lambda[l].astype(F32), subln=diff_subln_w[l][:, None].astype(F32),
        conv_w=conv_w[l].astype(F32), w_out=w_out[l].astype(BF16),
        norm2=norm2_w[l][None, :], r_w=router, r_b=r_b,
        wg=expert_w_gate[l].astype(BF16), wu=expert_w_up[l].astype(BF16), wd=expert_w_down[l].astype(BF16),
    )


def _trunk(x, layers, final_w, slopes, pieces):
    B, S, D = x.shape
    tm = _tile(S, 1024)
    tq = _tile(S, 512)
    tk = tq // 2
    t_moe = _tile(B * S, 1024)
    aux = _position_features(S)
    for l, lw in enumerate(layers):
        rqkv, rg, dk, dqvt, cv, kn = _in_proj(x, lw["norm1"], lw["w_nat"], lw["w_t"], tm)
        ret = _retention(lw["lg"], rqkv, rg, lw["gn_w"])
        diff = _diff_attention(slopes, pieces, lw["lam"], lw["subln"], kn, dqvt, dk, aux, l, tq, tk)
        x = _out_proj(x, ret, diff, cv, lw["conv_w"], lw["w_out"], tm)
        last = l == len(layers) - 1
        x = _moe(x.reshape(B * S, D), lw["norm2"], lw["r_w"], lw["r_b"], lw["wg"], lw["wu"],
                 lw["wd"], final_w, last, t_moe).reshape(B, S, D)
    return x


def kernel(x_prompt, x_sample, norm1_w, w_in, ret_decay_logit, ret_gn_w, diff_lambda, diff_subln_w, conv_w, w_out,
           norm2_w, router_group_w, router_group_b, router_expert_w, router_expert_b, expert_w_gate, expert_w_up,
           expert_w_down, final_norm_w):
    depth = w_in.shape[0]
    layers = [
        _prep_layer(l, norm1_w, w_in, ret_decay_logit, ret_gn_w, diff_lambda, diff_subln_w, conv_w, w_out, norm2_w,
                    router_group_w, router_group_b, router_expert_w, router_expert_b, expert_w_gate, expert_w_up,
                    expert_w_down)
        for l in range(depth)
    ]
    final_w = final_norm_w[None, :].astype(F32)
    slopes = (np.float32(2.0) ** (np.float32(-8.0) * np.arange(1, DIFF_HEADS + 1, dtype=np.float32)
                                  / np.float32(DIFF_HEADS))) * np.float32(LOG2E)
    s_hi = slopes.astype(BF16).astype(np.float32)
    s_mid = (slopes - s_hi).astype(BF16).astype(np.float32)
    s_lo = (slopes - s_hi - s_mid).astype(BF16).astype(np.float32)
    pieces = jnp.asarray(np.stack([s_hi, s_mid, s_lo], axis=1).reshape(-1))
    slopes = jnp.asarray(slopes)
    return (_trunk(x_prompt, layers, final_w, slopes, pieces), _trunk(x_sample, layers, final_w, slopes, pieces))
```
